```python
import jax, jax.numpy as jnp
from jax import lax
import numpy as np

D_MODEL = 1024
BATCH = 16
SEQ = 256
DEPTH = 2
DEC_BATCH = 4
DEC_SEQ = 1024
PAST_LEN = 512

GRID_W = 64
N_HEADS = 8
KV_HEADS = 2
GROUP = N_HEADS // KV_HEADS
HEAD_DIM = 64
ATTN_WIDTH = N_HEADS * HEAD_DIM
KV_WIDTH = KV_HEADS * HEAD_DIM
Q_BLOCK = 128
ROPE_THETA = 10000.0
GLA_HEADS = 4
GLA_DK = 64
GLA_DV = 128
GLA_KW = GLA_HEADS * GLA_DK
GLA_VW = GLA_HEADS * GLA_DV
GLA_RANK = 16
GLA_GATE_NORM = 16.0
GLA_CHUNK = 16
N_EXPERTS = 16
N_GROUPS = 4
EXPERTS_PER_GROUP = N_EXPERTS // N_GROUPS
TOP_K = 2
D_EXPERT = 512
NORM_EPS = 1e-6
IN_SPLIT_WIDTHS = (ATTN_WIDTH, KV_WIDTH, KV_WIDTH, GLA_KW, GLA_KW, GLA_VW, GLA_VW, GLA_RANK, GLA_RANK, D_MODEL, D_MODEL)
IN_WIDTH = ATTN_WIDTH + 2 * KV_WIDTH + 2 * GLA_KW + 2 * GLA_VW + 2 * GLA_RANK + 2 * D_MODEL

kernel_name = 'hybrid_diffusion_gqa_gla_groupmoe_step'


def rmsnorm(x, g):
    x32 = x.astype(jnp.float32)
    y = x32 * lax.rsqrt(jnp.mean(x32 * x32, axis=-1, keepdims=True) + NORM_EPS)
    return y.astype(x.dtype) * g


def axial_rope_angles(n_tokens):
    n_rows = n_tokens // GRID_W
    rows = jnp.repeat(jnp.arange(n_rows), GRID_W).astype(jnp.float32)
    cols = jnp.tile(jnp.arange(GRID_W), n_rows).astype(jnp.float32)
    half = HEAD_DIM // 2
    inv_freq = ROPE_THETA ** (-jnp.arange(0, half, 2, dtype=jnp.float32) / half)
    return rows[:, None] * inv_freq[None, :], cols[:, None] * inv_freq[None, :]


def _rotate_half(x, ang):
    n = ang.shape[-1]
    x1, x2 = x[..., :n], x[..., n:]
    cos, sin = jnp.cos(ang).astype(x.dtype), jnp.sin(ang).astype(x.dtype)
    return jnp.concatenate([x1 * cos - x2 * sin, x1 * sin + x2 * cos], axis=-1)


def apply_axial_rope(x, ang_r, ang_c):
    shape = (ang_r.shape[0],) + (1,) * (x.ndim - 3) + (ang_r.shape[1],)
    half = HEAD_DIM // 2
    return jnp.concatenate([_rotate_half(x[..., :half], ang_r.reshape(shape)),
                            _rotate_half(x[..., half:], ang_c.reshape(shape))], axis=-1)


def block_attention(q, k, v):
    b, t = q.shape[:2]
    nb = t // Q_BLOCK
    qb = jnp.moveaxis(q.reshape(b, nb, Q_BLOCK, KV_HEADS, GROUP, HEAD_DIM), 1, 0)

    def one_block(qi):
        s = jnp.einsum('bqhgd,bkhd->bhgqk', qi, k, preferred_element_type=jnp.float32)
        p = jax.nn.softmax(s, axis=-1)
        return jnp.einsum('bhgqk,bkhd->bqhgd', p.astype(v.dtype), v)

    o = lax.map(one_block, qb)
    return jnp.moveaxis(o, 0, 1).reshape(b, t, ATTN_WIDTH)


def gla_chunked(q, k, v, log_a, s0):
    b, t = q.shape[:2]
    n = t // GLA_CHUNK

    def chunks(z):
        return z.astype(jnp.float32).reshape(b, n, GLA_CHUNK, GLA_HEADS, z.shape[-1]).transpose(1, 0, 3, 2, 4)

    q, k, v, la = chunks(q), chunks(k), chunks(v), chunks(log_a)
    bcum = jnp.cumsum(la, axis=3)
    causal = jnp.tril(jnp.ones((GLA_CHUNK, GLA_CHUNK), dtype=bool))[:, :, None]
    diff = bcum[..., :, None, :] - bcum[..., None, :, :]
    decay = jnp.where(causal, jnp.exp(jnp.where(causal, diff, 0.0)), 0.0)
    scores = jnp.einsum('nbhtk,nbhsk,nbhtsk->nbhts', q, k, decay)
    o_intra = jnp.einsum('nbhts,nbhsv->nbhtv', scores, v)
    b_last = bcum[..., -1:, :]
    q_in = q * jnp.exp(bcum)
    k_st = k * jnp.exp(b_last - bcum)
    a_tot = jnp.exp(b_last[..., 0, :])

    def step(S, inp):
        qi, ki, vi, ai = inp
        o = jnp.einsum('bhtk,bhkv->bhtv', qi, S)
        S = ai[..., None] * S + jnp.einsum('bhsk,bhsv->bhkv', ki, vi)
        return S, o

    s_fin, o_inter = lax.scan(step, s0.astype(jnp.float32), (q_in, k_st, v, a_tot))
    o = (o_intra + o_inter).transpose(1, 0, 3, 2, 4).reshape(b, t, GLA_HEADS, GLA_DV)
    return o, s_fin


def bidirectional_gla(q, k, v, la_f, la_b, s0_f, s0_b):
    o_f, s_f = gla_chunked(q, k, v, la_f, s0_f)
    flip = lambda z: jnp.flip(z, axis=1)
    o_b, s_b = gla_chunked(flip(q), flip(k), flip(v), flip(la_b), s0_b)
    return o_f + flip(o_b), s_f, s_b


def grouped_moe(h, router_w, router_bias, w_gate, w_up, w_down):
    b, t, d = h.shape
    hf = h.reshape(b * t, d)
    scores = jax.nn.sigmoid(jnp.matmul(hf, router_w, preferred_element_type=jnp.float32))
    biased = scores + router_bias.astype(jnp.float32)
    group_score = lax.top_k(biased.reshape(-1, N_GROUPS, EXPERTS_PER_GROUP), TOP_K)[0].sum(-1)
    best_group = jnp.argmax(group_score, axis=-1)
    in_group = (jnp.arange(N_EXPERTS) // EXPERTS_PER_GROUP)[None, :] == best_group[:, None]
    _, top_idx = lax.top_k(jnp.where(in_group, biased, -jnp.inf), TOP_K)
    top_w = jnp.take_along_axis(scores, top_idx, axis=-1)
    top_w = top_w / jnp.sum(top_w, axis=-1, keepdims=True)
    gates = jnp.sum(jax.nn.one_hot(top_idx, N_EXPERTS, dtype=jnp.float32) * top_w[..., None], axis=1)
    act = jax.nn.silu(jnp.einsum('nd,edf->nef', hf, w_gate)) * jnp.einsum('nd,edf->nef', hf, w_up)
    y = jnp.einsum('nef,efd->nd', act * gates[..., None].astype(act.dtype), w_down)
    return y.reshape(b, t, d)


def _split_proj(proj):
    offsets = np.cumsum(IN_SPLIT_WIDTHS)[:-1].tolist()
    return jnp.split(proj, offsets, axis=-1)


def trunk_layer(x, cond, p, router_w, router_bias, rope=None, ctx_k=None, ctx_v=None, s0_f=None, s0_b=None):
    b, t, _ = x.shape
    mod = (jnp.matmul(jax.nn.silu(cond), p['w_ada']) + p['b_ada'])[:, None, :]
    sh1, sc1, g1, sh2, sc2, g2 = jnp.split(mod, 6, axis=-1)
    h = rmsnorm(x, p['norm1']) * (1 + sc1) + sh1
    q, k, v, gq, gk, gv, gg, lr_f, lr_b, ga, gb = _split_proj(jnp.matmul(h, p['w_in']))
    q = rmsnorm(q.reshape(b, t, KV_HEADS, GROUP, HEAD_DIM), p['q_norm'])
    k = rmsnorm(k.reshape(b, t, KV_HEADS, HEAD_DIM), p['k_norm'])
    v = v.reshape(b, t, KV_HEADS, HEAD_DIM)
    if rope is None:
        k_all, v_all = k, v
        s0_f = jnp.zeros((b, GLA_HEADS, GLA_DK, GLA_DV), jnp.float32)
        s0_b = jnp.zeros((b, GLA_HEADS, GLA_DK, GLA_DV), jnp.float32)
    else:
        ang_r, ang_c = rope
        q = apply_axial_rope(q, ang_r, ang_c)
        k_all = jnp.concatenate([apply_axial_rope(k, ang_r, ang_c), ctx_k.astype(k.dtype)], axis=1)
        v_all = jnp.concatenate([v, ctx_v.astype(v.dtype)], axis=1)
    o_attn = block_attention(q * HEAD_DIM ** -0.5, k_all, v_all)
    gq = gq.reshape(b, t, GLA_HEADS, GLA_DK) * GLA_DK ** -0.5
    gk = gk.reshape(b, t, GLA_HEADS, GLA_DK)
    gv = gv.reshape(b, t, GLA_HEADS, GLA_DV)
    la_f = jax.nn.log_sigmoid((jnp.matmul(lr_f, p['gla_w2'][0]) + p['gla_b'][0]).astype(jnp.float32)) / GLA_GATE_NORM
    la_b = jax.nn.log_sigmoid((jnp.matmul(lr_b, p['gla_w2'][1]) + p['gla_b'][1]).astype(jnp.float32)) / GLA_GATE_NORM
    la_f = la_f.reshape(b, t, GLA_HEADS, GLA_DK)
    la_b = la_b.reshape(b, t, GLA_HEADS, GLA_DK)
    o_gla, s_f, s_b = bidirectional_gla(gq, gk, gv, la_f, la_b, s0_f, s0_b)
    o_gla = rmsnorm(o_gla.astype(x.dtype), p['gla_norm']).reshape(b, t, GLA_VW) * jax.nn.silu(gg)
    merged = jax.nn.sigmoid(ga) * jnp.matmul(o_attn, p['w_pa']) + jax.nn.sigmoid(gb) * jnp.matmul(o_gla, p['w_pb'])
    x = x + g1 * jnp.matmul(merged, p['w_out'])
    h2 = rmsnorm(x, p['norm2']) * (1 + sc2) + sh2
    x = x + g2 * grouped_moe(h2, router_w, router_bias, p['w_gate'], p['w_up'], p['w_down'])
    return x, k, v, s_f, s_b


def setup_inputs(seed: int = 0) -> dict:
    key = jax.random.key(seed)
    ks = jax.random.split(key, 26)
    f32 = jnp.float32
    nrm = lambda k, shape, s=1.0: jax.random.normal(k, shape, f32) * s
    return {
        'x_prompt': nrm(ks[0], (BATCH, SEQ, D_MODEL)),
        'x_sample': nrm(ks[1], (DEC_BATCH, DEC_SEQ, D_MODEL)),
        'cache_k': nrm(ks[2], (DEC_BATCH, DEPTH, PAST_LEN, KV_HEADS, HEAD_DIM)),
        'cache_v': nrm(ks[3], (DEC_BATCH, DEPTH, PAST_LEN, KV_HEADS, HEAD_DIM)),
        'state_gla': nrm(ks[4], (DEC_BATCH, DEPTH, 2, GLA_HEADS, GLA_DK, GLA_DV)),
        'c': nrm(ks[5], (DEC_BATCH, D_MODEL)),
        'c_ctx': nrm(ks[6], (D_MODEL,)),
        'norm1': 1.0 + nrm(ks[7], (DEPTH, D_MODEL), 0.02),
        'norm2': 1.0 + nrm(ks[8], (DEPTH, D_MODEL), 0.02),
        'w_ada': nrm(ks[9], (DEPTH, D_MODEL, 6 * D_MODEL), 0.5 * D_MODEL ** -0.5),
        'b_ada': nrm(ks[10], (DEPTH, 6 * D_MODEL), 0.02),
        'w_in': nrm(ks[11], (DEPTH, D_MODEL, IN_WIDTH), D_MODEL ** -0.5),
        'q_norm': 1.0 + nrm(ks[12], (DEPTH, HEAD_DIM), 0.02),
        'k_norm': 1.0 + nrm(ks[13], (DEPTH, HEAD_DIM), 0.02),
        'gla_w2': nrm(ks[14], (DEPTH, 2, GLA_RANK, GLA_KW), GLA_RANK ** -0.5),
        'gla_b': nrm(ks[15], (DEPTH, 2, GLA_KW), 0.1),
        'gla_norm': 1.0 + nrm(ks[16], (DEPTH, GLA_DV), 0.02),
        'w_pa': nrm(ks[17], (DEPTH, ATTN_WIDTH, D_MODEL), ATTN_WIDTH ** -0.5),
        'w_pb': nrm(ks[18], (DEPTH, GLA_VW, D_MODEL), GLA_VW ** -0.5),
        'w_out': nrm(ks[19], (DEPTH, D_MODEL, D_MODEL), D_MODEL ** -0.5),
        'router_w': nrm(ks[20], (D_MODEL, N_EXPERTS), D_MODEL ** -0.5),
        'router_bias': nrm(ks[21], (N_EXPERTS,), 0.01),
        'w_gate': nrm(ks[22], (DEPTH, N_EXPERTS, D_MODEL, D_EXPERT), D_MODEL ** -0.5),
        'w_up': nrm(ks[23], (DEPTH, N_EXPERTS, D_MODEL, D_EXPERT), D_MODEL ** -0.5),
        'w_down': nrm(ks[24], (DEPTH, N_EXPERTS, D_EXPERT, D_MODEL), D_EXPERT ** -0.5),
    }


def reference(x_prompt, x_sample, cache_k, cache_v, state_gla, c, c_ctx, norm1, norm2, w_ada, b_ada, w_in,
              q_norm, k_norm, gla_w2, gla_b, gla_norm, w_pa, w_pb, w_out, router_w, router_bias,
              w_gate, w_up, w_down):
    def layer_params(l):
        return {'norm1': norm1[l], 'norm2': norm2[l], 'w_ada': w_ada[l], 'b_ada': b_ada[l], 'w_in': w_in[l],
                'q_norm': q_norm[l], 'k_norm': k_norm[l], 'gla_w2': gla_w2[l], 'gla_b': gla_b[l],
                'gla_norm': gla_norm[l], 'w_pa': w_pa[l], 'w_pb': w_pb[l], 'w_out': w_out[l],
                'w_gate': w_gate[l], 'w_up': w_up[l], 'w_down': w_down[l]}

    y_prompt = x_prompt
    ctx_keys, ctx_vals, ctx_states = [], [], []
    for l in range(DEPTH):
        y_prompt, k_l, v_l, s_f, s_b = trunk_layer(y_prompt, c_ctx[None, :], layer_params(l), router_w, router_bias)
        ctx_keys.append(k_l)
        ctx_vals.append(v_l)
        ctx_states.append(jnp.stack([s_f, s_b], axis=1))
    new_cache_k = jnp.stack(ctx_keys, axis=1)
    new_cache_v = jnp.stack(ctx_vals, axis=1)
    new_state_gla = jnp.stack(ctx_states, axis=1)

    rope = axial_rope_angles(x_sample.shape[1])
    y_sample = x_sample
    for l in range(DEPTH):
        y_sample, _, _, _, _ = trunk_layer(y_sample, c, layer_params(l), router_w, router_bias, rope=rope,
                                           ctx_k=cache_k[:, l], ctx_v=cache_v[:, l],
                                           s0_f=state_gla[:, l, 0], s0_b=state_gla[:, l, 1])
    return (y_prompt, y_sample, new_cache_k, new_cache_v, new_state_gla)
```

```python
from functools import partial

import jax
import jax.numpy as jnp
import numpy as np
from jax import lax
from jax.experimental import pallas as pl
from jax.experimental.pallas import tpu as pltpu

F32 = jnp.float32
BF16 = jnp.bfloat16

D_MODEL = 1024
BATCH = 16
SEQ = 256
DEPTH = 2
DEC_BATCH = 4
DEC_SEQ = 1024
PAST_LEN = 512
GRID_W = 64
N_HEADS = 8
KV_HEADS = 2
GROUP = N_HEADS // KV_HEADS
HEAD_DIM = 64
ATTN_WIDTH = N_HEADS * HEAD_DIM
KV_WIDTH = KV_HEADS * HEAD_DIM
ROPE_THETA = 10000.0
GLA_HEADS = 4
GLA_DK = 64
GLA_DV = 128
GLA_KW = GLA_HEADS * GLA_DK
GLA_VW = GLA_HEADS * GLA_DV
GLA_RANK = 16
GLA_GATE_NORM = 16.0
N_EXPERTS = 16
N_GROUPS = 4
EXPERTS_PER_GROUP = N_EXPERTS // N_GROUPS
D_EXPERT = 512
NORM_EPS = 1e-6

N_CTX = BATCH * SEQ
N_LAT = DEC_BATCH * DEC_SEQ
N_TOK = N_CTX + N_LAT
N_COND = 8
MOD_W = 6 * D_MODEL

TM = 256
TM_MOE = 1024
MOE_SUB = 256
GLA_CHUNK = 64
GLA_LEVELS = (32, 16, 8)
GLA_DIAG = 8
LR_PAD = 128
SEG_MIX = ATTN_WIDTH + 2 * KV_WIDTH + 2 * GLA_KW + 2 * GLA_VW
OFF_GA = SEG_MIX
OFF_GB = OFF_GA + D_MODEL
OFF_LR = OFF_GB + D_MODEL
IN_PACKED = OFF_LR + LR_PAD
VMEM_LIMIT = 56 * 1024 * 1024

_NT = (((1,), (1,)), ((), ()))
_TN = (((0,), (0,)), ((), ()))


def _mod_row(tile, tm):
    start = tile * tm
    return jnp.where(start < N_CTX, 0, 1 + (start - N_CTX) // DEC_SEQ)


def _split3(x):
    hi = x.astype(BF16)
    r1 = x - hi.astype(F32)
    mid = r1.astype(BF16)
    lo = (r1 - mid.astype(F32)).astype(BF16)
    return hi, mid, lo


def _split2(x):
    hi = x.astype(BF16)
    lo = (x - hi.astype(F32)).astype(BF16)
    return hi, lo


def _dot(a, b):
    return jnp.dot(a, b, preferred_element_type=F32)


def _rms(x):
    return x * lax.rsqrt(jnp.mean(x * x, axis=-1, keepdims=True) + NORM_EPS)


def _ada_kernel(cond_ref, w_ref, b_ref, out_ref):
    c = cond_ref[...]
    s = c * jax.nn.sigmoid(c)
    out_ref[...] = _dot(s.astype(BF16), w_ref[...].astype(BF16)) + b_ref[...]


def _ada(cond, w_ada, b_ada):
    nb = MOD_W // D_MODEL
    return pl.pallas_call(
        _ada_kernel,
        grid=(DEPTH, nb),
        in_specs=[
            pl.BlockSpec((N_COND, D_MODEL), lambda l, j: (0, 0)),
            pl.BlockSpec((None, D_MODEL, D_MODEL), lambda l, j: (l, 0, j)),
            pl.BlockSpec((None, 1, D_MODEL), lambda l, j: (l, 0, j)),
        ],
        out_specs=pl.BlockSpec((None, N_COND, D_MODEL), lambda l, j: (l, 0, j)),
        out_shape=jax.ShapeDtypeStruct((DEPTH, N_COND, MOD_W), F32),
        compiler_params=pltpu.CompilerParams(vmem_limit_bytes=VMEM_LIMIT),
        name="ada",
    )(cond, w_ada, b_ada.reshape(DEPTH, 1, MOD_W))


def _head_rmsnorm(x, bd):
    hi, lo = _split2(x * x)
    ssq = _dot(hi, bd) + _dot(lo, bd)
    return x * lax.rsqrt(ssq * (1.0 / HEAD_DIM) + NORM_EPS)


def _rope(x, cos, sin):
    n = x.shape[-1]
    lane = lax.broadcasted_iota(jnp.int32, x.shape, 1)
    first = (lane & (HEAD_DIM // 4)) == 0
    partner = jnp.where(first, pltpu.roll(x, n - HEAD_DIM // 4, 1), pltpu.roll(x, HEAD_DIM // 4, 1))
    return x * cos + partner * sin


def _inproj_kernel(x_ref, mod_ref, n1_ref, w_ref, qn_ref, kn_ref, cos_ref, sin_ref, bd_ref, w2_ref, gb_ref,
                   q_out, kf_out, kr_out, v_out, gq_out, gk_out, gv_out, gg_out, la_out, sa_out, sb_out):
    x = x_ref[...]
    sh1 = mod_ref[:, 0:D_MODEL]
    sc1 = mod_ref[:, D_MODEL:2 * D_MODEL]
    h = (_rms(x) * n1_ref[...] * (1.0 + sc1) + sh1).astype(BF16)

    def proj(lo, width):
        return _dot(h, w_ref[:, lo:lo + width])

    cos = cos_ref[...]
    sin = sin_ref[...]
    bd = bd_ref[...]
    off = 0
    q = _head_rmsnorm(proj(off, ATTN_WIDTH), bd) * qn_ref[...]
    q_out[...] = (_rope(q, cos, sin) * HEAD_DIM ** -0.5).astype(BF16)
    off += ATTN_WIDTH
    k = _head_rmsnorm(proj(off, KV_WIDTH), bd[:KV_WIDTH, :KV_WIDTH]) * kn_ref[...]
    kf_out[...] = k
    kr_out[...] = _rope(k, cos[:, :KV_WIDTH], sin[:, :KV_WIDTH]).astype(BF16)
    off += KV_WIDTH
    v_out[...] = proj(off, KV_WIDTH)
    off += KV_WIDTH
    gq_out[...] = (proj(off, GLA_KW) * GLA_DK ** -0.5).astype(BF16)
    off += GLA_KW
    gk_out[...] = proj(off, GLA_KW).astype(BF16)
    off += GLA_KW
    gv_out[...] = proj(off, GLA_VW).astype(BF16)
    off += GLA_VW
    gg = proj(off, GLA_VW)
    gg_out[...] = (gg * jax.nn.sigmoid(gg)).astype(BF16)
    sa_out[...] = jax.nn.sigmoid(proj(OFF_GA, D_MODEL)).astype(BF16)
    sb_out[...] = jax.nn.sigmoid(proj(OFF_GB, D_MODEL)).astype(BF16)
    lr = proj(OFF_LR, LR_PAD)
    z = _dot(lr.astype(BF16), w2_ref[...]) + gb_ref[...]
    log_sig = jnp.minimum(z, 0.0) - jnp.log1p(jnp.exp(-jnp.abs(z)))
    la_out[...] = log_sig * (1.0 / GLA_GATE_NORM)


def _inproj(x, mod_l, norm1, w_in_p, qn, kn, cos, sin, bd, w2p, gb):
    nt = N_TOK // TM
    n_ctx_t = N_CTX // TM
    t_per_seq = DEC_SEQ // TM
    row = lambda i: (i, 0)
    const = lambda i: (0, 0)
    rope_blk = lambda i: (jnp.where(i < n_ctx_t, 0, 1 + (i - n_ctx_t) % t_per_seq), 0)

    def out(width, dtype):
        return pl.BlockSpec((TM, width), row), jax.ShapeDtypeStruct((N_TOK, width), dtype)

    outs = [out(ATTN_WIDTH, BF16), out(KV_WIDTH, F32), out(KV_WIDTH, BF16), out(KV_WIDTH, F32),
            out(GLA_KW, BF16), out(GLA_KW, BF16), out(GLA_VW, BF16), out(GLA_VW, BF16),
            out(2 * GLA_KW, F32), out(D_MODEL, BF16), out(D_MODEL, BF16)]
    return pl.pallas_call(
        _inproj_kernel,
        grid=(nt,),
        in_specs=[
            pl.BlockSpec((TM, D_MODEL), row),
            pl.BlockSpec((None, 1, MOD_W), lambda i: (_mod_row(i, TM), 0, 0)),
            pl.BlockSpec((1, D_MODEL), const),
            pl.BlockSpec((D_MODEL, IN_PACKED), const, pipeline_mode=pl.Buffered(1)),
            pl.BlockSpec((1, ATTN_WIDTH), const),
            pl.BlockSpec((1, KV_WIDTH), const),
            pl.BlockSpec((TM, ATTN_WIDTH), rope_blk),
            pl.BlockSpec((TM, ATTN_WIDTH), rope_blk),
            pl.BlockSpec((ATTN_WIDTH, ATTN_WIDTH), const),
            pl.BlockSpec((LR_PAD, 2 * GLA_KW), const),
            pl.BlockSpec((1, 2 * GLA_KW), const),
        ],
        out_specs=[o[0] for o in outs],
        out_shape=[o[1] for o in outs],
        compiler_params=pltpu.CompilerParams(dimension_semantics=("parallel",), vmem_limit_bytes=VMEM_LIMIT),
        name="inproj",
    )(x, mod_l, norm1, w_in_p, qn, kn, cos, sin, bd, w2p, gb)


def _attn_kernel(*refs, has_cache):
    if has_cache:
        q_ref, k_ref, v_ref, ck_ref, cv_ref, _, o_ref = refs
    else:
        q_ref, k_ref, v_ref, o_ref = refs
    tq = q_ref.shape[0]
    for g in range(KV_HEADS):
        ks = slice(g * HEAD_DIM, (g + 1) * HEAD_DIM)
        qs = jnp.concatenate(
            [q_ref[:, (g * GROUP + j) * HEAD_DIM:(g * GROUP + j + 1) * HEAD_DIM] for j in range(GROUP)], axis=0)
        k = k_ref[:, ks]
        v = v_ref[:, ks].astype(BF16)
        s = lax.dot_general(qs, k, _NT, preferred_element_type=F32)
        m = jnp.max(s, axis=-1, keepdims=True)
        if has_cache:
            ck = ck_ref[:, ks].astype(BF16)
            cv = cv_ref[:, ks].astype(BF16)
            s2 = lax.dot_general(qs, ck, _NT, preferred_element_type=F32)
            m = jnp.maximum(m, jnp.max(s2, axis=-1, keepdims=True))
        p = jnp.exp(s - m)
        den = jnp.sum(p, axis=-1, keepdims=True)
        o = _dot(p.astype(BF16), v)
        if has_cache:
            p2 = jnp.exp(s2 - m)
            den = den + jnp.sum(p2, axis=-1, keepdims=True)
            o = o + _dot(p2.astype(BF16), cv)
        o = o / den
        for j in range(GROUP):
            hd = g * GROUP + j
            o_ref[:, hd * HEAD_DIM:(hd + 1) * HEAD_DIM] = o[j * tq:(j + 1) * tq, :].astype(BF16)


def _attn_ctx(q, kr, v):
    return pl.pallas_call(
        partial(_attn_kernel, has_cache=False),
        grid=(BATCH,),
        in_specs=[
            pl.BlockSpec((SEQ, ATTN_WIDTH), lambda i: (i, 0)),
            pl.BlockSpec((SEQ, KV_WIDTH), lambda i: (i, 0)),
            pl.BlockSpec((SEQ, KV_WIDTH), lambda i: (i, 0)),
        ],
        out_specs=pl.BlockSpec((SEQ, ATTN_WIDTH), lambda i: (i, 0)),
        out_shape=jax.ShapeDtypeStruct((N_TOK, ATTN_WIDTH), BF16),
        compiler_params=pltpu.CompilerParams(dimension_semantics=("parallel",), vmem_limit_bytes=VMEM_LIMIT),
        name="attn_ctx",
    )(q, kr, v)


def _attn_lat(q, kr, v, cache_k_l, cache_v_l, o_prev):
    tq = 256
    nq = DEC_SEQ // tq
    q0 = N_CTX // tq
    s0 = N_CTX // DEC_SEQ
    return pl.pallas_call(
        partial(_attn_kernel, has_cache=True),
        grid=(DEC_BATCH, nq),
        in_specs=[
            pl.BlockSpec((tq, ATTN_WIDTH), lambda b, i: (q0 + b * nq + i, 0)),
            pl.BlockSpec((DEC_SEQ, KV_WIDTH), lambda b, i: (s0 + b, 0)),
            pl.BlockSpec((DEC_SEQ, KV_WIDTH), lambda b, i: (s0 + b, 0)),
            pl.BlockSpec((None, PAST_LEN, KV_WIDTH), lambda b, i: (b, 0, 0)),
            pl.BlockSpec((None, PAST_LEN, KV_WIDTH), lambda b, i: (b, 0, 0)),
            pl.BlockSpec(memory_space=pl.ANY),
        ],
        out_specs=pl.BlockSpec((tq, ATTN_WIDTH), lambda b, i: (q0 + b * nq + i, 0)),
        out_shape=jax.ShapeDtypeStruct((N_TOK, ATTN_WIDTH), BF16),
        input_output_aliases={5: 0},
        compiler_params=pltpu.CompilerParams(dimension_semantics=("parallel", "parallel"),
                                             vmem_limit_bytes=VMEM_LIMIT),
        name="attn_lat",
    )(q, kr, v, cache_k_l, cache_v_l, o_prev)


def _bd_rows(x, n_heads, width):
    head = lax.broadcasted_iota(jnp.int32, x.shape, 1) // width
    return jnp.concatenate([jnp.where(head == hh, x, 0.0) for hh in range(n_heads)], axis=0)


def _ref_rows(b, rows, rep):
    return jnp.concatenate([jnp.broadcast_to(b[r:r + 1, :], (rep, b.shape[1])) for r in rows], axis=0)


def _gla_scores(q, k, b, reverse):
    c = GLA_CHUNK
    t_idx = lax.broadcasted_iota(jnp.int32, (c, GLA_KW), 0)
    a_t = lax.broadcasted_iota(jnp.int32, (c, GLA_HEADS * c), 0)
    a_s = lax.broadcasted_iota(jnp.int32, (c, GLA_HEADS * c), 1) & (c - 1)
    total = jnp.zeros((c, GLA_HEADS * c), F32)

    def scores(qt, kt):
        return lax.dot_general(qt.astype(BF16), _bd_rows(kt, GLA_HEADS, GLA_DK).astype(BF16), _NT,
                               preferred_element_type=F32)

    for m in GLA_LEVELS:
        nblk = c // (2 * m)
        ref_rows = [2 * m * j + (m if reverse else m - 1) for j in range(nblk)]
        ref = _ref_rows(b, ref_rows, 2 * m)
        q_side = ((t_idx & m) == 0) if reverse else ((t_idx & m) != 0)
        qt = jnp.where(q_side, q * jnp.exp(jnp.minimum(b - ref, 0.0)), 0.0)
        kt = jnp.where(q_side, 0.0, k * jnp.exp(jnp.minimum(ref - b, 0.0)))
        shift = int(np.log2(2 * m))
        same = (a_t >> shift) == (a_s >> shift)
        total = total + jnp.where(same, scores(qt, kt), 0.0)
    d = GLA_DIAG
    ref = _ref_rows(b, [d * j + d // 2 for j in range(c // d)], d)
    x = b - ref
    shift = int(np.log2(d))
    same = (a_t >> shift) == (a_s >> shift)
    tri = (a_s >= a_t) if reverse else (a_s <= a_t)
    total = total + jnp.where(same, jnp.where(tri, scores(q * jnp.exp(x), k * jnp.exp(-x)), 0.0), 0.0)
    return total


def _cumsum_rows(tri, la):
    hi, mid, lo = _split3(la)
    return _dot(tri, hi) + _dot(tri, mid) + _dot(tri, lo)


def _gla_kernel(*refs, seq_len, has_state):
    if has_state:
        gq_ref, gk_ref, gv_ref, la_ref, gg_ref, gn_ref, s0_ref, _, o_ref, acc_ref, st_ref = refs
    else:
        gq_ref, gk_ref, gv_ref, la_ref, gg_ref, gn_ref, o_ref, sfin_ref, acc_ref, st_ref = refs
    c = GLA_CHUNK
    nc = seq_len // c
    r_i = lax.broadcasted_iota(jnp.int32, (c, c), 0)
    c_i = lax.broadcasted_iota(jnp.int32, (c, c), 1)
    tril = jnp.where(c_i <= r_i, 1.0, 0.0).astype(BF16)
    triu = jnp.where(c_i >= r_i, 1.0, 0.0).astype(BF16)
    st_head_r = lax.broadcasted_iota(jnp.int32, (GLA_VW, GLA_KW), 0) // GLA_DV
    st_head_c = lax.broadcasted_iota(jnp.int32, (GLA_VW, GLA_KW), 1) // GLA_DK
    st_mask = st_head_r == st_head_c

    for d in range(2):
        if has_state:
            st_ref[d] = jnp.zeros((GLA_VW, GLA_KW), F32)
            for hh in range(GLA_HEADS):
                st_ref[d, hh * GLA_DV:(hh + 1) * GLA_DV, hh * GLA_DK:(hh + 1) * GLA_DK] = s0_ref[d, hh].T
        else:
            st_ref[d] = jnp.zeros((GLA_VW, GLA_KW), F32)

    def load(n):
        rows = pl.ds(pl.multiple_of(n * c, c), c)
        return (rows, gq_ref[rows, :].astype(F32), gk_ref[rows, :].astype(F32), gv_ref[rows, :].astype(F32))

    def inter(d, q, k, v, b, b_end):
        st = st_ref[d]
        q_in = (q * jnp.exp(b)).astype(BF16)
        o = lax.dot_general(q_in, st.astype(BF16), _NT, preferred_element_type=F32)
        k_st = (k * jnp.exp(b_end - b)).astype(BF16)
        upd = lax.dot_general(v.astype(BF16), k_st, _TN, preferred_element_type=F32)
        st_ref[d] = st * jnp.exp(b_end) + jnp.where(st_mask, upd, 0.0)
        return o

    def fwd_body(n, carry):
        rows, q, k, v = load(n)
        b_f = _cumsum_rows(tril, la_ref[rows, 0:GLA_KW])
        b_b = _cumsum_rows(triu, la_ref[rows, GLA_KW:2 * GLA_KW])
        a = _gla_scores(q, k, b_f, False) + _gla_scores(q, k, b_b, True)
        o = _dot(a.astype(BF16), _bd_rows(v, GLA_HEADS, GLA_DV).astype(BF16))
        o = o + inter(0, q, k, v, b_f, b_f[c - 1:c, :])
        acc_ref[rows, :] = o
        return carry

    lax.fori_loop(0, nc, fwd_body, 0)

    def bwd_body(i, carry):
        n = nc - 1 - i
        rows, q, k, v = load(n)
        b_b = _cumsum_rows(triu, la_ref[rows, GLA_KW:2 * GLA_KW])
        acc_ref[rows, :] = acc_ref[rows, :] + inter(1, q, k, v, b_b, b_b[0:1, :])
        return carry

    lax.fori_loop(0, nc, bwd_body, 0)

    o = acc_ref[...]
    gn = gn_ref[...]
    for hh in range(GLA_HEADS):
        sl = slice(hh * GLA_DV, (hh + 1) * GLA_DV)
        o_ref[:, sl] = (_rms(o[:, sl]) * gn[:, sl] * gg_ref[:, sl].astype(F32)).astype(BF16)
    if not has_state:
        for d in range(2):
            for hh in range(GLA_HEADS):
                sfin_ref[d, hh] = st_ref[d, hh * GLA_DV:(hh + 1) * GLA_DV, hh * GLA_DK:(hh + 1) * GLA_DK].T


def _gla_scratch(seq_len):
    return [pltpu.VMEM((seq_len, GLA_VW), F32), pltpu.VMEM((2, GLA_VW, GLA_KW), F32)]


def _gla_ctx(gq, gk, gv, la, gg, gn):
    seq = lambda w: pl.BlockSpec((SEQ, w), lambda i: (i, 0))
    return pl.pallas_call(
        partial(_gla_kernel, seq_len=SEQ, has_state=False),
        grid=(BATCH,),
        in_specs=[seq(GLA_KW), seq(GLA_KW), seq(GLA_VW), seq(2 * GLA_KW), seq(GLA_VW),
                  pl.BlockSpec((1, GLA_VW), lambda i: (0, 0))],
        out_specs=[seq(GLA_VW),
                   pl.BlockSpec((None, 2, GLA_HEADS, GLA_DK, GLA_DV), lambda i: (i, 0, 0, 0, 0))],
        out_shape=[jax.ShapeDtypeStruct((N_TOK, GLA_VW), BF16),
                   jax.ShapeDtypeStruct((BATCH, 2, GLA_HEADS, GLA_DK, GLA_DV), F32)],
        scratch_shapes=_gla_scratch(SEQ),
        compiler_params=pltpu.CompilerParams(dimension_semantics=("parallel",), vmem_limit_bytes=VMEM_LIMIT),
        name="gla_ctx",
    )(gq, gk, gv, la, gg, gn)


def _gla_lat(gq, gk, gv, la, gg, gn, s0, o_prev):
    s0_blk = N_CTX // DEC_SEQ
    seq = lambda w: pl.BlockSpec((DEC_SEQ, w), lambda b: (s0_blk + b, 0))
    return pl.pallas_call(
        partial(_gla_kernel, seq_len=DEC_SEQ, has_state=True),
        grid=(DEC_BATCH,),
        in_specs=[seq(GLA_KW), seq(GLA_KW), seq(GLA_VW), seq(2 * GLA_KW), seq(GLA_VW),
                  pl.BlockSpec((1, GLA_VW), lambda b: (0, 0)),
                  pl.BlockSpec((None, 2, GLA_HEADS, GLA_DK, GLA_DV), lambda b: (b, 0, 0, 0, 0)),
                  pl.BlockSpec(memory_space=pl.ANY)],
        out_specs=seq(GLA_VW),
        out_shape=jax.ShapeDtypeStruct((N_TOK, GLA_VW), BF16),
        input_output_aliases={7: 0},
        scratch_shapes=_gla_scratch(DEC_SEQ),
        compiler_params=pltpu.CompilerParams(dimension_semantics=("parallel",), vmem_limit_bytes=VMEM_LIMIT),
        name="gla_lat",
    )(gq, gk, gv, la, gg, gn, s0, o_prev)


def _route(scores, bias):
    biased = scores + bias
    v = [biased[e:e + 1, :] for e in range(N_EXPERTS)]
    s = [scores[e:e + 1, :] for e in range(N_EXPERTS)]
    gscore = []
    for g in range(N_GROUPS):
        a, b, c, d = v[4 * g:4 * g + 4]
        hi1, lo1, hi2, lo2 = jnp.maximum(a, b), jnp.minimum(a, b), jnp.maximum(c, d), jnp.minimum(c, d)
        gscore.append(jnp.maximum(hi1, hi2) + jnp.maximum(jnp.minimum(hi1, hi2), jnp.maximum(lo1, lo2)))
    one = lambda cond: jnp.where(cond, 1.0, 0.0)
    picked = []
    for g in range(N_GROUPS):
        best = None
        for g2 in range(N_GROUPS):
            if g2 == g:
                continue
            ok = one((gscore[g] > gscore[g2]) if g2 < g else (gscore[g] >= gscore[g2]))
            best = ok if best is None else best * ok
        for i in range(EXPERTS_PER_GROUP):
            e = 4 * g + i
            beaten = None
            for j in range(EXPERTS_PER_GROUP):
                if j == i:
                    continue
                e2 = 4 * g + j
                lost = one((v[e2] > v[e]) if j > i else (v[e2] >= v[e]))
                beaten = lost if beaten is None else beaten + lost
            picked.append(best * one(beaten < 1.5) * s[e])
    den = picked[0]
    for e in range(1, N_EXPERTS):
        den = den + picked[e]
    return jnp.concatenate([p / den for p in picked], axis=0)


def _mix_kernel(x_ref, oa_ref, og_ref, sa_ref, sb_ref, mod_ref, wpa_ref, wpb_ref, wo_ref, n2_ref,
                rw_ref, rb_ref, x1_out, h2_out, gt_out):
    a = _dot(oa_ref[...], wpa_ref[...])
    b = _dot(og_ref[...], wpb_ref[...])
    merged = sa_ref[...].astype(F32) * a + sb_ref[...].astype(F32) * b
    g1 = mod_ref[:, 2 * D_MODEL:3 * D_MODEL]
    sh2 = mod_ref[:, 3 * D_MODEL:4 * D_MODEL]
    sc2 = mod_ref[:, 4 * D_MODEL:5 * D_MODEL]
    x1 = x_ref[...] + g1 * _dot(merged.astype(BF16), wo_ref[...])
    x1_out[...] = x1
    h2 = _rms(x1) * n2_ref[...] * (1.0 + sc2) + sh2
    h2_out[...] = h2.astype(BF16)
    hh, hm, hl = _split3(h2)
    rh, rm, rl = rw_ref[0], rw_ref[1], rw_ref[2]
    nt = lambda r, t: lax.dot_general(r, t, _NT, preferred_element_type=F32)
    logits = nt(rl, hh) + nt(rh, hl) + nt(rm, hm) + nt(rm, hh) + nt(rh, hm) + nt(rh, hh)
    gt_out[...] = _route(jax.nn.sigmoid(logits), rb_ref[...])


def _mix(x, oa, og, sa, sb, mod_l, wpa, wpb, wo, norm2, rw3, rb):
    nt = N_TOK // TM
    row = lambda i: (i, 0)
    const = lambda i: (0, 0)
    return pl.pallas_call(
        _mix_kernel,
        grid=(nt,),
        in_specs=[
            pl.BlockSpec((TM, D_MODEL), row),
            pl.BlockSpec((TM, ATTN_WIDTH), row),
            pl.BlockSpec((TM, GLA_VW), row),
            pl.BlockSpec((TM, D_MODEL), row),
            pl.BlockSpec((TM, D_MODEL), row),
            pl.BlockSpec((None, 1, MOD_W), lambda i: (_mod_row(i, TM), 0, 0)),
            pl.BlockSpec((ATTN_WIDTH, D_MODEL), const),
            pl.BlockSpec((GLA_VW, D_MODEL), const),
            pl.BlockSpec((D_MODEL, D_MODEL), const),
            pl.BlockSpec((1, D_MODEL), const),
            pl.BlockSpec((3, N_EXPERTS, D_MODEL), lambda i: (0, 0, 0)),
            pl.BlockSpec((N_EXPERTS, 1), const),
        ],
        out_specs=[pl.BlockSpec((TM, D_MODEL), row), pl.BlockSpec((TM, D_MODEL), row),
                   pl.BlockSpec((N_EXPERTS, TM), lambda i: (0, i))],
        out_shape=[jax.ShapeDtypeStruct((N_TOK, D_MODEL), F32), jax.ShapeDtypeStruct((N_TOK, D_MODEL), BF16),
                   jax.ShapeDtypeStruct((N_EXPERTS, N_TOK), F32)],
        compiler_params=pltpu.CompilerParams(dimension_semantics=("parallel",), vmem_limit_bytes=VMEM_LIMIT),
        name="mix",
    )(x, oa, og, sa, sb, mod_l, wpa, wpb, wo, norm2, rw3, rb)


def _moe_kernel(h_ref, g_ref, x1_ref, mod_ref, wg_ref, wu_ref, wd_ref, out_ref, wg_s, wu_s, wd_s):
    e = pl.program_id(1)
    wg_s[...] = wg_ref[...].astype(BF16)
    wu_s[...] = wu_ref[...].astype(BF16)
    wd_s[...] = wd_ref[...].astype(BF16)

    @pl.when(e == 0)
    def _():
        out_ref[...] = jnp.zeros(out_ref.shape, F32)

    lane = lax.broadcasted_iota(jnp.int32, (MOE_SUB, N_EXPERTS), 1)

    def sub(i, carry):
        rows = pl.ds(pl.multiple_of(i * MOE_SUB, MOE_SUB), MOE_SUB)
        h = h_ref[rows, :]
        gate = jnp.sum(jnp.where(lane == e, g_ref[rows, :], 0.0), axis=-1, keepdims=True)
        up = _dot(h, wu_s[...])
        gt = _dot(h, wg_s[...])
        act = (gt * jax.nn.sigmoid(gt) * up * gate).astype(BF16)
        out_ref[rows, :] += _dot(act, wd_s[...])
        return carry

    lax.fori_loop(0, TM_MOE // MOE_SUB, sub, 0)

    @pl.when(e == N_EXPERTS - 1)
    def _():
        g2 = mod_ref[:, 5 * D_MODEL:6 * D_MODEL]
        out_ref[...] = x1_ref[...] + g2 * out_ref[...]


def _moe(h2, gates, x1, mod_l, w_gate_l, w_up_l, w_down_l):
    nt = N_TOK // TM_MOE
    row = lambda i, e: (i, 0)
    return pl.pallas_call(
        _moe_kernel,
        grid=(nt, N_EXPERTS),
        in_specs=[
            pl.BlockSpec((TM_MOE, D_MODEL), row),
            pl.BlockSpec((TM_MOE, N_EXPERTS), row),
            pl.BlockSpec((TM_MOE, D_MODEL), row),
            pl.BlockSpec((None, 1, MOD_W), lambda i, e: (_mod_row(i, TM_MOE), 0, 0)),
            pl.BlockSpec((None, D_MODEL, D_EXPERT), lambda i, e: (e, 0, 0)),
            pl.BlockSpec((None, D_MODEL, D_EXPERT), lambda i, e: (e, 0, 0)),
            pl.BlockSpec((None, D_EXPERT, D_MODEL), lambda i, e: (e, 0, 0)),
        ],
        out_specs=pl.BlockSpec((TM_MOE, D_MODEL), row),
        out_shape=jax.ShapeDtypeStruct((N_TOK, D_MODEL), F32),
        scratch_shapes=[pltpu.VMEM((D_MODEL, D_EXPERT), BF16), pltpu.VMEM((D_MODEL, D_EXPERT), BF16),
                        pltpu.VMEM((D_EXPERT, D_MODEL), BF16)],
        compiler_params=pltpu.CompilerParams(dimension_semantics=("parallel", "arbitrary"),
                                             vmem_limit_bytes=VMEM_LIMIT),
        name="moe",
    )(h2, gates, x1, mod_l, w_gate_l, w_up_l, w_down_l)


def _rope_tables():
    pos = jnp.arange(DEC_SEQ)
    rows = (pos // GRID_W).astype(F32)
    cols = (pos % GRID_W).astype(F32)
    half = HEAD_DIM // 2
    inv_freq = ROPE_THETA ** (-jnp.arange(0, half, 2, dtype=F32) / half)
    ang_r = rows[:, None] * inv_freq[None, :]
    ang_c = cols[:, None] * inv_freq[None, :]
    cos_h = jnp.concatenate([jnp.cos(ang_r), jnp.cos(ang_r), jnp.cos(ang_c), jnp.cos(ang_c)], axis=-1)
    sin_h = jnp.concatenate([-jnp.sin(ang_r), jnp.sin(ang_r), -jnp.sin(ang_c), jnp.sin(ang_c)], axis=-1)
    cos = jnp.concatenate([jnp.ones((TM, HEAD_DIM), F32), cos_h], axis=0)
    sin = jnp.concatenate([jnp.zeros((TM, HEAD_DIM), F32), sin_h], axis=0)
    return jnp.tile(cos, (1, N_HEADS)), jnp.tile(sin, (1, N_HEADS))


def _pack_w_in(w):
    lr0 = SEG_MIX
    ga0 = lr0 + 2 * GLA_RANK
    gb0 = ga0 + D_MODEL
    pad = jnp.zeros((D_MODEL, LR_PAD - 2 * GLA_RANK), w.dtype)
    return jnp.concatenate([w[:, :SEG_MIX], w[:, ga0:gb0], w[:, gb0:gb0 + D_MODEL], w[:, lr0:ga0], pad],
                           axis=1).astype(BF16)


def _pack_w2(w2):
    z = jnp.zeros((GLA_RANK, GLA_KW), w2.dtype)
    top = jnp.concatenate([w2[0], z], axis=1)
    bot = jnp.concatenate([z, w2[1]], axis=1)
    pad = jnp.zeros((LR_PAD - 2 * GLA_RANK, 2 * GLA_KW), w2.dtype)
    return jnp.concatenate([top, bot, pad], axis=0).astype(BF16)


def kernel(x_prompt, x_sample, cache_k, cache_v, state_gla, c, c_ctx, norm1, norm2, w_ada, b_ada, w_in,
           q_norm, k_norm, gla_w2, gla_b, gla_norm, w_pa, w_pb, w_out, router_w, router_bias,
           w_gate, w_up, w_down):
    x = jnp.concatenate([x_prompt.reshape(N_CTX, D_MODEL), x_sample.reshape(N_LAT, D_MODEL)], axis=0)
    cond = jnp.concatenate([c_ctx[None, :], c, jnp.zeros((N_COND - 1 - DEC_BATCH, D_MODEL), F32)], axis=0)
    mod = _ada(cond, w_ada, b_ada).reshape(DEPTH, N_COND, 1, MOD_W)
    cos, sin = _rope_tables()
    lane_head = np.arange(ATTN_WIDTH) // HEAD_DIM
    bd = jnp.asarray(lane_head[:, None] == lane_head[None, :], BF16)
    rw3 = jnp.stack(_split3(router_w.T), axis=0)
    rb = router_bias.reshape(N_EXPERTS, 1)
    ck = cache_k.reshape(DEC_BATCH, DEPTH, PAST_LEN, KV_WIDTH)
    cv = cache_v.reshape(DEC_BATCH, DEPTH, PAST_LEN, KV_WIDTH)

    keys, vals, states = [], [], []
    for l in range(DEPTH):
        q, kf, kr, v, gq, gk, gv, gg, la, sa, sb = _inproj(
            x, mod[l], norm1[l][None, :], _pack_w_in(w_in[l]), jnp.tile(q_norm[l], N_HEADS)[None, :],
            jnp.tile(k_norm[l], KV_HEADS)[None, :], cos, sin, bd, _pack_w2(gla_w2[l]),
            gla_b[l].reshape(1, 2 * GLA_KW))
        oa = _attn_ctx(q, kr, v)
        oa = _attn_lat(q, kr, v, ck[:, l], cv[:, l], oa)
        gn = jnp.tile(gla_norm[l], GLA_HEADS)[None, :]
        og, s_fin = _gla_ctx(gq, gk, gv, la, gg, gn)
        og = _gla_lat(gq, gk, gv, la, gg, gn, state_gla[:, l], og)
        x1, h2, gates_t = _mix(x, oa, og, sa, sb, mod[l], w_pa[l].astype(BF16), w_pb[l].astype(BF16),
                               w_out[l].astype(BF16), norm2[l][None, :], rw3, rb)
        x = _moe(h2, gates_t.T, x1, mod[l], w_gate[l], w_up[l], w_down[l])
        keys.append(kf[:N_CTX].reshape(BATCH, SEQ, KV_HEADS, HEAD_DIM))
        vals.append(v[:N_CTX].reshape(BATCH, SEQ, KV_HEADS, HEAD_DIM))
        states.append(s_fin)
    y_prompt = x[:N_CTX].reshape(BATCH, SEQ, D_MODEL)
    y_sample = x[N_CTX:].reshape(DEC_BATCH, DEC_SEQ, D_MODEL)
    return (y_prompt, y_sample, jnp.stack(keys, axis=1), jnp.stack(vals, axis=1), jnp.stack(states, axis=1))
```

```python
from functools import partial

import jax
import jax.numpy as jnp
import numpy as np
from jax import lax
from jax.experimental import pallas as pl
from jax.experimental.pallas import tpu as pltpu

F32 = jnp.float32
BF16 = jnp.bfloat16

D_MODEL = 1024
BATCH = 16
SEQ = 256
DEPTH = 2
DEC_BATCH = 4
DEC_SEQ = 1024
PAST_LEN = 512
GRID_W = 64
N_HEADS = 8
KV_HEADS = 2
GROUP = N_HEADS // KV_HEADS
HEAD_DIM = 64
ATTN_WIDTH = N_HEADS * HEAD_DIM
KV_WIDTH = KV_HEADS * HEAD_DIM
ROPE_THETA = 10000.0
GLA_HEADS = 4
GLA_DK = 64
GLA_DV = 128
GLA_KW = GLA_HEADS * GLA_DK
GLA_VW = GLA_HEADS * GLA_DV
GLA_RANK = 16
GLA_GATE_NORM = 16.0
N_EXPERTS = 16
N_GROUPS = 4
EXPERTS_PER_GROUP = N_EXPERTS // N_GROUPS
D_EXPERT = 512
NORM_EPS = 1e-6

N_CTX = BATCH * SEQ
N_LAT = DEC_BATCH * DEC_SEQ
N_TOK = N_CTX + N_LAT
N_COND = 8
MOD_W = 6 * D_MODEL

TM = 256
TM_MOE = 1024
MOE_SUB = 256
GLA_CHUNK = 64
GLA_LEVELS = (32, 16, 8)
GLA_DIAG = 8
LR_PAD = 128
SEG_MIX = ATTN_WIDTH + 2 * KV_WIDTH + 2 * GLA_KW + 2 * GLA_VW
OFF_GA = SEG_MIX
OFF_GB = OFF_GA + D_MODEL
OFF_LR = OFF_GB + D_MODEL
IN_PACKED = OFF_LR + LR_PAD
IN_WIDTH = SEG_MIX + 2 * GLA_RANK + 2 * D_MODEL
VMEM_LIMIT = 56 * 1024 * 1024

_NT = (((1,), (1,)), ((), ()))
_TN = (((0,), (0,)), ((), ()))


def _mod_row(tile, tm):
    start = tile * tm
    return jnp.where(start < N_CTX, 0, 1 + (start - N_CTX) // DEC_SEQ)


def _split3(x):
    hi = x.astype(BF16)
    r1 = x - hi.astype(F32)
    mid = r1.astype(BF16)
    lo = (r1 - mid.astype(F32)).astype(BF16)
    return hi, mid, lo


def _split2(x):
    hi = x.astype(BF16)
    lo = (x - hi.astype(F32)).astype(BF16)
    return hi, lo


def _dot(a, b):
    return jnp.dot(a, b, preferred_element_type=F32)


def _rms(x):
    return x * lax.rsqrt(jnp.mean(x * x, axis=-1, keepdims=True) + NORM_EPS)


def _ada_kernel(cond_ref, w_ref, b_ref, out_ref):
    c = cond_ref[...]
    s = c * jax.nn.sigmoid(c)
    out_ref[...] = _dot(s.astype(BF16), w_ref[...].astype(BF16)) + b_ref[...]


def _ada(cond, w_ada, b_ada):
    nb = MOD_W // D_MODEL
    return pl.pallas_call(
        _ada_kernel,
        grid=(DEPTH, nb),
        in_specs=[
            pl.BlockSpec((N_COND, D_MODEL), lambda l, j: (0, 0)),
            pl.BlockSpec((None, D_MODEL, D_MODEL), lambda l, j: (l, 0, j)),
            pl.BlockSpec((None, 1, D_MODEL), lambda l, j: (l, 0, j)),
        ],
        out_specs=pl.BlockSpec((None, N_COND, D_MODEL), lambda l, j: (l, 0, j)),
        out_shape=jax.ShapeDtypeStruct((DEPTH, N_COND, MOD_W), F32),
        compiler_params=pltpu.CompilerParams(vmem_limit_bytes=VMEM_LIMIT),
        name="ada",
    )(cond, w_ada, b_ada.reshape(DEPTH, 1, MOD_W))


def _head_rmsnorm(x, bd):
    hi, lo = _split2(x * x)
    ssq = _dot(hi, bd) + _dot(lo, bd)
    return x * lax.rsqrt(ssq * (1.0 / HEAD_DIM) + NORM_EPS)


def _rope(x, cos, sin):
    n = x.shape[-1]
    lane = lax.broadcasted_iota(jnp.int32, x.shape, 1)
    first = (lane & (HEAD_DIM // 4)) == 0
    partner = jnp.where(first, pltpu.roll(x, n - HEAD_DIM // 4, 1), pltpu.roll(x, HEAD_DIM // 4, 1))
    return x * cos + partner * sin


def _pack_w_in(w_ref, wp_ref):
    lr0 = SEG_MIX
    ga0 = lr0 + 2 * GLA_RANK
    rows_per_step = 128

    def body(i, carry):
        rows = pl.ds(pl.multiple_of(i * rows_per_step, rows_per_step), rows_per_step)
        wp_ref[rows, 0:SEG_MIX] = w_ref[rows, 0:SEG_MIX].astype(BF16)
        tail = w_ref[rows, SEG_MIX:IN_WIDTH]
        lr = tail[:, 0:2 * GLA_RANK]
        wp_ref[rows, OFF_LR:IN_PACKED] = jnp.concatenate(
            [lr, jnp.zeros((rows_per_step, LR_PAD - 2 * GLA_RANK), F32)], axis=1).astype(BF16)
        wp_ref[rows, OFF_GA:OFF_LR] = tail[:, ga0 - lr0:ga0 - lr0 + 2 * D_MODEL].astype(BF16)
        return carry

    lax.fori_loop(0, D_MODEL // rows_per_step, body, 0)


def _inproj_kernel(x_ref, mod_ref, n1_ref, wf_ref, qn_ref, kn_ref, cos_ref, sin_ref, bd_ref, w2_ref, gb_ref,
                   q_out, kf_out, kr_out, v_out, gq_out, gk_out, gv_out, gg_out, la_out, sa_out, sb_out,
                   w_ref):
    @pl.when(pl.program_id(0) == 0)
    def _():
        _pack_w_in(wf_ref, w_ref)

    x = x_ref[...]
    sh1 = mod_ref[:, 0:D_MODEL]
    sc1 = mod_ref[:, D_MODEL:2 * D_MODEL]
    h = (_rms(x) * n1_ref[...] * (1.0 + sc1) + sh1).astype(BF16)

    def proj(lo, width):
        return _dot(h, w_ref[:, lo:lo + width])

    cos = cos_ref[...]
    sin = sin_ref[...]
    bd = bd_ref[...]
    off = 0
    q = _head_rmsnorm(proj(off, ATTN_WIDTH), bd) * qn_ref[...]
    q_out[...] = (_rope(q, cos, sin) * HEAD_DIM ** -0.5).astype(BF16)
    off += ATTN_WIDTH
    k = _head_rmsnorm(proj(off, KV_WIDTH), bd[:KV_WIDTH, :KV_WIDTH]) * kn_ref[...]
    kf_out[...] = k
    kr_out[...] = _rope(k, cos[:, :KV_WIDTH], sin[:, :KV_WIDTH]).astype(BF16)
    off += KV_WIDTH
    v_out[...] = proj(off, KV_WIDTH)
    off += KV_WIDTH
    gq_out[...] = (proj(off, GLA_KW) * GLA_DK ** -0.5).astype(BF16)
    off += GLA_KW
    gk_out[...] = proj(off, GLA_KW).astype(BF16)
    off += GLA_KW
    gv_out[...] = proj(off, GLA_VW).astype(BF16)
    off += GLA_VW
    gg = proj(off, GLA_VW)
    gg_out[...] = (gg * jax.nn.sigmoid(gg)).astype(BF16)
    sa_out[...] = jax.nn.sigmoid(proj(OFF_GA, D_MODEL)).astype(BF16)
    sb_out[...] = jax.nn.sigmoid(proj(OFF_GB, D_MODEL)).astype(BF16)
    lr = proj(OFF_LR, LR_PAD)
    z = _dot(lr.astype(BF16), w2_ref[...]) + gb_ref[...]
    log_sig = jnp.minimum(z, 0.0) - jnp.log1p(jnp.exp(-jnp.abs(z)))
    la_out[...] = log_sig * (1.0 / GLA_GATE_NORM)


def _inproj(layer, x, mod_l, norm1, w_in, qn, kn, cos, sin, bd, w2p, gb):
    nt = N_TOK // TM
    n_ctx_t = N_CTX // TM
    t_per_seq = DEC_SEQ // TM
    row = lambda i: (i, 0)
    const = lambda i: (0, 0)
    rope_blk = lambda i: (jnp.where(i < n_ctx_t, 0, 1 + (i - n_ctx_t) % t_per_seq), 0)

    def out(width, dtype):
        return pl.BlockSpec((TM, width), row), jax.ShapeDtypeStruct((N_TOK, width), dtype)

    outs = [out(ATTN_WIDTH, BF16), out(KV_WIDTH, F32), out(KV_WIDTH, BF16), out(KV_WIDTH, F32),
            out(GLA_KW, BF16), out(GLA_KW, BF16), out(GLA_VW, BF16), out(GLA_VW, BF16),
            out(2 * GLA_KW, F32), out(D_MODEL, BF16), out(D_MODEL, BF16)]
    return pl.pallas_call(
        _inproj_kernel,
        grid=(nt,),
        in_specs=[
            pl.BlockSpec((TM, D_MODEL), row),
            pl.BlockSpec((None, 1, MOD_W), lambda i: (_mod_row(i, TM), 0, 0)),
            pl.BlockSpec((1, D_MODEL), const),
            pl.BlockSpec((None, D_MODEL, IN_WIDTH), lambda i: (layer, 0, 0), pipeline_mode=pl.Buffered(1)),
            pl.BlockSpec((1, ATTN_WIDTH), const),
            pl.BlockSpec((1, KV_WIDTH), const),
            pl.BlockSpec((TM, ATTN_WIDTH), rope_blk),
            pl.BlockSpec((TM, ATTN_WIDTH), rope_blk),
            pl.BlockSpec((ATTN_WIDTH, ATTN_WIDTH), const),
            pl.BlockSpec((LR_PAD, 2 * GLA_KW), const),
            pl.BlockSpec((1, 2 * GLA_KW), const),
        ],
        out_specs=[o[0] for o in outs],
        out_shape=[o[1] for o in outs],
        scratch_shapes=[pltpu.VMEM((D_MODEL, IN_PACKED), BF16)],
        compiler_params=pltpu.CompilerParams(dimension_semantics=("arbitrary",), vmem_limit_bytes=VMEM_LIMIT),
        name="inproj",
    )(x, mod_l, norm1, w_in, qn, kn, cos, sin, bd, w2p, gb)


def _attn_kernel(*refs, has_cache):
    if has_cache:
        q_ref, k_ref, v_ref, ck_ref, cv_ref, _, o_ref = refs
    else:
        q_ref, k_ref, v_ref, o_ref = refs
    tq = q_ref.shape[0]
    for g in range(KV_HEADS):
        ks = slice(g * HEAD_DIM, (g + 1) * HEAD_DIM)
        qs = jnp.concatenate(
            [q_ref[:, (g * GROUP + j) * HEAD_DIM:(g * GROUP + j + 1) * HEAD_DIM] for j in range(GROUP)], axis=0)
        k = k_ref[:, ks]
        v = v_ref[:, ks].astype(BF16)
        s = lax.dot_general(qs, k, _NT, preferred_element_type=F32)
        m = jnp.max(s, axis=-1, keepdims=True)
        if has_cache:
            ck = ck_ref[:, ks].astype(BF16)
            cv = cv_ref[:, ks].astype(BF16)
            s2 = lax.dot_general(qs, ck, _NT, preferred_element_type=F32)
            m = jnp.maximum(m, jnp.max(s2, axis=-1, keepdims=True))
        p = jnp.exp(s - m)
        den = jnp.sum(p, axis=-1, keepdims=True)
        o = _dot(p.astype(BF16), v)
        if has_cache:
            p2 = jnp.exp(s2 - m)
            den = den + jnp.sum(p2, axis=-1, keepdims=True)
            o = o + _dot(p2.astype(BF16), cv)
        o = o / den
        for j in range(GROUP):
            hd = g * GROUP + j
            o_ref[:, hd * HEAD_DIM:(hd + 1) * HEAD_DIM] = o[j * tq:(j + 1) * tq, :].astype(BF16)


def _attn_ctx(q, kr, v):
    return pl.pallas_call(
        partial(_attn_kernel, has_cache=False),
        grid=(BATCH,),
        in_specs=[
            pl.BlockSpec((SEQ, ATTN_WIDTH), lambda i: (i, 0)),
            pl.BlockSpec((SEQ, KV_WIDTH), lambda i: (i, 0)),
            pl.BlockSpec((SEQ, KV_WIDTH), lambda i: (i, 0)),
        ],
        out_specs=pl.BlockSpec((SEQ, ATTN_WIDTH), lambda i: (i, 0)),
        out_shape=jax.ShapeDtypeStruct((N_TOK, ATTN_WIDTH), BF16),
        compiler_params=pltpu.CompilerParams(dimension_semantics=("parallel",), vmem_limit_bytes=VMEM_LIMIT),
        name="attn_ctx",
    )(q, kr, v)


def _attn_lat(layer, q, kr, v, cache_k, cache_v, o_prev):
    tq = 256
    nq = DEC_SEQ // tq
    q0 = N_CTX // tq
    s0 = N_CTX // DEC_SEQ
    return pl.pallas_call(
        partial(_attn_kernel, has_cache=True),
        grid=(DEC_BATCH, nq),
        in_specs=[
            pl.BlockSpec((tq, ATTN_WIDTH), lambda b, i: (q0 + b * nq + i, 0)),
            pl.BlockSpec((DEC_SEQ, KV_WIDTH), lambda b, i: (s0 + b, 0)),
            pl.BlockSpec((DEC_SEQ, KV_WIDTH), lambda b, i: (s0 + b, 0)),
            pl.BlockSpec((None, None, PAST_LEN, KV_WIDTH), lambda b, i: (b, layer, 0, 0)),
            pl.BlockSpec((None, None, PAST_LEN, KV_WIDTH), lambda b, i: (b, layer, 0, 0)),
            pl.BlockSpec(memory_space=pl.ANY),
        ],
        out_specs=pl.BlockSpec((tq, ATTN_WIDTH), lambda b, i: (q0 + b * nq + i, 0)),
        out_shape=jax.ShapeDtypeStruct((N_TOK, ATTN_WIDTH), BF16),
        input_output_aliases={5: 0},
        compiler_params=pltpu.CompilerParams(dimension_semantics=("parallel", "parallel"),
                                             vmem_limit_bytes=VMEM_LIMIT),
        name="attn_lat",
    )(q, kr, v, cache_k, cache_v, o_prev)


def _bd_rows(x, n_heads, width):
    head = lax.broadcasted_iota(jnp.int32, x.shape, 1) // width
    return jnp.concatenate([jnp.where(head == hh, x, 0.0) for hh in range(n_heads)], axis=0)


def _ref_rows(b, rows, rep):
    return jnp.concatenate([jnp.broadcast_to(b[r:r + 1, :], (rep, b.shape[1])) for r in rows], axis=0)


def _gla_scores(q, k, b, reverse):
    c = GLA_CHUNK
    t_idx = lax.broadcasted_iota(jnp.int32, (c, GLA_KW), 0)
    a_t = lax.broadcasted_iota(jnp.int32, (c, GLA_HEADS * c), 0)
    a_s = lax.broadcasted_iota(jnp.int32, (c, GLA_HEADS * c), 1) & (c - 1)
    total = jnp.zeros((c, GLA_HEADS * c), F32)

    def scores(qt, kt):
        return lax.dot_general(qt.astype(BF16), _bd_rows(kt, GLA_HEADS, GLA_DK).astype(BF16), _NT,
                               preferred_element_type=F32)

    for m in GLA_LEVELS:
        nblk = c // (2 * m)
        ref_rows = [2 * m * j + (m if reverse else m - 1) for j in range(nblk)]
        ref = _ref_rows(b, ref_rows, 2 * m)
        q_side = ((t_idx & m) == 0) if reverse else ((t_idx & m) != 0)
        qt = jnp.where(q_side, q * jnp.exp(jnp.minimum(b - ref, 0.0)), 0.0)
        kt = jnp.where(q_side, 0.0, k * jnp.exp(jnp.minimum(ref - b, 0.0)))
        shift = int(np.log2(2 * m))
        same = (a_t >> shift) == (a_s >> shift)
        total = total + jnp.where(same, scores(qt, kt), 0.0)
    d = GLA_DIAG
    ref = _ref_rows(b, [d * j + d // 2 for j in range(c // d)], d)
    x = b - ref
    shift = int(np.log2(d))
    same = (a_t >> shift) == (a_s >> shift)
    tri = (a_s >= a_t) if reverse else (a_s <= a_t)
    total = total + jnp.where(same, jnp.where(tri, scores(q * jnp.exp(x), k * jnp.exp(-x)), 0.0), 0.0)
    return total


def _cumsum_rows(tri, la):
    hi, mid, lo = _split3(la)
    return _dot(tri, hi) + _dot(tri, mid) + _dot(tri, lo)


def _gla_kernel(*refs, seq_len, has_state):
    if has_state:
        gq_ref, gk_ref, gv_ref, la_ref, gg_ref, gn_ref, s0_ref, _, o_ref, acc_ref, st_ref = refs
    else:
        gq_ref, gk_ref, gv_ref, la_ref, gg_ref, gn_ref, o_ref, sfin_ref, acc_ref, st_ref = refs
    c = GLA_CHUNK
    nc = seq_len // c
    r_i = lax.broadcasted_iota(jnp.int32, (c, c), 0)
    c_i = lax.broadcasted_iota(jnp.int32, (c, c), 1)
    tril = jnp.where(c_i <= r_i, 1.0, 0.0).astype(BF16)
    triu = jnp.where(c_i >= r_i, 1.0, 0.0).astype(BF16)
    st_head_r = lax.broadcasted_iota(jnp.int32, (GLA_VW, GLA_KW), 0) // GLA_DV
    st_head_c = lax.broadcasted_iota(jnp.int32, (GLA_VW, GLA_KW), 1) // GLA_DK
    st_mask = st_head_r == st_head_c

    for d in range(2):
        if has_state:
            st_ref[d] = jnp.zeros((GLA_VW, GLA_KW), F32)
            for hh in range(GLA_HEADS):
                st_ref[d, hh * GLA_DV:(hh + 1) * GLA_DV, hh * GLA_DK:(hh + 1) * GLA_DK] = s0_ref[d, hh].T
        else:
            st_ref[d] = jnp.zeros((GLA_VW, GLA_KW), F32)

    def load(n):
        rows = pl.ds(pl.multiple_of(n * c, c), c)
        return (rows, gq_ref[rows, :].astype(F32), gk_ref[rows, :].astype(F32), gv_ref[rows, :].astype(F32))

    def inter(d, q, k, v, b, b_end):
        st = st_ref[d]
        q_in = (q * jnp.exp(b)).astype(BF16)
        o = lax.dot_general(q_in, st.astype(BF16), _NT, preferred_element_type=F32)
        k_st = (k * jnp.exp(b_end - b)).astype(BF16)
        upd = lax.dot_general(v.astype(BF16), k_st, _TN, preferred_element_type=F32)
        st_ref[d] = st * jnp.exp(b_end) + jnp.where(st_mask, upd, 0.0)
        return o

    def fwd_body(n, carry):
        rows, q, k, v = load(n)
        b_f = _cumsum_rows(tril, la_ref[rows, 0:GLA_KW])
        b_b = _cumsum_rows(triu, la_ref[rows, GLA_KW:2 * GLA_KW])
        a = _gla_scores(q, k, b_f, False) + _gla_scores(q, k, b_b, True)
        o = _dot(a.astype(BF16), _bd_rows(v, GLA_HEADS, GLA_DV).astype(BF16))
        o = o + inter(0, q, k, v, b_f, b_f[c - 1:c, :])
        acc_ref[rows, :] = o
        return carry

    lax.fori_loop(0, nc, fwd_body, 0)

    def bwd_body(i, carry):
        n = nc - 1 - i
        rows, q, k, v = load(n)
        b_b = _cumsum_rows(triu, la_ref[rows, GLA_KW:2 * GLA_KW])
        acc_ref[rows, :] = acc_ref[rows, :] + inter(1, q, k, v, b_b, b_b[0:1, :])
        return carry

    lax.fori_loop(0, nc, bwd_body, 0)

    o = acc_ref[...]
    gn = gn_ref[...]
    for hh in range(GLA_HEADS):
        sl = slice(hh * GLA_DV, (hh + 1) * GLA_DV)
        o_ref[:, sl] = (_rms(o[:, sl]) * gn[:, sl] * gg_ref[:, sl].astype(F32)).astype(BF16)
    if not has_state:
        for d in range(2):
            for hh in range(GLA_HEADS):
                sfin_ref[d, hh] = st_ref[d, hh * GLA_DV:(hh + 1) * GLA_DV, hh * GLA_DK:(hh + 1) * GLA_DK].T


def _gla_scratch(seq_len):
    return [pltpu.VMEM((seq_len, GLA_VW), F32), pltpu.VMEM((2, GLA_VW, GLA_KW), F32)]


def _gla_ctx(gq, gk, gv, la, gg, gn):
    seq = lambda w: pl.BlockSpec((SEQ, w), lambda i: (i, 0))
    return pl.pallas_call(
        partial(_gla_kernel, seq_len=SEQ, has_state=False),
        grid=(BATCH,),
        in_specs=[seq(GLA_KW), seq(GLA_KW), seq(GLA_VW), seq(2 * GLA_KW), seq(GLA_VW),
                  pl.BlockSpec((1, GLA_VW), lambda i: (0, 0))],
        out_specs=[seq(GLA_VW),
                   pl.BlockSpec((None, 2, GLA_HEADS, GLA_DK, GLA_DV), lambda i: (i, 0, 0, 0, 0))],
        out_shape=[jax.ShapeDtypeStruct((N_TOK, GLA_VW), BF16),
                   jax.ShapeDtypeStruct((BATCH, 2, GLA_HEADS, GLA_DK, GLA_DV), F32)],
        scratch_shapes=_gla_scratch(SEQ),
        compiler_params=pltpu.CompilerParams(dimension_semantics=("parallel",), vmem_limit_bytes=VMEM_LIMIT),
        name="gla_ctx",
    )(gq, gk, gv, la, gg, gn)


def _gla_lat(layer, gq, gk, gv, la, gg, gn, s0, o_prev):
    s0_blk = N_CTX // DEC_SEQ
    seq = lambda w: pl.BlockSpec((DEC_SEQ, w), lambda b: (s0_blk + b, 0))
    return pl.pallas_call(
        partial(_gla_kernel, seq_len=DEC_SEQ, has_state=True),
        grid=(DEC_BATCH,),
        in_specs=[seq(GLA_KW), seq(GLA_KW), seq(GLA_VW), seq(2 * GLA_KW), seq(GLA_VW),
                  pl.BlockSpec((1, GLA_VW), lambda b: (0, 0)),
                  pl.BlockSpec((None, None, 2, GLA_HEADS, GLA_DK, GLA_DV), lambda b: (b, layer, 0, 0, 0, 0)),
                  pl.BlockSpec(memory_space=pl.ANY)],
        out_specs=seq(GLA_VW),
        out_shape=jax.ShapeDtypeStruct((N_TOK, GLA_VW), BF16),
        input_output_aliases={7: 0},
        scratch_shapes=_gla_scratch(DEC_SEQ),
        compiler_params=pltpu.CompilerParams(dimension_semantics=("parallel",), vmem_limit_bytes=VMEM_LIMIT),
        name="gla_lat",
    )(gq, gk, gv, la, gg, gn, s0, o_prev)


def _route(scores, bias):
    biased = scores + bias
    v = [biased[e:e + 1, :] for e in range(N_EXPERTS)]
    s = [scores[e:e + 1, :] for e in range(N_EXPERTS)]
    gscore = []
    for g in range(N_GROUPS):
        a, b, c, d = v[4 * g:4 * g + 4]
        hi1, lo1, hi2, lo2 = jnp.maximum(a, b), jnp.minimum(a, b), jnp.maximum(c, d), jnp.minimum(c, d)
        gscore.append(jnp.maximum(hi1, hi2) + jnp.maximum(jnp.minimum(hi1, hi2), jnp.maximum(lo1, lo2)))
    one = lambda cond: jnp.where(cond, 1.0, 0.0)
    picked = []
    for g in range(N_GROUPS):
        best = None
        for g2 in range(N_GROUPS):
            if g2 == g:
                continue
            ok = one((gscore[g] > gscore[g2]) if g2 < g else (gscore[g] >= gscore[g2]))
            best = ok if best is None else best * ok
        for i in range(EXPERTS_PER_GROUP):
            e = 4 * g + i
            beaten = None
            for j in range(EXPERTS_PER_GROUP):
                if j == i:
                    continue
                e2 = 4 * g + j
                lost = one((v[e2] > v[e]) if j > i else (v[e2] >= v[e]))
                beaten = lost if beaten is None else beaten + lost
            picked.append(best * one(beaten < 1.5) * s[e])
    den = picked[0]
    for e in range(1, N_EXPERTS):
        den = den + picked[e]
    return jnp.concatenate([p / den for p in picked], axis=0)


def _mix_kernel(x_ref, oa_ref, og_ref, sa_ref, sb_ref, mod_ref, wpa_f, wpb_f, wo_f, n2_ref,
                rw_ref, rb_ref, x1_out, h2_out, gt_out, wpa_ref, wpb_ref, wo_ref):
    @pl.when(pl.program_id(0) == 0)
    def _():
        wpa_ref[...] = wpa_f[...].astype(BF16)
        wpb_ref[...] = wpb_f[...].astype(BF16)
        wo_ref[...] = wo_f[...].astype(BF16)

    a = _dot(oa_ref[...], wpa_ref[...])
    b = _dot(og_ref[...], wpb_ref[...])
    merged = sa_ref[...].astype(F32) * a + sb_ref[...].astype(F32) * b
    g1 = mod_ref[:, 2 * D_MODEL:3 * D_MODEL]
    sh2 = mod_ref[:, 3 * D_MODEL:4 * D_MODEL]
    sc2 = mod_ref[:, 4 * D_MODEL:5 * D_MODEL]
    x1 = x_ref[...] + g1 * _dot(merged.astype(BF16), wo_ref[...])
    x1_out[...] = x1
    h2 = _rms(x1) * n2_ref[...] * (1.0 + sc2) + sh2
    h2_out[...] = h2.astype(BF16)
    hh, hm, hl = _split3(h2)
    rh, rm, rl = rw_ref[0], rw_ref[1], rw_ref[2]
    nt = lambda r, t: lax.dot_general(r, t, _NT, preferred_element_type=F32)
    logits = nt(rl, hh) + nt(rh, hl) + nt(rm, hm) + nt(rm, hh) + nt(rh, hm) + nt(rh, hh)
    gt_out[...] = _route(jax.nn.sigmoid(logits), rb_ref[...])


def _mix(layer, x, oa, og, sa, sb, mod_l, wpa, wpb, wo, norm2, rw3, rb):
    nt = N_TOK // TM
    row = lambda i: (i, 0)
    const = lambda i: (0, 0)
    return pl.pallas_call(
        _mix_kernel,
        grid=(nt,),
        in_specs=[
            pl.BlockSpec((TM, D_MODEL), row),
            pl.BlockSpec((TM, ATTN_WIDTH), row),
            pl.BlockSpec((TM, GLA_VW), row),
            pl.BlockSpec((TM, D_MODEL), row),
            pl.BlockSpec((TM, D_MODEL), row),
            pl.BlockSpec((None, 1, MOD_W), lambda i: (_mod_row(i, TM), 0, 0)),
            pl.BlockSpec((None, ATTN_WIDTH, D_MODEL), lambda i: (layer, 0, 0), pipeline_mode=pl.Buffered(1)),
            pl.BlockSpec((None, GLA_VW, D_MODEL), lambda i: (layer, 0, 0), pipeline_mode=pl.Buffered(1)),
            pl.BlockSpec((None, D_MODEL, D_MODEL), lambda i: (layer, 0, 0), pipeline_mode=pl.Buffered(1)),
            pl.BlockSpec((1, D_MODEL), const),
            pl.BlockSpec((3, N_EXPERTS, D_MODEL), lambda i: (0, 0, 0)),
            pl.BlockSpec((N_EXPERTS, 1), const),
        ],
        out_specs=[pl.BlockSpec((TM, D_MODEL), row), pl.BlockSpec((TM, D_MODEL), row),
                   pl.BlockSpec((N_EXPERTS, TM), lambda i: (0, i))],
        out_shape=[jax.ShapeDtypeStruct((N_TOK, D_MODEL), F32), jax.ShapeDtypeStruct((N_TOK, D_MODEL), BF16),
                   jax.ShapeDtypeStruct((N_EXPERTS, N_TOK), F32)],
        scratch_shapes=[pltpu.VMEM((ATTN_WIDTH, D_MODEL), BF16), pltpu.VMEM((GLA_VW, D_MODEL), BF16),
                        pltpu.VMEM((D_MODEL, D_MODEL), BF16)],
        compiler_params=pltpu.CompilerParams(dimension_semantics=("arbitrary",), vmem_limit_bytes=VMEM_LIMIT),
        name="mix",
    )(x, oa, og, sa, sb, mod_l, wpa, wpb, wo, norm2, rw3, rb)


def _moe_kernel(h_ref, g_ref, x1_ref, mod_ref, wg_ref, wu_ref, wd_ref, out_ref, wg_s, wu_s, wd_s):
    e = pl.program_id(1)
    wg_s[...] = wg_ref[...].astype(BF16)
    wu_s[...] = wu_ref[...].astype(BF16)
    wd_s[...] = wd_ref[...].astype(BF16)

    @pl.when(e == 0)
    def _():
        out_ref[...] = jnp.zeros(out_ref.shape, F32)

    lane = lax.broadcasted_iota(jnp.int32, (MOE_SUB, N_EXPERTS), 1)

    def sub(i, carry):
        rows = pl.ds(pl.multiple_of(i * MOE_SUB, MOE_SUB), MOE_SUB)
        h = h_ref[rows, :]
        gate = jnp.sum(jnp.where(lane == e, g_ref[rows, :], 0.0), axis=-1, keepdims=True)
        up = _dot(h, wu_s[...])
        gt = _dot(h, wg_s[...])
        act = (gt * jax.nn.sigmoid(gt) * up * gate).astype(BF16)
        out_ref[rows, :] += _dot(act, wd_s[...])
        return carry

    lax.fori_loop(0, TM_MOE // MOE_SUB, sub, 0)

    @pl.when(e == N_EXPERTS - 1)
    def _():
        g2 = mod_ref[:, 5 * D_MODEL:6 * D_MODEL]
        out_ref[...] = x1_ref[...] + g2 * out_ref[...]


def _moe(layer, h2, gates, x1, mod_l, w_gate, w_up, w_down):
    nt = N_TOK // TM_MOE
    row = lambda i, e: (i, 0)
    return pl.pallas_call(
        _moe_kernel,
        grid=(nt, N_EXPERTS),
        in_specs=[
            pl.BlockSpec((TM_MOE, D_MODEL), row),
            pl.BlockSpec((TM_MOE, N_EXPERTS), row),
            pl.BlockSpec((TM_MOE, D_MODEL), row),
            pl.BlockSpec((None, 1, MOD_W), lambda i, e: (_mod_row(i, TM_MOE), 0, 0)),
            pl.BlockSpec((None, None, D_MODEL, D_EXPERT), lambda i, e: (layer, e, 0, 0)),
            pl.BlockSpec((None, None, D_MODEL, D_EXPERT), lambda i, e: (layer, e, 0, 0)),
            pl.BlockSpec((None, None, D_EXPERT, D_MODEL), lambda i, e: (layer, e, 0, 0)),
        ],
        out_specs=pl.BlockSpec((TM_MOE, D_MODEL), row),
        out_shape=jax.ShapeDtypeStruct((N_TOK, D_MODEL), F32),
        scratch_shapes=[pltpu.VMEM((D_MODEL, D_EXPERT), BF16), pltpu.VMEM((D_MODEL, D_EXPERT), BF16),
                        pltpu.VMEM((D_EXPERT, D_MODEL), BF16)],
        compiler_params=pltpu.CompilerParams(dimension_semantics=("parallel", "arbitrary"),
                                             vmem_limit_bytes=VMEM_LIMIT),
        name="moe",
    )(h2, gates, x1, mod_l, w_gate, w_up, w_down)


def _rope_tables():
    pos = np.arange(DEC_SEQ)
    rows = (pos // GRID_W).astype(np.float64)
    cols = (pos % GRID_W).astype(np.float64)
    half = HEAD_DIM // 2
    inv_freq = ROPE_THETA ** (-np.arange(0, half, 2, dtype=np.float64) / half)
    ang_r = rows[:, None] * inv_freq[None, :]
    ang_c = cols[:, None] * inv_freq[None, :]
    cos_h = np.concatenate([np.cos(ang_r), np.cos(ang_r), np.cos(ang_c), np.cos(ang_c)], axis=-1)
    sin_h = np.concatenate([-np.sin(ang_r), np.sin(ang_r), -np.sin(ang_c), np.sin(ang_c)], axis=-1)
    cos = np.concatenate([np.ones((TM, HEAD_DIM)), cos_h], axis=0)
    sin = np.concatenate([np.zeros((TM, HEAD_DIM)), sin_h], axis=0)
    return (jnp.asarray(np.tile(cos, (1, N_HEADS)), F32), jnp.asarray(np.tile(sin, (1, N_HEADS)), F32))


def _pack_w2(w2):
    z = jnp.zeros((GLA_RANK, GLA_KW), w2.dtype)
    top = jnp.concatenate([w2[0], z], axis=1)
    bot = jnp.concatenate([z, w2[1]], axis=1)
    pad = jnp.zeros((LR_PAD - 2 * GLA_RANK, 2 * GLA_KW), w2.dtype)
    return jnp.concatenate([top, bot, pad], axis=0).astype(BF16)


def kernel(x_prompt, x_sample, cache_k, cache_v, state_gla, c, c_ctx, norm1, norm2, w_ada, b_ada, w_in,
           q_norm, k_norm, gla_w2, gla_b, gla_norm, w_pa, w_pb, w_out, router_w, router_bias,
           w_gate, w_up, w_down):
    x = jnp.concatenate([x_prompt.reshape(N_CTX, D_MODEL), x_sample.reshape(N_LAT, D_MODEL)], axis=0)
    cond = jnp.concatenate([c_ctx[None, :], c, jnp.zeros((N_COND - 1 - DEC_BATCH, D_MODEL), F32)], axis=0)
    mod = _ada(cond, w_ada, b_ada).reshape(DEPTH, N_COND, 1, MOD_W)
    cos, sin = _rope_tables()
    lane_head = np.arange(ATTN_WIDTH) // HEAD_DIM
    bd = jnp.asarray(lane_head[:, None] == lane_head[None, :], BF16)
    rw3 = jnp.stack(_split3(router_w.T), axis=0)
    rb = router_bias.reshape(N_EXPERTS, 1)
    ck = cache_k.reshape(DEC_BATCH, DEPTH, PAST_LEN, KV_WIDTH)
    cv = cache_v.reshape(DEC_BATCH, DEPTH, PAST_LEN, KV_WIDTH)

    keys, vals, states = [], [], []
    for l in range(DEPTH):
        q, kf, kr, v, gq, gk, gv, gg, la, sa, sb = _inproj(
            l, x, mod[l], norm1[l][None, :], w_in, jnp.tile(q_norm[l], N_HEADS)[None, :],
            jnp.tile(k_norm[l], KV_HEADS)[None, :], cos, sin, bd, _pack_w2(gla_w2[l]),
            gla_b[l].reshape(1, 2 * GLA_KW))
        oa = _attn_ctx(q, kr, v)
        oa = _attn_lat(l, q, kr, v, ck, cv, oa)
        gn = jnp.tile(gla_norm[l], GLA_HEADS)[None, :]
        og, s_fin = _gla_ctx(gq, gk, gv, la, gg, gn)
        og = _gla_lat(l, gq, gk, gv, la, gg, gn, state_gla, og)
        x1, h2, gates_t = _mix(l, x, oa, og, sa, sb, mod[l], w_pa, w_pb, w_out, norm2[l][None, :], rw3, rb)
        x = _moe(l, h2, gates_t.T, x1, mod[l], w_gate, w_up, w_down)
        keys.append(kf[:N_CTX].reshape(BATCH, SEQ, KV_HEADS, HEAD_DIM))
        vals.append(v[:N_CTX].reshape(BATCH, SEQ, KV_HEADS, HEAD_DIM))
        states.append(s_fin)
    y_prompt = x[:N_CTX].reshape(BATCH, SEQ, D_MODEL)
    y_sample = x[N_CTX:].reshape(DEC_BATCH, DEC_SEQ, D_MODEL)
    return (y_prompt, y_sample, jnp.stack(keys, axis=1), jnp.stack(vals, axis=1), jnp.stack(states, axis=1))
```

```python
from functools import partial

import jax
import jax.numpy as jnp
import numpy as np
from jax import lax
from jax.experimental import pallas as pl
from jax.experimental.pallas import tpu as pltpu

F32 = jnp.float32
BF16 = jnp.bfloat16

D_MODEL = 1024
BATCH = 16
SEQ = 256
DEPTH = 2
DEC_BATCH = 4
DEC_SEQ = 1024
PAST_LEN = 512
GRID_W = 64
N_HEADS = 8
KV_HEADS = 2
GROUP = N_HEADS // KV_HEADS
HEAD_DIM = 64
ATTN_WIDTH = N_HEADS * HEAD_DIM
KV_WIDTH = KV_HEADS * HEAD_DIM
ROPE_THETA = 10000.0
GLA_HEADS = 4
GLA_DK = 64
GLA_DV = 128
GLA_KW = GLA_HEADS * GLA_DK
GLA_VW = GLA_HEADS * GLA_DV
GLA_RANK = 16
GLA_GATE_NORM = 16.0
N_EXPERTS = 16
N_GROUPS = 4
EXPERTS_PER_GROUP = N_EXPERTS // N_GROUPS
D_EXPERT = 512
NORM_EPS = 1e-6

N_CTX = BATCH * SEQ
N_LAT = DEC_BATCH * DEC_SEQ
N_TOK = N_CTX + N_LAT
N_COND = 8
MOD_W = 6 * D_MODEL

TM = 256
TS = 256
N_TILE = 2 * N_TOK // TS + N_EXPERTS
N_SLOT = N_TILE * TS
GLA_CHUNK = 64
GLA_LEVELS = (32, 16, 8)
GLA_DIAG = 8
LR_PAD = 128
SEG_MIX = ATTN_WIDTH + 2 * KV_WIDTH + 2 * GLA_KW + 2 * GLA_VW
OFF_GA = SEG_MIX
OFF_GB = OFF_GA + D_MODEL
OFF_LR = OFF_GB + D_MODEL
IN_PACKED = OFF_LR + LR_PAD
IN_WIDTH = SEG_MIX + 2 * GLA_RANK + 2 * D_MODEL
VMEM_LIMIT = 56 * 1024 * 1024

_NT = (((1,), (1,)), ((), ()))
_TN = (((0,), (0,)), ((), ()))


def _mod_row(tile, tm):
    start = tile * tm
    return jnp.where(start < N_CTX, 0, 1 + (start - N_CTX) // DEC_SEQ)


def _split3(x):
    hi = x.astype(BF16)
    r1 = x - hi.astype(F32)
    mid = r1.astype(BF16)
    lo = (r1 - mid.astype(F32)).astype(BF16)
    return hi, mid, lo


def _split2(x):
    hi = x.astype(BF16)
    lo = (x - hi.astype(F32)).astype(BF16)
    return hi, lo


def _dot(a, b):
    return jnp.dot(a, b, preferred_element_type=F32)


def _rms(x):
    return x * lax.rsqrt(jnp.mean(x * x, axis=-1, keepdims=True) + NORM_EPS)


def _ada_kernel(cond_ref, w_ref, b_ref, out_ref):
    c = cond_ref[...]
    s = c * jax.nn.sigmoid(c)
    out_ref[...] = _dot(s.astype(BF16), w_ref[...].astype(BF16)) + b_ref[...]


def _ada(cond, w_ada, b_ada):
    nb = MOD_W // D_MODEL
    return pl.pallas_call(
        _ada_kernel,
        grid=(DEPTH, nb),
        in_specs=[
            pl.BlockSpec((N_COND, D_MODEL), lambda l, j: (0, 0)),
            pl.BlockSpec((None, D_MODEL, D_MODEL), lambda l, j: (l, 0, j)),
            pl.BlockSpec((None, 1, D_MODEL), lambda l, j: (l, 0, j)),
        ],
        out_specs=pl.BlockSpec((None, N_COND, D_MODEL), lambda l, j: (l, 0, j)),
        out_shape=jax.ShapeDtypeStruct((DEPTH, N_COND, MOD_W), F32),
        compiler_params=pltpu.CompilerParams(vmem_limit_bytes=VMEM_LIMIT),
        name="ada",
    )(cond, w_ada, b_ada.reshape(DEPTH, 1, MOD_W))


def _head_rmsnorm(x, bd):
    hi, lo = _split2(x * x)
    ssq = _dot(hi, bd) + _dot(lo, bd)
    return x * lax.rsqrt(ssq * (1.0 / HEAD_DIM) + NORM_EPS)


def _rope(x, cos, sin):
    n = x.shape[-1]
    lane = lax.broadcasted_iota(jnp.int32, x.shape, 1)
    first = (lane & (HEAD_DIM // 4)) == 0
    partner = jnp.where(first, pltpu.roll(x, n - HEAD_DIM // 4, 1), pltpu.roll(x, HEAD_DIM // 4, 1))
    return x * cos + partner * sin


def _pack_w_in(w_ref, wp_ref):
    lr0 = SEG_MIX
    ga0 = lr0 + 2 * GLA_RANK
    rows_per_step = 128

    def body(i, carry):
        rows = pl.ds(pl.multiple_of(i * rows_per_step, rows_per_step), rows_per_step)
        wp_ref[rows, 0:SEG_MIX] = w_ref[rows, 0:SEG_MIX].astype(BF16)
        tail = w_ref[rows, SEG_MIX:IN_WIDTH]
        lr = tail[:, 0:2 * GLA_RANK]
        wp_ref[rows, OFF_LR:IN_PACKED] = jnp.concatenate(
            [lr, jnp.zeros((rows_per_step, LR_PAD - 2 * GLA_RANK), F32)], axis=1).astype(BF16)
        wp_ref[rows, OFF_GA:OFF_LR] = tail[:, ga0 - lr0:ga0 - lr0 + 2 * D_MODEL].astype(BF16)
        return carry

    lax.fori_loop(0, D_MODEL // rows_per_step, body, 0)


def _inproj_kernel(x_ref, mod_ref, n1_ref, wf_ref, qn_ref, kn_ref, cos_ref, sin_ref, bd_ref, w2_ref, gb_ref,
                   q_out, kf_out, kr_out, v_out, gq_out, gk_out, gv_out, gg_out, la_out, sa_out, sb_out,
                   w_ref):
    @pl.when(pl.program_id(0) == 0)
    def _():
        _pack_w_in(wf_ref, w_ref)

    x = x_ref[...]
    sh1 = mod_ref[:, 0:D_MODEL]
    sc1 = mod_ref[:, D_MODEL:2 * D_MODEL]
    h = (_rms(x) * n1_ref[...] * (1.0 + sc1) + sh1).astype(BF16)

    def proj(lo, width):
        return _dot(h, w_ref[:, lo:lo + width])

    cos = cos_ref[...]
    sin = sin_ref[...]
    bd = bd_ref[...]
    off = 0
    q = _head_rmsnorm(proj(off, ATTN_WIDTH), bd) * qn_ref[...]
    q_out[...] = (_rope(q, cos, sin) * HEAD_DIM ** -0.5).astype(BF16)
    off += ATTN_WIDTH
    k = _head_rmsnorm(proj(off, KV_WIDTH), bd[:KV_WIDTH, :KV_WIDTH]) * kn_ref[...]
    kf_out[...] = k
    kr_out[...] = _rope(k, cos[:, :KV_WIDTH], sin[:, :KV_WIDTH]).astype(BF16)
    off += KV_WIDTH
    v_out[...] = proj(off, KV_WIDTH)
    off += KV_WIDTH
    gq_out[...] = (proj(off, GLA_KW) * GLA_DK ** -0.5).astype(BF16)
    off += GLA_KW
    gk_out[...] = proj(off, GLA_KW).astype(BF16)
    off += GLA_KW
    gv_out[...] = proj(off, GLA_VW).astype(BF16)
    off += GLA_VW
    gg = proj(off, GLA_VW)
    gg_out[...] = (gg * jax.nn.sigmoid(gg)).astype(BF16)
    sa_out[...] = jax.nn.sigmoid(proj(OFF_GA, D_MODEL)).astype(BF16)
    sb_out[...] = jax.nn.sigmoid(proj(OFF_GB, D_MODEL)).astype(BF16)
    lr = proj(OFF_LR, LR_PAD)
    z = _dot(lr.astype(BF16), w2_ref[...]) + gb_ref[...]
    log_sig = jnp.minimum(z, 0.0) - jnp.log1p(jnp.exp(-jnp.abs(z)))
    la_out[...] = log_sig * (1.0 / GLA_GATE_NORM)


def _inproj(layer, x, mod_l, norm1, w_in, qn, kn, cos, sin, bd, w2p, gb):
    nt = N_TOK // TM
    n_ctx_t = N_CTX // TM
    t_per_seq = DEC_SEQ // TM
    row = lambda i: (i, 0)
    const = lambda i: (0, 0)
    rope_blk = lambda i: (jnp.where(i < n_ctx_t, 0, 1 + (i - n_ctx_t) % t_per_seq), 0)

    def out(width, dtype):
        return pl.BlockSpec((TM, width), row), jax.ShapeDtypeStruct((N_TOK, width), dtype)

    outs = [out(ATTN_WIDTH, BF16), out(KV_WIDTH, F32), out(KV_WIDTH, BF16), out(KV_WIDTH, F32),
            out(GLA_KW, BF16), out(GLA_KW, BF16), out(GLA_VW, BF16), out(GLA_VW, BF16),
            out(2 * GLA_KW, F32), out(D_MODEL, BF16), out(D_MODEL, BF16)]
    return pl.pallas_call(
        _inproj_kernel,
        grid=(nt,),
        in_specs=[
            pl.BlockSpec((TM, D_MODEL), row),
            pl.BlockSpec((None, 1, MOD_W), lambda i: (_mod_row(i, TM), 0, 0)),
            pl.BlockSpec((1, D_MODEL), const),
            pl.BlockSpec((None, D_MODEL, IN_WIDTH), lambda i: (layer, 0, 0), pipeline_mode=pl.Buffered(1)),
            pl.BlockSpec((1, ATTN_WIDTH), const),
            pl.BlockSpec((1, KV_WIDTH), const),
            pl.BlockSpec((TM, ATTN_WIDTH), rope_blk),
            pl.BlockSpec((TM, ATTN_WIDTH), rope_blk),
            pl.BlockSpec((ATTN_WIDTH, ATTN_WIDTH), const),
            pl.BlockSpec((LR_PAD, 2 * GLA_KW), const),
            pl.BlockSpec((1, 2 * GLA_KW), const),
        ],
        out_specs=[o[0] for o in outs],
        out_shape=[o[1] for o in outs],
        scratch_shapes=[pltpu.VMEM((D_MODEL, IN_PACKED), BF16)],
        compiler_params=pltpu.CompilerParams(dimension_semantics=("arbitrary",), vmem_limit_bytes=VMEM_LIMIT),
        name="inproj",
    )(x, mod_l, norm1, w_in, qn, kn, cos, sin, bd, w2p, gb)


def _attn_kernel(*refs, has_cache):
    if has_cache:
        q_ref, k_ref, v_ref, ck_ref, cv_ref, o_ref = refs
    else:
        q_ref, k_ref, v_ref, o_ref = refs
    tq = q_ref.shape[0]
    for g in range(KV_HEADS):
        ks = slice(g * HEAD_DIM, (g + 1) * HEAD_DIM)
        qs = jnp.concatenate(
            [q_ref[:, (g * GROUP + j) * HEAD_DIM:(g * GROUP + j + 1) * HEAD_DIM] for j in range(GROUP)], axis=0)
        k = k_ref[:, ks]
        v = v_ref[:, ks].astype(BF16)
        s = lax.dot_general(qs, k, _NT, preferred_element_type=F32)
        m = jnp.max(s, axis=-1, keepdims=True)
        if has_cache:
            ck = ck_ref[:, ks].astype(BF16)
            cv = cv_ref[:, ks].astype(BF16)
            s2 = lax.dot_general(qs, ck, _NT, preferred_element_type=F32)
            m = jnp.maximum(m, jnp.max(s2, axis=-1, keepdims=True))
        p = jnp.exp(s - m)
        den = jnp.sum(p, axis=-1, keepdims=True)
        o = _dot(p.astype(BF16), v)
        if has_cache:
            p2 = jnp.exp(s2 - m)
            den = den + jnp.sum(p2, axis=-1, keepdims=True)
            o = o + _dot(p2.astype(BF16), cv)
        o = o / den
        for j in range(GROUP):
            hd = g * GROUP + j
            o_ref[:, hd * HEAD_DIM:(hd + 1) * HEAD_DIM] = o[j * tq:(j + 1) * tq, :].astype(BF16)


def _attn_ctx(q, kr, v):
    return pl.pallas_call(
        partial(_attn_kernel, has_cache=False),
        grid=(BATCH,),
        in_specs=[
            pl.BlockSpec((SEQ, ATTN_WIDTH), lambda i: (i, 0)),
            pl.BlockSpec((SEQ, KV_WIDTH), lambda i: (i, 0)),
            pl.BlockSpec((SEQ, KV_WIDTH), lambda i: (i, 0)),
        ],
        out_specs=pl.BlockSpec((SEQ, ATTN_WIDTH), lambda i: (i, 0)),
        out_shape=jax.ShapeDtypeStruct((N_CTX, ATTN_WIDTH), BF16),
        compiler_params=pltpu.CompilerParams(dimension_semantics=("parallel",), vmem_limit_bytes=VMEM_LIMIT),
        name="attn_ctx",
    )(q, kr, v)


def _attn_lat(layer, q, kr, v, cache_k, cache_v):
    tq = 256
    nq = DEC_SEQ // tq
    q0 = N_CTX // tq
    s0 = N_CTX // DEC_SEQ
    return pl.pallas_call(
        partial(_attn_kernel, has_cache=True),
        grid=(DEC_BATCH, nq),
        in_specs=[
            pl.BlockSpec((tq, ATTN_WIDTH), lambda b, i: (q0 + b * nq + i, 0)),
            pl.BlockSpec((DEC_SEQ, KV_WIDTH), lambda b, i: (s0 + b, 0)),
            pl.BlockSpec((DEC_SEQ, KV_WIDTH), lambda b, i: (s0 + b, 0)),
            pl.BlockSpec((None, None, PAST_LEN, KV_WIDTH), lambda b, i: (b, layer, 0, 0)),
            pl.BlockSpec((None, None, PAST_LEN, KV_WIDTH), lambda b, i: (b, layer, 0, 0)),
        ],
        out_specs=pl.BlockSpec((tq, ATTN_WIDTH), lambda b, i: (b * nq + i, 0)),
        out_shape=jax.ShapeDtypeStruct((N_LAT, ATTN_WIDTH), BF16),
        compiler_params=pltpu.CompilerParams(dimension_semantics=("parallel", "parallel"),
                                             vmem_limit_bytes=VMEM_LIMIT),
        name="attn_lat",
    )(q, kr, v, cache_k, cache_v)


def _bd_rows(x, n_heads, width):
    head = lax.broadcasted_iota(jnp.int32, x.shape, 1) // width
    return jnp.concatenate([jnp.where(head == hh, x, 0.0) for hh in range(n_heads)], axis=0)


def _ref_rows(b, rows, rep):
    return jnp.concatenate([jnp.broadcast_to(b[r:r + 1, :], (rep, b.shape[1])) for r in rows], axis=0)


def _gla_scores(q, k, b, reverse):
    c = GLA_CHUNK
    t_idx = lax.broadcasted_iota(jnp.int32, (c, GLA_KW), 0)
    a_t = lax.broadcasted_iota(jnp.int32, (c, GLA_HEADS * c), 0)
    a_s = lax.broadcasted_iota(jnp.int32, (c, GLA_HEADS * c), 1) & (c - 1)
    total = jnp.zeros((c, GLA_HEADS * c), F32)

    def scores(qt, kt):
        return lax.dot_general(qt.astype(BF16), _bd_rows(kt, GLA_HEADS, GLA_DK).astype(BF16), _NT,
                               preferred_element_type=F32)

    for m in GLA_LEVELS:
        nblk = c // (2 * m)
        ref_rows = [2 * m * j + (m if reverse else m - 1) for j in range(nblk)]
        ref = _ref_rows(b, ref_rows, 2 * m)
        q_side = ((t_idx & m) == 0) if reverse else ((t_idx & m) != 0)
        qt = jnp.where(q_side, q * jnp.exp(jnp.minimum(b - ref, 0.0)), 0.0)
        kt = jnp.where(q_side, 0.0, k * jnp.exp(jnp.minimum(ref - b, 0.0)))
        shift = int(np.log2(2 * m))
        same = (a_t >> shift) == (a_s >> shift)
        total = total + jnp.where(same, scores(qt, kt), 0.0)
    d = GLA_DIAG
    ref = _ref_rows(b, [d * j + d // 2 for j in range(c // d)], d)
    x = b - ref
    shift = int(np.log2(d))
    same = (a_t >> shift) == (a_s >> shift)
    tri = (a_s >= a_t) if reverse else (a_s <= a_t)
    total = total + jnp.where(same, jnp.where(tri, scores(q * jnp.exp(x), k * jnp.exp(-x)), 0.0), 0.0)
    return total


def _cumsum_rows(tri, la):
    hi, mid, lo = _split3(la)
    return _dot(tri, hi) + _dot(tri, mid) + _dot(tri, lo)


def _gla_kernel(*refs, seq_len, has_state):
    if has_state:
        gq_ref, gk_ref, gv_ref, la_ref, gg_ref, gn_ref, s0_ref, o_ref, acc_ref, st_ref = refs
    else:
        gq_ref, gk_ref, gv_ref, la_ref, gg_ref, gn_ref, o_ref, sfin_ref, acc_ref, st_ref = refs
    c = GLA_CHUNK
    nc = seq_len // c
    r_i = lax.broadcasted_iota(jnp.int32, (c, c), 0)
    c_i = lax.broadcasted_iota(jnp.int32, (c, c), 1)
    tril = jnp.where(c_i <= r_i, 1.0, 0.0).astype(BF16)
    triu = jnp.where(c_i >= r_i, 1.0, 0.0).astype(BF16)
    st_head_r = lax.broadcasted_iota(jnp.int32, (GLA_VW, GLA_KW), 0) // GLA_DV
    st_head_c = lax.broadcasted_iota(jnp.int32, (GLA_VW, GLA_KW), 1) // GLA_DK
    st_mask = st_head_r == st_head_c

    for d in range(2):
        if has_state:
            st_ref[d] = jnp.zeros((GLA_VW, GLA_KW), F32)
            for hh in range(GLA_HEADS):
                st_ref[d, hh * GLA_DV:(hh + 1) * GLA_DV, hh * GLA_DK:(hh + 1) * GLA_DK] = s0_ref[d, hh].T
        else:
            st_ref[d] = jnp.zeros((GLA_VW, GLA_KW), F32)

    def load(n):
        rows = pl.ds(pl.multiple_of(n * c, c), c)
        return (rows, gq_ref[rows, :].astype(F32), gk_ref[rows, :].astype(F32), gv_ref[rows, :].astype(F32))

    def inter(d, q, k, v, b, b_end):
        st = st_ref[d]
        q_in = (q * jnp.exp(b)).astype(BF16)
        o = lax.dot_general(q_in, st.astype(BF16), _NT, preferred_element_type=F32)
        k_st = (k * jnp.exp(b_end - b)).astype(BF16)
        upd = lax.dot_general(v.astype(BF16), k_st, _TN, preferred_element_type=F32)
        st_ref[d] = st * jnp.exp(b_end) + jnp.where(st_mask, upd, 0.0)
        return o

    def fwd_body(n, carry):
        rows, q, k, v = load(n)
        b_f = _cumsum_rows(tril, la_ref[rows, 0:GLA_KW])
        b_b = _cumsum_rows(triu, la_ref[rows, GLA_KW:2 * GLA_KW])
        a = _gla_scores(q, k, b_f, False) + _gla_scores(q, k, b_b, True)
        o = _dot(a.astype(BF16), _bd_rows(v, GLA_HEADS, GLA_DV).astype(BF16))
        o = o + inter(0, q, k, v, b_f, b_f[c - 1:c, :])
        acc_ref[rows, :] = o
        return carry

    lax.fori_loop(0, nc, fwd_body, 0)

    def bwd_body(i, carry):
        n = nc - 1 - i
        rows, q, k, v = load(n)
        b_b = _cumsum_rows(triu, la_ref[rows, GLA_KW:2 * GLA_KW])
        acc_ref[rows, :] = acc_ref[rows, :] + inter(1, q, k, v, b_b, b_b[0:1, :])
        return carry

    lax.fori_loop(0, nc, bwd_body, 0)

    o = acc_ref[...]
    gn = gn_ref[...]
    for hh in range(GLA_HEADS):
        sl = slice(hh * GLA_DV, (hh + 1) * GLA_DV)
        o_ref[:, sl] = (_rms(o[:, sl]) * gn[:, sl] * gg_ref[:, sl].astype(F32)).astype(BF16)
    if not has_state:
        for d in range(2):
            for hh in range(GLA_HEADS):
                sfin_ref[d, hh] = st_ref[d, hh * GLA_DV:(hh + 1) * GLA_DV, hh * GLA_DK:(hh + 1) * GLA_DK].T


def _gla_scratch(seq_len):
    return [pltpu.VMEM((seq_len, GLA_VW), F32), pltpu.VMEM((2, GLA_VW, GLA_KW), F32)]


def _gla_ctx(gq, gk, gv, la, gg, gn):
    seq = lambda w: pl.BlockSpec((SEQ, w), lambda i: (i, 0))
    return pl.pallas_call(
        partial(_gla_kernel, seq_len=SEQ, has_state=False),
        grid=(BATCH,),
        in_specs=[seq(GLA_KW), seq(GLA_KW), seq(GLA_VW), seq(2 * GLA_KW), seq(GLA_VW),
                  pl.BlockSpec((1, GLA_VW), lambda i: (0, 0))],
        out_specs=[seq(GLA_VW),
                   pl.BlockSpec((None, 2, GLA_HEADS, GLA_DK, GLA_DV), lambda i: (i, 0, 0, 0, 0))],
        out_shape=[jax.ShapeDtypeStruct((N_CTX, GLA_VW), BF16),
                   jax.ShapeDtypeStruct((BATCH, 2, GLA_HEADS, GLA_DK, GLA_DV), F32)],
        scratch_shapes=_gla_scratch(SEQ),
        compiler_params=pltpu.CompilerParams(dimension_semantics=("parallel",), vmem_limit_bytes=VMEM_LIMIT),
        name="gla_ctx",
    )(gq, gk, gv, la, gg, gn)


def _gla_lat(layer, gq, gk, gv, la, gg, gn, s0):
    s0_blk = N_CTX // DEC_SEQ
    seq = lambda w: pl.BlockSpec((DEC_SEQ, w), lambda b: (s0_blk + b, 0))
    return pl.pallas_call(
        partial(_gla_kernel, seq_len=DEC_SEQ, has_state=True),
        grid=(DEC_BATCH,),
        in_specs=[seq(GLA_KW), seq(GLA_KW), seq(GLA_VW), seq(2 * GLA_KW), seq(GLA_VW),
                  pl.BlockSpec((1, GLA_VW), lambda b: (0, 0)),
                  pl.BlockSpec((None, None, 2, GLA_HEADS, GLA_DK, GLA_DV), lambda b: (b, layer, 0, 0, 0, 0))],
        out_specs=pl.BlockSpec((DEC_SEQ, GLA_VW), lambda b: (b, 0)),
        out_shape=jax.ShapeDtypeStruct((N_LAT, GLA_VW), BF16),
        scratch_shapes=_gla_scratch(DEC_SEQ),
        compiler_params=pltpu.CompilerParams(dimension_semantics=("parallel",), vmem_limit_bytes=VMEM_LIMIT),
        name="gla_lat",
    )(gq, gk, gv, la, gg, gn, s0)


def _route(scores, bias):
    biased = scores + bias
    v = [biased[e:e + 1, :] for e in range(N_EXPERTS)]
    s = [scores[e:e + 1, :] for e in range(N_EXPERTS)]
    gscore = []
    for g in range(N_GROUPS):
        a, b, c, d = v[4 * g:4 * g + 4]
        hi1, lo1, hi2, lo2 = jnp.maximum(a, b), jnp.minimum(a, b), jnp.maximum(c, d), jnp.minimum(c, d)
        gscore.append(jnp.maximum(hi1, hi2) + jnp.maximum(jnp.minimum(hi1, hi2), jnp.maximum(lo1, lo2)))
    one = lambda cond: jnp.where(cond, 1.0, 0.0)
    picked, chosen = [], []
    for g in range(N_GROUPS):
        best = None
        for g2 in range(N_GROUPS):
            if g2 == g:
                continue
            ok = one((gscore[g] > gscore[g2]) if g2 < g else (gscore[g] >= gscore[g2]))
            best = ok if best is None else best * ok
        for i in range(EXPERTS_PER_GROUP):
            e = 4 * g + i
            beaten = None
            for j in range(EXPERTS_PER_GROUP):
                if j == i:
                    continue
                e2 = 4 * g + j
                lost = one((v[e2] > v[e]) if j > i else (v[e2] >= v[e]))
                beaten = lost if beaten is None else beaten + lost
            chosen.append(best * one(beaten < 1.5))
            picked.append(chosen[-1] * s[e])
    den = picked[0]
    for e in range(1, N_EXPERTS):
        den = den + picked[e]
    return jnp.concatenate([p / den for p in picked] + chosen, axis=0)


def _mix_kernel(x_ref, oac_ref, oal_ref, ogc_ref, ogl_ref, sa_ref, sb_ref, mod_ref, wpa_f, wpb_f, wo_f, n2_ref,
                rw_ref, rb_ref, x1_out, h2_out, gt_out, wpa_ref, wpb_ref, wo_ref):
    @pl.when(pl.program_id(0) == 0)
    def _():
        wpa_ref[...] = wpa_f[...].astype(BF16)
        wpb_ref[...] = wpb_f[...].astype(BF16)
        wo_ref[...] = wo_f[...].astype(BF16)

    is_ctx = pl.program_id(0) < N_CTX // TM
    oa = jnp.where(is_ctx, oac_ref[...], oal_ref[...])
    og = jnp.where(is_ctx, ogc_ref[...], ogl_ref[...])
    a = _dot(oa, wpa_ref[...])
    b = _dot(og, wpb_ref[...])
    merged = sa_ref[...].astype(F32) * a + sb_ref[...].astype(F32) * b
    g1 = mod_ref[:, 2 * D_MODEL:3 * D_MODEL]
    sh2 = mod_ref[:, 3 * D_MODEL:4 * D_MODEL]
    sc2 = mod_ref[:, 4 * D_MODEL:5 * D_MODEL]
    x1 = x_ref[...] + g1 * _dot(merged.astype(BF16), wo_ref[...])
    x1_out[...] = x1
    h2 = _rms(x1) * n2_ref[...] * (1.0 + sc2) + sh2
    h2_out[...] = h2
    hh, hm, hl = _split3(h2)
    rh, rm, rl = rw_ref[0], rw_ref[1], rw_ref[2]
    nt = lambda r, t: lax.dot_general(r, t, _NT, preferred_element_type=F32)
    logits = nt(rl, hh) + nt(rh, hl) + nt(rm, hm) + nt(rm, hh) + nt(rh, hm) + nt(rh, hh)
    gt_out[...] = _route(jax.nn.sigmoid(logits), rb_ref[...])


def _mix(layer, x, oa_c, oa_l, og_c, og_l, sa, sb, mod_l, wpa, wpb, wo, norm2, rw3, rb):
    nt = N_TOK // TM
    n_ctx_t = N_CTX // TM
    row = lambda i: (i, 0)
    const = lambda i: (0, 0)
    ctx_row = lambda i: (jnp.minimum(i, n_ctx_t - 1), 0)
    lat_row = lambda i: (jnp.maximum(i - n_ctx_t, 0), 0)
    return pl.pallas_call(
        _mix_kernel,
        grid=(nt,),
        in_specs=[
            pl.BlockSpec((TM, D_MODEL), row),
            pl.BlockSpec((TM, ATTN_WIDTH), ctx_row),
            pl.BlockSpec((TM, ATTN_WIDTH), lat_row),
            pl.BlockSpec((TM, GLA_VW), ctx_row),
            pl.BlockSpec((TM, GLA_VW), lat_row),
            pl.BlockSpec((TM, D_MODEL), row),
            pl.BlockSpec((TM, D_MODEL), row),
            pl.BlockSpec((None, 1, MOD_W), lambda i: (_mod_row(i, TM), 0, 0)),
            pl.BlockSpec((None, ATTN_WIDTH, D_MODEL), lambda i: (layer, 0, 0), pipeline_mode=pl.Buffered(1)),
            pl.BlockSpec((None, GLA_VW, D_MODEL), lambda i: (layer, 0, 0), pipeline_mode=pl.Buffered(1)),
            pl.BlockSpec((None, D_MODEL, D_MODEL), lambda i: (layer, 0, 0), pipeline_mode=pl.Buffered(1)),
            pl.BlockSpec((1, D_MODEL), const),
            pl.BlockSpec((3, N_EXPERTS, D_MODEL), lambda i: (0, 0, 0)),
            pl.BlockSpec((N_EXPERTS, 1), const),
        ],
        out_specs=[pl.BlockSpec((TM, D_MODEL), row), pl.BlockSpec((TM, D_MODEL), row),
                   pl.BlockSpec((2 * N_EXPERTS, TM), lambda i: (0, i))],
        out_shape=[jax.ShapeDtypeStruct((N_TOK, D_MODEL), F32), jax.ShapeDtypeStruct((N_TOK, D_MODEL), F32),
                   jax.ShapeDtypeStruct((2 * N_EXPERTS, N_TOK), F32)],
        scratch_shapes=[pltpu.VMEM((ATTN_WIDTH, D_MODEL), BF16), pltpu.VMEM((GLA_VW, D_MODEL), BF16),
                        pltpu.VMEM((D_MODEL, D_MODEL), BF16)],
        compiler_params=pltpu.CompilerParams(dimension_semantics=("arbitrary",), vmem_limit_bytes=VMEM_LIMIT),
        name="mix",
    )(x, oa_c, oa_l, og_c, og_l, sa, sb, mod_l, wpa, wpb, wo, norm2, rw3, rb)


def _route_plan(gsel):
    gates = gsel[:N_EXPERTS]
    sel = gsel[N_EXPERTS:] > 0.5
    seli = sel.astype(jnp.int32)
    incl = jnp.cumsum(seli, axis=1)
    count = incl[:, -1]
    ntile = (count + TS - 1) // TS
    tile_end = jnp.cumsum(ntile)
    tile_start = tile_end - ntile
    total = tile_end[-1]
    slot = tile_start[:, None] * TS + incl - seli
    pos_a = jnp.min(jnp.where(sel, slot, N_SLOT), axis=0)
    pos_b = jnp.max(jnp.where(sel, slot, -1), axis=0)
    w_a = jnp.sum(jnp.where(sel & (slot == pos_a[None, :]), gates, 0.0), axis=0)
    w_b = jnp.sum(jnp.where(sel & (slot == pos_b[None, :]), gates, 0.0), axis=0)
    tile_id = jnp.minimum(jnp.arange(N_TILE, dtype=jnp.int32), total - 1)
    tile_expert = jnp.sum((tile_id[:, None] >= tile_end[None, :]).astype(jnp.int32), axis=1)
    nt = N_TOK // TM
    pos = jnp.concatenate([pos_a.reshape(nt, 1, TM), pos_b.reshape(nt, 1, TM)], axis=2).astype(jnp.int32)
    plan = jnp.concatenate([tile_start, ntile, total[None]]).astype(jnp.int32)
    return pos, jnp.stack([w_a, w_b], axis=1), tile_expert.astype(jnp.int32), plan


def _row_copy(src, src_row, dst, dst_row, sem):
    return pltpu.make_async_copy(src.at[pl.ds(src_row, 1), :], dst.at[pl.ds(dst_row, 1), :], sem)


def _dispatch_kernel(plan_ref, pos_ref, h_ref, xs_ref, zero_ref, sem_z, sem):
    total = plan_ref[2 * N_EXPERTS]

    def zero_copy(tile):
        return pltpu.make_async_copy(zero_ref, xs_ref.at[pl.ds(pl.multiple_of(tile * TS, TS), TS), :], sem_z)

    def last_tile(e):
        return plan_ref[e] + plan_ref[N_EXPERTS + e] - 1

    @pl.when(pl.program_id(0) == 0)
    def _():
        zero_ref[...] = jnp.zeros(zero_ref.shape, F32)

        def each(fn):
            def expert(e, c):
                @pl.when(plan_ref[N_EXPERTS + e] > 0)
                def _():
                    fn(zero_copy(last_tile(e)))
                return c

            def unused(t, c):
                @pl.when(t >= total)
                def _():
                    fn(zero_copy(t))
                return c

            lax.fori_loop(0, N_EXPERTS, expert, 0)
            lax.fori_loop(0, N_TILE, unused, 0)

        each(lambda cp: cp.start())
        each(lambda cp: cp.wait())

    def start(r, c):
        _row_copy(h_ref, r, xs_ref, pos_ref[0, r], sem).start()
        _row_copy(h_ref, r, xs_ref, pos_ref[0, TM + r], sem).start()
        return c

    def wait(r, c):
        _row_copy(h_ref, 0, xs_ref, 0, sem).wait()
        return c

    lax.fori_loop(0, TM, start, 0)
    lax.fori_loop(0, 2 * TM, wait, 0)


def _dispatch(plan, pos, h2):
    return pl.pallas_call(
        _dispatch_kernel,
        grid_spec=pltpu.PrefetchScalarGridSpec(
            num_scalar_prefetch=1,
            grid=(N_TOK // TM,),
            in_specs=[
                pl.BlockSpec((None, 1, 2 * TM), lambda i, plan: (i, 0, 0), memory_space=pltpu.SMEM),
                pl.BlockSpec((TM, D_MODEL), lambda i, plan: (i, 0)),
            ],
            out_specs=pl.BlockSpec(memory_space=pl.ANY),
            scratch_shapes=[pltpu.VMEM((TS, D_MODEL), F32), pltpu.SemaphoreType.DMA, pltpu.SemaphoreType.DMA],
        ),
        out_shape=jax.ShapeDtypeStruct((N_SLOT, D_MODEL), F32),
        compiler_params=pltpu.CompilerParams(dimension_semantics=("arbitrary",), vmem_limit_bytes=VMEM_LIMIT),
        name="dispatch",
    )(plan, pos, h2)


def _moe_kernel(te_ref, plan_ref, xs_ref, wg_ref, wu_ref, wd_ref, ys_ref, wg_s, wu_s, wd_s):
    i = pl.program_id(0)
    total = plan_ref[2 * N_EXPERTS]

    @pl.when(i < total)
    def _():
        @pl.when((i == 0) | (te_ref[i] != te_ref[jnp.maximum(i - 1, 0)]))
        def _():
            wg_s[...] = wg_ref[...].astype(BF16)
            wu_s[...] = wu_ref[...].astype(BF16)
            wd_s[...] = wd_ref[...].astype(BF16)

        x = xs_ref[...].astype(BF16)
        up = _dot(x, wu_s[...])
        gt = _dot(x, wg_s[...])
        act = (gt * jax.nn.sigmoid(gt) * up).astype(BF16)
        ys_ref[...] = _dot(act, wd_s[...])

    @pl.when(i >= total)
    def _():
        ys_ref[...] = jnp.zeros(ys_ref.shape, F32)


def _moe(layer, tile_expert, plan, xs, w_gate, w_up, w_down):
    last = lambda i, plan: jnp.minimum(i, plan[2 * N_EXPERTS] - 1)
    w_spec = lambda a, b: pl.BlockSpec((None, None, a, b), lambda i, te, plan: (layer, te[i], 0, 0))
    return pl.pallas_call(
        _moe_kernel,
        grid_spec=pltpu.PrefetchScalarGridSpec(
            num_scalar_prefetch=2,
            grid=(N_TILE,),
            in_specs=[
                pl.BlockSpec((TS, D_MODEL), lambda i, te, plan: (last(i, plan), 0)),
                w_spec(D_MODEL, D_EXPERT), w_spec(D_MODEL, D_EXPERT), w_spec(D_EXPERT, D_MODEL),
            ],
            out_specs=pl.BlockSpec((TS, D_MODEL), lambda i, te, plan: (i, 0)),
            scratch_shapes=[pltpu.VMEM((D_MODEL, D_EXPERT), BF16), pltpu.VMEM((D_MODEL, D_EXPERT), BF16),
                            pltpu.VMEM((D_EXPERT, D_MODEL), BF16)],
        ),
        out_shape=jax.ShapeDtypeStruct((N_SLOT, D_MODEL), F32),
        compiler_params=pltpu.CompilerParams(dimension_semantics=("arbitrary",), vmem_limit_bytes=VMEM_LIMIT),
        name="moe",
    )(tile_expert, plan, xs, w_gate, w_up, w_down)


def _combine_kernel(pos_ref, w_ref, x1_ref, mod_ref, ys_ref, out_ref, ya_ref, yb_ref, sem):
    def start(r, c):
        _row_copy(ys_ref, pos_ref[0, r], ya_ref, r, sem).start()
        _row_copy(ys_ref, pos_ref[0, TM + r], yb_ref, r, sem).start()
        return c

    def wait(r, c):
        _row_copy(ys_ref, 0, ya_ref, 0, sem).wait()
        return c

    lax.fori_loop(0, TM, start, 0)
    lax.fori_loop(0, 2 * TM, wait, 0)
    g2 = mod_ref[:, 5 * D_MODEL:6 * D_MODEL]
    w = w_ref[...]
    out_ref[...] = x1_ref[...] + g2 * (w[:, 0:1] * ya_ref[...] + w[:, 1:2] * yb_ref[...])


def _combine(pos, w_ab, x1, mod_l, ys):
    return pl.pallas_call(
        _combine_kernel,
        grid=(N_TOK // TM,),
        in_specs=[
            pl.BlockSpec((None, 1, 2 * TM), lambda i: (i, 0, 0), memory_space=pltpu.SMEM),
            pl.BlockSpec((TM, 2), lambda i: (i, 0)),
            pl.BlockSpec((TM, D_MODEL), lambda i: (i, 0)),
            pl.BlockSpec((None, 1, MOD_W), lambda i: (_mod_row(i, TM), 0, 0)),
            pl.BlockSpec(memory_space=pl.ANY),
        ],
        out_specs=pl.BlockSpec((TM, D_MODEL), lambda i: (i, 0)),
        out_shape=jax.ShapeDtypeStruct((N_TOK, D_MODEL), F32),
        scratch_shapes=[pltpu.VMEM((TM, D_MODEL), F32), pltpu.VMEM((TM, D_MODEL), F32), pltpu.SemaphoreType.DMA],
        compiler_params=pltpu.CompilerParams(dimension_semantics=("arbitrary",), vmem_limit_bytes=VMEM_LIMIT),
        name="combine",
    )(pos, w_ab, x1, mod_l, ys)


def _rope_tables():
    pos = np.arange(DEC_SEQ)
    rows = (pos // GRID_W).astype(np.float64)
    cols = (pos % GRID_W).astype(np.float64)
    half = HEAD_DIM // 2
    inv_freq = ROPE_THETA ** (-np.arange(0, half, 2, dtype=np.float64) / half)
    ang_r = rows[:, None] * inv_freq[None, :]
    ang_c = cols[:, None] * inv_freq[None, :]
    cos_h = np.concatenate([np.cos(ang_r), np.cos(ang_r), np.cos(ang_c), np.cos(ang_c)], axis=-1)
    sin_h = np.concatenate([-np.sin(ang_r), np.sin(ang_r), -np.sin(ang_c), np.sin(ang_c)], axis=-1)
    cos = np.concatenate([np.ones((TM, HEAD_DIM)), cos_h], axis=0)
    sin = np.concatenate([np.zeros((TM, HEAD_DIM)), sin_h], axis=0)
    return (jnp.asarray(np.tile(cos, (1, N_HEADS)), F32), jnp.asarray(np.tile(sin, (1, N_HEADS)), F32))


def _pack_w2(w2):
    z = jnp.zeros((GLA_RANK, GLA_KW), w2.dtype)
    top = jnp.concatenate([w2[0], z], axis=1)
    bot = jnp.concatenate([z, w2[1]], axis=1)
    pad = jnp.zeros((LR_PAD - 2 * GLA_RANK, 2 * GLA_KW), w2.dtype)
    return jnp.concatenate([top, bot, pad], axis=0).astype(BF16)


def kernel(x_prompt, x_sample, cache_k, cache_v, state_gla, c, c_ctx, norm1, norm2, w_ada, b_ada, w_in,
           q_norm, k_norm, gla_w2, gla_b, gla_norm, w_pa, w_pb, w_out, router_w, router_bias,
           w_gate, w_up, w_down):
    x = jnp.concatenate([x_prompt.reshape(N_CTX, D_MODEL), x_sample.reshape(N_LAT, D_MODEL)], axis=0)
    cond = jnp.concatenate([c_ctx[None, :], c, jnp.zeros((N_COND - 1 - DEC_BATCH, D_MODEL), F32)], axis=0)
    mod = _ada(cond, w_ada, b_ada).reshape(DEPTH, N_COND, 1, MOD_W)
    cos, sin = _rope_tables()
    lane_head = np.arange(ATTN_WIDTH) // HEAD_DIM
    bd = jnp.asarray(lane_head[:, None] == lane_head[None, :], BF16)
    rw3 = jnp.stack(_split3(router_w.T), axis=0)
    rb = router_bias.reshape(N_EXPERTS, 1)
    ck = cache_k.reshape(DEC_BATCH, DEPTH, PAST_LEN, KV_WIDTH)
    cv = cache_v.reshape(DEC_BATCH, DEPTH, PAST_LEN, KV_WIDTH)

    keys, vals, states = [], [], []
    for l in range(DEPTH):
        q, kf, kr, v, gq, gk, gv, gg, la, sa, sb = _inproj(
            l, x, mod[l], norm1[l][None, :], w_in, jnp.tile(q_norm[l], N_HEADS)[None, :],
            jnp.tile(k_norm[l], KV_HEADS)[None, :], cos, sin, bd, _pack_w2(gla_w2[l]),
            gla_b[l].reshape(1, 2 * GLA_KW))
        oa_c = _attn_ctx(q, kr, v)
        oa_l = _attn_lat(l, q, kr, v, ck, cv)
        gn = jnp.tile(gla_norm[l], GLA_HEADS)[None, :]
        og_c, s_fin = _gla_ctx(gq, gk, gv, la, gg, gn)
        og_l = _gla_lat(l, gq, gk, gv, la, gg, gn, state_gla)
        x1, h2, gsel = _mix(l, x, oa_c, oa_l, og_c, og_l, sa, sb, mod[l], w_pa, w_pb, w_out,
                            norm2[l][None, :], rw3, rb)
        pos, w_ab, tile_expert, plan = _route_plan(gsel)
        xs = _dispatch(plan, pos, h2)
        ys = _moe(l, tile_expert, plan, xs, w_gate, w_up, w_down)
        x = _combine(pos, w_ab, x1, mod[l], ys)
        keys.append(kf[:N_CTX].reshape(BATCH, SEQ, KV_HEADS, HEAD_DIM))
        vals.append(v[:N_CTX].reshape(BATCH, SEQ, KV_HEADS, HEAD_DIM))
        states.append(s_fin)
    y_prompt = x[:N_CTX].reshape(BATCH, SEQ, D_MODEL)
    y_sample = x[N_CTX:].reshape(DEC_BATCH, DEC_SEQ, D_MODEL)
    return (y_prompt, y_sample, jnp.stack(keys, axis=1), jnp.stack(vals, axis=1), jnp.stack(states, axis=1))
```

```python
from functools import partial

import jax
import jax.numpy as jnp
import numpy as np
from jax import lax
from jax.experimental import pallas as pl
from jax.experimental.pallas import tpu as pltpu

F32 = jnp.float32
BF16 = jnp.bfloat16

D_MODEL = 1024
BATCH = 16
SEQ = 256
DEPTH = 2
DEC_BATCH = 4
DEC_SEQ = 1024
PAST_LEN = 512
GRID_W = 64
N_HEADS = 8
KV_HEADS = 2
GROUP = N_HEADS // KV_HEADS
HEAD_DIM = 64
ATTN_WIDTH = N_HEADS * HEAD_DIM
KV_WIDTH = KV_HEADS * HEAD_DIM
ROPE_THETA = 10000.0
GLA_HEADS = 4
GLA_DK = 64
GLA_DV = 128
GLA_KW = GLA_HEADS * GLA_DK
GLA_VW = GLA_HEADS * GLA_DV
GLA_RANK = 16
GLA_GATE_NORM = 16.0
N_EXPERTS = 16
N_GROUPS = 4
EXPERTS_PER_GROUP = N_EXPERTS // N_GROUPS
D_EXPERT = 512
NORM_EPS = 1e-6

N_CTX = BATCH * SEQ
N_LAT = DEC_BATCH * DEC_SEQ
N_TOK = N_CTX + N_LAT
N_COND = 8
MOD_W = 6 * D_MODEL

TM = 256
TS = 256
N_TILE = 2 * N_TOK // TS + N_EXPERTS
N_SLOT = N_TILE * TS
ROW_TILE = 8
LANES = D_MODEL // ROW_TILE
GLA_CHUNK = 64
GLA_LEVELS = (32, 16, 8)
GLA_DIAG = 8
LR_PAD = 128
SEG_MIX = ATTN_WIDTH + 2 * KV_WIDTH + 2 * GLA_KW + 2 * GLA_VW
OFF_GA = SEG_MIX
OFF_GB = OFF_GA + D_MODEL
OFF_LR = OFF_GB + D_MODEL
IN_PACKED = OFF_LR + LR_PAD
IN_WIDTH = SEG_MIX + 2 * GLA_RANK + 2 * D_MODEL
VMEM_LIMIT = 56 * 1024 * 1024

_NT = (((1,), (1,)), ((), ()))
_TN = (((0,), (0,)), ((), ()))


def _mod_row(tile, tm):
    start = tile * tm
    return jnp.where(start < N_CTX, 0, 1 + (start - N_CTX) // DEC_SEQ)


def _split3(x):
    hi = x.astype(BF16)
    r1 = x - hi.astype(F32)
    mid = r1.astype(BF16)
    lo = (r1 - mid.astype(F32)).astype(BF16)
    return hi, mid, lo


def _split2(x):
    hi = x.astype(BF16)
    lo = (x - hi.astype(F32)).astype(BF16)
    return hi, lo


def _dot(a, b):
    return jnp.dot(a, b, preferred_element_type=F32)


def _rms(x):
    return x * lax.rsqrt(jnp.mean(x * x, axis=-1, keepdims=True) + NORM_EPS)


def _ada_kernel(cond_ref, w_ref, b_ref, out_ref):
    c = cond_ref[...]
    s = c * jax.nn.sigmoid(c)
    out_ref[...] = _dot(s.astype(BF16), w_ref[...].astype(BF16)) + b_ref[...]


def _ada(cond, w_ada, b_ada):
    nb = MOD_W // D_MODEL
    return pl.pallas_call(
        _ada_kernel,
        grid=(DEPTH, nb),
        in_specs=[
            pl.BlockSpec((N_COND, D_MODEL), lambda l, j: (0, 0)),
            pl.BlockSpec((None, D_MODEL, D_MODEL), lambda l, j: (l, 0, j)),
            pl.BlockSpec((None, 1, D_MODEL), lambda l, j: (l, 0, j)),
        ],
        out_specs=pl.BlockSpec((None, N_COND, D_MODEL), lambda l, j: (l, 0, j)),
        out_shape=jax.ShapeDtypeStruct((DEPTH, N_COND, MOD_W), F32),
        compiler_params=pltpu.CompilerParams(vmem_limit_bytes=VMEM_LIMIT),
        name="ada",
    )(cond, w_ada, b_ada.reshape(DEPTH, 1, MOD_W))


def _head_rmsnorm(x, bd):
    hi, lo = _split2(x * x)
    ssq = _dot(hi, bd) + _dot(lo, bd)
    return x * lax.rsqrt(ssq * (1.0 / HEAD_DIM) + NORM_EPS)


def _rope(x, cos, sin):
    n = x.shape[-1]
    lane = lax.broadcasted_iota(jnp.int32, x.shape, 1)
    first = (lane & (HEAD_DIM // 4)) == 0
    partner = jnp.where(first, pltpu.roll(x, n - HEAD_DIM // 4, 1), pltpu.roll(x, HEAD_DIM // 4, 1))
    return x * cos + partner * sin


def _pack_w_in(w_ref, wp_ref):
    lr0 = SEG_MIX
    ga0 = lr0 + 2 * GLA_RANK
    rows_per_step = 128

    def body(i, carry):
        rows = pl.ds(pl.multiple_of(i * rows_per_step, rows_per_step), rows_per_step)
        wp_ref[rows, 0:SEG_MIX] = w_ref[rows, 0:SEG_MIX].astype(BF16)
        tail = w_ref[rows, SEG_MIX:IN_WIDTH]
        lr = tail[:, 0:2 * GLA_RANK]
        wp_ref[rows, OFF_LR:IN_PACKED] = jnp.concatenate(
            [lr, jnp.zeros((rows_per_step, LR_PAD - 2 * GLA_RANK), F32)], axis=1).astype(BF16)
        wp_ref[rows, OFF_GA:OFF_LR] = tail[:, ga0 - lr0:ga0 - lr0 + 2 * D_MODEL].astype(BF16)
        return carry

    lax.fori_loop(0, D_MODEL // rows_per_step, body, 0)


def _inproj_kernel(x_ref, mod_ref, n1_ref, wf_ref, qn_ref, kn_ref, cos_ref, sin_ref, bd_ref, w2_ref, gb_ref,
                   q_out, kf_out, kr_out, v_out, gq_out, gk_out, gv_out, gg_out, la_out, sa_out, sb_out,
                   w_ref):
    @pl.when(pl.program_id(0) == 0)
    def _():
        _pack_w_in(wf_ref, w_ref)

    x = x_ref[...]
    sh1 = mod_ref[:, 0:D_MODEL]
    sc1 = mod_ref[:, D_MODEL:2 * D_MODEL]
    h = (_rms(x) * n1_ref[...] * (1.0 + sc1) + sh1).astype(BF16)

    def proj(lo, width):
        return _dot(h, w_ref[:, lo:lo + width])

    cos = cos_ref[...]
    sin = sin_ref[...]
    bd = bd_ref[...]
    off = 0
    q = _head_rmsnorm(proj(off, ATTN_WIDTH), bd) * qn_ref[...]
    q_out[...] = (_rope(q, cos, sin) * HEAD_DIM ** -0.5).astype(BF16)
    off += ATTN_WIDTH
    k = _head_rmsnorm(proj(off, KV_WIDTH), bd[:KV_WIDTH, :KV_WIDTH]) * kn_ref[...]
    kf_out[...] = k
    kr_out[...] = _rope(k, cos[:, :KV_WIDTH], sin[:, :KV_WIDTH]).astype(BF16)
    off += KV_WIDTH
    v_out[...] = proj(off, KV_WIDTH)
    off += KV_WIDTH
    gq_out[...] = (proj(off, GLA_KW) * GLA_DK ** -0.5).astype(BF16)
    off += GLA_KW
    gk_out[...] = proj(off, GLA_KW).astype(BF16)
    off += GLA_KW
    gv_out[...] = proj(off, GLA_VW).astype(BF16)
    off += GLA_VW
    gg = proj(off, GLA_VW)
    gg_out[...] = (gg * jax.nn.sigmoid(gg)).astype(BF16)
    sa_out[...] = jax.nn.sigmoid(proj(OFF_GA, D_MODEL)).astype(BF16)
    sb_out[...] = jax.nn.sigmoid(proj(OFF_GB, D_MODEL)).astype(BF16)
    lr = proj(OFF_LR, LR_PAD)
    z = _dot(lr.astype(BF16), w2_ref[...]) + gb_ref[...]
    log_sig = jnp.minimum(z, 0.0) - jnp.log1p(jnp.exp(-jnp.abs(z)))
    la_out[...] = log_sig * (1.0 / GLA_GATE_NORM)


def _inproj(layer, x, mod_l, norm1, w_in, qn, kn, cos, sin, bd, w2p, gb):
    nt = N_TOK // TM
    n_ctx_t = N_CTX // TM
    t_per_seq = DEC_SEQ // TM
    row = lambda i: (i, 0)
    const = lambda i: (0, 0)
    rope_blk = lambda i: (jnp.where(i < n_ctx_t, 0, 1 + (i - n_ctx_t) % t_per_seq), 0)

    def out(width, dtype):
        return pl.BlockSpec((TM, width), row), jax.ShapeDtypeStruct((N_TOK, width), dtype)

    outs = [out(ATTN_WIDTH, BF16), out(KV_WIDTH, F32), out(KV_WIDTH, BF16), out(KV_WIDTH, F32),
            out(GLA_KW, BF16), out(GLA_KW, BF16), out(GLA_VW, BF16), out(GLA_VW, BF16),
            out(2 * GLA_KW, F32), out(D_MODEL, BF16), out(D_MODEL, BF16)]
    return pl.pallas_call(
        _inproj_kernel,
        grid=(nt,),
        in_specs=[
            pl.BlockSpec((TM, D_MODEL), row),
            pl.BlockSpec((None, 1, MOD_W), lambda i: (_mod_row(i, TM), 0, 0)),
            pl.BlockSpec((1, D_MODEL), const),
            pl.BlockSpec((None, D_MODEL, IN_WIDTH), lambda i: (layer, 0, 0), pipeline_mode=pl.Buffered(1)),
            pl.BlockSpec((1, ATTN_WIDTH), const),
            pl.BlockSpec((1, KV_WIDTH), const),
            pl.BlockSpec((TM, ATTN_WIDTH), rope_blk),
            pl.BlockSpec((TM, ATTN_WIDTH), rope_blk),
            pl.BlockSpec((ATTN_WIDTH, ATTN_WIDTH), const),
            pl.BlockSpec((LR_PAD, 2 * GLA_KW), const),
            pl.BlockSpec((1, 2 * GLA_KW), const),
        ],
        out_specs=[o[0] for o in outs],
        out_shape=[o[1] for o in outs],
        scratch_shapes=[pltpu.VMEM((D_MODEL, IN_PACKED), BF16)],
        compiler_params=pltpu.CompilerParams(dimension_semantics=("arbitrary",), vmem_limit_bytes=VMEM_LIMIT),
        name="inproj",
    )(x, mod_l, norm1, w_in, qn, kn, cos, sin, bd, w2p, gb)


def _attn_kernel(*refs, has_cache):
    if has_cache:
        q_ref, k_ref, v_ref, ck_ref, cv_ref, o_ref = refs
    else:
        q_ref, k_ref, v_ref, o_ref = refs
    tq = q_ref.shape[0]
    for g in range(KV_HEADS):
        ks = slice(g * HEAD_DIM, (g + 1) * HEAD_DIM)
        qs = jnp.concatenate(
            [q_ref[:, (g * GROUP + j) * HEAD_DIM:(g * GROUP + j + 1) * HEAD_DIM] for j in range(GROUP)], axis=0)
        k = k_ref[:, ks]
        v = v_ref[:, ks].astype(BF16)
        s = lax.dot_general(qs, k, _NT, preferred_element_type=F32)
        m = jnp.max(s, axis=-1, keepdims=True)
        if has_cache:
            ck = ck_ref[:, ks].astype(BF16)
            cv = cv_ref[:, ks].astype(BF16)
            s2 = lax.dot_general(qs, ck, _NT, preferred_element_type=F32)
            m = jnp.maximum(m, jnp.max(s2, axis=-1, keepdims=True))
        p = jnp.exp(s - m)
        den = jnp.sum(p, axis=-1, keepdims=True)
        o = _dot(p.astype(BF16), v)
        if has_cache:
            p2 = jnp.exp(s2 - m)
            den = den + jnp.sum(p2, axis=-1, keepdims=True)
            o = o + _dot(p2.astype(BF16), cv)
        o = o / den
        for j in range(GROUP):
            hd = g * GROUP + j
            o_ref[:, hd * HEAD_DIM:(hd + 1) * HEAD_DIM] = o[j * tq:(j + 1) * tq, :].astype(BF16)


def _attn_ctx(q, kr, v):
    return pl.pallas_call(
        partial(_attn_kernel, has_cache=False),
        grid=(BATCH,),
        in_specs=[
            pl.BlockSpec((SEQ, ATTN_WIDTH), lambda i: (i, 0)),
            pl.BlockSpec((SEQ, KV_WIDTH), lambda i: (i, 0)),
            pl.BlockSpec((SEQ, KV_WIDTH), lambda i: (i, 0)),
        ],
        out_specs=pl.BlockSpec((SEQ, ATTN_WIDTH), lambda i: (i, 0)),
        out_shape=jax.ShapeDtypeStruct((N_CTX, ATTN_WIDTH), BF16),
        compiler_params=pltpu.CompilerParams(dimension_semantics=("parallel",), vmem_limit_bytes=VMEM_LIMIT),
        name="attn_ctx",
    )(q, kr, v)


def _attn_lat(layer, q, kr, v, cache_k, cache_v):
    tq = 256
    nq = DEC_SEQ // tq
    q0 = N_CTX // tq
    s0 = N_CTX // DEC_SEQ
    return pl.pallas_call(
        partial(_attn_kernel, has_cache=True),
        grid=(DEC_BATCH, nq),
        in_specs=[
            pl.BlockSpec((tq, ATTN_WIDTH), lambda b, i: (q0 + b * nq + i, 0)),
            pl.BlockSpec((DEC_SEQ, KV_WIDTH), lambda b, i: (s0 + b, 0)),
            pl.BlockSpec((DEC_SEQ, KV_WIDTH), lambda b, i: (s0 + b, 0)),
            pl.BlockSpec((None, None, PAST_LEN, KV_WIDTH), lambda b, i: (b, layer, 0, 0)),
            pl.BlockSpec((None, None, PAST_LEN, KV_WIDTH), lambda b, i: (b, layer, 0, 0)),
        ],
        out_specs=pl.BlockSpec((tq, ATTN_WIDTH), lambda b, i: (b * nq + i, 0)),
        out_shape=jax.ShapeDtypeStruct((N_LAT, ATTN_WIDTH), BF16),
        compiler_params=pltpu.CompilerParams(dimension_semantics=("parallel", "parallel"),
                                             vmem_limit_bytes=VMEM_LIMIT),
        name="attn_lat",
    )(q, kr, v, cache_k, cache_v)


def _bd_rows(x, n_heads, width):
    head = lax.broadcasted_iota(jnp.int32, x.shape, 1) // width
    return jnp.concatenate([jnp.where(head == hh, x, 0.0) for hh in range(n_heads)], axis=0)


def _ref_rows(b, rows, rep):
    return jnp.concatenate([jnp.broadcast_to(b[r:r + 1, :], (rep, b.shape[1])) for r in rows], axis=0)


def _gla_scores(q, k, b, reverse):
    c = GLA_CHUNK
    t_idx = lax.broadcasted_iota(jnp.int32, (c, GLA_KW), 0)
    a_t = lax.broadcasted_iota(jnp.int32, (c, GLA_HEADS * c), 0)
    a_s = lax.broadcasted_iota(jnp.int32, (c, GLA_HEADS * c), 1) & (c - 1)
    total = jnp.zeros((c, GLA_HEADS * c), F32)

    def scores(qt, kt):
        return lax.dot_general(qt.astype(BF16), _bd_rows(kt, GLA_HEADS, GLA_DK).astype(BF16), _NT,
                               preferred_element_type=F32)

    for m in GLA_LEVELS:
        nblk = c // (2 * m)
        ref_rows = [2 * m * j + (m if reverse else m - 1) for j in range(nblk)]
        ref = _ref_rows(b, ref_rows, 2 * m)
        q_side = ((t_idx & m) == 0) if reverse else ((t_idx & m) != 0)
        qt = jnp.where(q_side, q * jnp.exp(jnp.minimum(b - ref, 0.0)), 0.0)
        kt = jnp.where(q_side, 0.0, k * jnp.exp(jnp.minimum(ref - b, 0.0)))
        shift = int(np.log2(2 * m))
        same = (a_t >> shift) == (a_s >> shift)
        total = total + jnp.where(same, scores(qt, kt), 0.0)
    d = GLA_DIAG
    ref = _ref_rows(b, [d * j + d // 2 for j in range(c // d)], d)
    x = b - ref
    shift = int(np.log2(d))
    same = (a_t >> shift) == (a_s >> shift)
    tri = (a_s >= a_t) if reverse else (a_s <= a_t)
    total = total + jnp.where(same, jnp.where(tri, scores(q * jnp.exp(x), k * jnp.exp(-x)), 0.0), 0.0)
    return total


def _cumsum_rows(tri, la):
    hi, mid, lo = _split3(la)
    return _dot(tri, hi) + _dot(tri, mid) + _dot(tri, lo)


def _gla_kernel(*refs, seq_len, has_state):
    if has_state:
        gq_ref, gk_ref, gv_ref, la_ref, gg_ref, gn_ref, s0_ref, o_ref, acc_ref, st_ref = refs
    else:
        gq_ref, gk_ref, gv_ref, la_ref, gg_ref, gn_ref, o_ref, sfin_ref, acc_ref, st_ref = refs
    c = GLA_CHUNK
    nc = seq_len // c
    r_i = lax.broadcasted_iota(jnp.int32, (c, c), 0)
    c_i = lax.broadcasted_iota(jnp.int32, (c, c), 1)
    tril = jnp.where(c_i <= r_i, 1.0, 0.0).astype(BF16)
    triu = jnp.where(c_i >= r_i, 1.0, 0.0).astype(BF16)
    st_head_r = lax.broadcasted_iota(jnp.int32, (GLA_VW, GLA_KW), 0) // GLA_DV
    st_head_c = lax.broadcasted_iota(jnp.int32, (GLA_VW, GLA_KW), 1) // GLA_DK
    st_mask = st_head_r == st_head_c

    for d in range(2):
        if has_state:
            st_ref[d] = jnp.zeros((GLA_VW, GLA_KW), F32)
            for hh in range(GLA_HEADS):
                st_ref[d, hh * GLA_DV:(hh + 1) * GLA_DV, hh * GLA_DK:(hh + 1) * GLA_DK] = s0_ref[d, hh].T
        else:
            st_ref[d] = jnp.zeros((GLA_VW, GLA_KW), F32)

    def load(n):
        rows = pl.ds(pl.multiple_of(n * c, c), c)
        return (rows, gq_ref[rows, :].astype(F32), gk_ref[rows, :].astype(F32), gv_ref[rows, :].astype(F32))

    def inter(d, q, k, v, b, b_end):
        st = st_ref[d]
        q_in = (q * jnp.exp(b)).astype(BF16)
        o = lax.dot_general(q_in, st.astype(BF16), _NT, preferred_element_type=F32)
        k_st = (k * jnp.exp(b_end - b)).astype(BF16)
        upd = lax.dot_general(v.astype(BF16), k_st, _TN, preferred_element_type=F32)
        st_ref[d] = st * jnp.exp(b_end) + jnp.where(st_mask, upd, 0.0)
        return o

    def fwd_body(n, carry):
        rows, q, k, v = load(n)
        b_f = _cumsum_rows(tril, la_ref[rows, 0:GLA_KW])
        b_b = _cumsum_rows(triu, la_ref[rows, GLA_KW:2 * GLA_KW])
        a = _gla_scores(q, k, b_f, False) + _gla_scores(q, k, b_b, True)
        o = _dot(a.astype(BF16), _bd_rows(v, GLA_HEADS, GLA_DV).astype(BF16))
        o = o + inter(0, q, k, v, b_f, b_f[c - 1:c, :])
        acc_ref[rows, :] = o
        return carry

    lax.fori_loop(0, nc, fwd_body, 0)

    def bwd_body(i, carry):
        n = nc - 1 - i
        rows, q, k, v = load(n)
        b_b = _cumsum_rows(triu, la_ref[rows, GLA_KW:2 * GLA_KW])
        acc_ref[rows, :] = acc_ref[rows, :] + inter(1, q, k, v, b_b, b_b[0:1, :])
        return carry

    lax.fori_loop(0, nc, bwd_body, 0)

    o = acc_ref[...]
    gn = gn_ref[...]
    for hh in range(GLA_HEADS):
        sl = slice(hh * GLA_DV, (hh + 1) * GLA_DV)
        o_ref[:, sl] = (_rms(o[:, sl]) * gn[:, sl] * gg_ref[:, sl].astype(F32)).astype(BF16)
    if not has_state:
        for d in range(2):
            for hh in range(GLA_HEADS):
                sfin_ref[d, hh] = st_ref[d, hh * GLA_DV:(hh + 1) * GLA_DV, hh * GLA_DK:(hh + 1) * GLA_DK].T


def _gla_scratch(seq_len):
    return [pltpu.VMEM((seq_len, GLA_VW), F32), pltpu.VMEM((2, GLA_VW, GLA_KW), F32)]


def _gla_ctx(gq, gk, gv, la, gg, gn):
    seq = lambda w: pl.BlockSpec((SEQ, w), lambda i: (i, 0))
    return pl.pallas_call(
        partial(_gla_kernel, seq_len=SEQ, has_state=False),
        grid=(BATCH,),
        in_specs=[seq(GLA_KW), seq(GLA_KW), seq(GLA_VW), seq(2 * GLA_KW), seq(GLA_VW),
                  pl.BlockSpec((1, GLA_VW), lambda i: (0, 0))],
        out_specs=[seq(GLA_VW),
                   pl.BlockSpec((None, 2, GLA_HEADS, GLA_DK, GLA_DV), lambda i: (i, 0, 0, 0, 0))],
        out_shape=[jax.ShapeDtypeStruct((N_CTX, GLA_VW), BF16),
                   jax.ShapeDtypeStruct((BATCH, 2, GLA_HEADS, GLA_DK, GLA_DV), F32)],
        scratch_shapes=_gla_scratch(SEQ),
        compiler_params=pltpu.CompilerParams(dimension_semantics=("parallel",), vmem_limit_bytes=VMEM_LIMIT),
        name="gla_ctx",
    )(gq, gk, gv, la, gg, gn)


def _gla_lat(layer, gq, gk, gv, la, gg, gn, s0):
    s0_blk = N_CTX // DEC_SEQ
    seq = lambda w: pl.BlockSpec((DEC_SEQ, w), lambda b: (s0_blk + b, 0))
    return pl.pallas_call(
        partial(_gla_kernel, seq_len=DEC_SEQ, has_state=True),
        grid=(DEC_BATCH,),
        in_specs=[seq(GLA_KW), seq(GLA_KW), seq(GLA_VW), seq(2 * GLA_KW), seq(GLA_VW),
                  pl.BlockSpec((1, GLA_VW), lambda b: (0, 0)),
                  pl.BlockSpec((None, None, 2, GLA_HEADS, GLA_DK, GLA_DV), lambda b: (b, layer, 0, 0, 0, 0))],
        out_specs=pl.BlockSpec((DEC_SEQ, GLA_VW), lambda b: (b, 0)),
        out_shape=jax.ShapeDtypeStruct((N_LAT, GLA_VW), BF16),
        scratch_shapes=_gla_scratch(DEC_SEQ),
        compiler_params=pltpu.CompilerParams(dimension_semantics=("parallel",), vmem_limit_bytes=VMEM_LIMIT),
        name="gla_lat",
    )(gq, gk, gv, la, gg, gn, s0)


def _route(scores, bias):
    biased = scores + bias
    v = [biased[e:e + 1, :] for e in range(N_EXPERTS)]
    s = [scores[e:e + 1, :] for e in range(N_EXPERTS)]
    gscore = []
    for g in range(N_GROUPS):
        a, b, c, d = v[4 * g:4 * g + 4]
        hi1, lo1, hi2, lo2 = jnp.maximum(a, b), jnp.minimum(a, b), jnp.maximum(c, d), jnp.minimum(c, d)
        gscore.append(jnp.maximum(hi1, hi2) + jnp.maximum(jnp.minimum(hi1, hi2), jnp.maximum(lo1, lo2)))
    one = lambda cond: jnp.where(cond, 1.0, 0.0)
    picked, chosen = [], []
    for g in range(N_GROUPS):
        best = None
        for g2 in range(N_GROUPS):
            if g2 == g:
                continue
            ok = one((gscore[g] > gscore[g2]) if g2 < g else (gscore[g] >= gscore[g2]))
            best = ok if best is None else best * ok
        for i in range(EXPERTS_PER_GROUP):
            e = 4 * g + i
            beaten = None
            for j in range(EXPERTS_PER_GROUP):
                if j == i:
                    continue
                e2 = 4 * g + j
                lost = one((v[e2] > v[e]) if j > i else (v[e2] >= v[e]))
                beaten = lost if beaten is None else beaten + lost
            chosen.append(best * one(beaten < 1.5))
            picked.append(chosen[-1] * s[e])
    den = picked[0]
    for e in range(1, N_EXPERTS):
        den = den + picked[e]
    return jnp.concatenate([p / den for p in picked] + chosen, axis=0)


def _mix_kernel(x_ref, oac_ref, oal_ref, ogc_ref, ogl_ref, sa_ref, sb_ref, mod_ref, wpa_f, wpb_f, wo_f, n2_ref,
                rw_ref, rb_ref, x1_out, h2_out, gt_out, wpa_ref, wpb_ref, wo_ref):
    @pl.when(pl.program_id(0) == 0)
    def _():
        wpa_ref[...] = wpa_f[...].astype(BF16)
        wpb_ref[...] = wpb_f[...].astype(BF16)
        wo_ref[...] = wo_f[...].astype(BF16)

    is_ctx = pl.program_id(0) < N_CTX // TM
    oa = jnp.where(is_ctx, oac_ref[...], oal_ref[...])
    og = jnp.where(is_ctx, ogc_ref[...], ogl_ref[...])
    a = _dot(oa, wpa_ref[...])
    b = _dot(og, wpb_ref[...])
    merged = sa_ref[...].astype(F32) * a + sb_ref[...].astype(F32) * b
    g1 = mod_ref[:, 2 * D_MODEL:3 * D_MODEL]
    sh2 = mod_ref[:, 3 * D_MODEL:4 * D_MODEL]
    sc2 = mod_ref[:, 4 * D_MODEL:5 * D_MODEL]
    x1 = x_ref[...] + g1 * _dot(merged.astype(BF16), wo_ref[...])
    x1_out[...] = x1
    h2 = _rms(x1) * n2_ref[...] * (1.0 + sc2) + sh2
    _to_row_tiles(h2_out, h2)
    hh, hm, hl = _split3(h2)
    rh, rm, rl = rw_ref[0], rw_ref[1], rw_ref[2]
    nt = lambda r, t: lax.dot_general(r, t, _NT, preferred_element_type=F32)
    logits = nt(rl, hh) + nt(rh, hl) + nt(rm, hm) + nt(rm, hh) + nt(rh, hm) + nt(rh, hh)
    gt_out[...] = _route(jax.nn.sigmoid(logits), rb_ref[...])


def _mix(layer, x, oa_c, oa_l, og_c, og_l, sa, sb, mod_l, wpa, wpb, wo, norm2, rw3, rb):
    nt = N_TOK // TM
    n_ctx_t = N_CTX // TM
    row = lambda i: (i, 0)
    const = lambda i: (0, 0)
    ctx_row = lambda i: (jnp.minimum(i, n_ctx_t - 1), 0)
    lat_row = lambda i: (jnp.maximum(i - n_ctx_t, 0), 0)
    return pl.pallas_call(
        _mix_kernel,
        grid=(nt,),
        in_specs=[
            pl.BlockSpec((TM, D_MODEL), row),
            pl.BlockSpec((TM, ATTN_WIDTH), ctx_row),
            pl.BlockSpec((TM, ATTN_WIDTH), lat_row),
            pl.BlockSpec((TM, GLA_VW), ctx_row),
            pl.BlockSpec((TM, GLA_VW), lat_row),
            pl.BlockSpec((TM, D_MODEL), row),
            pl.BlockSpec((TM, D_MODEL), row),
            pl.BlockSpec((None, 1, MOD_W), lambda i: (_mod_row(i, TM), 0, 0)),
            pl.BlockSpec((None, ATTN_WIDTH, D_MODEL), lambda i: (layer, 0, 0), pipeline_mode=pl.Buffered(1)),
            pl.BlockSpec((None, GLA_VW, D_MODEL), lambda i: (layer, 0, 0), pipeline_mode=pl.Buffered(1)),
            pl.BlockSpec((None, D_MODEL, D_MODEL), lambda i: (layer, 0, 0), pipeline_mode=pl.Buffered(1)),
            pl.BlockSpec((1, D_MODEL), const),
            pl.BlockSpec((3, N_EXPERTS, D_MODEL), lambda i: (0, 0, 0)),
            pl.BlockSpec((N_EXPERTS, 1), const),
        ],
        out_specs=[pl.BlockSpec((TM, D_MODEL), row), pl.BlockSpec((TM * ROW_TILE, LANES), row),
                   pl.BlockSpec((2 * N_EXPERTS, TM), lambda i: (0, i))],
        out_shape=[jax.ShapeDtypeStruct((N_TOK, D_MODEL), F32),
                   jax.ShapeDtypeStruct((N_TOK * ROW_TILE, LANES), F32),
                   jax.ShapeDtypeStruct((2 * N_EXPERTS, N_TOK), F32)],
        scratch_shapes=[pltpu.VMEM((ATTN_WIDTH, D_MODEL), BF16), pltpu.VMEM((GLA_VW, D_MODEL), BF16),
                        pltpu.VMEM((D_MODEL, D_MODEL), BF16)],
        compiler_params=pltpu.CompilerParams(dimension_semantics=("arbitrary",), vmem_limit_bytes=VMEM_LIMIT),
        name="mix",
    )(x, oa_c, oa_l, og_c, og_l, sa, sb, mod_l, wpa, wpb, wo, norm2, rw3, rb)


def _route_plan(gsel):
    gates = gsel[:N_EXPERTS]
    sel = gsel[N_EXPERTS:] > 0.5
    seli = sel.astype(jnp.int32)
    incl = jnp.cumsum(seli, axis=1)
    count = incl[:, -1]
    ntile = (count + TS - 1) // TS
    tile_end = jnp.cumsum(ntile)
    tile_start = tile_end - ntile
    total = tile_end[-1]
    slot = tile_start[:, None] * TS + incl - seli
    pos_a = jnp.min(jnp.where(sel, slot, N_SLOT), axis=0)
    pos_b = jnp.max(jnp.where(sel, slot, -1), axis=0)
    w_a = jnp.sum(jnp.where(sel & (slot == pos_a[None, :]), gates, 0.0), axis=0)
    w_b = jnp.sum(jnp.where(sel & (slot == pos_b[None, :]), gates, 0.0), axis=0)
    tile_id = jnp.minimum(jnp.arange(N_TILE, dtype=jnp.int32), total - 1)
    tile_expert = jnp.sum((tile_id[:, None] >= tile_end[None, :]).astype(jnp.int32), axis=1)
    nt = N_TOK // TM
    pos = jnp.concatenate([pos_a.reshape(nt, 1, TM), pos_b.reshape(nt, 1, TM)], axis=2).astype(jnp.int32)
    plan = jnp.concatenate([tile_start, ntile, total[None]]).astype(jnp.int32)
    return pos, jnp.stack([w_a, w_b], axis=1), tile_expert.astype(jnp.int32), plan


def _to_row_tiles(ref, x):
    for k in range(ROW_TILE):
        ref[pl.ds(k, x.shape[0], stride=ROW_TILE), :] = x[:, k * LANES:(k + 1) * LANES]


def _from_row_tiles(ref, rows):
    return jnp.concatenate([ref[pl.ds(k, rows, stride=ROW_TILE), :] for k in range(ROW_TILE)], axis=1)


def _tile_rows(row, n=1):
    return pl.ds(pl.multiple_of(row * ROW_TILE, ROW_TILE), n * ROW_TILE)


def _row_copy(src, src_row, dst, dst_row, sem):
    return pltpu.make_async_copy(src.at[_tile_rows(src_row), :], dst.at[_tile_rows(dst_row), :], sem)


def _dispatch_kernel(plan_ref, pos_ref, h_ref, xs_ref, zero_ref, sem_z, sem):
    i = pl.program_id(0)
    n = pl.num_programs(0)
    total = plan_ref[2 * N_EXPERTS]

    def zero_copy(tile):
        return pltpu.make_async_copy(zero_ref, xs_ref.at[_tile_rows(tile * TS, TS), :], sem_z)

    def last_tile(e):
        return plan_ref[e] + plan_ref[N_EXPERTS + e] - 1

    @pl.when(i == 0)
    def _():
        zero_ref[...] = jnp.zeros(zero_ref.shape, F32)

        def each(fn):
            def expert(e, c):
                @pl.when(plan_ref[N_EXPERTS + e] > 0)
                def _():
                    fn(zero_copy(last_tile(e)))
                return c

            def unused(t, c):
                @pl.when(t >= total)
                def _():
                    fn(zero_copy(t))
                return c

            lax.fori_loop(0, N_EXPERTS, expert, 0)
            lax.fori_loop(0, N_TILE, unused, 0)

        each(lambda cp: cp.start())
        each(lambda cp: cp.wait())

    def start(r, c):
        _row_copy(h_ref, i * TM + r, xs_ref, pos_ref[0, r], sem.at[i % 2]).start()
        _row_copy(h_ref, i * TM + r, xs_ref, pos_ref[0, TM + r], sem.at[i % 2]).start()
        return c

    def wait_step(slot):
        def wait(r, c):
            _row_copy(h_ref, 0, xs_ref, 0, sem.at[slot]).wait()
            return c
        lax.fori_loop(0, 2 * TM, wait, 0, unroll=8)

    lax.fori_loop(0, TM, start, 0, unroll=8)

    @pl.when(i > 0)
    def _():
        wait_step((i - 1) % 2)

    @pl.when(i == n - 1)
    def _():
        wait_step(i % 2)


def _dispatch(plan, pos, h2):
    return pl.pallas_call(
        _dispatch_kernel,
        grid_spec=pltpu.PrefetchScalarGridSpec(
            num_scalar_prefetch=1,
            grid=(N_TOK // TM,),
            in_specs=[
                pl.BlockSpec((None, 1, 2 * TM), lambda i, plan: (i, 0, 0), memory_space=pltpu.SMEM),
                pl.BlockSpec(memory_space=pl.ANY),
            ],
            out_specs=pl.BlockSpec(memory_space=pl.ANY),
            scratch_shapes=[pltpu.VMEM((TS * ROW_TILE, LANES), F32), pltpu.SemaphoreType.DMA,
                            pltpu.SemaphoreType.DMA((2,))],
        ),
        out_shape=jax.ShapeDtypeStruct((N_SLOT * ROW_TILE, LANES), F32),
        compiler_params=pltpu.CompilerParams(dimension_semantics=("arbitrary",), vmem_limit_bytes=VMEM_LIMIT),
        name="dispatch",
    )(plan, pos, h2)


def _moe_kernel(te_ref, plan_ref, xs_ref, wg_ref, wu_ref, wd_ref, ys_ref, wg_s, wu_s, wd_s):
    i = pl.program_id(0)
    total = plan_ref[2 * N_EXPERTS]

    @pl.when(i < total)
    def _():
        @pl.when((i == 0) | (te_ref[i] != te_ref[jnp.maximum(i - 1, 0)]))
        def _():
            wg_s[...] = wg_ref[...].astype(BF16)
            wu_s[...] = wu_ref[...].astype(BF16)
            wd_s[...] = wd_ref[...].astype(BF16)

        x = _from_row_tiles(xs_ref, TS).astype(BF16)
        up = _dot(x, wu_s[...])
        gt = _dot(x, wg_s[...])
        act = (gt * jax.nn.sigmoid(gt) * up).astype(BF16)
        _to_row_tiles(ys_ref, _dot(act, wd_s[...]))

    @pl.when(i >= total)
    def _():
        ys_ref[...] = jnp.zeros(ys_ref.shape, F32)


def _moe(layer, tile_expert, plan, xs, w_gate, w_up, w_down):
    last = lambda i, plan: jnp.minimum(i, plan[2 * N_EXPERTS] - 1)
    w_spec = lambda a, b: pl.BlockSpec((None, None, a, b), lambda i, te, plan: (layer, te[i], 0, 0))
    return pl.pallas_call(
        _moe_kernel,
        grid_spec=pltpu.PrefetchScalarGridSpec(
            num_scalar_prefetch=2,
            grid=(N_TILE,),
            in_specs=[
                pl.BlockSpec((TS * ROW_TILE, LANES), lambda i, te, plan: (last(i, plan), 0)),
                w_spec(D_MODEL, D_EXPERT), w_spec(D_MODEL, D_EXPERT), w_spec(D_EXPERT, D_MODEL),
            ],
            out_specs=pl.BlockSpec((TS * ROW_TILE, LANES), lambda i, te, plan: (i, 0)),
            scratch_shapes=[pltpu.VMEM((D_MODEL, D_EXPERT), BF16), pltpu.VMEM((D_MODEL, D_EXPERT), BF16),
                            pltpu.VMEM((D_EXPERT, D_MODEL), BF16)],
        ),
        out_shape=jax.ShapeDtypeStruct((N_SLOT * ROW_TILE, LANES), F32),
        compiler_params=pltpu.CompilerParams(dimension_semantics=("arbitrary",), vmem_limit_bytes=VMEM_LIMIT),
        name="moe",
    )(tile_expert, plan, xs, w_gate, w_up, w_down)


def _combine_kernel(pos_ref, nxt_ref, w_ref, x1_ref, mod_ref, ys_ref, out_ref, ya_ref, yb_ref, sem):
    i = pl.program_id(0)
    n = pl.num_programs(0)

    def gather(p_ref, slot):
        def start(r, c):
            _row_copy(ys_ref, p_ref[0, r], ya_ref.at[slot], r, sem.at[slot]).start()
            _row_copy(ys_ref, p_ref[0, TM + r], yb_ref.at[slot], r, sem.at[slot]).start()
            return c
        lax.fori_loop(0, TM, start, 0, unroll=8)

    @pl.when(i == 0)
    def _():
        gather(pos_ref, 0)

    @pl.when(i + 1 < n)
    def _():
        gather(nxt_ref, (i + 1) % 2)

    slot = i % 2

    def wait(r, c):
        _row_copy(ys_ref, 0, ya_ref.at[slot], 0, sem.at[slot]).wait()
        return c

    lax.fori_loop(0, 2 * TM, wait, 0, unroll=8)
    g2 = mod_ref[:, 5 * D_MODEL:6 * D_MODEL]
    w = w_ref[...]
    ya = _from_row_tiles(ya_ref.at[slot], TM)
    yb = _from_row_tiles(yb_ref.at[slot], TM)
    out_ref[...] = x1_ref[...] + g2 * (w[:, 0:1] * ya + w[:, 1:2] * yb)


def _combine(pos, w_ab, x1, mod_l, ys):
    nt = N_TOK // TM
    pos_spec = lambda fn: pl.BlockSpec((None, 1, 2 * TM), fn, memory_space=pltpu.SMEM)
    return pl.pallas_call(
        _combine_kernel,
        grid=(nt,),
        in_specs=[
            pos_spec(lambda i: (i, 0, 0)),
            pos_spec(lambda i: (jnp.minimum(i + 1, nt - 1), 0, 0)),
            pl.BlockSpec((TM, 2), lambda i: (i, 0)),
            pl.BlockSpec((TM, D_MODEL), lambda i: (i, 0)),
            pl.BlockSpec((None, 1, MOD_W), lambda i: (_mod_row(i, TM), 0, 0)),
            pl.BlockSpec(memory_space=pl.ANY),
        ],
        out_specs=pl.BlockSpec((TM, D_MODEL), lambda i: (i, 0)),
        out_shape=jax.ShapeDtypeStruct((N_TOK, D_MODEL), F32),
        scratch_shapes=[pltpu.VMEM((2, TM * ROW_TILE, LANES), F32), pltpu.VMEM((2, TM * ROW_TILE, LANES), F32),
                        pltpu.SemaphoreType.DMA((2,))],
        compiler_params=pltpu.CompilerParams(dimension_semantics=("arbitrary",), vmem_limit_bytes=VMEM_LIMIT),
        name="combine",
    )(pos, pos, w_ab, x1, mod_l, ys)


def _rope_tables():
    pos = np.arange(DEC_SEQ)
    rows = (pos // GRID_W).astype(np.float64)
    cols = (pos % GRID_W).astype(np.float64)
    half = HEAD_DIM // 2
    inv_freq = ROPE_THETA ** (-np.arange(0, half, 2, dtype=np.float64) / half)
    ang_r = rows[:, None] * inv_freq[None, :]
    ang_c = cols[:, None] * inv_freq[None, :]
    cos_h = np.concatenate([np.cos(ang_r), np.cos(ang_r), np.cos(ang_c), np.cos(ang_c)], axis=-1)
    sin_h = np.concatenate([-np.sin(ang_r), np.sin(ang_r), -np.sin(ang_c), np.sin(ang_c)], axis=-1)
    cos = np.concatenate([np.ones((TM, HEAD_DIM)), cos_h], axis=0)
    sin = np.concatenate([np.zeros((TM, HEAD_DIM)), sin_h], axis=0)
    return (jnp.asarray(np.tile(cos, (1, N_HEADS)), F32), jnp.asarray(np.tile(sin, (1, N_HEADS)), F32))


def _pack_w2(w2):
    z = jnp.zeros((GLA_RANK, GLA_KW), w2.dtype)
    top = jnp.concatenate([w2[0], z], axis=1)
    bot = jnp.concatenate([z, w2[1]], axis=1)
    pad = jnp.zeros((LR_PAD - 2 * GLA_RANK, 2 * GLA_KW), w2.dtype)
    return jnp.concatenate([top, bot, pad], axis=0).astype(BF16)


def kernel(x_prompt, x_sample, cache_k, cache_v, state_gla, c, c_ctx, norm1, norm2, w_ada, b_ada, w_in,
           q_norm, k_norm, gla_w2, gla_b, gla_norm, w_pa, w_pb, w_out, router_w, router_bias,
           w_gate, w_up, w_down):
    x = jnp.concatenate([x_prompt.reshape(N_CTX, D_MODEL), x_sample.reshape(N_LAT, D_MODEL)], axis=0)
    cond = jnp.concatenate([c_ctx[None, :], c, jnp.zeros((N_COND - 1 - DEC_BATCH, D_MODEL), F32)], axis=0)
    mod = _ada(cond, w_ada, b_ada).reshape(DEPTH, N_COND, 1, MOD_W)
    cos, sin = _rope_tables()
    lane_head = np.arange(ATTN_WIDTH) // HEAD_DIM
    bd = jnp.asarray(lane_head[:, None] == lane_head[None, :], BF16)
    rw3 = jnp.stack(_split3(router_w.T), axis=0)
    rb = router_bias.reshape(N_EXPERTS, 1)
    ck = cache_k.reshape(DEC_BATCH, DEPTH, PAST_LEN, KV_WIDTH)
    cv = cache_v.reshape(DEC_BATCH, DEPTH, PAST_LEN, KV_WIDTH)

    keys, vals, states = [], [], []
    for l in range(DEPTH):
        q, kf, kr, v, gq, gk, gv, gg, la, sa, sb = _inproj(
            l, x, mod[l], norm1[l][None, :], w_in, jnp.tile(q_norm[l], N_HEADS)[None, :],
            jnp.tile(k_norm[l], KV_HEADS)[None, :], cos, sin, bd, _pack_w2(gla_w2[l]),
            gla_b[l].reshape(1, 2 * GLA_KW))
        oa_c = _attn_ctx(q, kr, v)
        oa_l = _attn_lat(l, q, kr, v, ck, cv)
        gn = jnp.tile(gla_norm[l], GLA_HEADS)[None, :]
        og_c, s_fin = _gla_ctx(gq, gk, gv, la, gg, gn)
        og_l = _gla_lat(l, gq, gk, gv, la, gg, gn, state_gla)
        x1, h2, gsel = _mix(l, x, oa_c, oa_l, og_c, og_l, sa, sb, mod[l], w_pa, w_pb, w_out,
                            norm2[l][None, :], rw3, rb)
        pos, w_ab, tile_expert, plan = _route_plan(gsel)
        xs = _dispatch(plan, pos, h2)
        ys = _moe(l, tile_expert, plan, xs, w_gate, w_up, w_down)
        x = _combine(pos, w_ab, x1, mod[l], ys)
        keys.append(kf[:N_CTX].reshape(BATCH, SEQ, KV_HEADS, HEAD_DIM))
        vals.append(v[:N_CTX].reshape(BATCH, SEQ, KV_HEADS, HEAD_DIM))
        states.append(s_fin)
    y_prompt = x[:N_CTX].reshape(BATCH, SEQ, D_MODEL)
    y_sample = x[N_CTX:].reshape(DEC_BATCH, DEC_SEQ, D_MODEL)
    return (y_prompt, y_sample, jnp.stack(keys, axis=1), jnp.stack(vals, axis=1), jnp.stack(states, axis=1))
```

```python
from functools import partial

import jax
import jax.numpy as jnp
import numpy as np
from jax import lax
from jax.experimental import pallas as pl
from jax.experimental.pallas import tpu as pltpu

F32 = jnp.float32
BF16 = jnp.bfloat16

D_MODEL = 1024
BATCH = 16
SEQ = 256
DEPTH = 2
DEC_BATCH = 4
DEC_SEQ = 1024
PAST_LEN = 512
GRID_W = 64
N_HEADS = 8
KV_HEADS = 2
GROUP = N_HEADS // KV_HEADS
HEAD_DIM = 64
ATTN_WIDTH = N_HEADS * HEAD_DIM
KV_WIDTH = KV_HEADS * HEAD_DIM
ROPE_THETA = 10000.0
GLA_HEADS = 4
GLA_DK = 64
GLA_DV = 128
GLA_KW = GLA_HEADS * GLA_DK
GLA_VW = GLA_HEADS * GLA_DV
GLA_RANK = 16
GLA_GATE_NORM = 16.0
N_EXPERTS = 16
N_GROUPS = 4
EXPERTS_PER_GROUP = N_EXPERTS // N_GROUPS
D_EXPERT = 512
NORM_EPS = 1e-6

N_CTX = BATCH * SEQ
N_LAT = DEC_BATCH * DEC_SEQ
N_TOK = N_CTX + N_LAT
N_COND = 8
MOD_W = 6 * D_MODEL

TM = 512
TS = 256
N_TILE = 2 * N_TOK // TS + N_EXPERTS
N_SLOT = N_TILE * TS
ROW_TILE = 8
LANES = D_MODEL // ROW_TILE
GLA_CHUNK = 64
GLA_LEVELS = (32, 16, 8)
GLA_DIAG = 8
LR_PAD = 128
SEG_MIX = ATTN_WIDTH + 2 * KV_WIDTH + 2 * GLA_KW + 2 * GLA_VW
OFF_GA = SEG_MIX
OFF_GB = OFF_GA + D_MODEL
OFF_LR = OFF_GB + D_MODEL
IN_PACKED = OFF_LR + LR_PAD
IN_WIDTH = SEG_MIX + 2 * GLA_RANK + 2 * D_MODEL
VMEM_LIMIT = 56 * 1024 * 1024

_NT = (((1,), (1,)), ((), ()))
_TN = (((0,), (0,)), ((), ()))


def _mod_row(tile, tm):
    start = tile * tm
    return jnp.where(start < N_CTX, 0, 1 + (start - N_CTX) // DEC_SEQ)


def _split3(x):
    hi = x.astype(BF16)
    r1 = x - hi.astype(F32)
    mid = r1.astype(BF16)
    lo = (r1 - mid.astype(F32)).astype(BF16)
    return hi, mid, lo


def _split2(x):
    hi = x.astype(BF16)
    lo = (x - hi.astype(F32)).astype(BF16)
    return hi, lo


def _dot(a, b):
    return jnp.dot(a, b, preferred_element_type=F32)


def _rms(x):
    return x * lax.rsqrt(jnp.mean(x * x, axis=-1, keepdims=True) + NORM_EPS)


def _ada_kernel(cond_ref, w_ref, b_ref, out_ref):
    c = cond_ref[...]
    s = c * jax.nn.sigmoid(c)
    out_ref[...] = _dot(s.astype(BF16), w_ref[...].astype(BF16)) + b_ref[...]


def _ada(cond, w_ada, b_ada):
    nb = MOD_W // D_MODEL
    return pl.pallas_call(
        _ada_kernel,
        grid=(DEPTH, nb),
        in_specs=[
            pl.BlockSpec((N_COND, D_MODEL), lambda l, j: (0, 0)),
            pl.BlockSpec((None, D_MODEL, D_MODEL), lambda l, j: (l, 0, j)),
            pl.BlockSpec((None, 1, D_MODEL), lambda l, j: (l, 0, j)),
        ],
        out_specs=pl.BlockSpec((None, N_COND, D_MODEL), lambda l, j: (l, 0, j)),
        out_shape=jax.ShapeDtypeStruct((DEPTH, N_COND, MOD_W), F32),
        compiler_params=pltpu.CompilerParams(vmem_limit_bytes=VMEM_LIMIT),
        name="ada",
    )(cond, w_ada, b_ada.reshape(DEPTH, 1, MOD_W))


def _head_rmsnorm(x, bd):
    hi, lo = _split2(x * x)
    ssq = _dot(hi, bd) + _dot(lo, bd)
    return x * lax.rsqrt(ssq * (1.0 / HEAD_DIM) + NORM_EPS)


def _rope(x, cos, sin):
    n = x.shape[-1]
    lane = lax.broadcasted_iota(jnp.int32, x.shape, 1)
    first = (lane & (HEAD_DIM // 4)) == 0
    partner = jnp.where(first, pltpu.roll(x, n - HEAD_DIM // 4, 1), pltpu.roll(x, HEAD_DIM // 4, 1))
    return x * cos + partner * sin


def _pack_w_in(w_ref, wp_ref):
    lr0 = SEG_MIX
    ga0 = lr0 + 2 * GLA_RANK
    rows_per_step = 128

    def body(i, carry):
        rows = pl.ds(pl.multiple_of(i * rows_per_step, rows_per_step), rows_per_step)
        wp_ref[rows, 0:SEG_MIX] = w_ref[rows, 0:SEG_MIX].astype(BF16)
        tail = w_ref[rows, SEG_MIX:IN_WIDTH]
        lr = tail[:, 0:2 * GLA_RANK]
        wp_ref[rows, OFF_LR:IN_PACKED] = jnp.concatenate(
            [lr, jnp.zeros((rows_per_step, LR_PAD - 2 * GLA_RANK), F32)], axis=1).astype(BF16)
        wp_ref[rows, OFF_GA:OFF_LR] = tail[:, ga0 - lr0:ga0 - lr0 + 2 * D_MODEL].astype(BF16)
        return carry

    lax.fori_loop(0, D_MODEL // rows_per_step, body, 0)


def _inproj_kernel(x_ref, mod_ref, n1_ref, wf_ref, qn_ref, kn_ref, cos_ref, sin_ref, bd_ref, w2_ref, gb_ref,
                   q_out, kf_out, kr_out, v_out, gq_out, gk_out, gv_out, gg_out, la_out, sa_out, sb_out,
                   w_ref):
    @pl.when(pl.program_id(0) == 0)
    def _():
        _pack_w_in(wf_ref, w_ref)

    x = x_ref[...]
    sh1 = mod_ref[:, 0:D_MODEL]
    sc1 = mod_ref[:, D_MODEL:2 * D_MODEL]
    h = (_rms(x) * n1_ref[...] * (1.0 + sc1) + sh1).astype(BF16)

    def proj(lo, width):
        return _dot(h, w_ref[:, lo:lo + width])

    cos = cos_ref[...]
    sin = sin_ref[...]
    bd = bd_ref[...]
    off = 0
    q = _head_rmsnorm(proj(off, ATTN_WIDTH), bd) * qn_ref[...]
    q_out[...] = (_rope(q, cos, sin) * HEAD_DIM ** -0.5).astype(BF16)
    off += ATTN_WIDTH
    k = _head_rmsnorm(proj(off, KV_WIDTH), bd[:KV_WIDTH, :KV_WIDTH]) * kn_ref[...]
    kf_out[...] = k
    kr_out[...] = _rope(k, cos[:, :KV_WIDTH], sin[:, :KV_WIDTH]).astype(BF16)
    off += KV_WIDTH
    v_out[...] = proj(off, KV_WIDTH)
    off += KV_WIDTH
    gq_out[...] = (proj(off, GLA_KW) * GLA_DK ** -0.5).astype(BF16)
    off += GLA_KW
    gk_out[...] = proj(off, GLA_KW).astype(BF16)
    off += GLA_KW
    gv_out[...] = proj(off, GLA_VW).astype(BF16)
    off += GLA_VW
    gg = proj(off, GLA_VW)
    gg_out[...] = (gg * jax.nn.sigmoid(gg)).astype(BF16)
    sa_out[...] = jax.nn.sigmoid(proj(OFF_GA, D_MODEL)).astype(BF16)
    sb_out[...] = jax.nn.sigmoid(proj(OFF_GB, D_MODEL)).astype(BF16)
    lr = proj(OFF_LR, LR_PAD)
    z = _dot(lr.astype(BF16), w2_ref[...]) + gb_ref[...]
    log_sig = jnp.minimum(z, 0.0) - jnp.log1p(jnp.exp(-jnp.abs(z)))
    la_out[...] = log_sig * (1.0 / GLA_GATE_NORM)


def _inproj(layer, x, mod_l, norm1, w_in, qn, kn, cos, sin, bd, w2p, gb):
    nt = N_TOK // TM
    n_ctx_t = N_CTX // TM
    t_per_seq = DEC_SEQ // TM
    row = lambda i: (i, 0)
    const = lambda i: (0, 0)
    rope_blk = lambda i: (jnp.where(i < n_ctx_t, 0, 1 + (i - n_ctx_t) % t_per_seq), 0)

    def out(width, dtype):
        return pl.BlockSpec((TM, width), row), jax.ShapeDtypeStruct((N_TOK, width), dtype)

    outs = [out(ATTN_WIDTH, BF16), out(KV_WIDTH, F32), out(KV_WIDTH, BF16), out(KV_WIDTH, F32),
            out(GLA_KW, BF16), out(GLA_KW, BF16), out(GLA_VW, BF16), out(GLA_VW, BF16),
            out(2 * GLA_KW, F32), out(D_MODEL, BF16), out(D_MODEL, BF16)]
    return pl.pallas_call(
        _inproj_kernel,
        grid=(nt,),
        in_specs=[
            pl.BlockSpec((TM, D_MODEL), row),
            pl.BlockSpec((None, 1, MOD_W), lambda i: (_mod_row(i, TM), 0, 0)),
            pl.BlockSpec((1, D_MODEL), const),
            pl.BlockSpec((None, D_MODEL, IN_WIDTH), lambda i: (layer, 0, 0), pipeline_mode=pl.Buffered(1)),
            pl.BlockSpec((1, ATTN_WIDTH), const),
            pl.BlockSpec((1, KV_WIDTH), const),
            pl.BlockSpec((TM, ATTN_WIDTH), rope_blk),
            pl.BlockSpec((TM, ATTN_WIDTH), rope_blk),
            pl.BlockSpec((ATTN_WIDTH, ATTN_WIDTH), const),
            pl.BlockSpec((LR_PAD, 2 * GLA_KW), const),
            pl.BlockSpec((1, 2 * GLA_KW), const),
        ],
        out_specs=[o[0] for o in outs],
        out_shape=[o[1] for o in outs],
        scratch_shapes=[pltpu.VMEM((D_MODEL, IN_PACKED), BF16)],
        compiler_params=pltpu.CompilerParams(dimension_semantics=("arbitrary",), vmem_limit_bytes=VMEM_LIMIT),
        name="inproj",
    )(x, mod_l, norm1, w_in, qn, kn, cos, sin, bd, w2p, gb)


def _attn_kernel(*refs, has_cache):
    if has_cache:
        q_ref, k_ref, v_ref, ck_ref, cv_ref, o_ref = refs
    else:
        q_ref, k_ref, v_ref, o_ref = refs
    tq = q_ref.shape[0]
    for g in range(KV_HEADS):
        ks = slice(g * HEAD_DIM, (g + 1) * HEAD_DIM)
        qs = jnp.concatenate(
            [q_ref[:, (g * GROUP + j) * HEAD_DIM:(g * GROUP + j + 1) * HEAD_DIM] for j in range(GROUP)], axis=0)
        k = k_ref[:, ks]
        v = v_ref[:, ks].astype(BF16)
        s = lax.dot_general(qs, k, _NT, preferred_element_type=F32)
        m = jnp.max(s, axis=-1, keepdims=True)
        if has_cache:
            ck = ck_ref[:, ks].astype(BF16)
            cv = cv_ref[:, ks].astype(BF16)
            s2 = lax.dot_general(qs, ck, _NT, preferred_element_type=F32)
            m = jnp.maximum(m, jnp.max(s2, axis=-1, keepdims=True))
        p = jnp.exp(s - m)
        den = jnp.sum(p, axis=-1, keepdims=True)
        o = _dot(p.astype(BF16), v)
        if has_cache:
            p2 = jnp.exp(s2 - m)
            den = den + jnp.sum(p2, axis=-1, keepdims=True)
            o = o + _dot(p2.astype(BF16), cv)
        o = o / den
        for j in range(GROUP):
            hd = g * GROUP + j
            o_ref[:, hd * HEAD_DIM:(hd + 1) * HEAD_DIM] = o[j * tq:(j + 1) * tq, :].astype(BF16)


def _attn_ctx(q, kr, v):
    return pl.pallas_call(
        partial(_attn_kernel, has_cache=False),
        grid=(BATCH,),
        in_specs=[
            pl.BlockSpec((SEQ, ATTN_WIDTH), lambda i: (i, 0)),
            pl.BlockSpec((SEQ, KV_WIDTH), lambda i: (i, 0)),
            pl.BlockSpec((SEQ, KV_WIDTH), lambda i: (i, 0)),
        ],
        out_specs=pl.BlockSpec((SEQ, ATTN_WIDTH), lambda i: (i, 0)),
        out_shape=jax.ShapeDtypeStruct((N_CTX, ATTN_WIDTH), BF16),
        compiler_params=pltpu.CompilerParams(dimension_semantics=("parallel",), vmem_limit_bytes=VMEM_LIMIT),
        name="attn_ctx",
    )(q, kr, v)


def _attn_lat(layer, q, kr, v, cache_k, cache_v):
    tq = 256
    nq = DEC_SEQ // tq
    q0 = N_CTX // tq
    s0 = N_CTX // DEC_SEQ
    return pl.pallas_call(
        partial(_attn_kernel, has_cache=True),
        grid=(DEC_BATCH, nq),
        in_specs=[
            pl.BlockSpec((tq, ATTN_WIDTH), lambda b, i: (q0 + b * nq + i, 0)),
            pl.BlockSpec((DEC_SEQ, KV_WIDTH), lambda b, i: (s0 + b, 0)),
            pl.BlockSpec((DEC_SEQ, KV_WIDTH), lambda b, i: (s0 + b, 0)),
            pl.BlockSpec((None, None, PAST_LEN, KV_WIDTH), lambda b, i: (b, layer, 0, 0)),
            pl.BlockSpec((None, None, PAST_LEN, KV_WIDTH), lambda b, i: (b, layer, 0, 0)),
        ],
        out_specs=pl.BlockSpec((tq, ATTN_WIDTH), lambda b, i: (b * nq + i, 0)),
        out_shape=jax.ShapeDtypeStruct((N_LAT, ATTN_WIDTH), BF16),
        compiler_params=pltpu.CompilerParams(dimension_semantics=("parallel", "parallel"),
                                             vmem_limit_bytes=VMEM_LIMIT),
        name="attn_lat",
    )(q, kr, v, cache_k, cache_v)


def _bd_rows(x, n_heads, width):
    head = lax.broadcasted_iota(jnp.int32, x.shape, 1) // width
    return jnp.concatenate([jnp.where(head == hh, x, 0.0) for hh in range(n_heads)], axis=0)


def _ref_rows(b, rows, rep):
    return jnp.concatenate([jnp.broadcast_to(b[r:r + 1, :], (rep, b.shape[1])) for r in rows], axis=0)


def _gla_scores(q, k, b, reverse):
    c = GLA_CHUNK
    t_idx = lax.broadcasted_iota(jnp.int32, (c, GLA_KW), 0)
    a_t = lax.broadcasted_iota(jnp.int32, (c, GLA_HEADS * c), 0)
    a_s = lax.broadcasted_iota(jnp.int32, (c, GLA_HEADS * c), 1) & (c - 1)
    total = jnp.zeros((c, GLA_HEADS * c), F32)

    def scores(qt, kt):
        return lax.dot_general(qt.astype(BF16), _bd_rows(kt, GLA_HEADS, GLA_DK).astype(BF16), _NT,
                               preferred_element_type=F32)

    for m in GLA_LEVELS:
        nblk = c // (2 * m)
        ref_rows = [2 * m * j + (m if reverse else m - 1) for j in range(nblk)]
        ref = _ref_rows(b, ref_rows, 2 * m)
        q_side = ((t_idx & m) == 0) if reverse else ((t_idx & m) != 0)
        qt = jnp.where(q_side, q * jnp.exp(jnp.minimum(b - ref, 0.0)), 0.0)
        kt = jnp.where(q_side, 0.0, k * jnp.exp(jnp.minimum(ref - b, 0.0)))
        shift = int(np.log2(2 * m))
        same = (a_t >> shift) == (a_s >> shift)
        total = total + jnp.where(same, scores(qt, kt), 0.0)
    d = GLA_DIAG
    ref = _ref_rows(b, [d * j + d // 2 for j in range(c // d)], d)
    x = b - ref
    shift = int(np.log2(d))
    same = (a_t >> shift) == (a_s >> shift)
    tri = (a_s >= a_t) if reverse else (a_s <= a_t)
    total = total + jnp.where(same, jnp.where(tri, scores(q * jnp.exp(x), k * jnp.exp(-x)), 0.0), 0.0)
    return total


def _cumsum_rows(tri, la):
    hi, mid, lo = _split3(la)
    return _dot(tri, hi) + _dot(tri, mid) + _dot(tri, lo)


def _gla_kernel(*refs, seq_len, has_state):
    if has_state:
        gq_ref, gk_ref, gv_ref, la_ref, gg_ref, gn_ref, s0_ref, o_ref, acc_ref, st_ref = refs
    else:
        gq_ref, gk_ref, gv_ref, la_ref, gg_ref, gn_ref, o_ref, sfin_ref, acc_ref, st_ref = refs
    c = GLA_CHUNK
    nc = seq_len // c
    r_i = lax.broadcasted_iota(jnp.int32, (c, c), 0)
    c_i = lax.broadcasted_iota(jnp.int32, (c, c), 1)
    tril = jnp.where(c_i <= r_i, 1.0, 0.0).astype(BF16)
    triu = jnp.where(c_i >= r_i, 1.0, 0.0).astype(BF16)
    st_head_r = lax.broadcasted_iota(jnp.int32, (GLA_VW, GLA_KW), 0) // GLA_DV
    st_head_c = lax.broadcasted_iota(jnp.int32, (GLA_VW, GLA_KW), 1) // GLA_DK
    st_mask = st_head_r == st_head_c

    for d in range(2):
        if has_state:
            st_ref[d] = jnp.zeros((GLA_VW, GLA_KW), F32)
            for hh in range(GLA_HEADS):
                st_ref[d, hh * GLA_DV:(hh + 1) * GLA_DV, hh * GLA_DK:(hh + 1) * GLA_DK] = s0_ref[d, hh].T
        else:
            st_ref[d] = jnp.zeros((GLA_VW, GLA_KW), F32)

    def load(n):
        rows = pl.ds(pl.multiple_of(n * c, c), c)
        return (rows, gq_ref[rows, :].astype(F32), gk_ref[rows, :].astype(F32), gv_ref[rows, :].astype(F32))

    def inter(d, q, k, v, b, b_end):
        st = st_ref[d]
        q_in = (q * jnp.exp(b)).astype(BF16)
        o = lax.dot_general(q_in, st.astype(BF16), _NT, preferred_element_type=F32)
        k_st = (k * jnp.exp(b_end - b)).astype(BF16)
        upd = lax.dot_general(v.astype(BF16), k_st, _TN, preferred_element_type=F32)
        st_ref[d] = st * jnp.exp(b_end) + jnp.where(st_mask, upd, 0.0)
        return o

    def fwd_body(n, carry):
        rows, q, k, v = load(n)
        b_f = _cumsum_rows(tril, la_ref[rows, 0:GLA_KW])
        b_b = _cumsum_rows(triu, la_ref[rows, GLA_KW:2 * GLA_KW])
        a = _gla_scores(q, k, b_f, False) + _gla_scores(q, k, b_b, True)
        o = _dot(a.astype(BF16), _bd_rows(v, GLA_HEADS, GLA_DV).astype(BF16))
        o = o + inter(0, q, k, v, b_f, b_f[c - 1:c, :])
        acc_ref[rows, :] = o
        return carry

    lax.fori_loop(0, nc, fwd_body, 0)

    def bwd_body(i, carry):
        n = nc - 1 - i
        rows, q, k, v = load(n)
        b_b = _cumsum_rows(triu, la_ref[rows, GLA_KW:2 * GLA_KW])
        acc_ref[rows, :] = acc_ref[rows, :] + inter(1, q, k, v, b_b, b_b[0:1, :])
        return carry

    lax.fori_loop(0, nc, bwd_body, 0)

    o = acc_ref[...]
    gn = gn_ref[...]
    for hh in range(GLA_HEADS):
        sl = slice(hh * GLA_DV, (hh + 1) * GLA_DV)
        o_ref[:, sl] = (_rms(o[:, sl]) * gn[:, sl] * gg_ref[:, sl].astype(F32)).astype(BF16)
    if not has_state:
        for d in range(2):
            for hh in range(GLA_HEADS):
                sfin_ref[d, hh] = st_ref[d, hh * GLA_DV:(hh + 1) * GLA_DV, hh * GLA_DK:(hh + 1) * GLA_DK].T


def _gla_scratch(seq_len):
    return [pltpu.VMEM((seq_len, GLA_VW), F32), pltpu.VMEM((2, GLA_VW, GLA_KW), F32)]


def _gla_ctx(gq, gk, gv, la, gg, gn):
    seq = lambda w: pl.BlockSpec((SEQ, w), lambda i: (i, 0))
    return pl.pallas_call(
        partial(_gla_kernel, seq_len=SEQ, has_state=False),
        grid=(BATCH,),
        in_specs=[seq(GLA_KW), seq(GLA_KW), seq(GLA_VW), seq(2 * GLA_KW), seq(GLA_VW),
                  pl.BlockSpec((1, GLA_VW), lambda i: (0, 0))],
        out_specs=[seq(GLA_VW),
                   pl.BlockSpec((None, 2, GLA_HEADS, GLA_DK, GLA_DV), lambda i: (i, 0, 0, 0, 0))],
        out_shape=[jax.ShapeDtypeStruct((N_CTX, GLA_VW), BF16),
                   jax.ShapeDtypeStruct((BATCH, 2, GLA_HEADS, GLA_DK, GLA_DV), F32)],
        scratch_shapes=_gla_scratch(SEQ),
        compiler_params=pltpu.CompilerParams(dimension_semantics=("parallel",), vmem_limit_bytes=VMEM_LIMIT),
        name="gla_ctx",
    )(gq, gk, gv, la, gg, gn)


def _gla_lat(layer, gq, gk, gv, la, gg, gn, s0):
    s0_blk = N_CTX // DEC_SEQ
    seq = lambda w: pl.BlockSpec((DEC_SEQ, w), lambda b: (s0_blk + b, 0))
    return pl.pallas_call(
        partial(_gla_kernel, seq_len=DEC_SEQ, has_state=True),
        grid=(DEC_BATCH,),
        in_specs=[seq(GLA_KW), seq(GLA_KW), seq(GLA_VW), seq(2 * GLA_KW), seq(GLA_VW),
                  pl.BlockSpec((1, GLA_VW), lambda b: (0, 0)),
                  pl.BlockSpec((None, None, 2, GLA_HEADS, GLA_DK, GLA_DV), lambda b: (b, layer, 0, 0, 0, 0))],
        out_specs=pl.BlockSpec((DEC_SEQ, GLA_VW), lambda b: (b, 0)),
        out_shape=jax.ShapeDtypeStruct((N_LAT, GLA_VW), BF16),
        scratch_shapes=_gla_scratch(DEC_SEQ),
        compiler_params=pltpu.CompilerParams(dimension_semantics=("parallel",), vmem_limit_bytes=VMEM_LIMIT),
        name="gla_lat",
    )(gq, gk, gv, la, gg, gn, s0)


def _route(scores, bias):
    biased = scores + bias
    v = [biased[e:e + 1, :] for e in range(N_EXPERTS)]
    s = [scores[e:e + 1, :] for e in range(N_EXPERTS)]
    gscore = []
    for g in range(N_GROUPS):
        a, b, c, d = v[4 * g:4 * g + 4]
        hi1, lo1, hi2, lo2 = jnp.maximum(a, b), jnp.minimum(a, b), jnp.maximum(c, d), jnp.minimum(c, d)
        gscore.append(jnp.maximum(hi1, hi2) + jnp.maximum(jnp.minimum(hi1, hi2), jnp.maximum(lo1, lo2)))
    one = lambda cond: jnp.where(cond, 1.0, 0.0)
    picked, chosen = [], []
    for g in range(N_GROUPS):
        best = None
        for g2 in range(N_GROUPS):
            if g2 == g:
                continue
            ok = one((gscore[g] > gscore[g2]) if g2 < g else (gscore[g] >= gscore[g2]))
            best = ok if best is None else best * ok
        for i in range(EXPERTS_PER_GROUP):
            e = 4 * g + i
            beaten = None
            for j in range(EXPERTS_PER_GROUP):
                if j == i:
                    continue
                e2 = 4 * g + j
                lost = one((v[e2] > v[e]) if j > i else (v[e2] >= v[e]))
                beaten = lost if beaten is None else beaten + lost
            chosen.append(best * one(beaten < 1.5))
            picked.append(chosen[-1] * s[e])
    den = picked[0]
    for e in range(1, N_EXPERTS):
        den = den + picked[e]
    return jnp.concatenate([p / den for p in picked] + chosen, axis=0)


def _mix_kernel(x_ref, oac_ref, oal_ref, ogc_ref, ogl_ref, sa_ref, sb_ref, mod_ref, wpa_f, wpb_f, wo_f, n2_ref,
                rw_ref, rb_ref, x1_out, h2_out, gt_out, wpa_ref, wpb_ref, wo_ref):
    @pl.when(pl.program_id(0) == 0)
    def _():
        wpa_ref[...] = wpa_f[...].astype(BF16)
        wpb_ref[...] = wpb_f[...].astype(BF16)
        wo_ref[...] = wo_f[...].astype(BF16)

    is_ctx = pl.program_id(0) < N_CTX // TM
    oa = jnp.where(is_ctx, oac_ref[...], oal_ref[...])
    og = jnp.where(is_ctx, ogc_ref[...], ogl_ref[...])
    a = _dot(oa, wpa_ref[...])
    b = _dot(og, wpb_ref[...])
    merged = sa_ref[...].astype(F32) * a + sb_ref[...].astype(F32) * b
    g1 = mod_ref[:, 2 * D_MODEL:3 * D_MODEL]
    sh2 = mod_ref[:, 3 * D_MODEL:4 * D_MODEL]
    sc2 = mod_ref[:, 4 * D_MODEL:5 * D_MODEL]
    x1 = x_ref[...] + g1 * _dot(merged.astype(BF16), wo_ref[...])
    x1_out[...] = x1
    h2 = _rms(x1) * n2_ref[...] * (1.0 + sc2) + sh2
    _to_row_tiles(h2_out, h2)
    hh, hl = _split2(h2)
    rh, rl = rw_ref[0], rw_ref[1]
    nt = lambda r, t: lax.dot_general(r, t, _NT, preferred_element_type=F32)
    logits = nt(rl, hh) + nt(rh, hl) + nt(rh, hh)
    gt_out[...] = _route(jax.nn.sigmoid(logits), rb_ref[...])


def _mix(layer, x, oa_c, oa_l, og_c, og_l, sa, sb, mod_l, wpa, wpb, wo, norm2, rw3, rb):
    nt = N_TOK // TM
    n_ctx_t = N_CTX // TM
    row = lambda i: (i, 0)
    const = lambda i: (0, 0)
    ctx_row = lambda i: (jnp.minimum(i, n_ctx_t - 1), 0)
    lat_row = lambda i: (jnp.maximum(i - n_ctx_t, 0), 0)
    return pl.pallas_call(
        _mix_kernel,
        grid=(nt,),
        in_specs=[
            pl.BlockSpec((TM, D_MODEL), row),
            pl.BlockSpec((TM, ATTN_WIDTH), ctx_row),
            pl.BlockSpec((TM, ATTN_WIDTH), lat_row),
            pl.BlockSpec((TM, GLA_VW), ctx_row),
            pl.BlockSpec((TM, GLA_VW), lat_row),
            pl.BlockSpec((TM, D_MODEL), row),
            pl.BlockSpec((TM, D_MODEL), row),
            pl.BlockSpec((None, 1, MOD_W), lambda i: (_mod_row(i, TM), 0, 0)),
            pl.BlockSpec((None, ATTN_WIDTH, D_MODEL), lambda i: (layer, 0, 0), pipeline_mode=pl.Buffered(1)),
            pl.BlockSpec((None, GLA_VW, D_MODEL), lambda i: (layer, 0, 0), pipeline_mode=pl.Buffered(1)),
            pl.BlockSpec((None, D_MODEL, D_MODEL), lambda i: (layer, 0, 0), pipeline_mode=pl.Buffered(1)),
            pl.BlockSpec((1, D_MODEL), const),
            pl.BlockSpec((2, N_EXPERTS, D_MODEL), lambda i: (0, 0, 0)),
            pl.BlockSpec((N_EXPERTS, 1), const),
        ],
        out_specs=[pl.BlockSpec((TM, D_MODEL), row), pl.BlockSpec((TM * ROW_TILE, LANES), row),
                   pl.BlockSpec((2 * N_EXPERTS, TM), lambda i: (0, i))],
        out_shape=[jax.ShapeDtypeStruct((N_TOK, D_MODEL), F32),
                   jax.ShapeDtypeStruct((N_TOK * ROW_TILE, LANES), F32),
                   jax.ShapeDtypeStruct((2 * N_EXPERTS, N_TOK), F32)],
        scratch_shapes=[pltpu.VMEM((ATTN_WIDTH, D_MODEL), BF16), pltpu.VMEM((GLA_VW, D_MODEL), BF16),
                        pltpu.VMEM((D_MODEL, D_MODEL), BF16)],
        compiler_params=pltpu.CompilerParams(dimension_semantics=("arbitrary",), vmem_limit_bytes=VMEM_LIMIT),
        name="mix",
    )(x, oa_c, oa_l, og_c, og_l, sa, sb, mod_l, wpa, wpb, wo, norm2, rw3, rb)


def _route_plan(gsel):
    gates = gsel[:N_EXPERTS]
    sel = gsel[N_EXPERTS:] > 0.5
    seli = sel.astype(jnp.int32)
    incl = jnp.cumsum(seli, axis=1)
    count = incl[:, -1]
    ntile = (count + TS - 1) // TS
    tile_end = jnp.cumsum(ntile)
    tile_start = tile_end - ntile
    total = tile_end[-1]
    slot = tile_start[:, None] * TS + incl - seli
    pos_a = jnp.min(jnp.where(sel, slot, N_SLOT), axis=0)
    pos_b = jnp.max(jnp.where(sel, slot, -1), axis=0)
    w_a = jnp.sum(jnp.where(sel & (slot == pos_a[None, :]), gates, 0.0), axis=0)
    w_b = jnp.sum(jnp.where(sel & (slot == pos_b[None, :]), gates, 0.0), axis=0)
    tile_id = jnp.minimum(jnp.arange(N_TILE, dtype=jnp.int32), total - 1)
    tile_expert = jnp.sum((tile_id[:, None] >= tile_end[None, :]).astype(jnp.int32), axis=1)
    nt = N_TOK // TM
    pos = jnp.concatenate([pos_a.reshape(nt, 1, TM), pos_b.reshape(nt, 1, TM)], axis=2).astype(jnp.int32)
    plan = jnp.concatenate([tile_start, ntile, total[None]]).astype(jnp.int32)
    return pos, jnp.stack([w_a, w_b], axis=1), tile_expert.astype(jnp.int32), plan


def _to_row_tiles(ref, x):
    for k in range(ROW_TILE):
        ref[pl.ds(k, x.shape[0], stride=ROW_TILE), :] = x[:, k * LANES:(k + 1) * LANES]


def _from_row_tiles(ref, rows):
    return jnp.concatenate([ref[pl.ds(k, rows, stride=ROW_TILE), :] for k in range(ROW_TILE)], axis=1)


def _tile_rows(row, n=1):
    return pl.ds(pl.multiple_of(row * ROW_TILE, ROW_TILE), n * ROW_TILE)


def _row_copy(src, src_row, dst, dst_row, sem):
    return pltpu.make_async_copy(src.at[_tile_rows(src_row), :], dst.at[_tile_rows(dst_row), :], sem)


def _dispatch_kernel(plan_ref, pos_ref, h_ref, xs_ref, zero_ref, sem_z, sem):
    i = pl.program_id(0)
    total = plan_ref[2 * N_EXPERTS]

    def zero_copy(tile):
        return pltpu.make_async_copy(zero_ref, xs_ref.at[_tile_rows(tile * TS, TS), :], sem_z)

    def last_tile(e):
        return plan_ref[e] + plan_ref[N_EXPERTS + e] - 1

    @pl.when(i == 0)
    def _():
        zero_ref[...] = jnp.zeros(zero_ref.shape, F32)

        def each(fn):
            def expert(e, c):
                @pl.when(plan_ref[N_EXPERTS + e] > 0)
                def _():
                    fn(zero_copy(last_tile(e)))
                return c

            def unused(t, c):
                @pl.when(t >= total)
                def _():
                    fn(zero_copy(t))
                return c

            lax.fori_loop(0, N_EXPERTS, expert, 0)
            lax.fori_loop(0, N_TILE, unused, 0)

        each(lambda cp: cp.start())
        each(lambda cp: cp.wait())

    def start(r, c):
        _row_copy(h_ref, r, xs_ref, pos_ref[0, r], sem).start()
        _row_copy(h_ref, r, xs_ref, pos_ref[0, TM + r], sem).start()
        return c

    def wait(r, c):
        _row_copy(h_ref, 0, xs_ref, 0, sem).wait()
        return c

    lax.fori_loop(0, TM, start, 0, unroll=8)
    lax.fori_loop(0, 2 * TM, wait, 0, unroll=8)


def _dispatch(plan, pos, h2):
    return pl.pallas_call(
        _dispatch_kernel,
        grid_spec=pltpu.PrefetchScalarGridSpec(
            num_scalar_prefetch=1,
            grid=(N_TOK // TM,),
            in_specs=[
                pl.BlockSpec((None, 1, 2 * TM), lambda i, plan: (i, 0, 0), memory_space=pltpu.SMEM),
                pl.BlockSpec((TM * ROW_TILE, LANES), lambda i, plan: (i, 0)),
            ],
            out_specs=pl.BlockSpec(memory_space=pl.ANY),
            scratch_shapes=[pltpu.VMEM((TS * ROW_TILE, LANES), F32), pltpu.SemaphoreType.DMA,
                            pltpu.SemaphoreType.DMA],
        ),
        out_shape=jax.ShapeDtypeStruct((N_SLOT * ROW_TILE, LANES), F32),
        compiler_params=pltpu.CompilerParams(dimension_semantics=("arbitrary",), vmem_limit_bytes=VMEM_LIMIT),
        name="dispatch",
    )(plan, pos, h2)


def _moe_kernel(te_ref, plan_ref, xs_ref, wg_ref, wu_ref, wd_ref, ys_ref, wg_s, wu_s, wd_s):
    i = pl.program_id(0)
    total = plan_ref[2 * N_EXPERTS]

    @pl.when(i < total)
    def _():
        @pl.when((i == 0) | (te_ref[i] != te_ref[jnp.maximum(i - 1, 0)]))
        def _():
            wg_s[...] = wg_ref[...].astype(BF16)
            wu_s[...] = wu_ref[...].astype(BF16)
            wd_s[...] = wd_ref[...].astype(BF16)

        x = _from_row_tiles(xs_ref, TS).astype(BF16)
        up = _dot(x, wu_s[...])
        gt = _dot(x, wg_s[...])
        act = (gt * jax.nn.sigmoid(gt) * up).astype(BF16)
        _to_row_tiles(ys_ref, _dot(act, wd_s[...]))

    @pl.when(i >= total)
    def _():
        ys_ref[...] = jnp.zeros(ys_ref.shape, F32)


def _moe(layer, tile_expert, plan, xs, w_gate, w_up, w_down):
    last = lambda i, plan: jnp.minimum(i, plan[2 * N_EXPERTS] - 1)
    w_spec = lambda a, b: pl.BlockSpec((None, None, a, b), lambda i, te, plan: (layer, te[i], 0, 0))
    return pl.pallas_call(
        _moe_kernel,
        grid_spec=pltpu.PrefetchScalarGridSpec(
            num_scalar_prefetch=2,
            grid=(N_TILE,),
            in_specs=[
                pl.BlockSpec((TS * ROW_TILE, LANES), lambda i, te, plan: (last(i, plan), 0)),
                w_spec(D_MODEL, D_EXPERT), w_spec(D_MODEL, D_EXPERT), w_spec(D_EXPERT, D_MODEL),
            ],
            out_specs=pl.BlockSpec((TS * ROW_TILE, LANES), lambda i, te, plan: (i, 0)),
            scratch_shapes=[pltpu.VMEM((D_MODEL, D_EXPERT), BF16), pltpu.VMEM((D_MODEL, D_EXPERT), BF16),
                            pltpu.VMEM((D_EXPERT, D_MODEL), BF16)],
        ),
        out_shape=jax.ShapeDtypeStruct((N_SLOT * ROW_TILE, LANES), F32),
        compiler_params=pltpu.CompilerParams(dimension_semantics=("arbitrary",), vmem_limit_bytes=VMEM_LIMIT),
        name="moe",
    )(tile_expert, plan, xs, w_gate, w_up, w_down)


def _combine_kernel(pos_ref, nxt_ref, w_ref, x1_ref, mod_ref, ys_ref, out_ref, ya_ref, yb_ref, sem):
    i = pl.program_id(0)
    n = pl.num_programs(0)

    def gather(p_ref, slot):
        def start(r, c):
            _row_copy(ys_ref, p_ref[0, r], ya_ref.at[slot], r, sem.at[slot]).start()
            _row_copy(ys_ref, p_ref[0, TM + r], yb_ref.at[slot], r, sem.at[slot]).start()
            return c
        lax.fori_loop(0, TM, start, 0, unroll=8)

    @pl.when(i == 0)
    def _():
        gather(pos_ref, 0)

    @pl.when(i + 1 < n)
    def _():
        gather(nxt_ref, (i + 1) % 2)

    slot = i % 2

    def wait(r, c):
        _row_copy(ys_ref, 0, ya_ref.at[slot], 0, sem.at[slot]).wait()
        return c

    lax.fori_loop(0, 2 * TM, wait, 0, unroll=8)
    g2 = mod_ref[:, 5 * D_MODEL:6 * D_MODEL]
    w = w_ref[...]
    ya = _from_row_tiles(ya_ref.at[slot], TM)
    yb = _from_row_tiles(yb_ref.at[slot], TM)
    out_ref[...] = x1_ref[...] + g2 * (w[:, 0:1] * ya + w[:, 1:2] * yb)


def _combine(pos, w_ab, x1, mod_l, ys):
    nt = N_TOK // TM
    pos_spec = lambda fn: pl.BlockSpec((None, 1, 2 * TM), fn, memory_space=pltpu.SMEM)
    return pl.pallas_call(
        _combine_kernel,
        grid=(nt,),
        in_specs=[
            pos_spec(lambda i: (i, 0, 0)),
            pos_spec(lambda i: (jnp.minimum(i + 1, nt - 1), 0, 0)),
            pl.BlockSpec((TM, 2), lambda i: (i, 0)),
            pl.BlockSpec((TM, D_MODEL), lambda i: (i, 0)),
            pl.BlockSpec((None, 1, MOD_W), lambda i: (_mod_row(i, TM), 0, 0)),
            pl.BlockSpec(memory_space=pl.ANY),
        ],
        out_specs=pl.BlockSpec((TM, D_MODEL), lambda i: (i, 0)),
        out_shape=jax.ShapeDtypeStruct((N_TOK, D_MODEL), F32),
        scratch_shapes=[pltpu.VMEM((2, TM * ROW_TILE, LANES), F32), pltpu.VMEM((2, TM * ROW_TILE, LANES), F32),
                        pltpu.SemaphoreType.DMA((2,))],
        compiler_params=pltpu.CompilerParams(dimension_semantics=("arbitrary",), vmem_limit_bytes=VMEM_LIMIT),
        name="combine",
    )(pos, pos, w_ab, x1, mod_l, ys)


def _rope_tables():
    pos = np.arange(DEC_SEQ)
    rows = (pos // GRID_W).astype(np.float64)
    cols = (pos % GRID_W).astype(np.float64)
    half = HEAD_DIM // 2
    inv_freq = ROPE_THETA ** (-np.arange(0, half, 2, dtype=np.float64) / half)
    ang_r = rows[:, None] * inv_freq[None, :]
    ang_c = cols[:, None] * inv_freq[None, :]
    cos_h = np.concatenate([np.cos(ang_r), np.cos(ang_r), np.cos(ang_c), np.cos(ang_c)], axis=-1)
    sin_h = np.concatenate([-np.sin(ang_r), np.sin(ang_r), -np.sin(ang_c), np.sin(ang_c)], axis=-1)
    cos = np.concatenate([np.ones((TM, HEAD_DIM)), cos_h], axis=0)
    sin = np.concatenate([np.zeros((TM, HEAD_DIM)), sin_h], axis=0)
    return (jnp.asarray(np.tile(cos, (1, N_HEADS)), F32), jnp.asarray(np.tile(sin, (1, N_HEADS)), F32))


def _pack_w2(w2):
    z = jnp.zeros((GLA_RANK, GLA_KW), w2.dtype)
    top = jnp.concatenate([w2[0], z], axis=1)
    bot = jnp.concatenate([z, w2[1]], axis=1)
    pad = jnp.zeros((LR_PAD - 2 * GLA_RANK, 2 * GLA_KW), w2.dtype)
    return jnp.concatenate([top, bot, pad], axis=0).astype(BF16)


def kernel(x_prompt, x_sample, cache_k, cache_v, state_gla, c, c_ctx, norm1, norm2, w_ada, b_ada, w_in,
           q_norm, k_norm, gla_w2, gla_b, gla_norm, w_pa, w_pb, w_out, router_w, router_bias,
           w_gate, w_up, w_down):
    x = jnp.concatenate([x_prompt.reshape(N_CTX, D_MODEL), x_sample.reshape(N_LAT, D_MODEL)], axis=0)
    cond = jnp.concatenate([c_ctx[None, :], c, jnp.zeros((N_COND - 1 - DEC_BATCH, D_MODEL), F32)], axis=0)
    mod = _ada(cond, w_ada, b_ada).reshape(DEPTH, N_COND, 1, MOD_W)
    cos, sin = _rope_tables()
    lane_head = np.arange(ATTN_WIDTH) // HEAD_DIM
    bd = jnp.asarray(lane_head[:, None] == lane_head[None, :], BF16)
    rw3 = jnp.stack(_split2(router_w.T), axis=0)
    rb = router_bias.reshape(N_EXPERTS, 1)
    ck = cache_k.reshape(DEC_BATCH, DEPTH, PAST_LEN, KV_WIDTH)
    cv = cache_v.reshape(DEC_BATCH, DEPTH, PAST_LEN, KV_WIDTH)

    keys, vals, states = [], [], []
    for l in range(DEPTH):
        q, kf, kr, v, gq, gk, gv, gg, la, sa, sb = _inproj(
            l, x, mod[l], norm1[l][None, :], w_in, jnp.tile(q_norm[l], N_HEADS)[None, :],
            jnp.tile(k_norm[l], KV_HEADS)[None, :], cos, sin, bd, _pack_w2(gla_w2[l]),
            gla_b[l].reshape(1, 2 * GLA_KW))
        oa_c = _attn_ctx(q, kr, v)
        oa_l = _attn_lat(l, q, kr, v, ck, cv)
        gn = jnp.tile(gla_norm[l], GLA_HEADS)[None, :]
        og_c, s_fin = _gla_ctx(gq, gk, gv, la, gg, gn)
        og_l = _gla_lat(l, gq, gk, gv, la, gg, gn, state_gla)
        x1, h2, gsel = _mix(l, x, oa_c, oa_l, og_c, og_l, sa, sb, mod[l], w_pa, w_pb, w_out,
                            norm2[l][None, :], rw3, rb)
        pos, w_ab, tile_expert, plan = _route_plan(gsel)
        xs = _dispatch(plan, pos, h2)
        ys = _moe(l, tile_expert, plan, xs, w_gate, w_up, w_down)
        x = _combine(pos, w_ab, x1, mod[l], ys)
        keys.append(kf[:N_CTX].reshape(BATCH, SEQ, KV_HEADS, HEAD_DIM))
        vals.append(v[:N_CTX].reshape(BATCH, SEQ, KV_HEADS, HEAD_DIM))
        states.append(s_fin)
    y_prompt = x[:N_CTX].reshape(BATCH, SEQ, D_MODEL)
    y_sample = x[N_CTX:].reshape(DEC_BATCH, DEC_SEQ, D_MODEL)
    return (y_prompt, y_sample, jnp.stack(keys, axis=1), jnp.stack(vals, axis=1), jnp.stack(states, axis=1))
```

```python
from functools import partial

import jax
import jax.numpy as jnp
import numpy as np
from jax import lax
from jax.experimental import pallas as pl
from jax.experimental.pallas import tpu as pltpu

F32 = jnp.float32
BF16 = jnp.bfloat16

D_MODEL = 1024
BATCH = 16
SEQ = 256
DEPTH = 2
DEC_BATCH = 4
DEC_SEQ = 1024
PAST_LEN = 512
GRID_W = 64
N_HEADS = 8
KV_HEADS = 2
GROUP = N_HEADS // KV_HEADS
HEAD_DIM = 64
ATTN_WIDTH = N_HEADS * HEAD_DIM
KV_WIDTH = KV_HEADS * HEAD_DIM
ROPE_THETA = 10000.0
GLA_HEADS = 4
GLA_DK = 64
GLA_DV = 128
GLA_KW = GLA_HEADS * GLA_DK
GLA_VW = GLA_HEADS * GLA_DV
GLA_RANK = 16
GLA_GATE_NORM = 16.0
N_EXPERTS = 16
N_GROUPS = 4
EXPERTS_PER_GROUP = N_EXPERTS // N_GROUPS
D_EXPERT = 512
NORM_EPS = 1e-6

N_CTX = BATCH * SEQ
N_LAT = DEC_BATCH * DEC_SEQ
N_TOK = N_CTX + N_LAT
N_COND = 8
MOD_W = 6 * D_MODEL

TM = 512
TS = 512
N_TILE = 2 * N_TOK // TS + N_EXPERTS
N_SLOT = N_TILE * TS
ROW_TILE = 8
LANES = D_MODEL // ROW_TILE
GLA_CHUNK = 64
GLA_LEVELS = (32, 16, 8)
GLA_DIAG = 8
LR_PAD = 128
SEG_MIX = ATTN_WIDTH + 2 * KV_WIDTH + 2 * GLA_KW + 2 * GLA_VW
OFF_GA = SEG_MIX
OFF_GB = OFF_GA + D_MODEL
OFF_LR = OFF_GB + D_MODEL
IN_PACKED = OFF_LR + LR_PAD
IN_WIDTH = SEG_MIX + 2 * GLA_RANK + 2 * D_MODEL
VMEM_LIMIT = 56 * 1024 * 1024

_NT = (((1,), (1,)), ((), ()))
_TN = (((0,), (0,)), ((), ()))


def _mod_row(tile, tm):
    start = tile * tm
    return jnp.where(start < N_CTX, 0, 1 + (start - N_CTX) // DEC_SEQ)


def _split3(x):
    hi = x.astype(BF16)
    r1 = x - hi.astype(F32)
    mid = r1.astype(BF16)
    lo = (r1 - mid.astype(F32)).astype(BF16)
    return hi, mid, lo


def _split2(x):
    hi = x.astype(BF16)
    lo = (x - hi.astype(F32)).astype(BF16)
    return hi, lo


def _dot(a, b):
    return jnp.dot(a, b, preferred_element_type=F32)


def _rms(x):
    return x * lax.rsqrt(jnp.mean(x * x, axis=-1, keepdims=True) + NORM_EPS)


def _ada_kernel(cond_ref, w_ref, b_ref, out_ref):
    c = cond_ref[...]
    s = c * jax.nn.sigmoid(c)
    out_ref[...] = _dot(s.astype(BF16), w_ref[...].astype(BF16)) + b_ref[...]


def _ada(cond, w_ada, b_ada):
    nb = MOD_W // D_MODEL
    return pl.pallas_call(
        _ada_kernel,
        grid=(DEPTH, nb),
        in_specs=[
            pl.BlockSpec((N_COND, D_MODEL), lambda l, j: (0, 0)),
            pl.BlockSpec((None, D_MODEL, D_MODEL), lambda l, j: (l, 0, j)),
            pl.BlockSpec((None, 1, D_MODEL), lambda l, j: (l, 0, j)),
        ],
        out_specs=pl.BlockSpec((None, N_COND, D_MODEL), lambda l, j: (l, 0, j)),
        out_shape=jax.ShapeDtypeStruct((DEPTH, N_COND, MOD_W), F32),
        compiler_params=pltpu.CompilerParams(vmem_limit_bytes=VMEM_LIMIT),
        name="ada",
    )(cond, w_ada, b_ada.reshape(DEPTH, 1, MOD_W))


def _head_rmsnorm(x, bd):
    hi, lo = _split2(x * x)
    ssq = _dot(hi, bd) + _dot(lo, bd)
    return x * lax.rsqrt(ssq * (1.0 / HEAD_DIM) + NORM_EPS)


def _rope(x, cos, sin):
    n = x.shape[-1]
    lane = lax.broadcasted_iota(jnp.int32, x.shape, 1)
    first = (lane & (HEAD_DIM // 4)) == 0
    partner = jnp.where(first, pltpu.roll(x, n - HEAD_DIM // 4, 1), pltpu.roll(x, HEAD_DIM // 4, 1))
    return x * cos + partner * sin


def _pack_w_in(w_ref, wp_ref):
    lr0 = SEG_MIX
    ga0 = lr0 + 2 * GLA_RANK
    rows_per_step = 128

    def body(i, carry):
        rows = pl.ds(pl.multiple_of(i * rows_per_step, rows_per_step), rows_per_step)
        wp_ref[rows, 0:SEG_MIX] = w_ref[rows, 0:SEG_MIX].astype(BF16)
        tail = w_ref[rows, SEG_MIX:IN_WIDTH]
        lr = tail[:, 0:2 * GLA_RANK]
        wp_ref[rows, OFF_LR:IN_PACKED] = jnp.concatenate(
            [lr, jnp.zeros((rows_per_step, LR_PAD - 2 * GLA_RANK), F32)], axis=1).astype(BF16)
        wp_ref[rows, OFF_GA:OFF_LR] = tail[:, ga0 - lr0:ga0 - lr0 + 2 * D_MODEL].astype(BF16)
        return carry

    lax.fori_loop(0, D_MODEL // rows_per_step, body, 0)


def _inproj_kernel(xc_ref, xl_ref, mod_ref, n1_ref, wf_ref, qn_ref, kn_ref, cos_ref, sin_ref, bd_ref, w2_ref, gb_ref,
                   q_out, kf_out, kr_out, v_out, gq_out, gk_out, gv_out, gg_out, la_out, sa_out, sb_out,
                   w_ref):
    @pl.when(pl.program_id(0) == 0)
    def _():
        _pack_w_in(wf_ref, w_ref)

    x = jnp.where(pl.program_id(0) < N_CTX // TM, xc_ref[...], xl_ref[...])
    sh1 = mod_ref[:, 0:D_MODEL]
    sc1 = mod_ref[:, D_MODEL:2 * D_MODEL]
    h = (_rms(x) * n1_ref[...] * (1.0 + sc1) + sh1).astype(BF16)

    def proj(lo, width):
        return _dot(h, w_ref[:, lo:lo + width])

    cos = cos_ref[...]
    sin = sin_ref[...]
    bd = bd_ref[...]
    off = 0
    q = _head_rmsnorm(proj(off, ATTN_WIDTH), bd) * qn_ref[...]
    q_out[...] = (_rope(q, cos, sin) * HEAD_DIM ** -0.5).astype(BF16)
    off += ATTN_WIDTH
    k = _head_rmsnorm(proj(off, KV_WIDTH), bd[:KV_WIDTH, :KV_WIDTH]) * kn_ref[...]
    kf_out[...] = k
    kr_out[...] = _rope(k, cos[:, :KV_WIDTH], sin[:, :KV_WIDTH]).astype(BF16)
    off += KV_WIDTH
    v_out[...] = proj(off, KV_WIDTH)
    off += KV_WIDTH
    gq_out[...] = (proj(off, GLA_KW) * GLA_DK ** -0.5).astype(BF16)
    off += GLA_KW
    gk_out[...] = proj(off, GLA_KW).astype(BF16)
    off += GLA_KW
    gv_out[...] = proj(off, GLA_VW).astype(BF16)
    off += GLA_VW
    gg = proj(off, GLA_VW)
    gg_out[...] = (gg * jax.nn.sigmoid(gg)).astype(BF16)
    sa_out[...] = jax.nn.sigmoid(proj(OFF_GA, D_MODEL)).astype(BF16)
    sb_out[...] = jax.nn.sigmoid(proj(OFF_GB, D_MODEL)).astype(BF16)
    lr = proj(OFF_LR, LR_PAD)
    z = _dot(lr.astype(BF16), w2_ref[...]) + gb_ref[...]
    log_sig = jnp.minimum(z, 0.0) - jnp.log1p(jnp.exp(-jnp.abs(z)))
    la_out[...] = log_sig * (1.0 / GLA_GATE_NORM)


def _x_specs(x_pair):
    n_ctx_t = N_CTX // TM
    lat0 = n_ctx_t if x_pair[0] is x_pair[1] else 0
    return [pl.BlockSpec((TM, D_MODEL), lambda i: (jnp.minimum(i, n_ctx_t - 1), 0)),
            pl.BlockSpec((TM, D_MODEL), lambda i: (lat0 + jnp.maximum(i - n_ctx_t, 0), 0))]


def _inproj(layer, x_pair, mod_l, norm1, w_in, qn, kn, cos, sin, bd, w2p, gb):
    nt = N_TOK // TM
    n_ctx_t = N_CTX // TM
    t_per_seq = DEC_SEQ // TM
    row = lambda i: (i, 0)
    const = lambda i: (0, 0)
    rope_blk = lambda i: (jnp.where(i < n_ctx_t, 0, 1 + (i - n_ctx_t) % t_per_seq), 0)

    def out(width, dtype):
        return pl.BlockSpec((TM, width), row), jax.ShapeDtypeStruct((N_TOK, width), dtype)

    outs = [out(ATTN_WIDTH, BF16), out(KV_WIDTH, F32), out(KV_WIDTH, BF16), out(KV_WIDTH, F32),
            out(GLA_KW, BF16), out(GLA_KW, BF16), out(GLA_VW, BF16), out(GLA_VW, BF16),
            out(2 * GLA_KW, F32), out(D_MODEL, BF16), out(D_MODEL, BF16)]
    return pl.pallas_call(
        _inproj_kernel,
        grid=(nt,),
        in_specs=_x_specs(x_pair) + [
            pl.BlockSpec((None, 1, MOD_W), lambda i: (_mod_row(i, TM), 0, 0)),
            pl.BlockSpec((1, D_MODEL), const),
            pl.BlockSpec((None, D_MODEL, IN_WIDTH), lambda i: (layer, 0, 0), pipeline_mode=pl.Buffered(1)),
            pl.BlockSpec((1, ATTN_WIDTH), const),
            pl.BlockSpec((1, KV_WIDTH), const),
            pl.BlockSpec((TM, ATTN_WIDTH), rope_blk),
            pl.BlockSpec((TM, ATTN_WIDTH), rope_blk),
            pl.BlockSpec((ATTN_WIDTH, ATTN_WIDTH), const),
            pl.BlockSpec((LR_PAD, 2 * GLA_KW), const),
            pl.BlockSpec((1, 2 * GLA_KW), const),
        ],
        out_specs=[o[0] for o in outs],
        out_shape=[o[1] for o in outs],
        scratch_shapes=[pltpu.VMEM((D_MODEL, IN_PACKED), BF16)],
        compiler_params=pltpu.CompilerParams(dimension_semantics=("arbitrary",), vmem_limit_bytes=VMEM_LIMIT),
        name="inproj",
    )(*x_pair, mod_l, norm1, w_in, qn, kn, cos, sin, bd, w2p, gb)


def _attn_kernel(*refs, has_cache):
    if has_cache:
        q_ref, k_ref, v_ref, ck_ref, cv_ref, o_ref = refs
    else:
        q_ref, k_ref, v_ref, o_ref = refs
    tq = q_ref.shape[0]
    for g in range(KV_HEADS):
        ks = slice(g * HEAD_DIM, (g + 1) * HEAD_DIM)
        qs = jnp.concatenate(
            [q_ref[:, (g * GROUP + j) * HEAD_DIM:(g * GROUP + j + 1) * HEAD_DIM] for j in range(GROUP)], axis=0)
        k = k_ref[:, ks]
        v = v_ref[:, ks].astype(BF16)
        s = lax.dot_general(qs, k, _NT, preferred_element_type=F32)
        m = jnp.max(s, axis=-1, keepdims=True)
        if has_cache:
            ck = ck_ref[:, ks].astype(BF16)
            cv = cv_ref[:, ks].astype(BF16)
            s2 = lax.dot_general(qs, ck, _NT, preferred_element_type=F32)
            m = jnp.maximum(m, jnp.max(s2, axis=-1, keepdims=True))
        p = jnp.exp(s - m)
        den = jnp.sum(p, axis=-1, keepdims=True)
        o = _dot(p.astype(BF16), v)
        if has_cache:
            p2 = jnp.exp(s2 - m)
            den = den + jnp.sum(p2, axis=-1, keepdims=True)
            o = o + _dot(p2.astype(BF16), cv)
        o = o / den
        for j in range(GROUP):
            hd = g * GROUP + j
            o_ref[:, hd * HEAD_DIM:(hd + 1) * HEAD_DIM] = o[j * tq:(j + 1) * tq, :].astype(BF16)


def _attn_ctx(q, kr, v):
    return pl.pallas_call(
        partial(_attn_kernel, has_cache=False),
        grid=(BATCH,),
        in_specs=[
            pl.BlockSpec((SEQ, ATTN_WIDTH), lambda i: (i, 0)),
            pl.BlockSpec((SEQ, KV_WIDTH), lambda i: (i, 0)),
            pl.BlockSpec((SEQ, KV_WIDTH), lambda i: (i, 0)),
        ],
        out_specs=pl.BlockSpec((SEQ, ATTN_WIDTH), lambda i: (i, 0)),
        out_shape=jax.ShapeDtypeStruct((N_CTX, ATTN_WIDTH), BF16),
        compiler_params=pltpu.CompilerParams(dimension_semantics=("parallel",), vmem_limit_bytes=VMEM_LIMIT),
        name="attn_ctx",
    )(q, kr, v)


def _attn_lat(layer, q, kr, v, cache_k, cache_v):
    tq = 256
    nq = DEC_SEQ // tq
    q0 = N_CTX // tq
    s0 = N_CTX // DEC_SEQ
    return pl.pallas_call(
        partial(_attn_kernel, has_cache=True),
        grid=(DEC_BATCH, nq),
        in_specs=[
            pl.BlockSpec((tq, ATTN_WIDTH), lambda b, i: (q0 + b * nq + i, 0)),
            pl.BlockSpec((DEC_SEQ, KV_WIDTH), lambda b, i: (s0 + b, 0)),
            pl.BlockSpec((DEC_SEQ, KV_WIDTH), lambda b, i: (s0 + b, 0)),
            pl.BlockSpec((None, None, PAST_LEN, KV_WIDTH), lambda b, i: (b, layer, 0, 0)),
            pl.BlockSpec((None, None, PAST_LEN, KV_WIDTH), lambda b, i: (b, layer, 0, 0)),
        ],
        out_specs=pl.BlockSpec((tq, ATTN_WIDTH), lambda b, i: (b * nq + i, 0)),
        out_shape=jax.ShapeDtypeStruct((N_LAT, ATTN_WIDTH), BF16),
        compiler_params=pltpu.CompilerParams(dimension_semantics=("parallel", "parallel"),
                                             vmem_limit_bytes=VMEM_LIMIT),
        name="attn_lat",
    )(q, kr, v, cache_k, cache_v)


def _bd_rows(x, n_heads, width):
    head = lax.broadcasted_iota(jnp.int32, x.shape, 1) // width
    return jnp.concatenate([jnp.where(head == hh, x, 0.0) for hh in range(n_heads)], axis=0)


def _ref_rows(b, rows, rep):
    return jnp.concatenate([jnp.broadcast_to(b[r:r + 1, :], (rep, b.shape[1])) for r in rows], axis=0)


def _gla_scores(q, k, b, reverse):
    c = GLA_CHUNK
    t_idx = lax.broadcasted_iota(jnp.int32, (c, GLA_KW), 0)
    a_t = lax.broadcasted_iota(jnp.int32, (c, GLA_HEADS * c), 0)
    a_s = lax.broadcasted_iota(jnp.int32, (c, GLA_HEADS * c), 1) & (c - 1)
    total = jnp.zeros((c, GLA_HEADS * c), F32)

    def scores(qt, kt):
        return lax.dot_general(qt.astype(BF16), _bd_rows(kt, GLA_HEADS, GLA_DK).astype(BF16), _NT,
                               preferred_element_type=F32)

    for m in GLA_LEVELS:
        nblk = c // (2 * m)
        ref_rows = [2 * m * j + (m if reverse else m - 1) for j in range(nblk)]
        ref = _ref_rows(b, ref_rows, 2 * m)
        q_side = ((t_idx & m) == 0) if reverse else ((t_idx & m) != 0)
        qt = jnp.where(q_side, q * jnp.exp(jnp.minimum(b - ref, 0.0)), 0.0)
        kt = jnp.where(q_side, 0.0, k * jnp.exp(jnp.minimum(ref - b, 0.0)))
        shift = int(np.log2(2 * m))
        same = (a_t >> shift) == (a_s >> shift)
        total = total + jnp.where(same, scores(qt, kt), 0.0)
    d = GLA_DIAG
    ref = _ref_rows(b, [d * j + d // 2 for j in range(c // d)], d)
    x = b - ref
    shift = int(np.log2(d))
    same = (a_t >> shift) == (a_s >> shift)
    tri = (a_s >= a_t) if reverse else (a_s <= a_t)
    total = total + jnp.where(same, jnp.where(tri, scores(q * jnp.exp(x), k * jnp.exp(-x)), 0.0), 0.0)
    return total


def _cumsum_rows(tri, la):
    hi, mid, lo = _split3(la)
    return _dot(tri, hi) + _dot(tri, mid) + _dot(tri, lo)


def _gla_kernel(*refs, seq_len, has_state):
    if has_state:
        gq_ref, gk_ref, gv_ref, la_ref, gg_ref, gn_ref, s0_ref, o_ref, acc_ref, accb_ref, st_ref = refs
    else:
        gq_ref, gk_ref, gv_ref, la_ref, gg_ref, gn_ref, o_ref, sfin_ref, acc_ref, accb_ref, st_ref = refs
    c = GLA_CHUNK
    nc = seq_len // c
    r_i = lax.broadcasted_iota(jnp.int32, (c, c), 0)
    c_i = lax.broadcasted_iota(jnp.int32, (c, c), 1)
    tril = jnp.where(c_i <= r_i, 1.0, 0.0).astype(BF16)
    triu = jnp.where(c_i >= r_i, 1.0, 0.0).astype(BF16)
    st_head_r = lax.broadcasted_iota(jnp.int32, (GLA_VW, GLA_KW), 0) // GLA_DV
    st_head_c = lax.broadcasted_iota(jnp.int32, (GLA_VW, GLA_KW), 1) // GLA_DK
    st_mask = st_head_r == st_head_c

    for d in range(2):
        if has_state:
            st_ref[d] = jnp.zeros((GLA_VW, GLA_KW), F32)
            for hh in range(GLA_HEADS):
                st_ref[d, hh * GLA_DV:(hh + 1) * GLA_DV, hh * GLA_DK:(hh + 1) * GLA_DK] = s0_ref[d, hh].T
        else:
            st_ref[d] = jnp.zeros((GLA_VW, GLA_KW), F32)

    def load(n):
        rows = pl.ds(pl.multiple_of(n * c, c), c)
        return (rows, gq_ref[rows, :].astype(F32), gk_ref[rows, :].astype(F32), gv_ref[rows, :].astype(F32))

    def inter(d, q, k, v, b, b_end):
        st = st_ref[d]
        q_in = (q * jnp.exp(b)).astype(BF16)
        o = lax.dot_general(q_in, st.astype(BF16), _NT, preferred_element_type=F32)
        k_st = (k * jnp.exp(b_end - b)).astype(BF16)
        upd = lax.dot_general(v.astype(BF16), k_st, _TN, preferred_element_type=F32)
        st_ref[d] = st * jnp.exp(b_end) + jnp.where(st_mask, upd, 0.0)
        return o

    def body(n, carry):
        rows, q, k, v = load(n)
        b_f = _cumsum_rows(tril, la_ref[rows, 0:GLA_KW])
        b_b = _cumsum_rows(triu, la_ref[rows, GLA_KW:2 * GLA_KW])
        a = _gla_scores(q, k, b_f, False) + _gla_scores(q, k, b_b, True)
        o = _dot(a.astype(BF16), _bd_rows(v, GLA_HEADS, GLA_DV).astype(BF16))
        acc_ref[rows, :] = o + inter(0, q, k, v, b_f, b_f[c - 1:c, :])
        rows, q, k, v = load(nc - 1 - n)
        b_b = _cumsum_rows(triu, la_ref[rows, GLA_KW:2 * GLA_KW])
        accb_ref[rows, :] = inter(1, q, k, v, b_b, b_b[0:1, :])
        return carry

    lax.fori_loop(0, nc, body, 0)

    o = acc_ref[...] + accb_ref[...]
    gn = gn_ref[...]
    for hh in range(GLA_HEADS):
        sl = slice(hh * GLA_DV, (hh + 1) * GLA_DV)
        o_ref[:, sl] = (_rms(o[:, sl]) * gn[:, sl] * gg_ref[:, sl].astype(F32)).astype(BF16)
    if not has_state:
        for d in range(2):
            for hh in range(GLA_HEADS):
                sfin_ref[d, hh] = st_ref[d, hh * GLA_DV:(hh + 1) * GLA_DV, hh * GLA_DK:(hh + 1) * GLA_DK].T


def _gla_scratch(seq_len):
    return [pltpu.VMEM((seq_len, GLA_VW), F32), pltpu.VMEM((seq_len, GLA_VW), F32),
            pltpu.VMEM((2, GLA_VW, GLA_KW), F32)]


def _gla_ctx(gq, gk, gv, la, gg, gn):
    seq = lambda w: pl.BlockSpec((SEQ, w), lambda i: (i, 0))
    return pl.pallas_call(
        partial(_gla_kernel, seq_len=SEQ, has_state=False),
        grid=(BATCH,),
        in_specs=[seq(GLA_KW), seq(GLA_KW), seq(GLA_VW), seq(2 * GLA_KW), seq(GLA_VW),
                  pl.BlockSpec((1, GLA_VW), lambda i: (0, 0))],
        out_specs=[seq(GLA_VW),
                   pl.BlockSpec((None, 2, GLA_HEADS, GLA_DK, GLA_DV), lambda i: (i, 0, 0, 0, 0))],
        out_shape=[jax.ShapeDtypeStruct((N_CTX, GLA_VW), BF16),
                   jax.ShapeDtypeStruct((BATCH, 2, GLA_HEADS, GLA_DK, GLA_DV), F32)],
        scratch_shapes=_gla_scratch(SEQ),
        compiler_params=pltpu.CompilerParams(dimension_semantics=("parallel",), vmem_limit_bytes=VMEM_LIMIT),
        name="gla_ctx",
    )(gq, gk, gv, la, gg, gn)


def _gla_lat(layer, gq, gk, gv, la, gg, gn, s0):
    s0_blk = N_CTX // DEC_SEQ
    seq = lambda w: pl.BlockSpec((DEC_SEQ, w), lambda b: (s0_blk + b, 0))
    return pl.pallas_call(
        partial(_gla_kernel, seq_len=DEC_SEQ, has_state=True),
        grid=(DEC_BATCH,),
        in_specs=[seq(GLA_KW), seq(GLA_KW), seq(GLA_VW), seq(2 * GLA_KW), seq(GLA_VW),
                  pl.BlockSpec((1, GLA_VW), lambda b: (0, 0)),
                  pl.BlockSpec((None, None, 2, GLA_HEADS, GLA_DK, GLA_DV), lambda b: (b, layer, 0, 0, 0, 0))],
        out_specs=pl.BlockSpec((DEC_SEQ, GLA_VW), lambda b: (b, 0)),
        out_shape=jax.ShapeDtypeStruct((N_LAT, GLA_VW), BF16),
        scratch_shapes=_gla_scratch(DEC_SEQ),
        compiler_params=pltpu.CompilerParams(dimension_semantics=("parallel",), vmem_limit_bytes=VMEM_LIMIT),
        name="gla_lat",
    )(gq, gk, gv, la, gg, gn, s0)


def _route(scores, bias):
    biased = scores + bias
    v = [biased[e:e + 1, :] for e in range(N_EXPERTS)]
    s = [scores[e:e + 1, :] for e in range(N_EXPERTS)]
    gscore = []
    for g in range(N_GROUPS):
        a, b, c, d = v[4 * g:4 * g + 4]
        hi1, lo1, hi2, lo2 = jnp.maximum(a, b), jnp.minimum(a, b), jnp.maximum(c, d), jnp.minimum(c, d)
        gscore.append(jnp.maximum(hi1, hi2) + jnp.maximum(jnp.minimum(hi1, hi2), jnp.maximum(lo1, lo2)))
    one = lambda cond: jnp.where(cond, 1.0, 0.0)
    picked, chosen = [], []
    for g in range(N_GROUPS):
        best = None
        for g2 in range(N_GROUPS):
            if g2 == g:
                continue
            ok = one((gscore[g] > gscore[g2]) if g2 < g else (gscore[g] >= gscore[g2]))
            best = ok if best is None else best * ok
        for i in range(EXPERTS_PER_GROUP):
            e = 4 * g + i
            beaten = None
            for j in range(EXPERTS_PER_GROUP):
                if j == i:
                    continue
                e2 = 4 * g + j
                lost = one((v[e2] > v[e]) if j > i else (v[e2] >= v[e]))
                beaten = lost if beaten is None else beaten + lost
            chosen.append(best * one(beaten < 1.5))
            picked.append(chosen[-1] * s[e])
    den = picked[0]
    for e in range(1, N_EXPERTS):
        den = den + picked[e]
    return jnp.concatenate([p / den for p in picked] + chosen, axis=0)


def _mix_kernel(xc_ref, xl_ref, oac_ref, oal_ref, ogc_ref, ogl_ref, sa_ref, sb_ref, mod_ref, wpa_f, wpb_f, wo_f, n2_ref,
                rw_ref, rb_ref, x1_out, h2_out, gt_out, wpa_ref, wpb_ref, wo_ref):
    @pl.when(pl.program_id(0) == 0)
    def _():
        wpa_ref[...] = wpa_f[...].astype(BF16)
        wpb_ref[...] = wpb_f[...].astype(BF16)
        wo_ref[...] = wo_f[...].astype(BF16)

    is_ctx = pl.program_id(0) < N_CTX // TM
    oa = jnp.where(is_ctx, oac_ref[...], oal_ref[...])
    og = jnp.where(is_ctx, ogc_ref[...], ogl_ref[...])
    a = _dot(oa, wpa_ref[...])
    b = _dot(og, wpb_ref[...])
    merged = sa_ref[...].astype(F32) * a + sb_ref[...].astype(F32) * b
    g1 = mod_ref[:, 2 * D_MODEL:3 * D_MODEL]
    sh2 = mod_ref[:, 3 * D_MODEL:4 * D_MODEL]
    sc2 = mod_ref[:, 4 * D_MODEL:5 * D_MODEL]
    x = jnp.where(is_ctx, xc_ref[...], xl_ref[...])
    x1 = x + g1 * _dot(merged.astype(BF16), wo_ref[...])
    x1_out[...] = x1
    h2 = _rms(x1) * n2_ref[...] * (1.0 + sc2) + sh2
    _to_row_tiles(h2_out, h2)
    hh, hl = _split2(h2)
    rh, rl = rw_ref[0], rw_ref[1]
    nt = lambda r, t: lax.dot_general(r, t, _NT, preferred_element_type=F32)
    logits = nt(rl, hh) + nt(rh, hl) + nt(rh, hh)
    gt_out[...] = _route(jax.nn.sigmoid(logits), rb_ref[...])


def _mix(layer, x_pair, oa_c, oa_l, og_c, og_l, sa, sb, mod_l, wpa, wpb, wo, norm2, rw3, rb):
    nt = N_TOK // TM
    n_ctx_t = N_CTX // TM
    row = lambda i: (i, 0)
    const = lambda i: (0, 0)
    ctx_row = lambda i: (jnp.minimum(i, n_ctx_t - 1), 0)
    lat_row = lambda i: (jnp.maximum(i - n_ctx_t, 0), 0)
    return pl.pallas_call(
        _mix_kernel,
        grid=(nt,),
        in_specs=_x_specs(x_pair) + [
            pl.BlockSpec((TM, ATTN_WIDTH), ctx_row),
            pl.BlockSpec((TM, ATTN_WIDTH), lat_row),
            pl.BlockSpec((TM, GLA_VW), ctx_row),
            pl.BlockSpec((TM, GLA_VW), lat_row),
            pl.BlockSpec((TM, D_MODEL), row),
            pl.BlockSpec((TM, D_MODEL), row),
            pl.BlockSpec((None, 1, MOD_W), lambda i: (_mod_row(i, TM), 0, 0)),
            pl.BlockSpec((None, ATTN_WIDTH, D_MODEL), lambda i: (layer, 0, 0), pipeline_mode=pl.Buffered(1)),
            pl.BlockSpec((None, GLA_VW, D_MODEL), lambda i: (layer, 0, 0), pipeline_mode=pl.Buffered(1)),
            pl.BlockSpec((None, D_MODEL, D_MODEL), lambda i: (layer, 0, 0), pipeline_mode=pl.Buffered(1)),
            pl.BlockSpec((1, D_MODEL), const),
            pl.BlockSpec((2, N_EXPERTS, D_MODEL), lambda i: (0, 0, 0)),
            pl.BlockSpec((N_EXPERTS, 1), const),
        ],
        out_specs=[pl.BlockSpec((TM, D_MODEL), row), pl.BlockSpec((TM * ROW_TILE, LANES), row),
                   pl.BlockSpec((2 * N_EXPERTS, TM), lambda i: (0, i))],
        out_shape=[jax.ShapeDtypeStruct((N_TOK, D_MODEL), F32),
                   jax.ShapeDtypeStruct((N_TOK * ROW_TILE, LANES), F32),
                   jax.ShapeDtypeStruct((2 * N_EXPERTS, N_TOK), F32)],
        scratch_shapes=[pltpu.VMEM((ATTN_WIDTH, D_MODEL), BF16), pltpu.VMEM((GLA_VW, D_MODEL), BF16),
                        pltpu.VMEM((D_MODEL, D_MODEL), BF16)],
        compiler_params=pltpu.CompilerParams(dimension_semantics=("arbitrary",), vmem_limit_bytes=VMEM_LIMIT),
        name="mix",
    )(*x_pair, oa_c, oa_l, og_c, og_l, sa, sb, mod_l, wpa, wpb, wo, norm2, rw3, rb)


def _route_plan(gsel):
    gates = gsel[:N_EXPERTS]
    sel = gsel[N_EXPERTS:] > 0.5
    seli = sel.astype(jnp.int32)
    incl = jnp.cumsum(seli, axis=1)
    count = incl[:, -1]
    ntile = (count + TS - 1) // TS
    tile_end = jnp.cumsum(ntile)
    tile_start = tile_end - ntile
    total = tile_end[-1]
    slot = tile_start[:, None] * TS + incl - seli
    pos_a = jnp.min(jnp.where(sel, slot, N_SLOT), axis=0)
    pos_b = jnp.max(jnp.where(sel, slot, -1), axis=0)
    w_a = jnp.sum(jnp.where(sel & (slot == pos_a[None, :]), gates, 0.0), axis=0)
    w_b = jnp.sum(jnp.where(sel & (slot == pos_b[None, :]), gates, 0.0), axis=0)
    tile_id = jnp.minimum(jnp.arange(N_TILE, dtype=jnp.int32), total - 1)
    tile_expert = jnp.sum((tile_id[:, None] >= tile_end[None, :]).astype(jnp.int32), axis=1)
    nt = N_TOK // TM
    pos = jnp.concatenate([pos_a.reshape(nt, 1, TM), pos_b.reshape(nt, 1, TM)], axis=2).astype(jnp.int32)
    plan = jnp.concatenate([tile_start, ntile, total[None]]).astype(jnp.int32)
    return pos, jnp.stack([w_a, w_b], axis=1), tile_expert.astype(jnp.int32), plan


def _to_row_tiles(ref, x):
    for k in range(ROW_TILE):
        ref[pl.ds(k, x.shape[0], stride=ROW_TILE), :] = x[:, k * LANES:(k + 1) * LANES]


def _from_row_tiles(ref, rows):
    return jnp.concatenate([ref[pl.ds(k, rows, stride=ROW_TILE), :] for k in range(ROW_TILE)], axis=1)


def _tile_rows(row, n=1):
    return pl.ds(pl.multiple_of(row * ROW_TILE, ROW_TILE), n * ROW_TILE)


def _row_copy(src, src_row, dst, dst_row, sem):
    return pltpu.make_async_copy(src.at[_tile_rows(src_row), :], dst.at[_tile_rows(dst_row), :], sem)


def _dispatch_kernel(plan_ref, pos_ref, h_ref, xs_ref, zero_ref, sem_z, sem):
    i = pl.program_id(0)
    total = plan_ref[2 * N_EXPERTS]

    def zero_copy(tile):
        return pltpu.make_async_copy(zero_ref, xs_ref.at[_tile_rows(tile * TS, TS), :], sem_z)

    def last_tile(e):
        return plan_ref[e] + plan_ref[N_EXPERTS + e] - 1

    @pl.when(i == 0)
    def _():
        zero_ref[...] = jnp.zeros(zero_ref.shape, F32)

        def each(fn):
            def expert(e, c):
                @pl.when(plan_ref[N_EXPERTS + e] > 0)
                def _():
                    fn(zero_copy(last_tile(e)))
                return c

            def unused(t, c):
                @pl.when(t >= total)
                def _():
                    fn(zero_copy(t))
                return c

            lax.fori_loop(0, N_EXPERTS, expert, 0)
            lax.fori_loop(0, N_TILE, unused, 0)

        each(lambda cp: cp.start())
        each(lambda cp: cp.wait())

    def start(r, c):
        _row_copy(h_ref, r, xs_ref, pos_ref[0, r], sem).start()
        _row_copy(h_ref, r, xs_ref, pos_ref[0, TM + r], sem).start()
        return c

    def wait(r, c):
        _row_copy(h_ref, 0, xs_ref, 0, sem).wait()
        return c

    lax.fori_loop(0, TM, start, 0, unroll=8)
    lax.fori_loop(0, 2 * TM, wait, 0, unroll=8)


def _dispatch(plan, pos, h2):
    return pl.pallas_call(
        _dispatch_kernel,
        grid_spec=pltpu.PrefetchScalarGridSpec(
            num_scalar_prefetch=1,
            grid=(N_TOK // TM,),
            in_specs=[
                pl.BlockSpec((None, 1, 2 * TM), lambda i, plan: (i, 0, 0), memory_space=pltpu.SMEM),
                pl.BlockSpec((TM * ROW_TILE, LANES), lambda i, plan: (i, 0)),
            ],
            out_specs=pl.BlockSpec(memory_space=pl.ANY),
            scratch_shapes=[pltpu.VMEM((TS * ROW_TILE, LANES), F32), pltpu.SemaphoreType.DMA,
                            pltpu.SemaphoreType.DMA],
        ),
        out_shape=jax.ShapeDtypeStruct((N_SLOT * ROW_TILE, LANES), F32),
        compiler_params=pltpu.CompilerParams(dimension_semantics=("arbitrary",), vmem_limit_bytes=VMEM_LIMIT),
        name="dispatch",
    )(plan, pos, h2)


def _moe_kernel(te_ref, plan_ref, xs_ref, wg_ref, wu_ref, wd_ref, ys_ref, wg_s, wu_s, wd_s):
    i = pl.program_id(0)
    total = plan_ref[2 * N_EXPERTS]

    @pl.when(i < total)
    def _():
        @pl.when((i == 0) | (te_ref[i] != te_ref[jnp.maximum(i - 1, 0)]))
        def _():
            wg_s[...] = wg_ref[...].astype(BF16)
            wu_s[...] = wu_ref[...].astype(BF16)
            wd_s[...] = wd_ref[...].astype(BF16)

        x = _from_row_tiles(xs_ref, TS).astype(BF16)
        up = _dot(x, wu_s[...])
        gt = _dot(x, wg_s[...])
        act = (gt * jax.nn.sigmoid(gt) * up).astype(BF16)
        _to_row_tiles(ys_ref, _dot(act, wd_s[...]))

    @pl.when(i >= total)
    def _():
        ys_ref[...] = jnp.zeros(ys_ref.shape, F32)


def _moe(layer, tile_expert, plan, xs, w_gate, w_up, w_down):
    last = lambda i, plan: jnp.minimum(i, plan[2 * N_EXPERTS] - 1)
    w_spec = lambda a, b: pl.BlockSpec((None, None, a, b), lambda i, te, plan: (layer, te[i], 0, 0))
    return pl.pallas_call(
        _moe_kernel,
        grid_spec=pltpu.PrefetchScalarGridSpec(
            num_scalar_prefetch=2,
            grid=(N_TILE,),
            in_specs=[
                pl.BlockSpec((TS * ROW_TILE, LANES), lambda i, te, plan: (last(i, plan), 0)),
                w_spec(D_MODEL, D_EXPERT), w_spec(D_MODEL, D_EXPERT), w_spec(D_EXPERT, D_MODEL),
            ],
            out_specs=pl.BlockSpec((TS * ROW_TILE, LANES), lambda i, te, plan: (i, 0)),
            scratch_shapes=[pltpu.VMEM((D_MODEL, D_EXPERT), BF16), pltpu.VMEM((D_MODEL, D_EXPERT), BF16),
                            pltpu.VMEM((D_EXPERT, D_MODEL), BF16)],
        ),
        out_shape=jax.ShapeDtypeStruct((N_SLOT * ROW_TILE, LANES), F32),
        compiler_params=pltpu.CompilerParams(dimension_semantics=("arbitrary",), vmem_limit_bytes=VMEM_LIMIT),
        name="moe",
    )(tile_expert, plan, xs, w_gate, w_up, w_down)


def _combine_kernel(pos_ref, nxt_ref, w_ref, x1_ref, mod_ref, ys_ref, out_ref, ya_ref, yb_ref, sem):
    i = pl.program_id(0)
    n = pl.num_programs(0)

    def gather(p_ref, slot):
        def start(r, c):
            _row_copy(ys_ref, p_ref[0, r], ya_ref.at[slot], r, sem.at[slot]).start()
            _row_copy(ys_ref, p_ref[0, TM + r], yb_ref.at[slot], r, sem.at[slot]).start()
            return c
        lax.fori_loop(0, TM, start, 0, unroll=8)

    @pl.when(i == 0)
    def _():
        gather(pos_ref, 0)

    @pl.when(i + 1 < n)
    def _():
        gather(nxt_ref, (i + 1) % 2)

    slot = i % 2

    def wait(r, c):
        _row_copy(ys_ref, 0, ya_ref.at[slot], 0, sem.at[slot]).wait()
        return c

    lax.fori_loop(0, 2 * TM, wait, 0, unroll=8)
    g2 = mod_ref[:, 5 * D_MODEL:6 * D_MODEL]
    w = w_ref[...]
    ya = _from_row_tiles(ya_ref.at[slot], TM)
    yb = _from_row_tiles(yb_ref.at[slot], TM)
    out_ref[...] = x1_ref[...] + g2 * (w[:, 0:1] * ya + w[:, 1:2] * yb)


def _combine(pos, w_ab, x1, mod_l, ys, tile0, nt):
    pos_spec = lambda fn: pl.BlockSpec((None, 1, 2 * TM), fn, memory_space=pltpu.SMEM)
    return pl.pallas_call(
        _combine_kernel,
        grid=(nt,),
        in_specs=[
            pos_spec(lambda i: (tile0 + i, 0, 0)),
            pos_spec(lambda i: (tile0 + jnp.minimum(i + 1, nt - 1), 0, 0)),
            pl.BlockSpec((TM, 2), lambda i: (tile0 + i, 0)),
            pl.BlockSpec((TM, D_MODEL), lambda i: (tile0 + i, 0)),
            pl.BlockSpec((None, 1, MOD_W), lambda i: (_mod_row(tile0 + i, TM), 0, 0)),
            pl.BlockSpec(memory_space=pl.ANY),
        ],
        out_specs=pl.BlockSpec((TM, D_MODEL), lambda i: (i, 0)),
        out_shape=jax.ShapeDtypeStruct((nt * TM, D_MODEL), F32),
        scratch_shapes=[pltpu.VMEM((2, TM * ROW_TILE, LANES), F32), pltpu.VMEM((2, TM * ROW_TILE, LANES), F32),
                        pltpu.SemaphoreType.DMA((2,))],
        compiler_params=pltpu.CompilerParams(dimension_semantics=("arbitrary",), vmem_limit_bytes=VMEM_LIMIT),
        name="combine",
    )(pos, pos, w_ab, x1, mod_l, ys)


def _rope_tables():
    pos = np.arange(DEC_SEQ)
    rows = (pos // GRID_W).astype(np.float64)
    cols = (pos % GRID_W).astype(np.float64)
    half = HEAD_DIM // 2
    inv_freq = ROPE_THETA ** (-np.arange(0, half, 2, dtype=np.float64) / half)
    ang_r = rows[:, None] * inv_freq[None, :]
    ang_c = cols[:, None] * inv_freq[None, :]
    cos_h = np.concatenate([np.cos(ang_r), np.cos(ang_r), np.cos(ang_c), np.cos(ang_c)], axis=-1)
    sin_h = np.concatenate([-np.sin(ang_r), np.sin(ang_r), -np.sin(ang_c), np.sin(ang_c)], axis=-1)
    cos = np.concatenate([np.ones((TM, HEAD_DIM)), cos_h], axis=0)
    sin = np.concatenate([np.zeros((TM, HEAD_DIM)), sin_h], axis=0)
    return (jnp.asarray(np.tile(cos, (1, N_HEADS)), F32), jnp.asarray(np.tile(sin, (1, N_HEADS)), F32))


def _pack_w2(w2):
    z = jnp.zeros((GLA_RANK, GLA_KW), w2.dtype)
    top = jnp.concatenate([w2[0], z], axis=1)
    bot = jnp.concatenate([z, w2[1]], axis=1)
    pad = jnp.zeros((LR_PAD - 2 * GLA_RANK, 2 * GLA_KW), w2.dtype)
    return jnp.concatenate([top, bot, pad], axis=0).astype(BF16)


def kernel(x_prompt, x_sample, cache_k, cache_v, state_gla, c, c_ctx, norm1, norm2, w_ada, b_ada, w_in,
           q_norm, k_norm, gla_w2, gla_b, gla_norm, w_pa, w_pb, w_out, router_w, router_bias,
           w_gate, w_up, w_down):
    x_pair = (x_prompt.reshape(N_CTX, D_MODEL), x_sample.reshape(N_LAT, D_MODEL))
    cond =jnp.concatenate([c_ctx[None, :], c, jnp.zeros((N_COND - 1 - DEC_BATCH, D_MODEL), F32)], axis=0)
    mod = _ada(cond, w_ada, b_ada).reshape(DEPTH, N_COND, 1, MOD_W)
    cos, sin = _rope_tables()
    lane_head = np.arange(ATTN_WIDTH) // HEAD_DIM
    bd = jnp.asarray(lane_head[:, None] == lane_head[None, :], BF16)
    rw3 = jnp.stack(_split2(router_w.T), axis=0)
    rb = router_bias.reshape(N_EXPERTS, 1)
    ck = cache_k.reshape(DEC_BATCH, DEPTH, PAST_LEN, KV_WIDTH)
    cv = cache_v.reshape(DEC_BATCH, DEPTH, PAST_LEN, KV_WIDTH)

    keys, vals, states = [], [], []
    for l in range(DEPTH):
        q, kf, kr, v, gq, gk, gv, gg, la, sa, sb = _inproj(
            l, x_pair, mod[l], norm1[l][None, :], w_in, jnp.tile(q_norm[l], N_HEADS)[None, :],
            jnp.tile(k_norm[l], KV_HEADS)[None, :], cos, sin, bd, _pack_w2(gla_w2[l]),
            gla_b[l].reshape(1, 2 * GLA_KW))
        oa_c = _attn_ctx(q, kr, v)
        oa_l = _attn_lat(l, q, kr, v, ck, cv)
        gn = jnp.tile(gla_norm[l], GLA_HEADS)[None, :]
        og_c, s_fin = _gla_ctx(gq, gk, gv, la, gg, gn)
        og_l = _gla_lat(l, gq, gk, gv, la, gg, gn, state_gla)
        x1, h2, gsel = _mix(l, x_pair, oa_c, oa_l, og_c, og_l, sa, sb, mod[l], w_pa, w_pb, w_out,
                            norm2[l][None, :], rw3, rb)
        pos, w_ab, tile_expert, plan = _route_plan(gsel)
        xs = _dispatch(plan, pos, h2)
        ys = _moe(l, tile_expert, plan, xs, w_gate, w_up, w_down)
        n_ctx_t = N_CTX // TM
        if l + 1 < DEPTH:
            x = _combine(pos, w_ab, x1, mod[l], ys, 0, N_TOK // TM)
            x_pair = (x, x)
        else:
            x_pair = (_combine(pos, w_ab, x1, mod[l], ys, 0, n_ctx_t),
                      _combine(pos, w_ab, x1, mod[l], ys, n_ctx_t, N_LAT // TM))
        keys.append(kf[:N_CTX].reshape(BATCH, SEQ, KV_HEADS, HEAD_DIM))
        vals.append(v[:N_CTX].reshape(BATCH, SEQ, KV_HEADS, HEAD_DIM))
        states.append(s_fin)
    y_prompt = x_pair[0].reshape(BATCH, SEQ, D_MODEL)
    y_sample = x_pair[1].reshape(DEC_BATCH, DEC_SEQ, D_MODEL)
    return (y_prompt, y_sample, jnp.stack(keys, axis=1), jnp.stack(vals, axis=1), jnp.stack(states, axis=1))
```

```python
from functools import partial

import jax
import jax.numpy as jnp
import numpy as np
from jax import lax
from jax.experimental import pallas as pl
from jax.experimental.pallas import tpu as pltpu

F32 = jnp.float32
BF16 = jnp.bfloat16

D_MODEL = 1024
BATCH = 16
SEQ = 256
DEPTH = 2
DEC_BATCH = 4
DEC_SEQ = 1024
PAST_LEN = 512
GRID_W = 64
N_HEADS = 8
KV_HEADS = 2
GROUP = N_HEADS // KV_HEADS
HEAD_DIM = 64
ATTN_WIDTH = N_HEADS * HEAD_DIM
KV_WIDTH = KV_HEADS * HEAD_DIM
ROPE_THETA = 10000.0
GLA_HEADS = 4
GLA_DK = 64
GLA_DV = 128
GLA_KW = GLA_HEADS * GLA_DK
GLA_VW = GLA_HEADS * GLA_DV
GLA_RANK = 16
GLA_GATE_NORM = 16.0
N_EXPERTS = 16
N_GROUPS = 4
EXPERTS_PER_GROUP = N_EXPERTS // N_GROUPS
D_EXPERT = 512
NORM_EPS = 1e-6

N_CTX = BATCH * SEQ
N_LAT = DEC_BATCH * DEC_SEQ
N_TOK = N_CTX + N_LAT
N_COND = 8
MOD_W = 6 * D_MODEL

TM = 512
TS = 512
N_TILE = 2 * N_TOK // TS + N_EXPERTS
N_SLOT = N_TILE * TS
ROW_TILE = 8
LANES = D_MODEL // ROW_TILE
GLA_CHUNK = 64
GLA_LEVELS = (32, 16)
GLA_DIAG = 16
GLA_SEQS = 1
LR_PAD = 128
SEG_MIX = ATTN_WIDTH + 2 * KV_WIDTH + 2 * GLA_KW + 2 * GLA_VW
OFF_GA = SEG_MIX
OFF_GB = OFF_GA + D_MODEL
OFF_LR = OFF_GB + D_MODEL
IN_PACKED = OFF_LR + LR_PAD
IN_WIDTH = SEG_MIX + 2 * GLA_RANK + 2 * D_MODEL
VMEM_LIMIT = 56 * 1024 * 1024

_NT = (((1,), (1,)), ((), ()))
_TN = (((0,), (0,)), ((), ()))


def _mod_row(tile, tm):
    start = tile * tm
    return jnp.where(start < N_CTX, 0, 1 + (start - N_CTX) // DEC_SEQ)


def _split3(x):
    hi = x.astype(BF16)
    r1 = x - hi.astype(F32)
    mid = r1.astype(BF16)
    lo = (r1 - mid.astype(F32)).astype(BF16)
    return hi, mid, lo


def _split2(x):
    hi = x.astype(BF16)
    lo = (x - hi.astype(F32)).astype(BF16)
    return hi, lo


def _dot(a, b):
    return jnp.dot(a, b, preferred_element_type=F32)


def _rms(x):
    return x * lax.rsqrt(jnp.mean(x * x, axis=-1, keepdims=True) + NORM_EPS)


def _ada_kernel(cond_ref, w_ref, b_ref, out_ref):
    c = cond_ref[...]
    s = c * jax.nn.sigmoid(c)
    out_ref[...] = _dot(s.astype(BF16), w_ref[...].astype(BF16)) + b_ref[...]


def _ada(cond, w_ada, b_ada):
    nb = MOD_W // D_MODEL
    return pl.pallas_call(
        _ada_kernel,
        grid=(DEPTH, nb),
        in_specs=[
            pl.BlockSpec((N_COND, D_MODEL), lambda l, j: (0, 0)),
            pl.BlockSpec((None, D_MODEL, D_MODEL), lambda l, j: (l, 0, j)),
            pl.BlockSpec((None, 1, D_MODEL), lambda l, j: (l, 0, j)),
        ],
        out_specs=pl.BlockSpec((None, N_COND, D_MODEL), lambda l, j: (l, 0, j)),
        out_shape=jax.ShapeDtypeStruct((DEPTH, N_COND, MOD_W), F32),
        compiler_params=pltpu.CompilerParams(vmem_limit_bytes=VMEM_LIMIT),
        name="ada",
    )(cond, w_ada, b_ada.reshape(DEPTH, 1, MOD_W))


def _head_rmsnorm(x, bd):
    ssq = _dot((x * x).astype(BF16), bd)
    return x * lax.rsqrt(ssq * (1.0 / HEAD_DIM) + NORM_EPS)


def _rope(x, cos, sin):
    n = x.shape[-1]
    lane = lax.broadcasted_iota(jnp.int32, x.shape, 1)
    first = (lane & (HEAD_DIM // 4)) == 0
    partner = jnp.where(first, pltpu.roll(x, n - HEAD_DIM // 4, 1), pltpu.roll(x, HEAD_DIM // 4, 1))
    return x * cos + partner * sin


def _pack_w_in(w_ref, wp_ref):
    lr0 = SEG_MIX
    ga0 = lr0 + 2 * GLA_RANK
    rows_per_step = 128

    def body(i, carry):
        rows = pl.ds(pl.multiple_of(i * rows_per_step, rows_per_step), rows_per_step)
        wp_ref[rows, 0:SEG_MIX] = w_ref[rows, 0:SEG_MIX].astype(BF16)
        tail = w_ref[rows, SEG_MIX:IN_WIDTH]
        lr = tail[:, 0:2 * GLA_RANK]
        wp_ref[rows, OFF_LR:IN_PACKED] = jnp.concatenate(
            [lr, jnp.zeros((rows_per_step, LR_PAD - 2 * GLA_RANK), F32)], axis=1).astype(BF16)
        wp_ref[rows, OFF_GA:OFF_LR] = tail[:, ga0 - lr0:ga0 - lr0 + 2 * D_MODEL].astype(BF16)
        return carry

    lax.fori_loop(0, D_MODEL // rows_per_step, body, 0)


def _inproj_kernel(xc_ref, xl_ref, mod_ref, n1_ref, wf_ref, qn_ref, kn_ref, cos_ref, sin_ref, bd_ref, w2_ref, gb_ref,
                   q_out, kf_out, kr_out, v_out, gq_out, gk_out, gv_out, gg_out, la_out, sa_out, sb_out,
                   w_ref):
    @pl.when(pl.program_id(0) == 0)
    def _():
        _pack_w_in(wf_ref, w_ref)

    x = jnp.where(pl.program_id(0) < N_CTX // TM, xc_ref[...], xl_ref[...])
    sh1 = mod_ref[:, 0:D_MODEL]
    sc1 = mod_ref[:, D_MODEL:2 * D_MODEL]
    h = (_rms(x) * n1_ref[...] * (1.0 + sc1) + sh1).astype(BF16)

    def proj(lo, width):
        return _dot(h, w_ref[:, lo:lo + width])

    cos = cos_ref[...]
    sin = sin_ref[...]
    bd = bd_ref[...]
    off = 0
    q = _head_rmsnorm(proj(off, ATTN_WIDTH), bd) * qn_ref[...]
    q_out[...] = (_rope(q, cos, sin) * HEAD_DIM ** -0.5).astype(BF16)
    off += ATTN_WIDTH
    k = _head_rmsnorm(proj(off, KV_WIDTH), bd[:KV_WIDTH, :KV_WIDTH]) * kn_ref[...]
    kf_out[...] = k
    kr_out[...] = _rope(k, cos[:, :KV_WIDTH], sin[:, :KV_WIDTH]).astype(BF16)
    off += KV_WIDTH
    v_out[...] = proj(off, KV_WIDTH)
    off += KV_WIDTH
    gq_out[...] = (proj(off, GLA_KW) * GLA_DK ** -0.5).astype(BF16)
    off += GLA_KW
    gk_out[...] = proj(off, GLA_KW).astype(BF16)
    off += GLA_KW
    gv_out[...] = proj(off, GLA_VW).astype(BF16)
    off += GLA_VW
    gg = proj(off, GLA_VW)
    gg_out[...] = (gg * jax.nn.sigmoid(gg)).astype(BF16)
    sa_out[...] = jax.nn.sigmoid(proj(OFF_GA, D_MODEL)).astype(BF16)
    sb_out[...] = jax.nn.sigmoid(proj(OFF_GB, D_MODEL)).astype(BF16)
    lr = proj(OFF_LR, LR_PAD)
    z = _dot(lr.astype(BF16), w2_ref[...]) + gb_ref[...]
    log_sig = jnp.minimum(z, 0.0) - jnp.log1p(jnp.exp(-jnp.abs(z)))
    la_out[...] = log_sig * (1.0 / GLA_GATE_NORM)


def _x_specs(x_pair):
    n_ctx_t = N_CTX // TM
    lat0 = n_ctx_t if x_pair[0] is x_pair[1] else 0
    return [pl.BlockSpec((TM, D_MODEL), lambda i: (jnp.minimum(i, n_ctx_t - 1), 0)),
            pl.BlockSpec((TM, D_MODEL), lambda i: (lat0 + jnp.maximum(i - n_ctx_t, 0), 0))]


def _inproj(layer, x_pair, mod_l, norm1, w_in, qn, kn, cos, sin, bd, w2p, gb):
    nt = N_TOK // TM
    n_ctx_t = N_CTX // TM
    t_per_seq = DEC_SEQ // TM
    row = lambda i: (i, 0)
    const = lambda i: (0, 0)
    rope_blk = lambda i: (jnp.where(i < n_ctx_t, 0, 1 + (i - n_ctx_t) % t_per_seq), 0)

    def out(width, dtype):
        return pl.BlockSpec((TM, width), row), jax.ShapeDtypeStruct((N_TOK, width), dtype)

    outs = [out(ATTN_WIDTH, BF16), out(KV_WIDTH, F32), out(KV_WIDTH, BF16), out(KV_WIDTH, F32),
            out(GLA_KW, BF16), out(GLA_KW, BF16), out(GLA_VW, BF16), out(GLA_VW, BF16),
            out(2 * GLA_KW, F32), out(D_MODEL, BF16), out(D_MODEL, BF16)]
    return pl.pallas_call(
        _inproj_kernel,
        grid=(nt,),
        in_specs=_x_specs(x_pair) + [
            pl.BlockSpec((None, 1, MOD_W), lambda i: (_mod_row(i, TM), 0, 0)),
            pl.BlockSpec((1, D_MODEL), const),
            pl.BlockSpec((None, D_MODEL, IN_WIDTH), lambda i: (layer, 0, 0), pipeline_mode=pl.Buffered(1)),
            pl.BlockSpec((1, ATTN_WIDTH), const),
            pl.BlockSpec((1, KV_WIDTH), const),
            pl.BlockSpec((TM, ATTN_WIDTH), rope_blk),
            pl.BlockSpec((TM, ATTN_WIDTH), rope_blk),
            pl.BlockSpec((ATTN_WIDTH, ATTN_WIDTH), const),
            pl.BlockSpec((LR_PAD, 2 * GLA_KW), const),
            pl.BlockSpec((1, 2 * GLA_KW), const),
        ],
        out_specs=[o[0] for o in outs],
        out_shape=[o[1] for o in outs],
        scratch_shapes=[pltpu.VMEM((D_MODEL, IN_PACKED), BF16)],
        compiler_params=pltpu.CompilerParams(dimension_semantics=("arbitrary",), vmem_limit_bytes=VMEM_LIMIT),
        name="inproj",
    )(*x_pair, mod_l, norm1, w_in, qn, kn, cos, sin, bd, w2p, gb)


def _attn_kernel(*refs, has_cache):
    if has_cache:
        q_ref, k_ref, v_ref, ck_ref, cv_ref, o_ref = refs
    else:
        q_ref, k_ref, v_ref, o_ref = refs
    tq = q_ref.shape[0]
    for g in range(KV_HEADS):
        ks = slice(g * HEAD_DIM, (g + 1) * HEAD_DIM)
        qs = jnp.concatenate(
            [q_ref[:, (g * GROUP + j) * HEAD_DIM:(g * GROUP + j + 1) * HEAD_DIM] for j in range(GROUP)], axis=0)
        k = k_ref[:, ks]
        v = v_ref[:, ks].astype(BF16)
        s = lax.dot_general(qs, k, _NT, preferred_element_type=F32)
        m = jnp.max(s, axis=-1, keepdims=True)
        if has_cache:
            ck = ck_ref[:, ks].astype(BF16)
            cv = cv_ref[:, ks].astype(BF16)
            s2 = lax.dot_general(qs, ck, _NT, preferred_element_type=F32)
            m = jnp.maximum(m, jnp.max(s2, axis=-1, keepdims=True))
        p = jnp.exp(s - m)
        den = jnp.sum(p, axis=-1, keepdims=True)
        o = _dot(p.astype(BF16), v)
        if has_cache:
            p2 = jnp.exp(s2 - m)
            den = den + jnp.sum(p2, axis=-1, keepdims=True)
            o = o + _dot(p2.astype(BF16), cv)
        o = o / den
        for j in range(GROUP):
            hd = g * GROUP + j
            o_ref[:, hd * HEAD_DIM:(hd + 1) * HEAD_DIM] = o[j * tq:(j + 1) * tq, :].astype(BF16)


def _attn_ctx(q, kr, v):
    return pl.pallas_call(
        partial(_attn_kernel, has_cache=False),
        grid=(BATCH,),
        in_specs=[
            pl.BlockSpec((SEQ, ATTN_WIDTH), lambda i: (i, 0)),
            pl.BlockSpec((SEQ, KV_WIDTH), lambda i: (i, 0)),
            pl.BlockSpec((SEQ, KV_WIDTH), lambda i: (i, 0)),
        ],
        out_specs=pl.BlockSpec((SEQ, ATTN_WIDTH), lambda i: (i, 0)),
        out_shape=jax.ShapeDtypeStruct((N_CTX, ATTN_WIDTH), BF16),
        compiler_params=pltpu.CompilerParams(dimension_semantics=("parallel",), vmem_limit_bytes=VMEM_LIMIT),
        name="attn_ctx",
    )(q, kr, v)


def _attn_lat(layer, q, kr, v, cache_k, cache_v):
    tq = 256
    nq = DEC_SEQ // tq
    q0 = N_CTX // tq
    s0 = N_CTX // DEC_SEQ
    return pl.pallas_call(
        partial(_attn_kernel, has_cache=True),
        grid=(DEC_BATCH, nq),
        in_specs=[
            pl.BlockSpec((tq, ATTN_WIDTH), lambda b, i: (q0 + b * nq + i, 0)),
            pl.BlockSpec((DEC_SEQ, KV_WIDTH), lambda b, i: (s0 + b, 0)),
            pl.BlockSpec((DEC_SEQ, KV_WIDTH), lambda b, i: (s0 + b, 0)),
            pl.BlockSpec((None, None, PAST_LEN, KV_WIDTH), lambda b, i: (b, layer, 0, 0)),
            pl.BlockSpec((None, None, PAST_LEN, KV_WIDTH), lambda b, i: (b, layer, 0, 0)),
        ],
        out_specs=pl.BlockSpec((tq, ATTN_WIDTH), lambda b, i: (b * nq + i, 0)),
        out_shape=jax.ShapeDtypeStruct((N_LAT, ATTN_WIDTH), BF16),
        compiler_params=pltpu.CompilerParams(dimension_semantics=("parallel", "parallel"),
                                             vmem_limit_bytes=VMEM_LIMIT),
        name="attn_lat",
    )(q, kr, v, cache_k, cache_v)


def _gla_masks():
    c = GLA_CHUNK
    hrow = np.arange(GLA_HEADS * c) // c
    head_k = (hrow[:, None] == (np.arange(GLA_KW) // GLA_DK)[None, :])
    head_v = (hrow[:, None] == (np.arange(GLA_VW) // GLA_DV)[None, :])
    t = np.arange(c)[:, None]
    s = (np.arange(GLA_HEADS * c) % c)[None, :]
    pair = []
    for reverse in (False, True):
        for m in GLA_LEVELS:
            pair.append(t // (2 * m) == s // (2 * m))
        pair.append((t // GLA_DIAG == s // GLA_DIAG) & ((s >= t) if reverse else (s <= t)))
    return (jnp.asarray(head_k, BF16), jnp.asarray(head_v, BF16), jnp.asarray(np.stack(pair), F32))


def _block_diag(x, head_mask):
    return jnp.concatenate([x] * GLA_HEADS, axis=0) * head_mask


def _ref_rows(b, rows, rep):
    return jnp.concatenate([jnp.broadcast_to(b[r:r + 1, :], (rep, b.shape[1])) for r in rows], axis=0)


def _gla_scores(q, k, b, reverse, head_k, pair_ref):
    c = GLA_CHUNK
    t_idx = lax.broadcasted_iota(jnp.int32, (c, GLA_KW), 0)
    term0 = (len(GLA_LEVELS) + 1) if reverse else 0

    def scores(qt, kt):
        return lax.dot_general(qt.astype(BF16), _block_diag(kt.astype(BF16), head_k), _NT,
                               preferred_element_type=F32)

    total = None
    for i, m in enumerate(GLA_LEVELS):
        nblk = c // (2 * m)
        ref = _ref_rows(b, [2 * m * j + (m if reverse else m - 1) for j in range(nblk)], 2 * m)
        q_side = ((t_idx & m) == 0) if reverse else ((t_idx & m) != 0)
        qt = jnp.where(q_side, q * jnp.exp(b - ref), 0.0)
        kt = jnp.where(q_side, 0.0, k * jnp.exp(ref - b))
        part = scores(qt, kt) * pair_ref[term0 + i]
        total = part if total is None else total + part
    d = GLA_DIAG
    x = b - _ref_rows(b, [d * j + d // 2 for j in range(c // d)], d)
    diag = scores(q * jnp.exp(x), k * jnp.exp(-x))
    return total + jnp.where(pair_ref[term0 + len(GLA_LEVELS)] > 0.5, diag, 0.0)


def _cumsum_rows(tri, la):
    hi, lo = _split2(la)
    return _dot(tri, hi) + _dot(tri, lo)


def _gla_kernel(*refs, seq_len, has_state):
    gq_ref, gk_ref, gv_ref, la_ref, gg_ref, gn_ref, hk_ref, hv_ref, pair_ref = refs[:9]
    if has_state:
        s0_ref, o_ref, acc_ref, accb_ref, st_ref = refs[9:]
    else:
        o_ref, sfin_ref, acc_ref, accb_ref, st_ref = refs[9:]
    c = GLA_CHUNK
    nc = seq_len // c
    r_i = lax.broadcasted_iota(jnp.int32, (c, c), 0)
    c_i = lax.broadcasted_iota(jnp.int32, (c, c), 1)
    tril = jnp.where(c_i <= r_i, 1.0, 0.0).astype(BF16)
    triu = jnp.where(c_i >= r_i, 1.0, 0.0).astype(BF16)
    st_head_r = lax.broadcasted_iota(jnp.int32, (GLA_VW, GLA_KW), 0) // GLA_DV
    st_head_c = lax.broadcasted_iota(jnp.int32, (GLA_VW, GLA_KW), 1) // GLA_DK
    st_mask = st_head_r == st_head_c

    for s in range(GLA_SEQS):
        for d in range(2):
            st_ref[2 * s + d] = jnp.zeros((GLA_VW, GLA_KW), F32)
            if has_state:
                for hh in range(GLA_HEADS):
                    st_ref[2 * s + d, hh * GLA_DV:(hh + 1) * GLA_DV, hh * GLA_DK:(hh + 1) * GLA_DK] = s0_ref[s, d, hh].T

    def load(s, n):
        rows = pl.ds(pl.multiple_of(s * seq_len + n * c, c), c)
        return (rows, gq_ref[rows, :].astype(F32), gk_ref[rows, :].astype(F32), gv_ref[rows, :])

    def inter(d, q, k, v, b, b_end):
        st = st_ref[d]
        q_in = (q * jnp.exp(b)).astype(BF16)
        o = lax.dot_general(q_in, st.astype(BF16), _NT, preferred_element_type=F32)
        k_st = (k * jnp.exp(b_end - b)).astype(BF16)
        upd = lax.dot_general(v, k_st, _TN, preferred_element_type=F32)
        st_ref[d] = st * jnp.exp(b_end) + jnp.where(st_mask, upd, 0.0)
        return o

    def body(n, carry):
        for s in range(GLA_SEQS):
            rows, q, k, v = load(s, n)
            b_f = _cumsum_rows(tril, la_ref[rows, 0:GLA_KW])
            b_b = _cumsum_rows(triu, la_ref[rows, GLA_KW:2 * GLA_KW])
            head_k = hk_ref[...]
            a = _gla_scores(q, k, b_f, False, head_k, pair_ref) + _gla_scores(q, k, b_b, True, head_k, pair_ref)
            o = _dot(a.astype(BF16), _block_diag(v, hv_ref[...]))
            acc_ref[rows, :] = o + inter(2 * s, q, k, v, b_f, b_f[c - 1:c, :])
            rows, q, k, v = load(s, nc - 1 - n)
            b_b = _cumsum_rows(triu, la_ref[rows, GLA_KW:2 * GLA_KW])
            accb_ref[rows, :] = inter(2 * s + 1, q, k, v, b_b, b_b[0:1, :])
        return carry

    lax.fori_loop(0, nc, body, 0)

    o = acc_ref[...] + accb_ref[...]
    gn = gn_ref[...]
    for hh in range(GLA_HEADS):
        sl = slice(hh * GLA_DV, (hh + 1) * GLA_DV)
        o_ref[:, sl] = (_rms(o[:, sl]) * gn[:, sl] * gg_ref[:, sl].astype(F32)).astype(BF16)
    if not has_state:
        for s in range(GLA_SEQS):
            for d in range(2):
                for hh in range(GLA_HEADS):
                    sfin_ref[s, d, hh] = st_ref[2 * s + d, hh * GLA_DV:(hh + 1) * GLA_DV,
                                                hh * GLA_DK:(hh + 1) * GLA_DK].T


def _gla_scratch(seq_len):
    rows = GLA_SEQS * seq_len
    return [pltpu.VMEM((rows, GLA_VW), F32), pltpu.VMEM((rows, GLA_VW), F32),
            pltpu.VMEM((2 * GLA_SEQS, GLA_VW, GLA_KW), F32)]


def _gla_mask_specs():
    n_terms = 2 * (len(GLA_LEVELS) + 1)
    return [pl.BlockSpec((GLA_HEADS * GLA_CHUNK, GLA_KW), lambda i: (0, 0)),
            pl.BlockSpec((GLA_HEADS * GLA_CHUNK, GLA_VW), lambda i: (0, 0)),
            pl.BlockSpec((n_terms, GLA_CHUNK, GLA_HEADS * GLA_CHUNK), lambda i: (0, 0, 0))]


def _gla_ctx(gq, gk, gv, la, gg, gn, masks):
    seq = lambda w: pl.BlockSpec((GLA_SEQS * SEQ, w), lambda i: (i, 0))
    return pl.pallas_call(
        partial(_gla_kernel, seq_len=SEQ, has_state=False),
        grid=(BATCH // GLA_SEQS,),
        in_specs=[seq(GLA_KW), seq(GLA_KW), seq(GLA_VW), seq(2 * GLA_KW), seq(GLA_VW),
                  pl.BlockSpec((1, GLA_VW), lambda i: (0, 0))] + _gla_mask_specs(),
        out_specs=[seq(GLA_VW),
                   pl.BlockSpec((GLA_SEQS, 2, GLA_HEADS, GLA_DK, GLA_DV), lambda i: (i, 0, 0, 0, 0))],
        out_shape=[jax.ShapeDtypeStruct((N_CTX, GLA_VW), BF16),
                   jax.ShapeDtypeStruct((BATCH, 2, GLA_HEADS, GLA_DK, GLA_DV), F32)],
        scratch_shapes=_gla_scratch(SEQ),
        compiler_params=pltpu.CompilerParams(dimension_semantics=("parallel",), vmem_limit_bytes=VMEM_LIMIT),
        name="gla_ctx",
    )(gq, gk, gv, la, gg, gn, *masks)


def _gla_lat(layer, gq, gk, gv, la, gg, gn, masks, s0):
    rows = GLA_SEQS * DEC_SEQ
    blk0 = N_CTX // rows
    seq = lambda w: pl.BlockSpec((rows, w), lambda b: (blk0 + b, 0))
    return pl.pallas_call(
        partial(_gla_kernel, seq_len=DEC_SEQ, has_state=True),
        grid=(DEC_BATCH // GLA_SEQS,),
        in_specs=[seq(GLA_KW), seq(GLA_KW), seq(GLA_VW), seq(2 * GLA_KW), seq(GLA_VW),
                  pl.BlockSpec((1, GLA_VW), lambda b: (0, 0))] + _gla_mask_specs() + [
                  pl.BlockSpec((GLA_SEQS, None, 2, GLA_HEADS, GLA_DK, GLA_DV),
                               lambda b: (b, layer, 0, 0, 0, 0))],
        out_specs=pl.BlockSpec((rows, GLA_VW), lambda b: (b, 0)),
        out_shape=jax.ShapeDtypeStruct((N_LAT, GLA_VW), BF16),
        scratch_shapes=_gla_scratch(DEC_SEQ),
        compiler_params=pltpu.CompilerParams(dimension_semantics=("parallel",), vmem_limit_bytes=VMEM_LIMIT),
        name="gla_lat",
    )(gq, gk, gv, la, gg, gn, *masks, s0)


def _route(scores, bias):
    biased = scores + bias
    v = [biased[e:e + 1, :] for e in range(N_EXPERTS)]
    s = [scores[e:e + 1, :] for e in range(N_EXPERTS)]
    gscore = []
    for g in range(N_GROUPS):
        a, b, c, d = v[4 * g:4 * g + 4]
        hi1, lo1, hi2, lo2 = jnp.maximum(a, b), jnp.minimum(a, b), jnp.maximum(c, d), jnp.minimum(c, d)
        gscore.append(jnp.maximum(hi1, hi2) + jnp.maximum(jnp.minimum(hi1, hi2), jnp.maximum(lo1, lo2)))
    one = lambda cond: jnp.where(cond, 1.0, 0.0)
    picked, chosen = [], []
    for g in range(N_GROUPS):
        best = None
        for g2 in range(N_GROUPS):
            if g2 == g:
                continue
            ok = one((gscore[g] > gscore[g2]) if g2 < g else (gscore[g] >= gscore[g2]))
            best = ok if best is None else best * ok
        for i in range(EXPERTS_PER_GROUP):
            e = 4 * g + i
            beaten = None
            for j in range(EXPERTS_PER_GROUP):
                if j == i:
                    continue
                e2 = 4 * g + j
                lost = one((v[e2] > v[e]) if j > i else (v[e2] >= v[e]))
                beaten = lost if beaten is None else beaten + lost
            chosen.append(best * one(beaten < 1.5))
            picked.append(chosen[-1] * s[e])
    den = picked[0]
    for e in range(1, N_EXPERTS):
        den = den + picked[e]
    return jnp.concatenate([p / den for p in picked] + chosen, axis=0)


def _mix_kernel(xc_ref, xl_ref, oac_ref, oal_ref, ogc_ref, ogl_ref, sa_ref, sb_ref, mod_ref, wpa_f, wpb_f, wo_f, n2_ref,
                rw_ref, rb_ref, x1_out, h2_out, gt_out, wpa_ref, wpb_ref, wo_ref):
    @pl.when(pl.program_id(0) == 0)
    def _():
        wpa_ref[...] = wpa_f[...].astype(BF16)
        wpb_ref[...] = wpb_f[...].astype(BF16)
        wo_ref[...] = wo_f[...].astype(BF16)

    is_ctx = pl.program_id(0) < N_CTX // TM
    oa = jnp.where(is_ctx, oac_ref[...], oal_ref[...])
    og = jnp.where(is_ctx, ogc_ref[...], ogl_ref[...])
    a = _dot(oa, wpa_ref[...])
    b = _dot(og, wpb_ref[...])
    merged = sa_ref[...].astype(F32) * a + sb_ref[...].astype(F32) * b
    g1 = mod_ref[:, 2 * D_MODEL:3 * D_MODEL]
    sh2 = mod_ref[:, 3 * D_MODEL:4 * D_MODEL]
    sc2 = mod_ref[:, 4 * D_MODEL:5 * D_MODEL]
    x = jnp.where(is_ctx, xc_ref[...], xl_ref[...])
    x1 = x + g1 * _dot(merged.astype(BF16), wo_ref[...])
    x1_out[...] = x1
    h2 = _rms(x1) * n2_ref[...] * (1.0 + sc2) + sh2
    _to_row_tiles(h2_out, h2)
    hh, hl = _split2(h2)
    rh, rl = rw_ref[0], rw_ref[1]
    nt = lambda r, t: lax.dot_general(r, t, _NT, preferred_element_type=F32)
    logits = nt(rl, hh) + nt(rh, hl) + nt(rh, hh)
    gt_out[...] = _route(jax.nn.sigmoid(logits), rb_ref[...])


def _mix(layer, x_pair, oa_c, oa_l, og_c, og_l, sa, sb, mod_l, wpa, wpb, wo, norm2, rw3, rb):
    nt = N_TOK // TM
    n_ctx_t = N_CTX // TM
    row = lambda i: (i, 0)
    const = lambda i: (0, 0)
    ctx_row = lambda i: (jnp.minimum(i, n_ctx_t - 1), 0)
    lat_row = lambda i: (jnp.maximum(i - n_ctx_t, 0), 0)
    return pl.pallas_call(
        _mix_kernel,
        grid=(nt,),
        in_specs=_x_specs(x_pair) + [
            pl.BlockSpec((TM, ATTN_WIDTH), ctx_row),
            pl.BlockSpec((TM, ATTN_WIDTH), lat_row),
            pl.BlockSpec((TM, GLA_VW), ctx_row),
            pl.BlockSpec((TM, GLA_VW), lat_row),
            pl.BlockSpec((TM, D_MODEL), row),
            pl.BlockSpec((TM, D_MODEL), row),
            pl.BlockSpec((None, 1, MOD_W), lambda i: (_mod_row(i, TM), 0, 0)),
            pl.BlockSpec((None, ATTN_WIDTH, D_MODEL), lambda i: (layer, 0, 0), pipeline_mode=pl.Buffered(1)),
            pl.BlockSpec((None, GLA_VW, D_MODEL), lambda i: (layer, 0, 0), pipeline_mode=pl.Buffered(1)),
            pl.BlockSpec((None, D_MODEL, D_MODEL), lambda i: (layer, 0, 0), pipeline_mode=pl.Buffered(1)),
            pl.BlockSpec((1, D_MODEL), const),
            pl.BlockSpec((2, N_EXPERTS, D_MODEL), lambda i: (0, 0, 0)),
            pl.BlockSpec((N_EXPERTS, 1), const),
        ],
        out_specs=[pl.BlockSpec((TM, D_MODEL), row), pl.BlockSpec((TM * ROW_TILE, LANES), row),
                   pl.BlockSpec((2 * N_EXPERTS, TM), lambda i: (0, i))],
        out_shape=[jax.ShapeDtypeStruct((N_TOK, D_MODEL), F32),
                   jax.ShapeDtypeStruct((N_TOK * ROW_TILE, LANES), F32),
                   jax.ShapeDtypeStruct((2 * N_EXPERTS, N_TOK), F32)],
        scratch_shapes=[pltpu.VMEM((ATTN_WIDTH, D_MODEL), BF16), pltpu.VMEM((GLA_VW, D_MODEL), BF16),
                        pltpu.VMEM((D_MODEL, D_MODEL), BF16)],
        compiler_params=pltpu.CompilerParams(dimension_semantics=("arbitrary",), vmem_limit_bytes=VMEM_LIMIT),
        name="mix",
    )(*x_pair, oa_c, oa_l, og_c, og_l, sa, sb, mod_l, wpa, wpb, wo, norm2, rw3, rb)


def _route_plan(gsel):
    gates = gsel[:N_EXPERTS]
    sel = gsel[N_EXPERTS:] > 0.5
    seli = sel.astype(jnp.int32)
    incl = jnp.cumsum(seli, axis=1)
    count = incl[:, -1]
    ntile = (count + TS - 1) // TS
    tile_end = jnp.cumsum(ntile)
    tile_start = tile_end - ntile
    total = tile_end[-1]
    slot = tile_start[:, None] * TS + incl - seli
    pos_a = jnp.min(jnp.where(sel, slot, N_SLOT), axis=0)
    pos_b = jnp.max(jnp.where(sel, slot, -1), axis=0)
    w_a = jnp.sum(jnp.where(sel & (slot == pos_a[None, :]), gates, 0.0), axis=0)
    w_b = jnp.sum(jnp.where(sel & (slot == pos_b[None, :]), gates, 0.0), axis=0)
    tile_id = jnp.minimum(jnp.arange(N_TILE, dtype=jnp.int32), total - 1)
    tile_expert = jnp.sum((tile_id[:, None] >= tile_end[None, :]).astype(jnp.int32), axis=1)
    nt = N_TOK // TM
    pos = jnp.concatenate([pos_a.reshape(nt, 1, TM), pos_b.reshape(nt, 1, TM)], axis=2).astype(jnp.int32)
    plan = jnp.concatenate([tile_start, ntile, total[None]]).astype(jnp.int32)
    return pos, jnp.stack([w_a, w_b], axis=1), tile_expert.astype(jnp.int32), plan


def _to_row_tiles(ref, x):
    for k in range(ROW_TILE):
        ref[pl.ds(k, x.shape[0], stride=ROW_TILE), :] = x[:, k * LANES:(k + 1) * LANES]


def _from_row_tiles(ref, rows):
    return jnp.concatenate([ref[pl.ds(k, rows, stride=ROW_TILE), :] for k in range(ROW_TILE)], axis=1)


def _tile_rows(row, n=1):
    return pl.ds(pl.multiple_of(row * ROW_TILE, ROW_TILE), n * ROW_TILE)


def _row_copy(src, src_row, dst, dst_row, sem):
    return pltpu.make_async_copy(src.at[_tile_rows(src_row), :], dst.at[_tile_rows(dst_row), :], sem)


def _dispatch_kernel(plan_ref, pos_ref, h_ref, xs_ref, zero_ref, sem_z, sem):
    i = pl.program_id(0)
    total = plan_ref[2 * N_EXPERTS]

    def zero_copy(tile):
        return pltpu.make_async_copy(zero_ref, xs_ref.at[_tile_rows(tile * TS, TS), :], sem_z)

    def last_tile(e):
        return plan_ref[e] + plan_ref[N_EXPERTS + e] - 1

    @pl.when(i == 0)
    def _():
        zero_ref[...] = jnp.zeros(zero_ref.shape, F32)

        def each(fn):
            def expert(e, c):
                @pl.when(plan_ref[N_EXPERTS + e] > 0)
                def _():
                    fn(zero_copy(last_tile(e)))
                return c

            def unused(t, c):
                @pl.when(t >= total)
                def _():
                    fn(zero_copy(t))
                return c

            lax.fori_loop(0, N_EXPERTS, expert, 0)
            lax.fori_loop(0, N_TILE, unused, 0)

        each(lambda cp: cp.start())
        each(lambda cp: cp.wait())

    def start(r, c):
        _row_copy(h_ref, r, xs_ref, pos_ref[0, r], sem).start()
        _row_copy(h_ref, r, xs_ref, pos_ref[0, TM + r], sem).start()
        return c

    def wait(r, c):
        _row_copy(h_ref, 0, xs_ref, 0, sem).wait()
        return c

    lax.fori_loop(0, TM, start, 0, unroll=8)
    lax.fori_loop(0, 2 * TM, wait, 0, unroll=8)


def _dispatch(plan, pos, h2):
    return pl.pallas_call(
        _dispatch_kernel,
        grid_spec=pltpu.PrefetchScalarGridSpec(
            num_scalar_prefetch=1,
            grid=(N_TOK // TM,),
            in_specs=[
                pl.BlockSpec((None, 1, 2 * TM), lambda i, plan: (i, 0, 0), memory_space=pltpu.SMEM),
                pl.BlockSpec((TM * ROW_TILE, LANES), lambda i, plan: (i, 0)),
            ],
            out_specs=pl.BlockSpec(memory_space=pl.ANY),
            scratch_shapes=[pltpu.VMEM((TS * ROW_TILE, LANES), F32), pltpu.SemaphoreType.DMA,
                            pltpu.SemaphoreType.DMA],
        ),
        out_shape=jax.ShapeDtypeStruct((N_SLOT * ROW_TILE, LANES), F32),
        compiler_params=pltpu.CompilerParams(dimension_semantics=("arbitrary",), vmem_limit_bytes=VMEM_LIMIT),
        name="dispatch",
    )(plan, pos, h2)


def _moe_kernel(te_ref, plan_ref, xs_ref, wg_ref, wu_ref, wd_ref, ys_ref, wg_s, wu_s, wd_s):
    i = pl.program_id(0)
    total = plan_ref[2 * N_EXPERTS]

    @pl.when(i < total)
    def _():
        @pl.when((i == 0) | (te_ref[i] != te_ref[jnp.maximum(i - 1, 0)]))
        def _():
            wg_s[...] = wg_ref[...].astype(BF16)
            wu_s[...] = wu_ref[...].astype(BF16)
            wd_s[...] = wd_ref[...].astype(BF16)

        x = _from_row_tiles(xs_ref, TS).astype(BF16)
        up = _dot(x, wu_s[...])
        gt = _dot(x, wg_s[...])
        act = (gt * jax.nn.sigmoid(gt) * up).astype(BF16)
        _to_row_tiles(ys_ref, _dot(act, wd_s[...]))

    @pl.when(i >= total)
    def _():
        ys_ref[...] = jnp.zeros(ys_ref.shape, F32)


def _moe(layer, tile_expert, plan, xs, w_gate, w_up, w_down):
    last = lambda i, plan: jnp.minimum(i, plan[2 * N_EXPERTS] - 1)
    w_spec = lambda a, b: pl.BlockSpec((None, None, a, b), lambda i, te, plan: (layer, te[i], 0, 0))
    return pl.pallas_call(
        _moe_kernel,
        grid_spec=pltpu.PrefetchScalarGridSpec(
            num_scalar_prefetch=2,
            grid=(N_TILE,),
            in_specs=[
                pl.BlockSpec((TS * ROW_TILE, LANES), lambda i, te, plan: (last(i, plan), 0)),
                w_spec(D_MODEL, D_EXPERT), w_spec(D_MODEL, D_EXPERT), w_spec(D_EXPERT, D_MODEL),
            ],
            out_specs=pl.BlockSpec((TS * ROW_TILE, LANES), lambda i, te, plan: (i, 0)),
            scratch_shapes=[pltpu.VMEM((D_MODEL, D_EXPERT), BF16), pltpu.VMEM((D_MODEL, D_EXPERT), BF16),
                            pltpu.VMEM((D_EXPERT, D_MODEL), BF16)],
        ),
        out_shape=jax.ShapeDtypeStruct((N_SLOT * ROW_TILE, LANES), F32),
        compiler_params=pltpu.CompilerParams(dimension_semantics=("arbitrary",), vmem_limit_bytes=VMEM_LIMIT),
        name="moe",
    )(tile_expert, plan, xs, w_gate, w_up, w_down)


def _combine_kernel(pos_ref, nxt_ref, w_ref, x1_ref, mod_ref, ys_ref, out_ref, ya_ref, yb_ref, sem, *, n):
    i = pl.program_id(0)

    def gather(p_ref, slot):
        def start(r, c):
            _row_copy(ys_ref, p_ref[0, r], ya_ref.at[slot], r, sem.at[slot]).start()
            _row_copy(ys_ref, p_ref[0, TM + r], yb_ref.at[slot], r, sem.at[slot]).start()
            return c
        lax.fori_loop(0, TM, start, 0, unroll=8)

    @pl.when(i == 0)
    def _():
        gather(pos_ref, 0)

    @pl.when(i + 1 < n)
    def _():
        gather(nxt_ref, (i + 1) % 2)

    slot = i % 2

    def wait(r, c):
        _row_copy(ys_ref, 0, ya_ref.at[slot], 0, sem.at[slot]).wait()
        return c

    lax.fori_loop(0, 2 * TM, wait, 0, unroll=8)
    g2 = mod_ref[:, 5 * D_MODEL:6 * D_MODEL]
    w = w_ref[...]
    ya = _from_row_tiles(ya_ref.at[slot], TM)
    yb = _from_row_tiles(yb_ref.at[slot], TM)
    out_ref[...] = x1_ref[...] + g2 * (w[:, 0:1] * ya + w[:, 1:2] * yb)


def _combine(pos, w_ab, x1, mod_l, ys, tile0, nt):
    pos_spec = lambda fn: pl.BlockSpec((None, 1, 2 * TM), fn, memory_space=pltpu.SMEM)
    return pl.pallas_call(
        partial(_combine_kernel, n=nt),
        grid=(nt,),
        in_specs=[
            pos_spec(lambda i: (tile0 + i, 0, 0)),
            pos_spec(lambda i: (tile0 + jnp.minimum(i + 1, nt - 1), 0, 0)),
            pl.BlockSpec((TM, 2), lambda i: (tile0 + i, 0)),
            pl.BlockSpec((TM, D_MODEL), lambda i: (tile0 + i, 0)),
            pl.BlockSpec((None, 1, MOD_W), lambda i: (_mod_row(tile0 + i, TM), 0, 0)),
            pl.BlockSpec(memory_space=pl.ANY),
        ],
        out_specs=pl.BlockSpec((TM, D_MODEL), lambda i: (i, 0)),
        out_shape=jax.ShapeDtypeStruct((nt * TM, D_MODEL), F32),
        scratch_shapes=[pltpu.VMEM((2, TM * ROW_TILE, LANES), F32), pltpu.VMEM((2, TM * ROW_TILE, LANES), F32),
                        pltpu.SemaphoreType.DMA((2,))],
        compiler_params=pltpu.CompilerParams(dimension_semantics=("arbitrary",), vmem_limit_bytes=VMEM_LIMIT),
        name="combine",
    )(pos, pos, w_ab, x1, mod_l, ys)


def _rope_tables():
    pos = np.arange(DEC_SEQ)
    rows = (pos // GRID_W).astype(np.float64)
    cols = (pos % GRID_W).astype(np.float64)
    half = HEAD_DIM // 2
    inv_freq = ROPE_THETA ** (-np.arange(0, half, 2, dtype=np.float64) / half)
    ang_r = rows[:, None] * inv_freq[None, :]
    ang_c = cols[:, None] * inv_freq[None, :]
    cos_h = np.concatenate([np.cos(ang_r), np.cos(ang_r), np.cos(ang_c), np.cos(ang_c)], axis=-1)
    sin_h = np.concatenate([-np.sin(ang_r), np.sin(ang_r), -np.sin(ang_c), np.sin(ang_c)], axis=-1)
    cos = np.concatenate([np.ones((TM, HEAD_DIM)), cos_h], axis=0)
    sin = np.concatenate([np.zeros((TM, HEAD_DIM)), sin_h], axis=0)
    return (jnp.asarray(np.tile(cos, (1, N_HEADS)), F32), jnp.asarray(np.tile(sin, (1, N_HEADS)), F32))


def _pack_w2(w2):
    z = jnp.zeros((GLA_RANK, GLA_KW), w2.dtype)
    top = jnp.concatenate([w2[0], z], axis=1)
    bot = jnp.concatenate([z, w2[1]], axis=1)
    pad = jnp.zeros((LR_PAD - 2 * GLA_RANK, 2 * GLA_KW), w2.dtype)
    return jnp.concatenate([top, bot, pad], axis=0).astype(BF16)


def kernel(x_prompt, x_sample, cache_k, cache_v, state_gla, c, c_ctx, norm1, norm2, w_ada, b_ada, w_in,
           q_norm, k_norm, gla_w2, gla_b, gla_norm, w_pa, w_pb, w_out, router_w, router_bias,
           w_gate, w_up, w_down):
    x_pair = (x_prompt.reshape(N_CTX, D_MODEL), x_sample.reshape(N_LAT, D_MODEL))
    cond =jnp.concatenate([c_ctx[None, :], c, jnp.zeros((N_COND - 1 - DEC_BATCH, D_MODEL), F32)], axis=0)
    mod = _ada(cond, w_ada, b_ada).reshape(DEPTH, N_COND, 1, MOD_W)
    cos, sin = _rope_tables()
    gla_masks = _gla_masks()
    lane_head = np.arange(ATTN_WIDTH) // HEAD_DIM
    bd = jnp.asarray(lane_head[:, None] == lane_head[None, :], BF16)
    rw3 = jnp.stack(_split2(router_w.T), axis=0)
    rb = router_bias.reshape(N_EXPERTS, 1)
    ck = cache_k.reshape(DEC_BATCH, DEPTH, PAST_LEN, KV_WIDTH)
    cv = cache_v.reshape(DEC_BATCH, DEPTH, PAST_LEN, KV_WIDTH)

    keys, vals, states = [], [], []
    for l in range(DEPTH):
        q, kf, kr, v, gq, gk, gv, gg, la, sa, sb = _inproj(
            l, x_pair, mod[l], norm1[l][None, :], w_in, jnp.tile(q_norm[l], N_HEADS)[None, :],
            jnp.tile(k_norm[l], KV_HEADS)[None, :], cos, sin, bd, _pack_w2(gla_w2[l]),
            gla_b[l].reshape(1, 2 * GLA_KW))
        oa_c = _attn_ctx(q, kr, v)
        oa_l = _attn_lat(l, q, kr, v, ck, cv)
        gn = jnp.tile(gla_norm[l], GLA_HEADS)[None, :]
        og_c, s_fin = _gla_ctx(gq, gk, gv, la, gg, gn, gla_masks)
        og_l = _gla_lat(l, gq, gk, gv, la, gg, gn, gla_masks, state_gla)
        x1, h2, gsel = _mix(l, x_pair, oa_c, oa_l, og_c, og_l, sa, sb, mod[l], w_pa, w_pb, w_out,
                            norm2[l][None, :], rw3, rb)
        pos, w_ab, tile_expert, plan = _route_plan(gsel)
        xs = _dispatch(plan, pos, h2)
        ys = _moe(l, tile_expert, plan, xs, w_gate, w_up, w_down)
        n_ctx_t = N_CTX // TM
        if l + 1 < DEPTH:
            x = _combine(pos, w_ab, x1, mod[l], ys, 0, N_TOK // TM)
            x_pair = (x, x)
        else:
            x_pair = (_combine(pos, w_ab, x1, mod[l], ys, 0, n_ctx_t),
                      _combine(pos, w_ab, x1, mod[l], ys, n_ctx_t, N_LAT // TM))
        keys.append(kf[:N_CTX].reshape(BATCH, SEQ, KV_HEADS, HEAD_DIM))
        vals.append(v[:N_CTX].reshape(BATCH, SEQ, KV_HEADS, HEAD_DIM))
        states.append(s_fin)
    y_prompt = x_pair[0].reshape(BATCH, SEQ, D_MODEL)
    y_sample = x_pair[1].reshape(DEC_BATCH, DEC_SEQ, D_MODEL)
    return (y_prompt, y_sample, jnp.stack(keys, axis=1), jnp.stack(vals, axis=1), jnp.stack(states, axis=1))
```

```python
from functools import partial

import jax
import jax.numpy as jnp
import numpy as np
from jax import lax
from jax.experimental import pallas as pl
from jax.experimental.pallas import tpu as pltpu

F32 = jnp.float32
BF16 = jnp.bfloat16

D_MODEL = 1024
BATCH = 16
SEQ = 256
DEPTH = 2
DEC_BATCH = 4
DEC_SEQ = 1024
PAST_LEN = 512
GRID_W = 64
N_HEADS = 8
KV_HEADS = 2
GROUP = N_HEADS // KV_HEADS
HEAD_DIM = 64
ATTN_WIDTH = N_HEADS * HEAD_DIM
KV_WIDTH = KV_HEADS * HEAD_DIM
ROPE_THETA = 10000.0
GLA_HEADS = 4
GLA_DK = 64
GLA_DV = 128
GLA_KW = GLA_HEADS * GLA_DK
GLA_VW = GLA_HEADS * GLA_DV
GLA_RANK = 16
GLA_GATE_NORM = 16.0
N_EXPERTS = 16
N_GROUPS = 4
EXPERTS_PER_GROUP = N_EXPERTS // N_GROUPS
D_EXPERT = 512
NORM_EPS = 1e-6

N_CTX = BATCH * SEQ
N_LAT = DEC_BATCH * DEC_SEQ
N_TOK = N_CTX + N_LAT
N_COND = 8
MOD_W = 6 * D_MODEL

TM = 512
TS = 512
N_TILE = 2 * N_TOK // TS + N_EXPERTS
N_SLOT = N_TILE * TS
ROW_TILE = 8
LANES = D_MODEL // ROW_TILE
GLA_CHUNK = 128
GLA_LEVELS = (64, 32, 16)
GLA_DIAG = 16
GLA_SEQS = 1
LR_PAD = 128
SEG_MIX = ATTN_WIDTH + 2 * KV_WIDTH + 2 * GLA_KW + 2 * GLA_VW
OFF_GA = SEG_MIX
OFF_GB = OFF_GA + D_MODEL
OFF_LR = OFF_GB + D_MODEL
IN_PACKED = OFF_LR + LR_PAD
IN_WIDTH = SEG_MIX + 2 * GLA_RANK + 2 * D_MODEL
VMEM_LIMIT = 56 * 1024 * 1024

_NT = (((1,), (1,)), ((), ()))
_TN = (((0,), (0,)), ((), ()))


def _mod_row(tile, tm):
    start = tile * tm
    return jnp.where(start < N_CTX, 0, 1 + (start - N_CTX) // DEC_SEQ)


def _split3(x):
    hi = x.astype(BF16)
    r1 = x - hi.astype(F32)
    mid = r1.astype(BF16)
    lo = (r1 - mid.astype(F32)).astype(BF16)
    return hi, mid, lo


def _split2(x):
    hi = x.astype(BF16)
    lo = (x - hi.astype(F32)).astype(BF16)
    return hi, lo


def _dot(a, b):
    return jnp.dot(a, b, preferred_element_type=F32)


def _rms(x):
    return x * lax.rsqrt(jnp.mean(x * x, axis=-1, keepdims=True) + NORM_EPS)


def _ada_kernel(cond_ref, w_ref, b_ref, out_ref):
    c = cond_ref[...]
    s = c * jax.nn.sigmoid(c)
    out_ref[...] = _dot(s.astype(BF16), w_ref[...].astype(BF16)) + b_ref[...]


def _ada(cond, w_ada, b_ada):
    nb = MOD_W // D_MODEL
    return pl.pallas_call(
        _ada_kernel,
        grid=(DEPTH, nb),
        in_specs=[
            pl.BlockSpec((N_COND, D_MODEL), lambda l, j: (0, 0)),
            pl.BlockSpec((None, D_MODEL, D_MODEL), lambda l, j: (l, 0, j)),
            pl.BlockSpec((None, 1, D_MODEL), lambda l, j: (l, 0, j)),
        ],
        out_specs=pl.BlockSpec((None, N_COND, D_MODEL), lambda l, j: (l, 0, j)),
        out_shape=jax.ShapeDtypeStruct((DEPTH, N_COND, MOD_W), F32),
        compiler_params=pltpu.CompilerParams(vmem_limit_bytes=VMEM_LIMIT),
        name="ada",
    )(cond, w_ada, b_ada.reshape(DEPTH, 1, MOD_W))


def _head_rmsnorm(x, bd):
    ssq = _dot((x * x).astype(BF16), bd)
    return x * lax.rsqrt(ssq * (1.0 / HEAD_DIM) + NORM_EPS)


def _rope(x, cos, sin):
    n = x.shape[-1]
    lane = lax.broadcasted_iota(jnp.int32, x.shape, 1)
    first = (lane & (HEAD_DIM // 4)) == 0
    partner = jnp.where(first, pltpu.roll(x, n - HEAD_DIM // 4, 1), pltpu.roll(x, HEAD_DIM // 4, 1))
    return x * cos + partner * sin


def _pack_w_in(w_ref, wp_ref):
    lr0 = SEG_MIX
    ga0 = lr0 + 2 * GLA_RANK
    rows_per_step = 128

    def body(i, carry):
        rows = pl.ds(pl.multiple_of(i * rows_per_step, rows_per_step), rows_per_step)
        wp_ref[rows, 0:SEG_MIX] = w_ref[rows, 0:SEG_MIX].astype(BF16)
        tail = w_ref[rows, SEG_MIX:IN_WIDTH]
        lr = tail[:, 0:2 * GLA_RANK]
        wp_ref[rows, OFF_LR:IN_PACKED] = jnp.concatenate(
            [lr, jnp.zeros((rows_per_step, LR_PAD - 2 * GLA_RANK), F32)], axis=1).astype(BF16)
        wp_ref[rows, OFF_GA:OFF_LR] = tail[:, ga0 - lr0:ga0 - lr0 + 2 * D_MODEL].astype(BF16)
        return carry

    lax.fori_loop(0, D_MODEL // rows_per_step, body, 0)


def _inproj_kernel(xc_ref, xl_ref, mod_ref, n1_ref, wf_ref, qn_ref, kn_ref, cos_ref, sin_ref, bd_ref, w2_ref, gb_ref,
                   q_out, kf_out, kr_out, v_out, gq_out, gk_out, gv_out, gg_out, la_out, sa_out, sb_out,
                   w_ref):
    @pl.when(pl.program_id(0) == 0)
    def _():
        _pack_w_in(wf_ref, w_ref)

    x = jnp.where(pl.program_id(0) < N_CTX // TM, xc_ref[...], xl_ref[...])
    sh1 = mod_ref[:, 0:D_MODEL]
    sc1 = mod_ref[:, D_MODEL:2 * D_MODEL]
    h = (_rms(x) * n1_ref[...] * (1.0 + sc1) + sh1).astype(BF16)

    def proj(lo, width):
        return _dot(h, w_ref[:, lo:lo + width])

    cos = cos_ref[...]
    sin = sin_ref[...]
    bd = bd_ref[...]
    off = 0
    q = _head_rmsnorm(proj(off, ATTN_WIDTH), bd) * qn_ref[...]
    q_out[...] = (_rope(q, cos, sin) * HEAD_DIM ** -0.5).astype(BF16)
    off += ATTN_WIDTH
    k = _head_rmsnorm(proj(off, KV_WIDTH), bd[:KV_WIDTH, :KV_WIDTH]) * kn_ref[...]
    kf_out[...] = k
    kr_out[...] = _rope(k, cos[:, :KV_WIDTH], sin[:, :KV_WIDTH]).astype(BF16)
    off += KV_WIDTH
    v_out[...] = proj(off, KV_WIDTH)
    off += KV_WIDTH
    gq_out[...] = (proj(off, GLA_KW) * GLA_DK ** -0.5).astype(BF16)
    off += GLA_KW
    gk_out[...] = proj(off, GLA_KW).astype(BF16)
    off += GLA_KW
    gv_out[...] = proj(off, GLA_VW).astype(BF16)
    off += GLA_VW
    gg = proj(off, GLA_VW)
    gg_out[...] = (gg * jax.nn.sigmoid(gg)).astype(BF16)
    sa_out[...] = jax.nn.sigmoid(proj(OFF_GA, D_MODEL)).astype(BF16)
    sb_out[...] = jax.nn.sigmoid(proj(OFF_GB, D_MODEL)).astype(BF16)
    lr = proj(OFF_LR, LR_PAD)
    z = _dot(lr.astype(BF16), w2_ref[...]) + gb_ref[...]
    log_sig = jnp.minimum(z, 0.0) - jnp.log1p(jnp.exp(-jnp.abs(z)))
    la_out[...] = log_sig * (1.0 / GLA_GATE_NORM)


def _x_specs(x_pair):
    n_ctx_t = N_CTX // TM
    lat0 = n_ctx_t if x_pair[0] is x_pair[1] else 0
    return [pl.BlockSpec((TM, D_MODEL), lambda i: (jnp.minimum(i, n_ctx_t - 1), 0)),
            pl.BlockSpec((TM, D_MODEL), lambda i: (lat0 + jnp.maximum(i - n_ctx_t, 0), 0))]


def _inproj(layer, x_pair, mod_l, norm1, w_in, qn, kn, cos, sin, bd, w2p, gb):
    nt = N_TOK // TM
    n_ctx_t = N_CTX // TM
    t_per_seq = DEC_SEQ // TM
    row = lambda i: (i, 0)
    const = lambda i: (0, 0)
    rope_blk = lambda i: (jnp.where(i < n_ctx_t, 0, 1 + (i - n_ctx_t) % t_per_seq), 0)

    def out(width, dtype):
        return pl.BlockSpec((TM, width), row), jax.ShapeDtypeStruct((N_TOK, width), dtype)

    outs = [out(ATTN_WIDTH, BF16), out(KV_WIDTH, F32), out(KV_WIDTH, BF16), out(KV_WIDTH, F32),
            out(GLA_KW, BF16), out(GLA_KW, BF16), out(GLA_VW, BF16), out(GLA_VW, BF16),
            out(2 * GLA_KW, F32), out(D_MODEL, BF16), out(D_MODEL, BF16)]
    return pl.pallas_call(
        _inproj_kernel,
        grid=(nt,),
        in_specs=_x_specs(x_pair) + [
            pl.BlockSpec((None, 1, MOD_W), lambda i: (_mod_row(i, TM), 0, 0)),
            pl.BlockSpec((1, D_MODEL), const),
            pl.BlockSpec((None, D_MODEL, IN_WIDTH), lambda i: (layer, 0, 0), pipeline_mode=pl.Buffered(1)),
            pl.BlockSpec((1, ATTN_WIDTH), const),
            pl.BlockSpec((1, KV_WIDTH), const),
            pl.BlockSpec((TM, ATTN_WIDTH), rope_blk),
            pl.BlockSpec((TM, ATTN_WIDTH), rope_blk),
            pl.BlockSpec((ATTN_WIDTH, ATTN_WIDTH), const),
            pl.BlockSpec((LR_PAD, 2 * GLA_KW), const),
            pl.BlockSpec((1, 2 * GLA_KW), const),
        ],
        out_specs=[o[0] for o in outs],
        out_shape=[o[1] for o in outs],
        scratch_shapes=[pltpu.VMEM((D_MODEL, IN_PACKED), BF16)],
        compiler_params=pltpu.CompilerParams(dimension_semantics=("arbitrary",), vmem_limit_bytes=VMEM_LIMIT),
        name="inproj",
    )(*x_pair, mod_l, norm1, w_in, qn, kn, cos, sin, bd, w2p, gb)


def _attn_kernel(*refs, has_cache):
    if has_cache:
        q_ref, k_ref, v_ref, ck_ref, cv_ref, o_ref = refs
    else:
        q_ref, k_ref, v_ref, o_ref = refs
    tq = q_ref.shape[0]
    for g in range(KV_HEADS):
        ks = slice(g * HEAD_DIM, (g + 1) * HEAD_DIM)
        qs = jnp.concatenate(
            [q_ref[:, (g * GROUP + j) * HEAD_DIM:(g * GROUP + j + 1) * HEAD_DIM] for j in range(GROUP)], axis=0)
        k = k_ref[:, ks]
        v = v_ref[:, ks].astype(BF16)
        s = lax.dot_general(qs, k, _NT, preferred_element_type=F32)
        m = jnp.max(s, axis=-1, keepdims=True)
        if has_cache:
            ck = ck_ref[:, ks].astype(BF16)
            cv = cv_ref[:, ks].astype(BF16)
            s2 = lax.dot_general(qs, ck, _NT, preferred_element_type=F32)
            m = jnp.maximum(m, jnp.max(s2, axis=-1, keepdims=True))
        p = jnp.exp(s - m)
        den = jnp.sum(p, axis=-1, keepdims=True)
        o = _dot(p.astype(BF16), v)
        if has_cache:
            p2 = jnp.exp(s2 - m)
            den = den + jnp.sum(p2, axis=-1, keepdims=True)
            o = o + _dot(p2.astype(BF16), cv)
        o = o / den
        for j in range(GROUP):
            hd = g * GROUP + j
            o_ref[:, hd * HEAD_DIM:(hd + 1) * HEAD_DIM] = o[j * tq:(j + 1) * tq, :].astype(BF16)


def _attn_ctx(q, kr, v):
    return pl.pallas_call(
        partial(_attn_kernel, has_cache=False),
        grid=(BATCH,),
        in_specs=[
            pl.BlockSpec((SEQ, ATTN_WIDTH), lambda i: (i, 0)),
            pl.BlockSpec((SEQ, KV_WIDTH), lambda i: (i, 0)),
            pl.BlockSpec((SEQ, KV_WIDTH), lambda i: (i, 0)),
        ],
        out_specs=pl.BlockSpec((SEQ, ATTN_WIDTH), lambda i: (i, 0)),
        out_shape=jax.ShapeDtypeStruct((N_CTX, ATTN_WIDTH), BF16),
        compiler_params=pltpu.CompilerParams(dimension_semantics=("parallel",), vmem_limit_bytes=VMEM_LIMIT),
        name="attn_ctx",
    )(q, kr, v)


def _attn_lat(layer, q, kr, v, cache_k, cache_v):
    tq = 256
    nq = DEC_SEQ // tq
    q0 = N_CTX // tq
    s0 = N_CTX // DEC_SEQ
    return pl.pallas_call(
        partial(_attn_kernel, has_cache=True),
        grid=(DEC_BATCH, nq),
        in_specs=[
            pl.BlockSpec((tq, ATTN_WIDTH), lambda b, i: (q0 + b * nq + i, 0)),
            pl.BlockSpec((DEC_SEQ, KV_WIDTH), lambda b, i: (s0 + b, 0)),
            pl.BlockSpec((DEC_SEQ, KV_WIDTH), lambda b, i: (s0 + b, 0)),
            pl.BlockSpec((None, None, PAST_LEN, KV_WIDTH), lambda b, i: (b, layer, 0, 0)),
            pl.BlockSpec((None, None, PAST_LEN, KV_WIDTH), lambda b, i: (b, layer, 0, 0)),
        ],
        out_specs=pl.BlockSpec((tq, ATTN_WIDTH), lambda b, i: (b * nq + i, 0)),
        out_shape=jax.ShapeDtypeStruct((N_LAT, ATTN_WIDTH), BF16),
        compiler_params=pltpu.CompilerParams(dimension_semantics=("parallel", "parallel"),
                                             vmem_limit_bytes=VMEM_LIMIT),
        name="attn_lat",
    )(q, kr, v, cache_k, cache_v)


def _gla_masks():
    c = GLA_CHUNK
    hrow = np.arange(GLA_HEADS * c) // c
    head_k = (hrow[:, None] == (np.arange(GLA_KW) // GLA_DK)[None, :])
    head_v = (hrow[:, None] == (np.arange(GLA_VW) // GLA_DV)[None, :])
    t = np.arange(c)[:, None]
    s = (np.arange(GLA_HEADS * c) % c)[None, :]
    pair = []
    for reverse in (False, True):
        for m in GLA_LEVELS:
            pair.append(t // (2 * m) == s // (2 * m))
        pair.append((t // GLA_DIAG == s // GLA_DIAG) & ((s >= t) if reverse else (s <= t)))
    return (jnp.asarray(head_k, BF16), jnp.asarray(head_v, BF16), jnp.asarray(np.stack(pair), F32))


def _block_diag(x, head_mask):
    return jnp.concatenate([x] * GLA_HEADS, axis=0) * head_mask


def _ref_rows(b, rows, rep):
    return jnp.concatenate([jnp.broadcast_to(b[r:r + 1, :], (rep, b.shape[1])) for r in rows], axis=0)


def _gla_scores(q, k, b, reverse, head_k, pair_ref):
    c = GLA_CHUNK
    t_idx = lax.broadcasted_iota(jnp.int32, (c, GLA_KW), 0)
    term0 = (len(GLA_LEVELS) + 1) if reverse else 0

    def scores(qt, kt):
        return lax.dot_general(qt.astype(BF16), _block_diag(kt.astype(BF16), head_k), _NT,
                               preferred_element_type=F32)

    total = None
    for i, m in enumerate(GLA_LEVELS):
        nblk = c // (2 * m)
        ref = _ref_rows(b, [2 * m * j + (m if reverse else m - 1) for j in range(nblk)], 2 * m)
        q_side = ((t_idx & m) == 0) if reverse else ((t_idx & m) != 0)
        qt = jnp.where(q_side, q * jnp.exp(b - ref), 0.0)
        kt = jnp.where(q_side, 0.0, k * jnp.exp(ref - b))
        part = scores(qt, kt) * pair_ref[term0 + i]
        total = part if total is None else total + part
    d = GLA_DIAG
    x = b - _ref_rows(b, [d * j + d // 2 for j in range(c // d)], d)
    diag = scores(q * jnp.exp(x), k * jnp.exp(-x))
    return total + jnp.where(pair_ref[term0 + len(GLA_LEVELS)] > 0.5, diag, 0.0)


def _cumsum_rows(tri, la):
    hi, lo = _split2(la)
    return _dot(tri, hi) + _dot(tri, lo)


def _gla_kernel(*refs, seq_len, has_state):
    gq_ref, gk_ref, gv_ref, la_ref, gg_ref, gn_ref, hk_ref, hv_ref, pair_ref = refs[:9]
    if has_state:
        s0_ref, o_ref, acc_ref, accb_ref, st_ref = refs[9:]
    else:
        o_ref, sfin_ref, acc_ref, accb_ref, st_ref = refs[9:]
    c = GLA_CHUNK
    nc = seq_len // c
    r_i = lax.broadcasted_iota(jnp.int32, (c, c), 0)
    c_i = lax.broadcasted_iota(jnp.int32, (c, c), 1)
    tril = jnp.where(c_i <= r_i, 1.0, 0.0).astype(BF16)
    triu = jnp.where(c_i >= r_i, 1.0, 0.0).astype(BF16)
    st_head_r = lax.broadcasted_iota(jnp.int32, (GLA_VW, GLA_KW), 0) // GLA_DV
    st_head_c = lax.broadcasted_iota(jnp.int32, (GLA_VW, GLA_KW), 1) // GLA_DK
    st_mask = st_head_r == st_head_c

    for s in range(GLA_SEQS):
        for d in range(2):
            st_ref[2 * s + d] = jnp.zeros((GLA_VW, GLA_KW), F32)
            if has_state:
                for hh in range(GLA_HEADS):
                    st_ref[2 * s + d, hh * GLA_DV:(hh + 1) * GLA_DV, hh * GLA_DK:(hh + 1) * GLA_DK] = s0_ref[s, d, hh].T

    def load(s, n):
        rows = pl.ds(pl.multiple_of(s * seq_len + n * c, c), c)
        return (rows, gq_ref[rows, :].astype(F32), gk_ref[rows, :].astype(F32), gv_ref[rows, :])

    def inter(d, q, k, v, b, b_end):
        st = st_ref[d]
        q_in = (q * jnp.exp(b)).astype(BF16)
        o = lax.dot_general(q_in, st.astype(BF16), _NT, preferred_element_type=F32)
        k_st = (k * jnp.exp(b_end - b)).astype(BF16)
        upd = lax.dot_general(v, k_st, _TN, preferred_element_type=F32)
        st_ref[d] = st * jnp.exp(b_end) + jnp.where(st_mask, upd, 0.0)
        return o

    def body(n, carry):
        for s in range(GLA_SEQS):
            rows, q, k, v = load(s, n)
            b_f = _cumsum_rows(tril, la_ref[rows, 0:GLA_KW])
            b_b = _cumsum_rows(triu, la_ref[rows, GLA_KW:2 * GLA_KW])
            head_k = hk_ref[...]
            a = _gla_scores(q, k, b_f, False, head_k, pair_ref) + _gla_scores(q, k, b_b, True, head_k, pair_ref)
            o = _dot(a.astype(BF16), _block_diag(v, hv_ref[...]))
            acc_ref[rows, :] = o + inter(2 * s, q, k, v, b_f, b_f[c - 1:c, :])
            rows, q, k, v = load(s, nc - 1 - n)
            b_b = _cumsum_rows(triu, la_ref[rows, GLA_KW:2 * GLA_KW])
            accb_ref[rows, :] = inter(2 * s + 1, q, k, v, b_b, b_b[0:1, :])
        return carry

    lax.fori_loop(0, nc, body, 0)

    o = acc_ref[...] + accb_ref[...]
    gn = gn_ref[...]
    for hh in range(GLA_HEADS):
        sl = slice(hh * GLA_DV, (hh + 1) * GLA_DV)
        o_ref[:, sl] = (_rms(o[:, sl]) * gn[:, sl] * gg_ref[:, sl].astype(F32)).astype(BF16)
    if not has_state:
        for s in range(GLA_SEQS):
            for d in range(2):
                for hh in range(GLA_HEADS):
                    sfin_ref[s, d, hh] = st_ref[2 * s + d, hh * GLA_DV:(hh + 1) * GLA_DV,
                                                hh * GLA_DK:(hh + 1) * GLA_DK].T


def _gla_scratch(seq_len):
    rows = GLA_SEQS * seq_len
    return [pltpu.VMEM((rows, GLA_VW), F32), pltpu.VMEM((rows, GLA_VW), F32),
            pltpu.VMEM((2 * GLA_SEQS, GLA_VW, GLA_KW), F32)]


def _gla_mask_specs():
    n_terms = 2 * (len(GLA_LEVELS) + 1)
    return [pl.BlockSpec((GLA_HEADS * GLA_CHUNK, GLA_KW), lambda i: (0, 0)),
            pl.BlockSpec((GLA_HEADS * GLA_CHUNK, GLA_VW), lambda i: (0, 0)),
            pl.BlockSpec((n_terms, GLA_CHUNK, GLA_HEADS * GLA_CHUNK), lambda i: (0, 0, 0))]


def _gla_ctx(gq, gk, gv, la, gg, gn, masks):
    seq = lambda w: pl.BlockSpec((GLA_SEQS * SEQ, w), lambda i: (i, 0))
    return pl.pallas_call(
        partial(_gla_kernel, seq_len=SEQ, has_state=False),
        grid=(BATCH // GLA_SEQS,),
        in_specs=[seq(GLA_KW), seq(GLA_KW), seq(GLA_VW), seq(2 * GLA_KW), seq(GLA_VW),
                  pl.BlockSpec((1, GLA_VW), lambda i: (0, 0))] + _gla_mask_specs(),
        out_specs=[seq(GLA_VW),
                   pl.BlockSpec((GLA_SEQS, 2, GLA_HEADS, GLA_DK, GLA_DV), lambda i: (i, 0, 0, 0, 0))],
        out_shape=[jax.ShapeDtypeStruct((N_CTX, GLA_VW), BF16),
                   jax.ShapeDtypeStruct((BATCH, 2, GLA_HEADS, GLA_DK, GLA_DV), F32)],
        scratch_shapes=_gla_scratch(SEQ),
        compiler_params=pltpu.CompilerParams(dimension_semantics=("parallel",), vmem_limit_bytes=VMEM_LIMIT),
        name="gla_ctx",
    )(gq, gk, gv, la, gg, gn, *masks)


def _gla_lat(layer, gq, gk, gv, la, gg, gn, masks, s0):
    rows = GLA_SEQS * DEC_SEQ
    blk0 = N_CTX // rows
    seq = lambda w: pl.BlockSpec((rows, w), lambda b: (blk0 + b, 0))
    return pl.pallas_call(
        partial(_gla_kernel, seq_len=DEC_SEQ, has_state=True),
        grid=(DEC_BATCH // GLA_SEQS,),
        in_specs=[seq(GLA_KW), seq(GLA_KW), seq(GLA_VW), seq(2 * GLA_KW), seq(GLA_VW),
                  pl.BlockSpec((1, GLA_VW), lambda b: (0, 0))] + _gla_mask_specs() + [
                  pl.BlockSpec((GLA_SEQS, None, 2, GLA_HEADS, GLA_DK, GLA_DV),
                               lambda b: (b, layer, 0, 0, 0, 0))],
        out_specs=pl.BlockSpec((rows, GLA_VW), lambda b: (b, 0)),
        out_shape=jax.ShapeDtypeStruct((N_LAT, GLA_VW), BF16),
        scratch_shapes=_gla_scratch(DEC_SEQ),
        compiler_params=pltpu.CompilerParams(dimension_semantics=("parallel",), vmem_limit_bytes=VMEM_LIMIT),
        name="gla_lat",
    )(gq, gk, gv, la, gg, gn, *masks, s0)


def _route(scores, bias):
    biased = scores + bias
    v = [biased[e:e + 1, :] for e in range(N_EXPERTS)]
    s = [scores[e:e + 1, :] for e in range(N_EXPERTS)]
    gscore = []
    for g in range(N_GROUPS):
        a, b, c, d = v[4 * g:4 * g + 4]
        hi1, lo1, hi2, lo2 = jnp.maximum(a, b), jnp.minimum(a, b), jnp.maximum(c, d), jnp.minimum(c, d)
        gscore.append(jnp.maximum(hi1, hi2) + jnp.maximum(jnp.minimum(hi1, hi2), jnp.maximum(lo1, lo2)))
    one = lambda cond: jnp.where(cond, 1.0, 0.0)
    picked, chosen = [], []
    for g in range(N_GROUPS):
        best = None
        for g2 in range(N_GROUPS):
            if g2 == g:
                continue
            ok = one((gscore[g] > gscore[g2]) if g2 < g else (gscore[g] >= gscore[g2]))
            best = ok if best is None else best * ok
        for i in range(EXPERTS_PER_GROUP):
            e = 4 * g + i
            beaten = None
            for j in range(EXPERTS_PER_GROUP):
                if j == i:
                    continue
                e2 = 4 * g + j
                lost = one((v[e2] > v[e]) if j > i else (v[e2] >= v[e]))
                beaten = lost if beaten is None else beaten + lost
            chosen.append(best * one(beaten < 1.5))
            picked.append(chosen[-1] * s[e])
    den = picked[0]
    for e in range(1, N_EXPERTS):
        den = den + picked[e]
    return jnp.concatenate([p / den for p in picked] + chosen, axis=0)


def _mix_kernel(xc_ref, xl_ref, oac_ref, oal_ref, ogc_ref, ogl_ref, sa_ref, sb_ref, mod_ref, wpa_f, wpb_f, wo_f, n2_ref,
                rw_ref, rb_ref, x1_out, h2_out, gt_out, wpa_ref, wpb_ref, wo_ref):
    @pl.when(pl.program_id(0) == 0)
    def _():
        wpa_ref[...] = wpa_f[...].astype(BF16)
        wpb_ref[...] = wpb_f[...].astype(BF16)
        wo_ref[...] = wo_f[...].astype(BF16)

    is_ctx = pl.program_id(0) < N_CTX // TM
    oa = jnp.where(is_ctx, oac_ref[...], oal_ref[...])
    og = jnp.where(is_ctx, ogc_ref[...], ogl_ref[...])
    a = _dot(oa, wpa_ref[...])
    b = _dot(og, wpb_ref[...])
    merged = sa_ref[...].astype(F32) * a + sb_ref[...].astype(F32) * b
    g1 = mod_ref[:, 2 * D_MODEL:3 * D_MODEL]
    sh2 = mod_ref[:, 3 * D_MODEL:4 * D_MODEL]
    sc2 = mod_ref[:, 4 * D_MODEL:5 * D_MODEL]
    x = jnp.where(is_ctx, xc_ref[...], xl_ref[...])
    x1 = x + g1 * _dot(merged.astype(BF16), wo_ref[...])
    x1_out[...] = x1
    h2 = _rms(x1) * n2_ref[...] * (1.0 + sc2) + sh2
    _to_row_tiles(h2_out, h2)
    hh, hl = _split2(h2)
    rh, rl = rw_ref[0], rw_ref[1]
    nt = lambda r, t: lax.dot_general(r, t, _NT, preferred_element_type=F32)
    logits = nt(rl, hh) + nt(rh, hl) + nt(rh, hh)
    gt_out[...] = _route(jax.nn.sigmoid(logits), rb_ref[...])


def _mix(layer, x_pair, oa_c, oa_l, og_c, og_l, sa, sb, mod_l, wpa, wpb, wo, norm2, rw3, rb):
    nt = N_TOK // TM
    n_ctx_t = N_CTX // TM
    row = lambda i: (i, 0)
    const = lambda i: (0, 0)
    ctx_row = lambda i: (jnp.minimum(i, n_ctx_t - 1), 0)
    lat_row = lambda i: (jnp.maximum(i - n_ctx_t, 0), 0)
    return pl.pallas_call(
        _mix_kernel,
        grid=(nt,),
        in_specs=_x_specs(x_pair) + [
            pl.BlockSpec((TM, ATTN_WIDTH), ctx_row),
            pl.BlockSpec((TM, ATTN_WIDTH), lat_row),
            pl.BlockSpec((TM, GLA_VW), ctx_row),
            pl.BlockSpec((TM, GLA_VW), lat_row),
            pl.BlockSpec((TM, D_MODEL), row),
            pl.BlockSpec((TM, D_MODEL), row),
            pl.BlockSpec((None, 1, MOD_W), lambda i: (_mod_row(i, TM), 0, 0)),
            pl.BlockSpec((None, ATTN_WIDTH, D_MODEL), lambda i: (layer, 0, 0), pipeline_mode=pl.Buffered(1)),
            pl.BlockSpec((None, GLA_VW, D_MODEL), lambda i: (layer, 0, 0), pipeline_mode=pl.Buffered(1)),
            pl.BlockSpec((None, D_MODEL, D_MODEL), lambda i: (layer, 0, 0), pipeline_mode=pl.Buffered(1)),
            pl.BlockSpec((1, D_MODEL), const),
            pl.BlockSpec((2, N_EXPERTS, D_MODEL), lambda i: (0, 0, 0)),
            pl.BlockSpec((N_EXPERTS, 1), const),
        ],
        out_specs=[pl.BlockSpec((TM, D_MODEL), row), pl.BlockSpec((TM * ROW_TILE, LANES), row),
                   pl.BlockSpec((2 * N_EXPERTS, TM), lambda i: (0, i))],
        out_shape=[jax.ShapeDtypeStruct((N_TOK, D_MODEL), F32),
                   jax.ShapeDtypeStruct((N_TOK * ROW_TILE, LANES), F32),
                   jax.ShapeDtypeStruct((2 * N_EXPERTS, N_TOK), F32)],
        scratch_shapes=[pltpu.VMEM((ATTN_WIDTH, D_MODEL), BF16), pltpu.VMEM((GLA_VW, D_MODEL), BF16),
                        pltpu.VMEM((D_MODEL, D_MODEL), BF16)],
        compiler_params=pltpu.CompilerParams(dimension_semantics=("arbitrary",), vmem_limit_bytes=VMEM_LIMIT),
        name="mix",
    )(*x_pair, oa_c, oa_l, og_c, og_l, sa, sb, mod_l, wpa, wpb, wo, norm2, rw3, rb)


def _route_plan(gsel):
    gates = gsel[:N_EXPERTS]
    sel = gsel[N_EXPERTS:] > 0.5
    seli = sel.astype(jnp.int32)
    incl = jnp.cumsum(seli, axis=1)
    count = incl[:, -1]
    ntile = (count + TS - 1) // TS
    tile_end = jnp.cumsum(ntile)
    tile_start = tile_end - ntile
    total = tile_end[-1]
    slot = tile_start[:, None] * TS + incl - seli
    pos_a = jnp.min(jnp.where(sel, slot, N_SLOT), axis=0)
    pos_b = jnp.max(jnp.where(sel, slot, -1), axis=0)
    w_a = jnp.sum(jnp.where(sel & (slot == pos_a[None, :]), gates, 0.0), axis=0)
    w_b = jnp.sum(jnp.where(sel & (slot == pos_b[None, :]), gates, 0.0), axis=0)
    tile_id = jnp.minimum(jnp.arange(N_TILE, dtype=jnp.int32), total - 1)
    tile_expert = jnp.sum((tile_id[:, None] >= tile_end[None, :]).astype(jnp.int32), axis=1)
    nt = N_TOK // TM
    pos = jnp.concatenate([pos_a.reshape(nt, 1, TM), pos_b.reshape(nt, 1, TM)], axis=2).astype(jnp.int32)
    plan = jnp.concatenate([tile_start, ntile, total[None]]).astype(jnp.int32)
    return pos, jnp.stack([w_a, w_b], axis=1), tile_expert.astype(jnp.int32), plan


def _to_row_tiles(ref, x):
    for k in range(ROW_TILE):
        ref[pl.ds(k, x.shape[0], stride=ROW_TILE), :] = x[:, k * LANES:(k + 1) * LANES]


def _from_row_tiles(ref, rows):
    return jnp.concatenate([ref[pl.ds(k, rows, stride=ROW_TILE), :] for k in range(ROW_TILE)], axis=1)


def _tile_rows(row, n=1):
    return pl.ds(pl.multiple_of(row * ROW_TILE, ROW_TILE), n * ROW_TILE)


def _row_copy(src, src_row, dst, dst_row, sem):
    return pltpu.make_async_copy(src.at[_tile_rows(src_row), :], dst.at[_tile_rows(dst_row), :], sem)


def _dispatch_kernel(plan_ref, pos_ref, h_ref, xs_ref, zero_ref, sem_z, sem):
    i = pl.program_id(0)
    total = plan_ref[2 * N_EXPERTS]

    def zero_copy(tile):
        return pltpu.make_async_copy(zero_ref, xs_ref.at[_tile_rows(tile * TS, TS), :], sem_z)

    def last_tile(e):
        return plan_ref[e] + plan_ref[N_EXPERTS + e] - 1

    @pl.when(i == 0)
    def _():
        zero_ref[...] = jnp.zeros(zero_ref.shape, F32)

        def each(fn):
            def expert(e, c):
                @pl.when(plan_ref[N_EXPERTS + e] > 0)
                def _():
                    fn(zero_copy(last_tile(e)))
                return c

            def unused(t, c):
                @pl.when(t >= total)
                def _():
                    fn(zero_copy(t))
                return c

            lax.fori_loop(0, N_EXPERTS, expert, 0)
            lax.fori_loop(0, N_TILE, unused, 0)

        each(lambda cp: cp.start())
        each(lambda cp: cp.wait())

    def start(r, c):
        _row_copy(h_ref, r, xs_ref, pos_ref[0, r], sem).start()
        _row_copy(h_ref, r, xs_ref, pos_ref[0, TM + r], sem).start()
        return c

    def wait(r, c):
        _row_copy(h_ref, 0, xs_ref, 0, sem).wait()
        return c

    lax.fori_loop(0, TM, start, 0, unroll=8)
    lax.fori_loop(0, 2 * TM, wait, 0, unroll=8)


def _dispatch(plan, pos, h2):
    return pl.pallas_call(
        _dispatch_kernel,
        grid_spec=pltpu.PrefetchScalarGridSpec(
            num_scalar_prefetch=1,
            grid=(N_TOK // TM,),
            in_specs=[
                pl.BlockSpec((None, 1, 2 * TM), lambda i, plan: (i, 0, 0), memory_space=pltpu.SMEM),
                pl.BlockSpec((TM * ROW_TILE, LANES), lambda i, plan: (i, 0)),
            ],
            out_specs=pl.BlockSpec(memory_space=pl.ANY),
            scratch_shapes=[pltpu.VMEM((TS * ROW_TILE, LANES), F32), pltpu.SemaphoreType.DMA,
                            pltpu.SemaphoreType.DMA],
        ),
        out_shape=jax.ShapeDtypeStruct((N_SLOT * ROW_TILE, LANES), F32),
        compiler_params=pltpu.CompilerParams(dimension_semantics=("arbitrary",), vmem_limit_bytes=VMEM_LIMIT),
        name="dispatch",
    )(plan, pos, h2)


def _moe_kernel(te_ref, plan_ref, xs_ref, wg_ref, wu_ref, wd_ref, ys_ref, wg_s, wu_s, wd_s):
    i = pl.program_id(0)
    total = plan_ref[2 * N_EXPERTS]

    @pl.when(i < total)
    def _():
        @pl.when((i == 0) | (te_ref[i] != te_ref[jnp.maximum(i - 1, 0)]))
        def _():
            wg_s[...] = wg_ref[...].astype(BF16)
            wu_s[...] = wu_ref[...].astype(BF16)
            wd_s[...] = wd_ref[...].astype(BF16)

        x = _from_row_tiles(xs_ref, TS).astype(BF16)
        up = _dot(x, wu_s[...])
        gt = _dot(x, wg_s[...])
        act = (gt * jax.nn.sigmoid(gt) * up).astype(BF16)
        _to_row_tiles(ys_ref, _dot(act, wd_s[...]))

    @pl.when(i >= total)
    def _():
        ys_ref[...] = jnp.zeros(ys_ref.shape, F32)


def _moe(layer, tile_expert, plan, xs, w_gate, w_up, w_down):
    last = lambda i, plan: jnp.minimum(i, plan[2 * N_EXPERTS] - 1)
    w_spec = lambda a, b: pl.BlockSpec((None, None, a, b), lambda i, te, plan: (layer, te[i], 0, 0))
    return pl.pallas_call(
        _moe_kernel,
        grid_spec=pltpu.PrefetchScalarGridSpec(
            num_scalar_prefetch=2,
            grid=(N_TILE,),
            in_specs=[
                pl.BlockSpec((TS * ROW_TILE, LANES), lambda i, te, plan: (last(i, plan), 0)),
                w_spec(D_MODEL, D_EXPERT), w_spec(D_MODEL, D_EXPERT), w_spec(D_EXPERT, D_MODEL),
            ],
            out_specs=pl.BlockSpec((TS * ROW_TILE, LANES), lambda i, te, plan: (i, 0)),
            scratch_shapes=[pltpu.VMEM((D_MODEL, D_EXPERT), BF16), pltpu.VMEM((D_MODEL, D_EXPERT), BF16),
                            pltpu.VMEM((D_EXPERT, D_MODEL), BF16)],
        ),
        out_shape=jax.ShapeDtypeStruct((N_SLOT * ROW_TILE, LANES), F32),
        compiler_params=pltpu.CompilerParams(dimension_semantics=("arbitrary",), vmem_limit_bytes=VMEM_LIMIT),
        name="moe",
    )(tile_expert, plan, xs, w_gate, w_up, w_down)


def _combine_kernel(pos_ref, nxt_ref, w_ref, x1_ref, mod_ref, ys_ref, out_ref, ya_ref, yb_ref, sem, *, n):
    i = pl.program_id(0)

    def gather(p_ref, slot):
        def start(r, c):
            _row_copy(ys_ref, p_ref[0, r], ya_ref.at[slot], r, sem.at[slot]).start()
            _row_copy(ys_ref, p_ref[0, TM + r], yb_ref.at[slot], r, sem.at[slot]).start()
            return c
        lax.fori_loop(0, TM, start, 0, unroll=8)

    @pl.when(i == 0)
    def _():
        gather(pos_ref, 0)

    @pl.when(i + 1 < n)
    def _():
        gather(nxt_ref, (i + 1) % 2)

    slot = i % 2

    def wait(r, c):
        _row_copy(ys_ref, 0, ya_ref.at[slot], 0, sem.at[slot]).wait()
        return c

    lax.fori_loop(0, 2 * TM, wait, 0, unroll=8)
    g2 = mod_ref[:, 5 * D_MODEL:6 * D_MODEL]
    w = w_ref[...]
    ya = _from_row_tiles(ya_ref.at[slot], TM)
    yb = _from_row_tiles(yb_ref.at[slot], TM)
    out_ref[...] = x1_ref[...] + g2 * (w[:, 0:1] * ya + w[:, 1:2] * yb)


def _combine(pos, w_ab, x1, mod_l, ys, tile0, nt):
    pos_spec = lambda fn: pl.BlockSpec((None, 1, 2 * TM), fn, memory_space=pltpu.SMEM)
    return pl.pallas_call(
        partial(_combine_kernel, n=nt),
        grid=(nt,),
        in_specs=[
            pos_spec(lambda i: (tile0 + i, 0, 0)),
            pos_spec(lambda i: (tile0 + jnp.minimum(i + 1, nt - 1), 0, 0)),
            pl.BlockSpec((TM, 2), lambda i: (tile0 + i, 0)),
            pl.BlockSpec((TM, D_MODEL), lambda i: (tile0 + i, 0)),
            pl.BlockSpec((None, 1, MOD_W), lambda i: (_mod_row(tile0 + i, TM), 0, 0)),
            pl.BlockSpec(memory_space=pl.ANY),
        ],
        out_specs=pl.BlockSpec((TM, D_MODEL), lambda i: (i, 0)),
        out_shape=jax.ShapeDtypeStruct((nt * TM, D_MODEL), F32),
        scratch_shapes=[pltpu.VMEM((2, TM * ROW_TILE, LANES), F32), pltpu.VMEM((2, TM * ROW_TILE, LANES), F32),
                        pltpu.SemaphoreType.DMA((2,))],
        compiler_params=pltpu.CompilerParams(dimension_semantics=("arbitrary",), vmem_limit_bytes=VMEM_LIMIT),
        name="combine",
    )(pos, pos, w_ab, x1, mod_l, ys)


def _rope_tables():
    pos = np.arange(DEC_SEQ)
    rows = (pos // GRID_W).astype(np.float64)
    cols = (pos % GRID_W).astype(np.float64)
    half = HEAD_DIM // 2
    inv_freq = ROPE_THETA ** (-np.arange(0, half, 2, dtype=np.float64) / half)
    ang_r = rows[:, None] * inv_freq[None, :]
    ang_c = cols[:, None] * inv_freq[None, :]
    cos_h = np.concatenate([np.cos(ang_r), np.cos(ang_r), np.cos(ang_c), np.cos(ang_c)], axis=-1)
    sin_h = np.concatenate([-np.sin(ang_r), np.sin(ang_r), -np.sin(ang_c), np.sin(ang_c)], axis=-1)
    cos = np.concatenate([np.ones((TM, HEAD_DIM)), cos_h], axis=0)
    sin = np.concatenate([np.zeros((TM, HEAD_DIM)), sin_h], axis=0)
    return (jnp.asarray(np.tile(cos, (1, N_HEADS)), F32), jnp.asarray(np.tile(sin, (1, N_HEADS)), F32))


def _pack_w2(w2):
    z = jnp.zeros((GLA_RANK, GLA_KW), w2.dtype)
    top = jnp.concatenate([w2[0], z], axis=1)
    bot = jnp.concatenate([z, w2[1]], axis=1)
    pad = jnp.zeros((LR_PAD - 2 * GLA_RANK, 2 * GLA_KW), w2.dtype)
    return jnp.concatenate([top, bot, pad], axis=0).astype(BF16)


def kernel(x_prompt, x_sample, cache_k, cache_v, state_gla, c, c_ctx, norm1, norm2, w_ada, b_ada, w_in,
           q_norm, k_norm, gla_w2, gla_b, gla_norm, w_pa, w_pb, w_out, router_w, router_bias,
           w_gate, w_up, w_down):
    x_pair = (x_prompt.reshape(N_CTX, D_MODEL), x_sample.reshape(N_LAT, D_MODEL))
    cond =jnp.concatenate([c_ctx[None, :], c, jnp.zeros((N_COND - 1 - DEC_BATCH, D_MODEL), F32)], axis=0)
    mod = _ada(cond, w_ada, b_ada).reshape(DEPTH, N_COND, 1, MOD_W)
    cos, sin = _rope_tables()
    gla_masks = _gla_masks()
    lane_head = np.arange(ATTN_WIDTH) // HEAD_DIM
    bd = jnp.asarray(lane_head[:, None] == lane_head[None, :], BF16)
    rw3 = jnp.stack(_split2(router_w.T), axis=0)
    rb = router_bias.reshape(N_EXPERTS, 1)
    ck = cache_k.reshape(DEC_BATCH, DEPTH, PAST_LEN, KV_WIDTH)
    cv = cache_v.reshape(DEC_BATCH, DEPTH, PAST_LEN, KV_WIDTH)

    keys, vals, states = [], [], []
    for l in range(DEPTH):
        q, kf, kr, v, gq, gk, gv, gg, la, sa, sb = _inproj(
            l, x_pair, mod[l], norm1[l][None, :], w_in, jnp.tile(q_norm[l], N_HEADS)[None, :],
            jnp.tile(k_norm[l], KV_HEADS)[None, :], cos, sin, bd, _pack_w2(gla_w2[l]),
            gla_b[l].reshape(1, 2 * GLA_KW))
        oa_c = _attn_ctx(q, kr, v)
        oa_l = _attn_lat(l, q, kr, v, ck, cv)
        gn = jnp.tile(gla_norm[l], GLA_HEADS)[None, :]
        og_c, s_fin = _gla_ctx(gq, gk, gv, la, gg, gn, gla_masks)
        og_l = _gla_lat(l, gq, gk, gv, la, gg, gn, gla_masks, state_gla)
        x1, h2, gsel = _mix(l, x_pair, oa_c, oa_l, og_c, og_l, sa, sb, mod[l], w_pa, w_pb, w_out,
                            norm2[l][None, :], rw3, rb)
        pos, w_ab, tile_expert, plan = _route_plan(gsel)
        xs = _dispatch(plan, pos, h2)
        ys = _moe(l, tile_expert, plan, xs, w_gate, w_up, w_down)
        n_ctx_t = N_CTX // TM
        if l + 1 < DEPTH:
            x = _combine(pos, w_ab, x1, mod[l], ys, 0, N_TOK // TM)
            x_pair = (x, x)
        else:
            x_pair = (_combine(pos, w_ab, x1, mod[l], ys, 0, n_ctx_t),
                      _combine(pos, w_ab, x1, mod[l], ys, n_ctx_t, N_LAT // TM))
        keys.append(kf[:N_CTX].reshape(BATCH, SEQ, KV_HEADS, HEAD_DIM))
        vals.append(v[:N_CTX].reshape(BATCH, SEQ, KV_HEADS, HEAD_DIM))
        states.append(s_fin)
    y_prompt = x_pair[0].reshape(BATCH, SEQ, D_MODEL)
    y_sample = x_pair[1].reshape(DEC_BATCH, DEC_SEQ, D_MODEL)
    return (y_prompt, y_sample, jnp.stack(keys, axis=1), jnp.stack(vals, axis=1), jnp.stack(states, axis=1))
```

```python
from functools import partial

import jax
import jax.numpy as jnp
import numpy as np
from jax import lax
from jax.experimental import pallas as pl
from jax.experimental.pallas import tpu as pltpu

F32 = jnp.float32
BF16 = jnp.bfloat16

D_MODEL = 1024
BATCH = 16
SEQ = 256
DEPTH = 2
DEC_BATCH = 4
DEC_SEQ = 1024
PAST_LEN = 512
GRID_W = 64
N_HEADS = 8
KV_HEADS = 2
GROUP = N_HEADS // KV_HEADS
HEAD_DIM = 64
ATTN_WIDTH = N_HEADS * HEAD_DIM
KV_WIDTH = KV_HEADS * HEAD_DIM
ROPE_THETA = 10000.0
GLA_HEADS = 4
GLA_DK = 64
GLA_DV = 128
GLA_KW = GLA_HEADS * GLA_DK
GLA_VW = GLA_HEADS * GLA_DV
GLA_RANK = 16
GLA_GATE_NORM = 16.0
N_EXPERTS = 16
N_GROUPS = 4
EXPERTS_PER_GROUP = N_EXPERTS // N_GROUPS
D_EXPERT = 512
NORM_EPS = 1e-6

N_CTX = BATCH * SEQ
N_LAT = DEC_BATCH * DEC_SEQ
N_TOK = N_CTX + N_LAT
N_COND = 8
MOD_W = 6 * D_MODEL

TM = 512
TS = 512
N_TILE = 2 * N_TOK // TS + N_EXPERTS
N_SLOT = N_TILE * TS
PLAN_TOTAL = 2 * N_EXPERTS
ROW_TILE = 8
LANES = D_MODEL // ROW_TILE
GLA_CHUNK = 128
GLA_LEVELS = (64, 32, 16)
GLA_DIAG = 16
GLA_SEQS = 1
LR_PAD = 128
SEG_MIX = ATTN_WIDTH + 2 * KV_WIDTH + 2 * GLA_KW + 2 * GLA_VW
OFF_GA = SEG_MIX
OFF_GB = OFF_GA + D_MODEL
OFF_LR = OFF_GB + D_MODEL
IN_PACKED = OFF_LR + LR_PAD
IN_WIDTH = SEG_MIX + 2 * GLA_RANK + 2 * D_MODEL
VMEM_LIMIT = 56 * 1024 * 1024

_NT = (((1,), (1,)), ((), ()))
_TN = (((0,), (0,)), ((), ()))


def _mod_row(tile, tm):
    start = tile * tm
    return jnp.where(start < N_CTX, 0, 1 + (start - N_CTX) // DEC_SEQ)


def _split3(x):
    hi = x.astype(BF16)
    r1 = x - hi.astype(F32)
    mid = r1.astype(BF16)
    lo = (r1 - mid.astype(F32)).astype(BF16)
    return hi, mid, lo


def _split2(x):
    hi = x.astype(BF16)
    lo = (x - hi.astype(F32)).astype(BF16)
    return hi, lo


def _dot(a, b):
    return jnp.dot(a, b, preferred_element_type=F32)


def _rms(x):
    return x * lax.rsqrt(jnp.mean(x * x, axis=-1, keepdims=True) + NORM_EPS)


def _ada_kernel(cond_ref, w_ref, b_ref, out_ref):
    c = cond_ref[...]
    s = c * jax.nn.sigmoid(c)
    out_ref[...] = _dot(s.astype(BF16), w_ref[...].astype(BF16)) + b_ref[...]


def _ada(cond, w_ada, b_ada):
    nb = MOD_W // D_MODEL
    return pl.pallas_call(
        _ada_kernel,
        grid=(DEPTH, nb),
        in_specs=[
            pl.BlockSpec((N_COND, D_MODEL), lambda l, j: (0, 0)),
            pl.BlockSpec((None, D_MODEL, D_MODEL), lambda l, j: (l, 0, j)),
            pl.BlockSpec((None, 1, D_MODEL), lambda l, j: (l, 0, j)),
        ],
        out_specs=pl.BlockSpec((None, N_COND, D_MODEL), lambda l, j: (l, 0, j)),
        out_shape=jax.ShapeDtypeStruct((DEPTH, N_COND, MOD_W), F32),
        compiler_params=pltpu.CompilerParams(vmem_limit_bytes=VMEM_LIMIT),
        name="ada",
    )(cond, w_ada, b_ada.reshape(DEPTH, 1, MOD_W))


def _head_rmsnorm(x, bd):
    ssq = _dot((x * x).astype(BF16), bd)
    return x * lax.rsqrt(ssq * (1.0 / HEAD_DIM) + NORM_EPS)


def _rope(x, cos, sin):
    n = x.shape[-1]
    lane = lax.broadcasted_iota(jnp.int32, x.shape, 1)
    first = (lane & (HEAD_DIM // 4)) == 0
    partner = jnp.where(first, pltpu.roll(x, n - HEAD_DIM // 4, 1), pltpu.roll(x, HEAD_DIM // 4, 1))
    return x * cos + partner * sin


def _pack_w_in(w_ref, wp_ref):
    lr0 = SEG_MIX
    ga0 = lr0 + 2 * GLA_RANK
    rows_per_step = 128

    def body(i, carry):
        rows = pl.ds(pl.multiple_of(i * rows_per_step, rows_per_step), rows_per_step)
        wp_ref[rows, 0:SEG_MIX] = w_ref[rows, 0:SEG_MIX].astype(BF16)
        tail = w_ref[rows, SEG_MIX:IN_WIDTH]
        lr = tail[:, 0:2 * GLA_RANK]
        wp_ref[rows, OFF_LR:IN_PACKED] = jnp.concatenate(
            [lr, jnp.zeros((rows_per_step, LR_PAD - 2 * GLA_RANK), F32)], axis=1).astype(BF16)
        wp_ref[rows, OFF_GA:OFF_LR] = tail[:, ga0 - lr0:ga0 - lr0 + 2 * D_MODEL].astype(BF16)
        return carry

    lax.fori_loop(0, D_MODEL // rows_per_step, body, 0)


def _inproj_kernel(xc_ref, xl_ref, mod_ref, n1_ref, wf_ref, qn_ref, kn_ref, cos_ref, sin_ref, bd_ref, w2_ref, gb_ref,
                   q_out, kf_out, kr_out, v_out, gq_out, gk_out, gv_out, gg_out, la_out, sa_out, sb_out,
                   w_ref):
    @pl.when(pl.program_id(0) == 0)
    def _():
        _pack_w_in(wf_ref, w_ref)

    x = jnp.where(pl.program_id(0) < N_CTX // TM, xc_ref[...], xl_ref[...])
    sh1 = mod_ref[:, 0:D_MODEL]
    sc1 = mod_ref[:, D_MODEL:2 * D_MODEL]
    h = (_rms(x) * n1_ref[...] * (1.0 + sc1) + sh1).astype(BF16)

    def proj(lo, width):
        return _dot(h, w_ref[:, lo:lo + width])

    cos = cos_ref[...]
    sin = sin_ref[...]
    bd = bd_ref[...]
    off = 0
    q = _head_rmsnorm(proj(off, ATTN_WIDTH), bd) * qn_ref[...]
    q_out[...] = (_rope(q, cos, sin) * HEAD_DIM ** -0.5).astype(BF16)
    off += ATTN_WIDTH
    k = _head_rmsnorm(proj(off, KV_WIDTH), bd[:KV_WIDTH, :KV_WIDTH]) * kn_ref[...]
    kf_out[...] = k
    kr_out[...] = _rope(k, cos[:, :KV_WIDTH], sin[:, :KV_WIDTH]).astype(BF16)
    off += KV_WIDTH
    v_out[...] = proj(off, KV_WIDTH)
    off += KV_WIDTH
    gq_out[...] = (proj(off, GLA_KW) * GLA_DK ** -0.5).astype(BF16)
    off += GLA_KW
    gk_out[...] = proj(off, GLA_KW).astype(BF16)
    off += GLA_KW
    gv_out[...] = proj(off, GLA_VW).astype(BF16)
    off += GLA_VW
    gg = proj(off, GLA_VW)
    gg_out[...] = (gg * jax.nn.sigmoid(gg)).astype(BF16)
    sa_out[...] = jax.nn.sigmoid(proj(OFF_GA, D_MODEL)).astype(BF16)
    sb_out[...] = jax.nn.sigmoid(proj(OFF_GB, D_MODEL)).astype(BF16)
    lr = proj(OFF_LR, LR_PAD)
    z = _dot(lr.astype(BF16), w2_ref[...]) + gb_ref[...]
    log_sig = jnp.minimum(z, 0.0) - jnp.log1p(jnp.exp(-jnp.abs(z)))
    la_out[...] = log_sig * (1.0 / GLA_GATE_NORM)


def _x_specs(x_pair):
    n_ctx_t = N_CTX // TM
    lat0 = n_ctx_t if x_pair[0] is x_pair[1] else 0
    return [pl.BlockSpec((TM, D_MODEL), lambda i: (jnp.minimum(i, n_ctx_t - 1), 0)),
            pl.BlockSpec((TM, D_MODEL), lambda i: (lat0 + jnp.maximum(i - n_ctx_t, 0), 0))]


def _inproj(layer, x_pair, mod_l, norm1, w_in, qn, kn, cos, sin, bd, w2p, gb):
    nt = N_TOK // TM
    n_ctx_t = N_CTX // TM
    t_per_seq = DEC_SEQ // TM
    row = lambda i: (i, 0)
    const = lambda i: (0, 0)
    rope_blk = lambda i: (jnp.where(i < n_ctx_t, 0, 1 + (i - n_ctx_t) % t_per_seq), 0)

    def out(width, dtype):
        return pl.BlockSpec((TM, width), row), jax.ShapeDtypeStruct((N_TOK, width), dtype)

    outs = [out(ATTN_WIDTH, BF16), out(KV_WIDTH, F32), out(KV_WIDTH, BF16), out(KV_WIDTH, F32),
            out(GLA_KW, BF16), out(GLA_KW, BF16), out(GLA_VW, BF16), out(GLA_VW, BF16),
            out(2 * GLA_KW, F32), out(D_MODEL, BF16), out(D_MODEL, BF16)]
    return pl.pallas_call(
        _inproj_kernel,
        grid=(nt,),
        in_specs=_x_specs(x_pair) + [
            pl.BlockSpec((None, 1, MOD_W), lambda i: (_mod_row(i, TM), 0, 0)),
            pl.BlockSpec((1, D_MODEL), const),
            pl.BlockSpec((None, D_MODEL, IN_WIDTH), lambda i: (layer, 0, 0), pipeline_mode=pl.Buffered(1)),
            pl.BlockSpec((1, ATTN_WIDTH), const),
            pl.BlockSpec((1, KV_WIDTH), const),
            pl.BlockSpec((TM, ATTN_WIDTH), rope_blk),
            pl.BlockSpec((TM, ATTN_WIDTH), rope_blk),
            pl.BlockSpec((ATTN_WIDTH, ATTN_WIDTH), const),
            pl.BlockSpec((LR_PAD, 2 * GLA_KW), const),
            pl.BlockSpec((1, 2 * GLA_KW), const),
        ],
        out_specs=[o[0] for o in outs],
        out_shape=[o[1] for o in outs],
        scratch_shapes=[pltpu.VMEM((D_MODEL, IN_PACKED), BF16)],
        compiler_params=pltpu.CompilerParams(dimension_semantics=("arbitrary",), vmem_limit_bytes=VMEM_LIMIT),
        name="inproj",
    )(*x_pair, mod_l, norm1, w_in, qn, kn, cos, sin, bd, w2p, gb)


def _attn_kernel(*refs, has_cache):
    if has_cache:
        q_ref, k_ref, v_ref, ck_ref, cv_ref, o_ref = refs
    else:
        q_ref, k_ref, v_ref, o_ref = refs
    tq = q_ref.shape[0]
    for g in range(KV_HEADS):
        ks = slice(g * HEAD_DIM, (g + 1) * HEAD_DIM)
        qs = jnp.concatenate(
            [q_ref[:, (g * GROUP + j) * HEAD_DIM:(g * GROUP + j + 1) * HEAD_DIM] for j in range(GROUP)], axis=0)
        k = k_ref[:, ks]
        v = v_ref[:, ks].astype(BF16)
        s = lax.dot_general(qs, k, _NT, preferred_element_type=F32)
        m = jnp.max(s, axis=-1, keepdims=True)
        if has_cache:
            ck = ck_ref[:, ks].astype(BF16)
            cv = cv_ref[:, ks].astype(BF16)
            s2 = lax.dot_general(qs, ck, _NT, preferred_element_type=F32)
            m = jnp.maximum(m, jnp.max(s2, axis=-1, keepdims=True))
        p = jnp.exp(s - m)
        den = jnp.sum(p, axis=-1, keepdims=True)
        o = _dot(p.astype(BF16), v)
        if has_cache:
            p2 = jnp.exp(s2 - m)
            den = den + jnp.sum(p2, axis=-1, keepdims=True)
            o = o + _dot(p2.astype(BF16), cv)
        o = o / den
        for j in range(GROUP):
            hd = g * GROUP + j
            o_ref[:, hd * HEAD_DIM:(hd + 1) * HEAD_DIM] = o[j * tq:(j + 1) * tq, :].astype(BF16)


def _attn_ctx(q, kr, v):
    return pl.pallas_call(
        partial(_attn_kernel, has_cache=False),
        grid=(BATCH,),
        in_specs=[
            pl.BlockSpec((SEQ, ATTN_WIDTH), lambda i: (i, 0)),
            pl.BlockSpec((SEQ, KV_WIDTH), lambda i: (i, 0)),
            pl.BlockSpec((SEQ, KV_WIDTH), lambda i: (i, 0)),
        ],
        out_specs=pl.BlockSpec((SEQ, ATTN_WIDTH), lambda i: (i, 0)),
        out_shape=jax.ShapeDtypeStruct((N_CTX, ATTN_WIDTH), BF16),
        compiler_params=pltpu.CompilerParams(dimension_semantics=("parallel",), vmem_limit_bytes=VMEM_LIMIT),
        name="attn_ctx",
    )(q, kr, v)


def _attn_lat(layer, q, kr, v, cache_k, cache_v):
    tq = 256
    nq = DEC_SEQ // tq
    q0 = N_CTX // tq
    s0 = N_CTX // DEC_SEQ
    return pl.pallas_call(
        partial(_attn_kernel, has_cache=True),
        grid=(DEC_BATCH, nq),
        in_specs=[
            pl.BlockSpec((tq, ATTN_WIDTH), lambda b, i: (q0 + b * nq + i, 0)),
            pl.BlockSpec((DEC_SEQ, KV_WIDTH), lambda b, i: (s0 + b, 0)),
            pl.BlockSpec((DEC_SEQ, KV_WIDTH), lambda b, i: (s0 + b, 0)),
            pl.BlockSpec((None, None, PAST_LEN, KV_WIDTH), lambda b, i: (b, layer, 0, 0)),
            pl.BlockSpec((None, None, PAST_LEN, KV_WIDTH), lambda b, i: (b, layer, 0, 0)),
        ],
        out_specs=pl.BlockSpec((tq, ATTN_WIDTH), lambda b, i: (b * nq + i, 0)),
        out_shape=jax.ShapeDtypeStruct((N_LAT, ATTN_WIDTH), BF16),
        compiler_params=pltpu.CompilerParams(dimension_semantics=("parallel", "parallel"),
                                             vmem_limit_bytes=VMEM_LIMIT),
        name="attn_lat",
    )(q, kr, v, cache_k, cache_v)


def _gla_masks():
    c = GLA_CHUNK
    hrow = np.arange(GLA_HEADS * c) // c
    head_k = (hrow[:, None] == (np.arange(GLA_KW) // GLA_DK)[None, :])
    head_v = (hrow[:, None] == (np.arange(GLA_VW) // GLA_DV)[None, :])
    t = np.arange(c)[:, None]
    s = (np.arange(GLA_HEADS * c) % c)[None, :]
    pair = []
    for reverse in (False, True):
        for m in GLA_LEVELS:
            pair.append(t // (2 * m) == s // (2 * m))
        pair.append((t // GLA_DIAG == s // GLA_DIAG) & ((s >= t) if reverse else (s <= t)))
    return (jnp.asarray(head_k, BF16), jnp.asarray(head_v, BF16), jnp.asarray(np.stack(pair), F32))


def _block_diag(x, head_mask):
    return jnp.concatenate([x] * GLA_HEADS, axis=0) * head_mask


def _ref_rows(b, rows, rep):
    return jnp.concatenate([jnp.broadcast_to(b[r:r + 1, :], (rep, b.shape[1])) for r in rows], axis=0)


def _gla_scores(q, k, b, reverse, head_k, pair_ref):
    c = GLA_CHUNK
    t_idx = lax.broadcasted_iota(jnp.int32, (c, GLA_KW), 0)
    term0 = (len(GLA_LEVELS) + 1) if reverse else 0

    def scores(qt, kt):
        return lax.dot_general(qt.astype(BF16), _block_diag(kt.astype(BF16), head_k), _NT,
                               preferred_element_type=F32)

    total = None
    for i, m in enumerate(GLA_LEVELS):
        nblk = c // (2 * m)
        ref = _ref_rows(b, [2 * m * j + (m if reverse else m - 1) for j in range(nblk)], 2 * m)
        q_side = ((t_idx & m) == 0) if reverse else ((t_idx & m) != 0)
        qt = jnp.where(q_side, q * jnp.exp(b - ref), 0.0)
        kt = jnp.where(q_side, 0.0, k * jnp.exp(ref - b))
        part = scores(qt, kt) * pair_ref[term0 + i]
        total = part if total is None else total + part
    d = GLA_DIAG
    x = b - _ref_rows(b, [d * j + d // 2 for j in range(c // d)], d)
    diag = scores(q * jnp.exp(x), k * jnp.exp(-x))
    return total + jnp.where(pair_ref[term0 + len(GLA_LEVELS)] > 0.5, diag, 0.0)


def _cumsum_rows(tri, la):
    hi, lo = _split2(la)
    return _dot(tri, hi) + _dot(tri, lo)


def _gla_kernel(*refs, seq_len, has_state):
    gq_ref, gk_ref, gv_ref, la_ref, gg_ref, gn_ref, hk_ref, hv_ref, pair_ref = refs[:9]
    if has_state:
        s0_ref, o_ref, acc_ref, accb_ref, st_ref = refs[9:]
    else:
        o_ref, sfin_ref, acc_ref, accb_ref, st_ref = refs[9:]
    c = GLA_CHUNK
    nc = seq_len // c
    r_i = lax.broadcasted_iota(jnp.int32, (c, c), 0)
    c_i = lax.broadcasted_iota(jnp.int32, (c, c), 1)
    tril = jnp.where(c_i <= r_i, 1.0, 0.0).astype(BF16)
    triu = jnp.where(c_i >= r_i, 1.0, 0.0).astype(BF16)
    st_head_r = lax.broadcasted_iota(jnp.int32, (GLA_VW, GLA_KW), 0) // GLA_DV
    st_head_c = lax.broadcasted_iota(jnp.int32, (GLA_VW, GLA_KW), 1) // GLA_DK
    st_mask = st_head_r == st_head_c

    for s in range(GLA_SEQS):
        for d in range(2):
            st_ref[2 * s + d] = jnp.zeros((GLA_VW, GLA_KW), F32)
            if has_state:
                for hh in range(GLA_HEADS):
                    st_ref[2 * s + d, hh * GLA_DV:(hh + 1) * GLA_DV, hh * GLA_DK:(hh + 1) * GLA_DK] = s0_ref[s, d, hh].T

    def load(s, n):
        rows = pl.ds(pl.multiple_of(s * seq_len + n * c, c), c)
        return (rows, gq_ref[rows, :].astype(F32), gk_ref[rows, :].astype(F32), gv_ref[rows, :])

    def inter(d, q, k, v, b, b_end):
        st = st_ref[d]
        q_in = (q * jnp.exp(b)).astype(BF16)
        o = lax.dot_general(q_in, st.astype(BF16), _NT, preferred_element_type=F32)
        k_st = (k * jnp.exp(b_end - b)).astype(BF16)
        upd = lax.dot_general(v, k_st, _TN, preferred_element_type=F32)
        st_ref[d] = st * jnp.exp(b_end) + jnp.where(st_mask, upd, 0.0)
        return o

    def body(n, carry):
        for s in range(GLA_SEQS):
            rows, q, k, v = load(s, n)
            b_f = _cumsum_rows(tril, la_ref[rows, 0:GLA_KW])
            b_b = _cumsum_rows(triu, la_ref[rows, GLA_KW:2 * GLA_KW])
            head_k = hk_ref[...]
            a = _gla_scores(q, k, b_f, False, head_k, pair_ref) + _gla_scores(q, k, b_b, True, head_k, pair_ref)
            o = _dot(a.astype(BF16), _block_diag(v, hv_ref[...]))
            acc_ref[rows, :] = o + inter(2 * s, q, k, v, b_f, b_f[c - 1:c, :])
            rows, q, k, v = load(s, nc - 1 - n)
            b_b = _cumsum_rows(triu, la_ref[rows, GLA_KW:2 * GLA_KW])
            accb_ref[rows, :] = inter(2 * s + 1, q, k, v, b_b, b_b[0:1, :])
        return carry

    lax.fori_loop(0, nc, body, 0)

    o = acc_ref[...] + accb_ref[...]
    gn = gn_ref[...]
    for hh in range(GLA_HEADS):
        sl = slice(hh * GLA_DV, (hh + 1) * GLA_DV)
        o_ref[:, sl] = (_rms(o[:, sl]) * gn[:, sl] * gg_ref[:, sl].astype(F32)).astype(BF16)
    if not has_state:
        for s in range(GLA_SEQS):
            for d in range(2):
                for hh in range(GLA_HEADS):
                    sfin_ref[s, d, hh] = st_ref[2 * s + d, hh * GLA_DV:(hh + 1) * GLA_DV,
                                                hh * GLA_DK:(hh + 1) * GLA_DK].T


def _gla_scratch(seq_len):
    rows = GLA_SEQS * seq_len
    return [pltpu.VMEM((rows, GLA_VW), F32), pltpu.VMEM((rows, GLA_VW), F32),
            pltpu.VMEM((2 * GLA_SEQS, GLA_VW, GLA_KW), F32)]


def _gla_mask_specs():
    n_terms = 2 * (len(GLA_LEVELS) + 1)
    return [pl.BlockSpec((GLA_HEADS * GLA_CHUNK, GLA_KW), lambda i: (0, 0)),
            pl.BlockSpec((GLA_HEADS * GLA_CHUNK, GLA_VW), lambda i: (0, 0)),
            pl.BlockSpec((n_terms, GLA_CHUNK, GLA_HEADS * GLA_CHUNK), lambda i: (0, 0, 0))]


def _gla_ctx(gq, gk, gv, la, gg, gn, masks):
    seq = lambda w: pl.BlockSpec((GLA_SEQS * SEQ, w), lambda i: (i, 0))
    return pl.pallas_call(
        partial(_gla_kernel, seq_len=SEQ, has_state=False),
        grid=(BATCH // GLA_SEQS,),
        in_specs=[seq(GLA_KW), seq(GLA_KW), seq(GLA_VW), seq(2 * GLA_KW), seq(GLA_VW),
                  pl.BlockSpec((1, GLA_VW), lambda i: (0, 0))] + _gla_mask_specs(),
        out_specs=[seq(GLA_VW),
                   pl.BlockSpec((GLA_SEQS, 2, GLA_HEADS, GLA_DK, GLA_DV), lambda i: (i, 0, 0, 0, 0))],
        out_shape=[jax.ShapeDtypeStruct((N_CTX, GLA_VW), BF16),
                   jax.ShapeDtypeStruct((BATCH, 2, GLA_HEADS, GLA_DK, GLA_DV), F32)],
        scratch_shapes=_gla_scratch(SEQ),
        compiler_params=pltpu.CompilerParams(dimension_semantics=("parallel",), vmem_limit_bytes=VMEM_LIMIT),
        name="gla_ctx",
    )(gq, gk, gv, la, gg, gn, *masks)


def _gla_lat(layer, gq, gk, gv, la, gg, gn, masks, s0):
    rows = GLA_SEQS * DEC_SEQ
    blk0 = N_CTX // rows
    seq = lambda w: pl.BlockSpec((rows, w), lambda b: (blk0 + b, 0))
    return pl.pallas_call(
        partial(_gla_kernel, seq_len=DEC_SEQ, has_state=True),
        grid=(DEC_BATCH // GLA_SEQS,),
        in_specs=[seq(GLA_KW), seq(GLA_KW), seq(GLA_VW), seq(2 * GLA_KW), seq(GLA_VW),
                  pl.BlockSpec((1, GLA_VW), lambda b: (0, 0))] + _gla_mask_specs() + [
                  pl.BlockSpec((GLA_SEQS, None, 2, GLA_HEADS, GLA_DK, GLA_DV),
                               lambda b: (b, layer, 0, 0, 0, 0))],
        out_specs=pl.BlockSpec((rows, GLA_VW), lambda b: (b, 0)),
        out_shape=jax.ShapeDtypeStruct((N_LAT, GLA_VW), BF16),
        scratch_shapes=_gla_scratch(DEC_SEQ),
        compiler_params=pltpu.CompilerParams(dimension_semantics=("parallel",), vmem_limit_bytes=VMEM_LIMIT),
        name="gla_lat",
    )(gq, gk, gv, la, gg, gn, *masks, s0)


def _route(scores, bias):
    biased = scores + bias
    v = [biased[e:e + 1, :] for e in range(N_EXPERTS)]
    s = [scores[e:e + 1, :] for e in range(N_EXPERTS)]
    gscore = []
    for g in range(N_GROUPS):
        a, b, c, d = v[4 * g:4 * g + 4]
        hi1, lo1, hi2, lo2 = jnp.maximum(a, b), jnp.minimum(a, b), jnp.maximum(c, d), jnp.minimum(c, d)
        gscore.append(jnp.maximum(hi1, hi2) + jnp.maximum(jnp.minimum(hi1, hi2), jnp.maximum(lo1, lo2)))
    one = lambda cond: jnp.where(cond, 1.0, 0.0)
    picked, chosen = [], []
    for g in range(N_GROUPS):
        best = None
        for g2 in range(N_GROUPS):
            if g2 == g:
                continue
            ok = one((gscore[g] > gscore[g2]) if g2 < g else (gscore[g] >= gscore[g2]))
            best = ok if best is None else best * ok
        for i in range(EXPERTS_PER_GROUP):
            e = 4 * g + i
            beaten = None
            for j in range(EXPERTS_PER_GROUP):
                if j == i:
                    continue
                e2 = 4 * g + j
                lost = one((v[e2] > v[e]) if j > i else (v[e2] >= v[e]))
                beaten = lost if beaten is None else beaten + lost
            chosen.append(best * one(beaten < 1.5))
            picked.append(chosen[-1] * s[e])
    den = picked[0]
    for e in range(1, N_EXPERTS):
        den = den + picked[e]
    return jnp.concatenate([p / den for p in picked] + chosen, axis=0)


def _mix_kernel(xc_ref, xl_ref, oac_ref, oal_ref, ogc_ref, ogl_ref, sa_ref, sb_ref, mod_ref, wpa_f, wpb_f, wo_f, n2_ref,
                rw_ref, rb_ref, x1_out, h2_out, gt_out, wpa_ref, wpb_ref, wo_ref):
    @pl.when(pl.program_id(0) == 0)
    def _():
        wpa_ref[...] = wpa_f[...].astype(BF16)
        wpb_ref[...] = wpb_f[...].astype(BF16)
        wo_ref[...] = wo_f[...].astype(BF16)

    is_ctx = pl.program_id(0) < N_CTX // TM
    oa = jnp.where(is_ctx, oac_ref[...], oal_ref[...])
    og = jnp.where(is_ctx, ogc_ref[...], ogl_ref[...])
    a = _dot(oa, wpa_ref[...])
    b = _dot(og, wpb_ref[...])
    merged = sa_ref[...].astype(F32) * a + sb_ref[...].astype(F32) * b
    g1 = mod_ref[:, 2 * D_MODEL:3 * D_MODEL]
    sh2 = mod_ref[:, 3 * D_MODEL:4 * D_MODEL]
    sc2 = mod_ref[:, 4 * D_MODEL:5 * D_MODEL]
    x = jnp.where(is_ctx, xc_ref[...], xl_ref[...])
    x1 = x + g1 * _dot(merged.astype(BF16), wo_ref[...])
    x1_out[...] = x1
    h2 = _rms(x1) * n2_ref[...] * (1.0 + sc2) + sh2
    _to_row_tiles(h2_out, h2)
    hh, hl = _split2(h2)
    rh, rl = rw_ref[0], rw_ref[1]
    nt = lambda r, t: lax.dot_general(r, t, _NT, preferred_element_type=F32)
    logits = nt(rl, hh) + nt(rh, hl) + nt(rh, hh)
    gt_out[...] = _route(jax.nn.sigmoid(logits), rb_ref[...])


def _mix(layer, x_pair, oa_c, oa_l, og_c, og_l, sa, sb, mod_l, wpa, wpb, wo, norm2, rw3, rb):
    nt = N_TOK // TM
    n_ctx_t = N_CTX // TM
    row = lambda i: (i, 0)
    const = lambda i: (0, 0)
    ctx_row = lambda i: (jnp.minimum(i, n_ctx_t - 1), 0)
    lat_row = lambda i: (jnp.maximum(i - n_ctx_t, 0), 0)
    return pl.pallas_call(
        _mix_kernel,
        grid=(nt,),
        in_specs=_x_specs(x_pair) + [
            pl.BlockSpec((TM, ATTN_WIDTH), ctx_row),
            pl.BlockSpec((TM, ATTN_WIDTH), lat_row),
            pl.BlockSpec((TM, GLA_VW), ctx_row),
            pl.BlockSpec((TM, GLA_VW), lat_row),
            pl.BlockSpec((TM, D_MODEL), row),
            pl.BlockSpec((TM, D_MODEL), row),
            pl.BlockSpec((None, 1, MOD_W), lambda i: (_mod_row(i, TM), 0, 0)),
            pl.BlockSpec((None, ATTN_WIDTH, D_MODEL), lambda i: (layer, 0, 0), pipeline_mode=pl.Buffered(1)),
            pl.BlockSpec((None, GLA_VW, D_MODEL), lambda i: (layer, 0, 0), pipeline_mode=pl.Buffered(1)),
            pl.BlockSpec((None, D_MODEL, D_MODEL), lambda i: (layer, 0, 0), pipeline_mode=pl.Buffered(1)),
            pl.BlockSpec((1, D_MODEL), const),
            pl.BlockSpec((2, N_EXPERTS, D_MODEL), lambda i: (0, 0, 0)),
            pl.BlockSpec((N_EXPERTS, 1), const),
        ],
        out_specs=[pl.BlockSpec((TM, D_MODEL), row), pl.BlockSpec((TM * ROW_TILE, LANES), row),
                   pl.BlockSpec((2 * N_EXPERTS, TM), lambda i: (0, i))],
        out_shape=[jax.ShapeDtypeStruct((N_TOK, D_MODEL), F32),
                   jax.ShapeDtypeStruct((N_TOK * ROW_TILE, LANES), F32),
                   jax.ShapeDtypeStruct((2 * N_EXPERTS, N_TOK), F32)],
        scratch_shapes=[pltpu.VMEM((ATTN_WIDTH, D_MODEL), BF16), pltpu.VMEM((GLA_VW, D_MODEL), BF16),
                        pltpu.VMEM((D_MODEL, D_MODEL), BF16)],
        compiler_params=pltpu.CompilerParams(dimension_semantics=("arbitrary",), vmem_limit_bytes=VMEM_LIMIT),
        name="mix",
    )(*x_pair, oa_c, oa_l, og_c, og_l, sa, sb, mod_l, wpa, wpb, wo, norm2, rw3, rb)


def _route_plan(gsel):
    gates = gsel[:N_EXPERTS]
    sel = gsel[N_EXPERTS:] > 0.5
    seli = sel.astype(jnp.int32)
    incl = jnp.cumsum(seli, axis=1)
    count = incl[:, -1]
    ntile = (count + TS - 1) // TS
    tile_end = jnp.cumsum(ntile)
    tile_start = tile_end - ntile
    total = tile_end[-1]
    slot = tile_start[:, None] * TS + incl - seli
    pos_a = jnp.min(jnp.where(sel, slot, N_SLOT), axis=0)
    pos_b = jnp.max(jnp.where(sel, slot, -1), axis=0)
    w_a = jnp.sum(jnp.where(sel & (slot == pos_a[None, :]), gates, 0.0), axis=0)
    w_b = jnp.sum(jnp.where(sel & (slot == pos_b[None, :]), gates, 0.0), axis=0)
    tile_id = jnp.minimum(jnp.arange(N_TILE, dtype=jnp.int32), total - 1)
    tile_expert = jnp.sum((tile_id[:, None] >= tile_end[None, :]).astype(jnp.int32), axis=1)
    nt = N_TOK // TM
    pos = jnp.concatenate([pos_a.reshape(nt, 1, TM), pos_b.reshape(nt, 1, TM)], axis=2).astype(jnp.int32)
    plan = jnp.concatenate([tile_start, ntile, total[None]]).astype(jnp.int32)
    return pos, jnp.stack([w_a, w_b], axis=1), tile_expert.astype(jnp.int32), plan


def _to_row_tiles(ref, x):
    for k in range(ROW_TILE):
        ref[pl.ds(k, x.shape[0], stride=ROW_TILE), :] = x[:, k * LANES:(k + 1) * LANES]


def _from_row_tiles(ref, rows):
    return jnp.concatenate([ref[pl.ds(k, rows, stride=ROW_TILE), :] for k in range(ROW_TILE)], axis=1)


def _tile_rows(row, n=1):
    return pl.ds(pl.multiple_of(row * ROW_TILE, ROW_TILE), n * ROW_TILE)


def _row_copy(src, src_row, dst, dst_row, sem):
    return pltpu.make_async_copy(src.at[_tile_rows(src_row), :], dst.at[_tile_rows(dst_row), :], sem)


def _slot_rows_kernel(plan_ref, pos_ref, rows_ref):
    def zero_tile(tile):
        def zero(r, c):
            rows_ref[tile * TS + r] = 0
            return c
        lax.fori_loop(0, TS, zero, 0, unroll=16)

    def last_tile(e, c):
        @pl.when(plan_ref[N_EXPERTS + e] > 0)
        def _():
            zero_tile(plan_ref[e] + plan_ref[N_EXPERTS + e] - 1)
        return c

    def unused_tile(t, c):
        @pl.when(t >= plan_ref[PLAN_TOTAL])
        def _():
            zero_tile(t)
        return c

    lax.fori_loop(0, N_EXPERTS, last_tile, 0)
    lax.fori_loop(0, N_TILE, unused_tile, 0)

    def tile(i, c):
        def token(r, c2):
            rows_ref[pos_ref[i * (2 * TM) + r]] = i * TM + r
            rows_ref[pos_ref[i * (2 * TM) + TM + r]] = i * TM + r
            return c2
        return lax.fori_loop(0, TM, token, c, unroll=16)

    lax.fori_loop(0, N_TOK // TM, tile, 0)


def _slot_rows(plan, pos):
    smem = pl.BlockSpec(memory_space=pltpu.SMEM)
    rows = pl.pallas_call(
        _slot_rows_kernel,
        in_specs=[smem, smem],
        out_specs=smem,
        out_shape=jax.ShapeDtypeStruct((N_SLOT,), jnp.int32),
        name="slot_rows",
    )(plan, pos.reshape(2 * N_TOK))
    return rows.reshape(N_TILE, 1, TS)


def _moe_kernel(te_ref, plan_ref, rows_ref, nxt_ref, h_ref, wg_ref, wu_ref, wd_ref, ys_ref, x_ref, sem,
                wg_s, wu_s, wd_s):
    i = pl.program_id(0)
    total = plan_ref[PLAN_TOTAL]

    def gather(r_ref, slot):
        def start(r, c):
            _row_copy(h_ref, r_ref[0, r], x_ref.at[slot], r, sem.at[slot]).start()
            return c
        lax.fori_loop(0, TS, start, 0, unroll=8)

    @pl.when(i == 0)
    def _():
        gather(rows_ref, 0)

    @pl.when(i + 1 < total)
    def _():
        gather(nxt_ref, (i + 1) % 2)

    @pl.when(i < total)
    def _():
        slot = i % 2

        def wait(r, c):
            _row_copy(h_ref, 0, x_ref.at[slot], 0, sem.at[slot]).wait()
            return c

        lax.fori_loop(0, TS, wait, 0, unroll=8)

        @pl.when((i == 0) | (te_ref[i] != te_ref[jnp.maximum(i - 1, 0)]))
        def _():
            wg_s[...] = wg_ref[...].astype(BF16)
            wu_s[...] = wu_ref[...].astype(BF16)
            wd_s[...] = wd_ref[...].astype(BF16)

        x = _from_row_tiles(x_ref.at[slot], TS).astype(BF16)
        up = _dot(x, wu_s[...])
        gt = _dot(x, wg_s[...])
        act = (gt * jax.nn.sigmoid(gt) * up).astype(BF16)
        _to_row_tiles(ys_ref, _dot(act, wd_s[...]))

    @pl.when(i >= total)
    def _():
        ys_ref[...] = jnp.zeros(ys_ref.shape, F32)


def _moe(layer, tile_expert, plan, rows, h2, w_gate, w_up, w_down):
    used = lambda i, plan: jnp.minimum(i, plan[PLAN_TOTAL] - 1)
    w_spec = lambda a, b: pl.BlockSpec((None, None, a, b), lambda i, te, plan: (layer, te[i], 0, 0))
    rows_spec = lambda d: pl.BlockSpec((None, 1, TS), lambda i, te, plan: (used(i + d, plan), 0, 0),
                                       memory_space=pltpu.SMEM)
    return pl.pallas_call(
        _moe_kernel,
        grid_spec=pltpu.PrefetchScalarGridSpec(
            num_scalar_prefetch=2,
            grid=(N_TILE,),
            in_specs=[
                rows_spec(0), rows_spec(1), pl.BlockSpec(memory_space=pl.ANY),
                w_spec(D_MODEL, D_EXPERT), w_spec(D_MODEL, D_EXPERT), w_spec(D_EXPERT, D_MODEL),
            ],
            out_specs=pl.BlockSpec((TS * ROW_TILE, LANES), lambda i, te, plan: (i, 0)),
            scratch_shapes=[pltpu.VMEM((2, TS * ROW_TILE, LANES), F32), pltpu.SemaphoreType.DMA((2,)),
                            pltpu.VMEM((D_MODEL, D_EXPERT), BF16), pltpu.VMEM((D_MODEL, D_EXPERT), BF16),
                            pltpu.VMEM((D_EXPERT, D_MODEL), BF16)],
        ),
        out_shape=jax.ShapeDtypeStruct((N_SLOT * ROW_TILE, LANES), F32),
        compiler_params=pltpu.CompilerParams(dimension_semantics=("arbitrary",), vmem_limit_bytes=VMEM_LIMIT),
        name="moe",
    )(tile_expert, plan, rows, rows, h2, w_gate, w_up, w_down)


def _combine_kernel(pos_ref, nxt_ref, w_ref, x1_ref, mod_ref, ys_ref, out_ref, ya_ref, yb_ref, sem, *, n):
    i = pl.program_id(0)

    def gather(p_ref, slot):
        def start(r, c):
            _row_copy(ys_ref, p_ref[0, r], ya_ref.at[slot], r, sem.at[slot]).start()
            _row_copy(ys_ref, p_ref[0, TM + r], yb_ref.at[slot], r, sem.at[slot]).start()
            return c
        lax.fori_loop(0, TM, start, 0, unroll=8)

    @pl.when(i == 0)
    def _():
        gather(pos_ref, 0)

    @pl.when(i + 1 < n)
    def _():
        gather(nxt_ref, (i + 1) % 2)

    slot = i % 2

    def wait(r, c):
        _row_copy(ys_ref, 0, ya_ref.at[slot], 0, sem.at[slot]).wait()
        return c

    lax.fori_loop(0, 2 * TM, wait, 0, unroll=8)
    g2 = mod_ref[:, 5 * D_MODEL:6 * D_MODEL]
    w = w_ref[...]
    ya = _from_row_tiles(ya_ref.at[slot], TM)
    yb = _from_row_tiles(yb_ref.at[slot], TM)
    out_ref[...] = x1_ref[...] + g2 * (w[:, 0:1] * ya + w[:, 1:2] * yb)


def _combine(pos, w_ab, x1, mod_l, ys, tile0, nt):
    pos_spec = lambda fn: pl.BlockSpec((None, 1, 2 * TM), fn, memory_space=pltpu.SMEM)
    return pl.pallas_call(
        partial(_combine_kernel, n=nt),
        grid=(nt,),
        in_specs=[
            pos_spec(lambda i: (tile0 + i, 0, 0)),
            pos_spec(lambda i: (tile0 + jnp.minimum(i + 1, nt - 1), 0, 0)),
            pl.BlockSpec((TM, 2), lambda i: (tile0 + i, 0)),
            pl.BlockSpec((TM, D_MODEL), lambda i: (tile0 + i, 0)),
            pl.BlockSpec((None, 1, MOD_W), lambda i: (_mod_row(tile0 + i, TM), 0, 0)),
            pl.BlockSpec(memory_space=pl.ANY),
        ],
        out_specs=pl.BlockSpec((TM, D_MODEL), lambda i: (i, 0)),
        out_shape=jax.ShapeDtypeStruct((nt * TM, D_MODEL), F32),
        scratch_shapes=[pltpu.VMEM((2, TM * ROW_TILE, LANES), F32), pltpu.VMEM((2, TM * ROW_TILE, LANES), F32),
                        pltpu.SemaphoreType.DMA((2,))],
        compiler_params=pltpu.CompilerParams(dimension_semantics=("arbitrary",), vmem_limit_bytes=VMEM_LIMIT),
        name="combine",
    )(pos, pos, w_ab, x1, mod_l, ys)


def _rope_tables():
    pos = np.arange(DEC_SEQ)
    rows = (pos // GRID_W).astype(np.float64)
    cols = (pos % GRID_W).astype(np.float64)
    half = HEAD_DIM // 2
    inv_freq = ROPE_THETA ** (-np.arange(0, half, 2, dtype=np.float64) / half)
    ang_r = rows[:, None] * inv_freq[None, :]
    ang_c = cols[:, None] * inv_freq[None, :]
    cos_h = np.concatenate([np.cos(ang_r), np.cos(ang_r), np.cos(ang_c), np.cos(ang_c)], axis=-1)
    sin_h = np.concatenate([-np.sin(ang_r), np.sin(ang_r), -np.sin(ang_c), np.sin(ang_c)], axis=-1)
    cos = np.concatenate([np.ones((TM, HEAD_DIM)), cos_h], axis=0)
    sin = np.concatenate([np.zeros((TM, HEAD_DIM)), sin_h], axis=0)
    return (jnp.asarray(np.tile(cos, (1, N_HEADS)), F32), jnp.asarray(np.tile(sin, (1, N_HEADS)), F32))


def _pack_w2(w2):
    z = jnp.zeros((GLA_RANK, GLA_KW), w2.dtype)
    top = jnp.concatenate([w2[0], z], axis=1)
    bot = jnp.concatenate([z, w2[1]], axis=1)
    pad = jnp.zeros((LR_PAD - 2 * GLA_RANK, 2 * GLA_KW), w2.dtype)
    return jnp.concatenate([top, bot, pad], axis=0).astype(BF16)


def kernel(x_prompt, x_sample, cache_k, cache_v, state_gla, c, c_ctx, norm1, norm2, w_ada, b_ada, w_in,
           q_norm, k_norm, gla_w2, gla_b, gla_norm, w_pa, w_pb, w_out, router_w, router_bias,
           w_gate, w_up, w_down):
    x_pair = (x_prompt.reshape(N_CTX, D_MODEL), x_sample.reshape(N_LAT, D_MODEL))
    cond =jnp.concatenate([c_ctx[None, :], c, jnp.zeros((N_COND - 1 - DEC_BATCH, D_MODEL), F32)], axis=0)
    mod = _ada(cond, w_ada, b_ada).reshape(DEPTH, N_COND, 1, MOD_W)
    cos, sin = _rope_tables()
    gla_masks = _gla_masks()
    lane_head = np.arange(ATTN_WIDTH) // HEAD_DIM
    bd = jnp.asarray(lane_head[:, None] == lane_head[None, :], BF16)
    rw3 = jnp.stack(_split2(router_w.T), axis=0)
    rb = router_bias.reshape(N_EXPERTS, 1)
    ck = cache_k.reshape(DEC_BATCH, DEPTH, PAST_LEN, KV_WIDTH)
    cv = cache_v.reshape(DEC_BATCH, DEPTH, PAST_LEN, KV_WIDTH)

    keys, vals, states = [], [], []
    for l in range(DEPTH):
        q, kf, kr, v, gq, gk, gv, gg, la, sa, sb = _inproj(
            l, x_pair, mod[l], norm1[l][None, :], w_in, jnp.tile(q_norm[l], N_HEADS)[None, :],
            jnp.tile(k_norm[l], KV_HEADS)[None, :], cos, sin, bd, _pack_w2(gla_w2[l]),
            gla_b[l].reshape(1, 2 * GLA_KW))
        oa_c = _attn_ctx(q, kr, v)
        oa_l = _attn_lat(l, q, kr, v, ck, cv)
        gn = jnp.tile(gla_norm[l], GLA_HEADS)[None, :]
        og_c, s_fin = _gla_ctx(gq, gk, gv, la, gg, gn, gla_masks)
        og_l = _gla_lat(l, gq, gk, gv, la, gg, gn, gla_masks, state_gla)
        x1, h2, gsel = _mix(l, x_pair, oa_c, oa_l, og_c, og_l, sa, sb, mod[l], w_pa, w_pb, w_out,
                            norm2[l][None, :], rw3, rb)
        pos, w_ab, tile_expert, plan = _route_plan(gsel)
        ys = _moe(l, tile_expert, plan, _slot_rows(plan, pos), h2, w_gate, w_up, w_down)
        n_ctx_t = N_CTX // TM
        if l + 1 < DEPTH:
            x = _combine(pos, w_ab, x1, mod[l], ys, 0, N_TOK // TM)
            x_pair = (x, x)
        else:
            x_pair = (_combine(pos, w_ab, x1, mod[l], ys, 0, n_ctx_t),
                      _combine(pos, w_ab, x1, mod[l], ys, n_ctx_t, N_LAT // TM))
        keys.append(kf[:N_CTX].reshape(BATCH, SEQ, KV_HEADS, HEAD_DIM))
        vals.append(v[:N_CTX].reshape(BATCH, SEQ, KV_HEADS, HEAD_DIM))
        states.append(s_fin)
    y_prompt = x_pair[0].reshape(BATCH, SEQ, D_MODEL)
    y_sample = x_pair[1].reshape(DEC_BATCH, DEC_SEQ, D_MODEL)
    return (y_prompt, y_sample, jnp.stack(keys, axis=1), jnp.stack(vals, axis=1), jnp.stack(states, axis=1))
```

```python
from functools import partial

import jax
import jax.numpy as jnp
import numpy as np
from jax import lax
from jax.experimental import pallas as pl
from jax.experimental.pallas import tpu as pltpu

F32 = jnp.float32
BF16 = jnp.bfloat16

D_MODEL = 1024
BATCH = 16
SEQ = 256
DEPTH = 2
DEC_BATCH = 4
DEC_SEQ = 1024
PAST_LEN = 512
GRID_W = 64
N_HEADS = 8
KV_HEADS = 2
GROUP = N_HEADS // KV_HEADS
HEAD_DIM = 64
ATTN_WIDTH = N_HEADS * HEAD_DIM
KV_WIDTH = KV_HEADS * HEAD_DIM
ROPE_THETA = 10000.0
GLA_HEADS = 4
GLA_DK = 64
GLA_DV = 128
GLA_KW = GLA_HEADS * GLA_DK
GLA_VW = GLA_HEADS * GLA_DV
GLA_RANK = 16
GLA_GATE_NORM = 16.0
N_EXPERTS = 16
N_GROUPS = 4
EXPERTS_PER_GROUP = N_EXPERTS // N_GROUPS
D_EXPERT = 512
NORM_EPS = 1e-6

N_CTX = BATCH * SEQ
N_LAT = DEC_BATCH * DEC_SEQ
N_TOK = N_CTX + N_LAT
N_COND = 8
MOD_W = 6 * D_MODEL

TM = 512
TS = 512
N_TILE = 2 * N_TOK // TS + N_EXPERTS
N_SLOT = N_TILE * TS
PLAN_TOTAL = 2 * N_EXPERTS
ROW_TILE = 8
LANES = D_MODEL // ROW_TILE
GLA_CHUNK = 128
GLA_LEVELS = (64, 32, 16)
GLA_DIAG = 16
GLA_SEQS = 1
LR_PAD = 128
W_ROWS = 128
SEG_MIX = ATTN_WIDTH + 2 * KV_WIDTH + 2 * GLA_KW + 2 * GLA_VW
OFF_GA = SEG_MIX
OFF_GB = OFF_GA + D_MODEL
OFF_LR = OFF_GB + D_MODEL
IN_PACKED = OFF_LR + LR_PAD
IN_WIDTH = SEG_MIX + 2 * GLA_RANK + 2 * D_MODEL
VMEM_LIMIT = 56 * 1024 * 1024

_NT = (((1,), (1,)), ((), ()))
_TN = (((0,), (0,)), ((), ()))


def _mod_row(tile, tm):
    start = tile * tm
    return jnp.where(start < N_CTX, 0, 1 + (start - N_CTX) // DEC_SEQ)


def _split3(x):
    hi = x.astype(BF16)
    r1 = x - hi.astype(F32)
    mid = r1.astype(BF16)
    lo = (r1 - mid.astype(F32)).astype(BF16)
    return hi, mid, lo


def _split2(x):
    hi = x.astype(BF16)
    lo = (x - hi.astype(F32)).astype(BF16)
    return hi, lo


def _dot(a, b):
    return jnp.dot(a, b, preferred_element_type=F32)


def _rms(x):
    return x * lax.rsqrt(jnp.mean(x * x, axis=-1, keepdims=True) + NORM_EPS)


def _ada_kernel(cond_ref, w_ref, b_ref, out_ref):
    c = cond_ref[...]
    s = c * jax.nn.sigmoid(c)
    out_ref[...] = _dot(s.astype(BF16), w_ref[...].astype(BF16)) + b_ref[...]


def _ada(cond, w_ada, b_ada):
    nb = MOD_W // D_MODEL
    return pl.pallas_call(
        _ada_kernel,
        grid=(DEPTH, nb),
        in_specs=[
            pl.BlockSpec((N_COND, D_MODEL), lambda l, j: (0, 0)),
            pl.BlockSpec((None, D_MODEL, D_MODEL), lambda l, j: (l, 0, j)),
            pl.BlockSpec((None, 1, D_MODEL), lambda l, j: (l, 0, j)),
        ],
        out_specs=pl.BlockSpec((None, N_COND, D_MODEL), lambda l, j: (l, 0, j)),
        out_shape=jax.ShapeDtypeStruct((DEPTH, N_COND, MOD_W), F32),
        compiler_params=pltpu.CompilerParams(vmem_limit_bytes=VMEM_LIMIT),
        name="ada",
    )(cond, w_ada, b_ada.reshape(DEPTH, 1, MOD_W))


def _head_rmsnorm(x, bd):
    ssq = _dot((x * x).astype(BF16), bd)
    return x * lax.rsqrt(ssq * (1.0 / HEAD_DIM) + NORM_EPS)


def _rope(x, cos, sin):
    n = x.shape[-1]
    lane = lax.broadcasted_iota(jnp.int32, x.shape, 1)
    first = (lane & (HEAD_DIM // 4)) == 0
    partner = jnp.where(first, pltpu.roll(x, n - HEAD_DIM // 4, 1), pltpu.roll(x, HEAD_DIM // 4, 1))
    return x * cos + partner * sin


def _pack_w_in(w_ref, wp_ref):
    lr0 = SEG_MIX
    ga0 = lr0 + 2 * GLA_RANK

    def cast(dst, src, n):
        def body(i, carry):
            rows_out = pl.ds(pl.multiple_of(dst + i * W_ROWS, W_ROWS), W_ROWS)
            rows_in = pl.ds(pl.multiple_of(src + i * W_ROWS, 2 * GLA_RANK), W_ROWS)
            wp_ref[rows_out, :] = w_ref[rows_in, :].astype(BF16)
            return carry
        lax.fori_loop(0, n // W_ROWS, body, 0)

    cast(0, 0, SEG_MIX)
    cast(OFF_GA, ga0, 2 * D_MODEL)
    wp_ref[OFF_LR:IN_PACKED, :] = jnp.concatenate(
        [w_ref[lr0:ga0, :], jnp.zeros((LR_PAD - 2 * GLA_RANK, D_MODEL), F32)], axis=0).astype(BF16)


def _inproj_kernel(xc_ref, xl_ref, mod_ref, n1_ref, wf_ref, qn_ref, kn_ref, cos_ref, sin_ref, bd_ref, w2_ref, gb_ref,
                   q_out, kf_out, kr_out, v_out, gq_out, gk_out, gv_out, gg_out, la_out, sa_out, sb_out,
                   w_ref):
    @pl.when(pl.program_id(0) == 0)
    def _():
        _pack_w_in(wf_ref, w_ref)

    x = jnp.where(pl.program_id(0) < N_CTX // TM, xc_ref[...], xl_ref[...])
    sh1 = mod_ref[:, 0:D_MODEL]
    sc1 = mod_ref[:, D_MODEL:2 * D_MODEL]
    h = (_rms(x) * n1_ref[...] * (1.0 + sc1) + sh1).astype(BF16)

    def proj(lo, width):
        return lax.dot_general(h, w_ref[lo:lo + width, :], _NT, preferred_element_type=F32)

    cos = cos_ref[...]
    sin = sin_ref[...]
    bd = bd_ref[...]
    off = 0
    q = _head_rmsnorm(proj(off, ATTN_WIDTH), bd) * qn_ref[...]
    q_out[...] = (_rope(q, cos, sin) * HEAD_DIM ** -0.5).astype(BF16)
    off += ATTN_WIDTH
    k = _head_rmsnorm(proj(off, KV_WIDTH), bd[:KV_WIDTH, :KV_WIDTH]) * kn_ref[...]
    kf_out[...] = k
    kr_out[...] = _rope(k, cos[:, :KV_WIDTH], sin[:, :KV_WIDTH]).astype(BF16)
    off += KV_WIDTH
    v_out[...] = proj(off, KV_WIDTH)
    off += KV_WIDTH
    gq_out[...] = (proj(off, GLA_KW) * GLA_DK ** -0.5).astype(BF16)
    off += GLA_KW
    gk_out[...] = proj(off, GLA_KW).astype(BF16)
    off += GLA_KW
    gv_out[...] = proj(off, GLA_VW).astype(BF16)
    off += GLA_VW
    gg = proj(off, GLA_VW)
    gg_out[...] = (gg * jax.nn.sigmoid(gg)).astype(BF16)
    sa_out[...] = jax.nn.sigmoid(proj(OFF_GA, D_MODEL)).astype(BF16)
    sb_out[...] = jax.nn.sigmoid(proj(OFF_GB, D_MODEL)).astype(BF16)
    lr = proj(OFF_LR, LR_PAD)
    z = _dot(lr.astype(BF16), w2_ref[...]) + gb_ref[...]
    log_sig = jnp.minimum(z, 0.0) - jnp.log1p(jnp.exp(-jnp.abs(z)))
    la_out[...] = log_sig * (1.0 / GLA_GATE_NORM)


def _x_specs(x_pair):
    n_ctx_t = N_CTX // TM
    lat0 = n_ctx_t if x_pair[0] is x_pair[1] else 0
    return [pl.BlockSpec((TM, D_MODEL), lambda i: (jnp.minimum(i, n_ctx_t - 1), 0)),
            pl.BlockSpec((TM, D_MODEL), lambda i: (lat0 + jnp.maximum(i - n_ctx_t, 0), 0))]


def _inproj(layer, x_pair, mod_l, norm1, w_in, qn, kn, cos, sin, bd, w2p, gb):
    nt = N_TOK // TM
    n_ctx_t = N_CTX // TM
    t_per_seq = DEC_SEQ // TM
    row = lambda i: (i, 0)
    const = lambda i: (0, 0)
    rope_blk = lambda i: (jnp.where(i < n_ctx_t, 0, 1 + (i - n_ctx_t) % t_per_seq), 0)

    def out(width, dtype):
        return pl.BlockSpec((TM, width), row), jax.ShapeDtypeStruct((N_TOK, width), dtype)

    outs = [out(ATTN_WIDTH, BF16), out(KV_WIDTH, F32), out(KV_WIDTH, BF16), out(KV_WIDTH, F32),
            out(GLA_KW, BF16), out(GLA_KW, BF16), out(GLA_VW, BF16), out(GLA_VW, BF16),
            out(2 * GLA_KW, F32), out(D_MODEL, BF16), out(D_MODEL, BF16)]
    return pl.pallas_call(
        _inproj_kernel,
        grid=(nt,),
        in_specs=_x_specs(x_pair) + [
            pl.BlockSpec((None, 1, MOD_W), lambda i: (_mod_row(i, TM), 0, 0)),
            pl.BlockSpec((1, D_MODEL), const),
            pl.BlockSpec((None, IN_WIDTH, D_MODEL), lambda i: (layer, 0, 0), pipeline_mode=pl.Buffered(1)),
            pl.BlockSpec((1, ATTN_WIDTH), const),
            pl.BlockSpec((1, KV_WIDTH), const),
            pl.BlockSpec((TM, ATTN_WIDTH), rope_blk),
            pl.BlockSpec((TM, ATTN_WIDTH), rope_blk),
            pl.BlockSpec((ATTN_WIDTH, ATTN_WIDTH), const),
            pl.BlockSpec((LR_PAD, 2 * GLA_KW), const),
            pl.BlockSpec((1, 2 * GLA_KW), const),
        ],
        out_specs=[o[0] for o in outs],
        out_shape=[o[1] for o in outs],
        scratch_shapes=[pltpu.VMEM((IN_PACKED, D_MODEL), BF16)],
        compiler_params=pltpu.CompilerParams(dimension_semantics=("arbitrary",), vmem_limit_bytes=VMEM_LIMIT),
        name="inproj",
    )(*x_pair, mod_l, norm1, w_in, qn, kn, cos, sin, bd, w2p, gb)


def _attn_kernel(*refs, has_cache):
    if has_cache:
        q_ref, k_ref, v_ref, ck_ref, cv_ref, o_ref = refs
    else:
        q_ref, k_ref, v_ref, o_ref = refs
    tq = q_ref.shape[0]
    for g in range(KV_HEADS):
        ks = slice(g * HEAD_DIM, (g + 1) * HEAD_DIM)
        qs = jnp.concatenate(
            [q_ref[:, (g * GROUP + j) * HEAD_DIM:(g * GROUP + j + 1) * HEAD_DIM] for j in range(GROUP)], axis=0)
        k = k_ref[:, ks]
        v = v_ref[:, ks].astype(BF16)
        s = lax.dot_general(qs, k, _NT, preferred_element_type=F32)
        m = jnp.max(s, axis=-1, keepdims=True)
        if has_cache:
            ck = ck_ref[:, ks].astype(BF16)
            cv = cv_ref[:, ks].astype(BF16)
            s2 = lax.dot_general(qs, ck, _NT, preferred_element_type=F32)
            m = jnp.maximum(m, jnp.max(s2, axis=-1, keepdims=True))
        p = jnp.exp(s - m)
        den = jnp.sum(p, axis=-1, keepdims=True)
        o = _dot(p.astype(BF16), v)
        if has_cache:
            p2 = jnp.exp(s2 - m)
            den = den + jnp.sum(p2, axis=-1, keepdims=True)
            o = o + _dot(p2.astype(BF16), cv)
        o = o / den
        for j in range(GROUP):
            hd = g * GROUP + j
            o_ref[:, hd * HEAD_DIM:(hd + 1) * HEAD_DIM] = o[j * tq:(j + 1) * tq, :].astype(BF16)


def _attn_ctx(q, kr, v):
    return pl.pallas_call(
        partial(_attn_kernel, has_cache=False),
        grid=(BATCH,),
        in_specs=[
            pl.BlockSpec((SEQ, ATTN_WIDTH), lambda i: (i, 0)),
            pl.BlockSpec((SEQ, KV_WIDTH), lambda i: (i, 0)),
            pl.BlockSpec((SEQ, KV_WIDTH), lambda i: (i, 0)),
        ],
        out_specs=pl.BlockSpec((SEQ, ATTN_WIDTH), lambda i: (i, 0)),
        out_shape=jax.ShapeDtypeStruct((N_CTX, ATTN_WIDTH), BF16),
        compiler_params=pltpu.CompilerParams(dimension_semantics=("parallel",), vmem_limit_bytes=VMEM_LIMIT),
        name="attn_ctx",
    )(q, kr, v)


def _attn_lat(layer, q, kr, v, cache_k, cache_v):
    tq = 256
    nq = DEC_SEQ // tq
    q0 = N_CTX // tq
    s0 = N_CTX // DEC_SEQ
    return pl.pallas_call(
        partial(_attn_kernel, has_cache=True),
        grid=(DEC_BATCH, nq),
        in_specs=[
            pl.BlockSpec((tq, ATTN_WIDTH), lambda b, i: (q0 + b * nq + i, 0)),
            pl.BlockSpec((DEC_SEQ, KV_WIDTH), lambda b, i: (s0 + b, 0)),
            pl.BlockSpec((DEC_SEQ, KV_WIDTH), lambda b, i: (s0 + b, 0)),
            pl.BlockSpec((None, None, PAST_LEN, KV_WIDTH), lambda b, i: (b, layer, 0, 0)),
            pl.BlockSpec((None, None, PAST_LEN, KV_WIDTH), lambda b, i: (b, layer, 0, 0)),
        ],
        out_specs=pl.BlockSpec((tq, ATTN_WIDTH), lambda b, i: (b * nq + i, 0)),
        out_shape=jax.ShapeDtypeStruct((N_LAT, ATTN_WIDTH), BF16),
        compiler_params=pltpu.CompilerParams(dimension_semantics=("parallel", "parallel"),
                                             vmem_limit_bytes=VMEM_LIMIT),
        name="attn_lat",
    )(q, kr, v, cache_k, cache_v)


def _gla_masks():
    c = GLA_CHUNK
    hrow = np.arange(GLA_HEADS * c) // c
    head_k = (hrow[:, None] == (np.arange(GLA_KW) // GLA_DK)[None, :])
    head_v = (hrow[:, None] == (np.arange(GLA_VW) // GLA_DV)[None, :])
    t = np.arange(c)[:, None]
    s = (np.arange(GLA_HEADS * c) % c)[None, :]
    pair = []
    for reverse in (False, True):
        for m in GLA_LEVELS:
            pair.append(t // (2 * m) == s // (2 * m))
        pair.append((t // GLA_DIAG == s // GLA_DIAG) & ((s >= t) if reverse else (s <= t)))
    return (jnp.asarray(head_k, BF16), jnp.asarray(head_v, BF16), jnp.asarray(np.stack(pair), F32))


def _block_diag(x, head_mask):
    return jnp.concatenate([x] * GLA_HEADS, axis=0) * head_mask


def _ref_rows(b, rows, rep):
    return jnp.concatenate([jnp.broadcast_to(b[r:r + 1, :], (rep, b.shape[1])) for r in rows], axis=0)


def _gla_scores(q, k, b, reverse, head_k, pair_ref):
    c = GLA_CHUNK
    t_idx = lax.broadcasted_iota(jnp.int32, (c, GLA_KW), 0)
    term0 = (len(GLA_LEVELS) + 1) if reverse else 0

    def scores(qt, kt):
        return lax.dot_general(qt.astype(BF16), _block_diag(kt.astype(BF16), head_k), _NT,
                               preferred_element_type=F32)

    total = None
    for i, m in enumerate(GLA_LEVELS):
        nblk = c // (2 * m)
        ref = _ref_rows(b, [2 * m * j + (m if reverse else m - 1) for j in range(nblk)], 2 * m)
        q_side = ((t_idx & m) == 0) if reverse else ((t_idx & m) != 0)
        qt = jnp.where(q_side, q * jnp.exp(b - ref), 0.0)
        kt = jnp.where(q_side, 0.0, k * jnp.exp(ref - b))
        part = scores(qt, kt) * pair_ref[term0 + i]
        total = part if total is None else total + part
    d = GLA_DIAG
    x = b - _ref_rows(b, [d * j + d // 2 for j in range(c // d)], d)
    diag = scores(q * jnp.exp(x), k * jnp.exp(-x))
    return total + jnp.where(pair_ref[term0 + len(GLA_LEVELS)] > 0.5, diag, 0.0)


def _cumsum_rows(tri, la):
    hi, lo = _split2(la)
    return _dot(tri, hi) + _dot(tri, lo)


def _gla_kernel(*refs, seq_len, has_state):
    gq_ref, gk_ref, gv_ref, la_ref, gg_ref, gn_ref, hk_ref, hv_ref, pair_ref = refs[:9]
    if has_state:
        s0_ref, o_ref, acc_ref, accb_ref, st_ref = refs[9:]
    else:
        o_ref, sfin_ref, acc_ref, accb_ref, st_ref = refs[9:]
    c = GLA_CHUNK
    nc = seq_len // c
    r_i = lax.broadcasted_iota(jnp.int32, (c, c), 0)
    c_i = lax.broadcasted_iota(jnp.int32, (c, c), 1)
    tril = jnp.where(c_i <= r_i, 1.0, 0.0).astype(BF16)
    triu = jnp.where(c_i >= r_i, 1.0, 0.0).astype(BF16)
    st_head_r = lax.broadcasted_iota(jnp.int32, (GLA_VW, GLA_KW), 0) // GLA_DV
    st_head_c = lax.broadcasted_iota(jnp.int32, (GLA_VW, GLA_KW), 1) // GLA_DK
    st_mask = st_head_r == st_head_c

    for s in range(GLA_SEQS):
        for d in range(2):
            st_ref[2 * s + d] = jnp.zeros((GLA_VW, GLA_KW), F32)
            if has_state:
                for hh in range(GLA_HEADS):
                    st_ref[2 * s + d, hh * GLA_DV:(hh + 1) * GLA_DV, hh * GLA_DK:(hh + 1) * GLA_DK] = s0_ref[s, d, hh].T

    def load(s, n):
        rows = pl.ds(pl.multiple_of(s * seq_len + n * c, c), c)
        return (rows, gq_ref[rows, :].astype(F32), gk_ref[rows, :].astype(F32), gv_ref[rows, :])

    def inter(d, q, k, v, b, b_end):
        st = st_ref[d]
        q_in = (q * jnp.exp(b)).astype(BF16)
        o = lax.dot_general(q_in, st.astype(BF16), _NT, preferred_element_type=F32)
        k_st = (k * jnp.exp(b_end - b)).astype(BF16)
        upd = lax.dot_general(v, k_st, _TN, preferred_element_type=F32)
        st_ref[d] = st * jnp.exp(b_end) + jnp.where(st_mask, upd, 0.0)
        return o

    def body(n, carry):
        for s in range(GLA_SEQS):
            rows, q, k, v = load(s, n)
            b_f = _cumsum_rows(tril, la_ref[rows, 0:GLA_KW])
            b_b = _cumsum_rows(triu, la_ref[rows, GLA_KW:2 * GLA_KW])
            head_k = hk_ref[...]
            a = _gla_scores(q, k, b_f, False, head_k, pair_ref) + _gla_scores(q, k, b_b, True, head_k, pair_ref)
            o = _dot(a.astype(BF16), _block_diag(v, hv_ref[...]))
            acc_ref[rows, :] = o + inter(2 * s, q, k, v, b_f, b_f[c - 1:c, :])
            rows, q, k, v = load(s, nc - 1 - n)
            b_b = _cumsum_rows(triu, la_ref[rows, GLA_KW:2 * GLA_KW])
            accb_ref[rows, :] = inter(2 * s + 1, q, k, v, b_b, b_b[0:1, :])
        return carry

    lax.fori_loop(0, nc, body, 0)

    o = acc_ref[...] + accb_ref[...]
    gn = gn_ref[...]
    for hh in range(GLA_HEADS):
        sl = slice(hh * GLA_DV, (hh + 1) * GLA_DV)
        o_ref[:, sl] = (_rms(o[:, sl]) * gn[:, sl] * gg_ref[:, sl].astype(F32)).astype(BF16)
    if not has_state:
        for s in range(GLA_SEQS):
            for d in range(2):
                for hh in range(GLA_HEADS):
                    sfin_ref[s, d, hh] = st_ref[2 * s + d, hh * GLA_DV:(hh + 1) * GLA_DV,
                                                hh * GLA_DK:(hh + 1) * GLA_DK].T


def _gla_scratch(seq_len):
    rows = GLA_SEQS * seq_len
    return [pltpu.VMEM((rows, GLA_VW), F32), pltpu.VMEM((rows, GLA_VW), F32),
            pltpu.VMEM((2 * GLA_SEQS, GLA_VW, GLA_KW), F32)]


def _gla_mask_specs():
    n_terms = 2 * (len(GLA_LEVELS) + 1)
    return [pl.BlockSpec((GLA_HEADS * GLA_CHUNK, GLA_KW), lambda i: (0, 0)),
            pl.BlockSpec((GLA_HEADS * GLA_CHUNK, GLA_VW), lambda i: (0, 0)),
            pl.BlockSpec((n_terms, GLA_CHUNK, GLA_HEADS * GLA_CHUNK), lambda i: (0, 0, 0))]


def _gla_ctx(gq, gk, gv, la, gg, gn, masks):
    seq = lambda w: pl.BlockSpec((GLA_SEQS * SEQ, w), lambda i: (i, 0))
    return pl.pallas_call(
        partial(_gla_kernel, seq_len=SEQ, has_state=False),
        grid=(BATCH // GLA_SEQS,),
        in_specs=[seq(GLA_KW), seq(GLA_KW), seq(GLA_VW), seq(2 * GLA_KW), seq(GLA_VW),
                  pl.BlockSpec((1, GLA_VW), lambda i: (0, 0))] + _gla_mask_specs(),
        out_specs=[seq(GLA_VW),
                   pl.BlockSpec((GLA_SEQS, 2, GLA_HEADS, GLA_DK, GLA_DV), lambda i: (i, 0, 0, 0, 0))],
        out_shape=[jax.ShapeDtypeStruct((N_CTX, GLA_VW), BF16),
                   jax.ShapeDtypeStruct((BATCH, 2, GLA_HEADS, GLA_DK, GLA_DV), F32)],
        scratch_shapes=_gla_scratch(SEQ),
        compiler_params=pltpu.CompilerParams(dimension_semantics=("parallel",), vmem_limit_bytes=VMEM_LIMIT),
        name="gla_ctx",
    )(gq, gk, gv, la, gg, gn, *masks)


def _gla_lat(layer, gq, gk, gv, la, gg, gn, masks, s0):
    rows = GLA_SEQS * DEC_SEQ
    blk0 = N_CTX // rows
    seq = lambda w: pl.BlockSpec((rows, w), lambda b: (blk0 + b, 0))
    return pl.pallas_call(
        partial(_gla_kernel, seq_len=DEC_SEQ, has_state=True),
        grid=(DEC_BATCH // GLA_SEQS,),
        in_specs=[seq(GLA_KW), seq(GLA_KW), seq(GLA_VW), seq(2 * GLA_KW), seq(GLA_VW),
                  pl.BlockSpec((1, GLA_VW), lambda b: (0, 0))] + _gla_mask_specs() + [
                  pl.BlockSpec((GLA_SEQS, None, 2, GLA_HEADS, GLA_DK, GLA_DV),
                               lambda b: (b, layer, 0, 0, 0, 0))],
        out_specs=pl.BlockSpec((rows, GLA_VW), lambda b: (b, 0)),
        out_shape=jax.ShapeDtypeStruct((N_LAT, GLA_VW), BF16),
        scratch_shapes=_gla_scratch(DEC_SEQ),
        compiler_params=pltpu.CompilerParams(dimension_semantics=("parallel",), vmem_limit_bytes=VMEM_LIMIT),
        name="gla_lat",
    )(gq, gk, gv, la, gg, gn, *masks, s0)


def _route(scores, bias):
    biased = scores + bias
    v = [biased[e:e + 1, :] for e in range(N_EXPERTS)]
    s = [scores[e:e + 1, :] for e in range(N_EXPERTS)]
    gscore = []
    for g in range(N_GROUPS):
        a, b, c, d = v[4 * g:4 * g + 4]
        hi1, lo1, hi2, lo2 = jnp.maximum(a, b), jnp.minimum(a, b), jnp.maximum(c, d), jnp.minimum(c, d)
        gscore.append(jnp.maximum(hi1, hi2) + jnp.maximum(jnp.minimum(hi1, hi2), jnp.maximum(lo1, lo2)))
    one = lambda cond: jnp.where(cond, 1.0, 0.0)
    picked, chosen = [], []
    for g in range(N_GROUPS):
        best = None
        for g2 in range(N_GROUPS):
            if g2 == g:
                continue
            ok = one((gscore[g] > gscore[g2]) if g2 < g else (gscore[g] >= gscore[g2]))
            best = ok if best is None else best * ok
        for i in range(EXPERTS_PER_GROUP):
            e = 4 * g + i
            beaten = None
            for j in range(EXPERTS_PER_GROUP):
                if j == i:
                    continue
                e2 = 4 * g + j
                lost = one((v[e2] > v[e]) if j > i else (v[e2] >= v[e]))
                beaten = lost if beaten is None else beaten + lost
            chosen.append(best * one(beaten < 1.5))
            picked.append(chosen[-1] * s[e])
    den = picked[0]
    for e in range(1, N_EXPERTS):
        den = den + picked[e]
    return jnp.concatenate([p / den for p in picked] + chosen, axis=0)


def _mix_kernel(xc_ref, xl_ref, oac_ref, oal_ref, ogc_ref, ogl_ref, sa_ref, sb_ref, mod_ref, wpa_f, wpb_f, wo_f, n2_ref,
                rw_ref, rb_ref, x1_out, h2_out, gt_out, wpa_ref, wpb_ref, wo_ref):
    @pl.when(pl.program_id(0) == 0)
    def _():
        wpa_ref[...] = wpa_f[...].astype(BF16)
        wpb_ref[...] = wpb_f[...].astype(BF16)
        wo_ref[...] = wo_f[...].astype(BF16)

    is_ctx = pl.program_id(0) < N_CTX // TM
    oa = jnp.where(is_ctx, oac_ref[...], oal_ref[...])
    og = jnp.where(is_ctx, ogc_ref[...], ogl_ref[...])
    a = _dot(oa, wpa_ref[...])
    b = _dot(og, wpb_ref[...])
    merged = sa_ref[...].astype(F32) * a + sb_ref[...].astype(F32) * b
    g1 = mod_ref[:, 2 * D_MODEL:3 * D_MODEL]
    sh2 = mod_ref[:, 3 * D_MODEL:4 * D_MODEL]
    sc2 = mod_ref[:, 4 * D_MODEL:5 * D_MODEL]
    x = jnp.where(is_ctx, xc_ref[...], xl_ref[...])
    x1 = x + g1 * _dot(merged.astype(BF16), wo_ref[...])
    x1_out[...] = x1
    h2 = _rms(x1) * n2_ref[...] * (1.0 + sc2) + sh2
    _to_row_tiles(h2_out, h2)
    hh, hl = _split2(h2)
    rh, rl = rw_ref[0], rw_ref[1]
    nt = lambda r, t: lax.dot_general(r, t, _NT, preferred_element_type=F32)
    logits = nt(rl, hh) + nt(rh, hl) + nt(rh, hh)
    gt_out[...] = _route(jax.nn.sigmoid(logits), rb_ref[...])


def _mix(layer, x_pair, oa_c, oa_l, og_c, og_l, sa, sb, mod_l, wpa, wpb, wo, norm2, rw3, rb):
    nt = N_TOK // TM
    n_ctx_t = N_CTX // TM
    row = lambda i: (i, 0)
    const = lambda i: (0, 0)
    ctx_row = lambda i: (jnp.minimum(i, n_ctx_t - 1), 0)
    lat_row = lambda i: (jnp.maximum(i - n_ctx_t, 0), 0)
    return pl.pallas_call(
        _mix_kernel,
        grid=(nt,),
        in_specs=_x_specs(x_pair) + [
            pl.BlockSpec((TM, ATTN_WIDTH), ctx_row),
            pl.BlockSpec((TM, ATTN_WIDTH), lat_row),
            pl.BlockSpec((TM, GLA_VW), ctx_row),
            pl.BlockSpec((TM, GLA_VW), lat_row),
            pl.BlockSpec((TM, D_MODEL), row),
            pl.BlockSpec((TM, D_MODEL), row),
            pl.BlockSpec((None, 1, MOD_W), lambda i: (_mod_row(i, TM), 0, 0)),
            pl.BlockSpec((None, ATTN_WIDTH, D_MODEL), lambda i: (layer, 0, 0), pipeline_mode=pl.Buffered(1)),
            pl.BlockSpec((None, GLA_VW, D_MODEL), lambda i: (layer, 0, 0), pipeline_mode=pl.Buffered(1)),
            pl.BlockSpec((None, D_MODEL, D_MODEL), lambda i: (layer, 0, 0), pipeline_mode=pl.Buffered(1)),
            pl.BlockSpec((1, D_MODEL), const),
            pl.BlockSpec((2, N_EXPERTS, D_MODEL), lambda i: (0, 0, 0)),
            pl.BlockSpec((N_EXPERTS, 1), const),
        ],
        out_specs=[pl.BlockSpec((TM, D_MODEL), row), pl.BlockSpec((TM * ROW_TILE, LANES), row),
                   pl.BlockSpec((2 * N_EXPERTS, TM), lambda i: (0, i))],
        out_shape=[jax.ShapeDtypeStruct((N_TOK, D_MODEL), F32),
                   jax.ShapeDtypeStruct((N_TOK * ROW_TILE, LANES), F32),
                   jax.ShapeDtypeStruct((2 * N_EXPERTS, N_TOK), F32)],
        scratch_shapes=[pltpu.VMEM((ATTN_WIDTH, D_MODEL), BF16), pltpu.VMEM((GLA_VW, D_MODEL), BF16),
                        pltpu.VMEM((D_MODEL, D_MODEL), BF16)],
        compiler_params=pltpu.CompilerParams(dimension_semantics=("arbitrary",), vmem_limit_bytes=VMEM_LIMIT),
        name="mix",
    )(*x_pair, oa_c, oa_l, og_c, og_l, sa, sb, mod_l, wpa, wpb, wo, norm2, rw3, rb)


def _route_plan(gsel):
    gates = gsel[:N_EXPERTS]
    sel = gsel[N_EXPERTS:] > 0.5
    seli = sel.astype(jnp.int32)
    incl = jnp.cumsum(seli, axis=1)
    count = incl[:, -1]
    ntile = (count + TS - 1) // TS
    tile_end = jnp.cumsum(ntile)
    tile_start = tile_end - ntile
    total = tile_end[-1]
    slot = tile_start[:, None] * TS + incl - seli
    pos_a = jnp.min(jnp.where(sel, slot, N_SLOT), axis=0)
    pos_b = jnp.max(jnp.where(sel, slot, -1), axis=0)
    w_a = jnp.sum(jnp.where(sel & (slot == pos_a[None, :]), gates, 0.0), axis=0)
    w_b = jnp.sum(jnp.where(sel & (slot == pos_b[None, :]), gates, 0.0), axis=0)
    tile_id = jnp.minimum(jnp.arange(N_TILE, dtype=jnp.int32), total - 1)
    tile_expert = jnp.sum((tile_id[:, None] >= tile_end[None, :]).astype(jnp.int32), axis=1)
    nt = N_TOK // TM
    pos = jnp.concatenate([pos_a.reshape(nt, 1, TM), pos_b.reshape(nt, 1, TM)], axis=2).astype(jnp.int32)
    plan = jnp.concatenate([tile_start, ntile, total[None]]).astype(jnp.int32)
    return pos, jnp.stack([w_a, w_b], axis=1), tile_expert.astype(jnp.int32), plan


def _to_row_tiles(ref, x):
    for k in range(ROW_TILE):
        ref[pl.ds(k, x.shape[0], stride=ROW_TILE), :] = x[:, k * LANES:(k + 1) * LANES]


def _from_row_tiles(ref, rows):
    return jnp.concatenate([ref[pl.ds(k, rows, stride=ROW_TILE), :] for k in range(ROW_TILE)], axis=1)


def _tile_rows(row, n=1):
    return pl.ds(pl.multiple_of(row * ROW_TILE, ROW_TILE), n * ROW_TILE)


def _row_copy(src, src_row, dst, dst_row, sem):
    return pltpu.make_async_copy(src.at[_tile_rows(src_row), :], dst.at[_tile_rows(dst_row), :], sem)


def _dispatch_kernel(plan_ref, pos_ref, h_ref, xs_ref, zero_ref, sem_z, sem):
    i = pl.program_id(0)
    total = plan_ref[PLAN_TOTAL]

    def zero_copy(tile):
        return pltpu.make_async_copy(zero_ref, xs_ref.at[_tile_rows(tile * TS, TS), :], sem_z)

    def last_tile(e):
        return plan_ref[e] + plan_ref[N_EXPERTS + e] - 1

    @pl.when(i == 0)
    def _():
        zero_ref[...] = jnp.zeros(zero_ref.shape, F32)

        def each(fn):
            def expert(e, c):
                @pl.when(plan_ref[N_EXPERTS + e] > 0)
                def _():
                    fn(zero_copy(last_tile(e)))
                return c

            def unused(t, c):
                @pl.when(t >= total)
                def _():
                    fn(zero_copy(t))
                return c

            lax.fori_loop(0, N_EXPERTS, expert, 0)
            lax.fori_loop(0, N_TILE, unused, 0)

        each(lambda cp: cp.start())
        each(lambda cp: cp.wait())

    def start(r, c):
        _row_copy(h_ref, r, xs_ref, pos_ref[0, r], sem).start()
        _row_copy(h_ref, r, xs_ref, pos_ref[0, TM + r], sem).start()
        return c

    def wait(r, c):
        _row_copy(h_ref, 0, xs_ref, 0, sem).wait()
        return c

    lax.fori_loop(0, TM, start, 0, unroll=8)
    lax.fori_loop(0, 2 * TM, wait, 0, unroll=8)


def _dispatch(plan, pos, h2):
    return pl.pallas_call(
        _dispatch_kernel,
        grid_spec=pltpu.PrefetchScalarGridSpec(
            num_scalar_prefetch=1,
            grid=(N_TOK // TM,),
            in_specs=[
                pl.BlockSpec((None, 1, 2 * TM), lambda i, plan: (i, 0, 0), memory_space=pltpu.SMEM),
                pl.BlockSpec((TM * ROW_TILE, LANES), lambda i, plan: (i, 0)),
            ],
            out_specs=pl.BlockSpec(memory_space=pl.ANY),
            scratch_shapes=[pltpu.VMEM((TS * ROW_TILE, LANES), F32), pltpu.SemaphoreType.DMA,
                            pltpu.SemaphoreType.DMA],
        ),
        out_shape=jax.ShapeDtypeStruct((N_SLOT * ROW_TILE, LANES), F32),
        compiler_params=pltpu.CompilerParams(dimension_semantics=("arbitrary",), vmem_limit_bytes=VMEM_LIMIT),
        name="dispatch",
    )(plan, pos, h2)


def _moe_kernel(te_ref, plan_ref, xs_ref, wg_ref, wu_ref, wd_ref, ys_ref, wg_s, wu_s, wd_s):
    i = pl.program_id(0)
    total = plan_ref[PLAN_TOTAL]

    @pl.when(i < total)
    def _():
        @pl.when((i == 0) | (te_ref[i] != te_ref[jnp.maximum(i - 1, 0)]))
        def _():
            wg_s[...] = wg_ref[...].astype(BF16)
            wu_s[...] = wu_ref[...].astype(BF16)
            wd_s[...] = wd_ref[...].astype(BF16)

        x = _from_row_tiles(xs_ref, TS).astype(BF16)
        up = _dot(x, wu_s[...])
        gt = _dot(x, wg_s[...])
        act = (gt * jax.nn.sigmoid(gt) * up).astype(BF16)
        _to_row_tiles(ys_ref, _dot(act, wd_s[...]))

    @pl.when(i >= total)
    def _():
        ys_ref[...] = jnp.zeros(ys_ref.shape, F32)


def _moe(layer, tile_expert, plan, xs, w_gate, w_up, w_down):
    used = lambda i, plan: jnp.minimum(i, plan[PLAN_TOTAL] - 1)
    w_spec = lambda a, b: pl.BlockSpec((None, None, a, b), lambda i, te, plan: (layer, te[i], 0, 0))
    return pl.pallas_call(
        _moe_kernel,
        grid_spec=pltpu.PrefetchScalarGridSpec(
            num_scalar_prefetch=2,
            grid=(N_TILE,),
            in_specs=[
                pl.BlockSpec((TS * ROW_TILE, LANES), lambda i, te, plan: (used(i, plan), 0)),
                w_spec(D_MODEL, D_EXPERT), w_spec(D_MODEL, D_EXPERT), w_spec(D_EXPERT, D_MODEL),
            ],
            out_specs=pl.BlockSpec((TS * ROW_TILE, LANES), lambda i, te, plan: (i, 0)),
            scratch_shapes=[pltpu.VMEM((D_MODEL, D_EXPERT), BF16), pltpu.VMEM((D_MODEL, D_EXPERT), BF16),
                            pltpu.VMEM((D_EXPERT, D_MODEL), BF16)],
        ),
        out_shape=jax.ShapeDtypeStruct((N_SLOT * ROW_TILE, LANES), F32),
        compiler_params=pltpu.CompilerParams(dimension_semantics=("arbitrary",), vmem_limit_bytes=VMEM_LIMIT),
        name="moe",
    )(tile_expert, plan, xs, w_gate, w_up, w_down)


def _combine_kernel(pos_ref, nxt_ref, w_ref, x1_ref, mod_ref, ys_ref, out_ref, ya_ref, yb_ref, sem, *, n):
    i = pl.program_id(0)

    def gather(p_ref, slot):
        def start(r, c):
            _row_copy(ys_ref, p_ref[0, r], ya_ref.at[slot], r, sem.at[slot]).start()
            _row_copy(ys_ref, p_ref[0, TM + r], yb_ref.at[slot], r, sem.at[slot]).start()
            return c
        lax.fori_loop(0, TM, start, 0, unroll=8)

    @pl.when(i == 0)
    def _():
        gather(pos_ref, 0)

    @pl.when(i + 1 < n)
    def _():
        gather(nxt_ref, (i + 1) % 2)

    slot = i % 2

    def wait(r, c):
        _row_copy(ys_ref, 0, ya_ref.at[slot], 0, sem.at[slot]).wait()
        return c

    lax.fori_loop(0, 2 * TM, wait, 0, unroll=8)
    g2 = mod_ref[:, 5 * D_MODEL:6 * D_MODEL]
    w = w_ref[...]
    ya = _from_row_tiles(ya_ref.at[slot], TM)
    yb = _from_row_tiles(yb_ref.at[slot], TM)
    out_ref[...] = x1_ref[...] + g2 * (w[:, 0:1] * ya + w[:, 1:2] * yb)


def _combine(pos, w_ab, x1, mod_l, ys, tile0, nt):
    pos_spec = lambda fn: pl.BlockSpec((None, 1, 2 * TM), fn, memory_space=pltpu.SMEM)
    return pl.pallas_call(
        partial(_combine_kernel, n=nt),
        grid=(nt,),
        in_specs=[
            pos_spec(lambda i: (tile0 + i, 0, 0)),
            pos_spec(lambda i: (tile0 + jnp.minimum(i + 1, nt - 1), 0, 0)),
            pl.BlockSpec((TM, 2), lambda i: (tile0 + i, 0)),
            pl.BlockSpec((TM, D_MODEL), lambda i: (tile0 + i, 0)),
            pl.BlockSpec((None, 1, MOD_W), lambda i: (_mod_row(tile0 + i, TM), 0, 0)),
            pl.BlockSpec(memory_space=pl.ANY),
        ],
        out_specs=pl.BlockSpec((TM, D_MODEL), lambda i: (i, 0)),
        out_shape=jax.ShapeDtypeStruct((nt * TM, D_MODEL), F32),
        scratch_shapes=[pltpu.VMEM((2, TM * ROW_TILE, LANES), F32), pltpu.VMEM((2, TM * ROW_TILE, LANES), F32),
                        pltpu.SemaphoreType.DMA((2,))],
        compiler_params=pltpu.CompilerParams(dimension_semantics=("arbitrary",), vmem_limit_bytes=VMEM_LIMIT),
        name="combine",
    )(pos, pos, w_ab, x1, mod_l, ys)


def _rope_tables():
    pos = np.arange(DEC_SEQ)
    rows = (pos // GRID_W).astype(np.float64)
    cols = (pos % GRID_W).astype(np.float64)
    half = HEAD_DIM // 2
    inv_freq = ROPE_THETA ** (-np.arange(0, half, 2, dtype=np.float64) / half)
    ang_r = rows[:, None] * inv_freq[None, :]
    ang_c = cols[:, None] * inv_freq[None, :]
    cos_h = np.concatenate([np.cos(ang_r), np.cos(ang_r), np.cos(ang_c), np.cos(ang_c)], axis=-1)
    sin_h = np.concatenate([-np.sin(ang_r), np.sin(ang_r), -np.sin(ang_c), np.sin(ang_c)], axis=-1)
    cos = np.concatenate([np.ones((TM, HEAD_DIM)), cos_h], axis=0)
    sin = np.concatenate([np.zeros((TM, HEAD_DIM)), sin_h], axis=0)
    return (jnp.asarray(np.tile(cos, (1, N_HEADS)), F32), jnp.asarray(np.tile(sin, (1, N_HEADS)), F32))


def _pack_w2(w2):
    z = jnp.zeros((GLA_RANK, GLA_KW), w2.dtype)
    top = jnp.concatenate([w2[0], z], axis=1)
    bot = jnp.concatenate([z, w2[1]], axis=1)
    pad = jnp.zeros((LR_PAD - 2 * GLA_RANK, 2 * GLA_KW), w2.dtype)
    return jnp.concatenate([top, bot, pad], axis=0).astype(BF16)


def kernel(x_prompt, x_sample, cache_k, cache_v, state_gla, c, c_ctx, norm1, norm2, w_ada, b_ada, w_in,
           q_norm, k_norm, gla_w2, gla_b, gla_norm, w_pa, w_pb, w_out, router_w, router_bias,
           w_gate, w_up, w_down):
    x_pair = (x_prompt.reshape(N_CTX, D_MODEL), x_sample.reshape(N_LAT, D_MODEL))
    cond =jnp.concatenate([c_ctx[None, :], c, jnp.zeros((N_COND - 1 - DEC_BATCH, D_MODEL), F32)], axis=0)
    mod = _ada(cond, w_ada, b_ada).reshape(DEPTH, N_COND, 1, MOD_W)
    cos, sin = _rope_tables()
    gla_masks = _gla_masks()
    w_in_t = jnp.swapaxes(w_in, 1, 2)
    lane_head = np.arange(ATTN_WIDTH) // HEAD_DIM
    bd = jnp.asarray(lane_head[:, None] == lane_head[None, :], BF16)
    rw3 = jnp.stack(_split2(router_w.T), axis=0)
    rb = router_bias.reshape(N_EXPERTS, 1)
    ck = cache_k.reshape(DEC_BATCH, DEPTH, PAST_LEN, KV_WIDTH)
    cv = cache_v.reshape(DEC_BATCH, DEPTH, PAST_LEN, KV_WIDTH)

    keys, vals, states = [], [], []
    for l in range(DEPTH):
        q, kf, kr, v, gq, gk, gv, gg, la, sa, sb = _inproj(
            l, x_pair, mod[l], norm1[l][None, :], w_in_t, jnp.tile(q_norm[l], N_HEADS)[None, :],
            jnp.tile(k_norm[l], KV_HEADS)[None, :], cos, sin, bd, _pack_w2(gla_w2[l]),
            gla_b[l].reshape(1, 2 * GLA_KW))
        oa_c = _attn_ctx(q, kr, v)
        oa_l = _attn_lat(l, q, kr, v, ck, cv)
        gn = jnp.tile(gla_norm[l], GLA_HEADS)[None, :]
        og_c, s_fin = _gla_ctx(gq, gk, gv, la, gg, gn, gla_masks)
        og_l = _gla_lat(l, gq, gk, gv, la, gg, gn, gla_masks, state_gla)
        x1, h2, gsel = _mix(l, x_pair, oa_c, oa_l, og_c, og_l, sa, sb, mod[l], w_pa, w_pb, w_out,
                            norm2[l][None, :], rw3, rb)
        pos, w_ab, tile_expert, plan = _route_plan(gsel)
        xs = _dispatch(plan, pos, h2)
        ys = _moe(l, tile_expert, plan, xs, w_gate, w_up, w_down)
        n_ctx_t = N_CTX // TM
        if l + 1 < DEPTH:
            x = _combine(pos, w_ab, x1, mod[l], ys, 0, N_TOK // TM)
            x_pair = (x, x)
        else:
            x_pair = (_combine(pos, w_ab, x1, mod[l], ys, 0, n_ctx_t),
                      _combine(pos, w_ab, x1, mod[l], ys, n_ctx_t, N_LAT // TM))
        keys.append(kf[:N_CTX].reshape(BATCH, SEQ, KV_HEADS, HEAD_DIM))
        vals.append(v[:N_CTX].reshape(BATCH, SEQ, KV_HEADS, HEAD_DIM))
        states.append(s_fin)
    y_prompt = x_pair[0].reshape(BATCH, SEQ, D_MODEL)
    y_sample = x_pair[1].reshape(DEC_BATCH, DEC_SEQ, D_MODEL)
    return (y_prompt, y_sample, jnp.stack(keys, axis=1), jnp.stack(vals, axis=1), jnp.stack(states, axis=1))
```

```python
from functools import partial

import jax
import jax.numpy as jnp
import numpy as np
from jax import lax
from jax.experimental import pallas as pl
from jax.experimental.pallas import tpu as pltpu

F32 = jnp.float32
BF16 = jnp.bfloat16

D_MODEL = 1024
BATCH = 16
SEQ = 256
DEPTH = 2
DEC_BATCH = 4
DEC_SEQ = 1024
PAST_LEN = 512
GRID_W = 64
N_HEADS = 8
KV_HEADS = 2
GROUP = N_HEADS // KV_HEADS
HEAD_DIM = 64
ATTN_WIDTH = N_HEADS * HEAD_DIM
KV_WIDTH = KV_HEADS * HEAD_DIM
ROPE_THETA = 10000.0
GLA_HEADS = 4
GLA_DK = 64
GLA_DV = 128
GLA_KW = GLA_HEADS * GLA_DK
GLA_VW = GLA_HEADS * GLA_DV
GLA_RANK = 16
GLA_GATE_NORM = 16.0
N_EXPERTS = 16
N_GROUPS = 4
EXPERTS_PER_GROUP = N_EXPERTS // N_GROUPS
D_EXPERT = 512
NORM_EPS = 1e-6

N_CTX = BATCH * SEQ
N_LAT = DEC_BATCH * DEC_SEQ
N_TOK = N_CTX + N_LAT
N_COND = 8
MOD_W = 6 * D_MODEL

TM = 512
TS = 512
N_TILE = 2 * N_TOK // TS + N_EXPERTS
N_SLOT = N_TILE * TS
PLAN_TOTAL = 2 * N_EXPERTS
ROW_TILE = 8
LANES = D_MODEL // ROW_TILE
GLA_CHUNK = 128
GLA_LEVELS = (64, 32, 16)
GLA_DIAG = 16
LR_PAD = 128
W_ROWS = 128
SEG_MIX = ATTN_WIDTH + 2 * KV_WIDTH + 2 * GLA_KW + 2 * GLA_VW
OFF_GA = SEG_MIX
OFF_GB = OFF_GA + D_MODEL
OFF_LR = OFF_GB + D_MODEL
IN_PACKED = OFF_LR + LR_PAD
IN_WIDTH = SEG_MIX + 2 * GLA_RANK + 2 * D_MODEL
VMEM_LIMIT = 56 * 1024 * 1024

_NT = (((1,), (1,)), ((), ()))
_TN = (((0,), (0,)), ((), ()))


def _mod_row(tile, tm):
    start = tile * tm
    return jnp.where(start < N_CTX, 0, 1 + (start - N_CTX) // DEC_SEQ)


def _split3(x):
    hi = x.astype(BF16)
    r1 = x - hi.astype(F32)
    mid = r1.astype(BF16)
    lo = (r1 - mid.astype(F32)).astype(BF16)
    return hi, mid, lo


def _split2(x):
    hi = x.astype(BF16)
    lo = (x - hi.astype(F32)).astype(BF16)
    return hi, lo


def _dot(a, b):
    return jnp.dot(a, b, preferred_element_type=F32)


def _rms(x):
    return x * lax.rsqrt(jnp.mean(x * x, axis=-1, keepdims=True) + NORM_EPS)


def _ada_kernel(cond_ref, w_ref, b_ref, out_ref):
    c = cond_ref[...]
    s = c * jax.nn.sigmoid(c)
    out_ref[...] = _dot(s.astype(BF16), w_ref[...].astype(BF16)) + b_ref[...]


def _ada(cond, w_ada, b_ada):
    nb = MOD_W // D_MODEL
    return pl.pallas_call(
        _ada_kernel,
        grid=(DEPTH, nb),
        in_specs=[
            pl.BlockSpec((N_COND, D_MODEL), lambda l, j: (0, 0)),
            pl.BlockSpec((None, D_MODEL, D_MODEL), lambda l, j: (l, 0, j)),
            pl.BlockSpec((None, 1, D_MODEL), lambda l, j: (l, 0, j)),
        ],
        out_specs=pl.BlockSpec((None, N_COND, D_MODEL), lambda l, j: (l, 0, j)),
        out_shape=jax.ShapeDtypeStruct((DEPTH, N_COND, MOD_W), F32),
        compiler_params=pltpu.CompilerParams(vmem_limit_bytes=VMEM_LIMIT),
        name="ada",
    )(cond, w_ada, b_ada.reshape(DEPTH, 1, MOD_W))


def _head_rmsnorm(x, bd):
    ssq = _dot((x * x).astype(BF16), bd)
    return x * lax.rsqrt(ssq * (1.0 / HEAD_DIM) + NORM_EPS)


def _rope(x, cos, sin):
    n = x.shape[-1]
    lane = lax.broadcasted_iota(jnp.int32, x.shape, 1)
    first = (lane & (HEAD_DIM // 4)) == 0
    partner = jnp.where(first, pltpu.roll(x, n - HEAD_DIM // 4, 1), pltpu.roll(x, HEAD_DIM // 4, 1))
    return x * cos + partner * sin


def _pack_w_in(w_ref, wp_ref):
    lr0 = SEG_MIX
    ga0 = lr0 + 2 * GLA_RANK

    def cast(dst, src, n):
        def body(i, carry):
            rows_out = pl.ds(pl.multiple_of(dst + i * W_ROWS, W_ROWS), W_ROWS)
            rows_in = pl.ds(pl.multiple_of(src + i * W_ROWS, 2 * GLA_RANK), W_ROWS)
            wp_ref[rows_out, :] = w_ref[rows_in, :].astype(BF16)
            return carry
        lax.fori_loop(0, n // W_ROWS, body, 0)

    cast(0, 0, SEG_MIX)
    cast(OFF_GA, ga0, 2 * D_MODEL)
    wp_ref[OFF_LR:IN_PACKED, :] = jnp.concatenate(
        [w_ref[lr0:ga0, :], jnp.zeros((LR_PAD - 2 * GLA_RANK, D_MODEL), F32)], axis=0).astype(BF16)


def _inproj_kernel(xc_ref, xl_ref, mod_ref, n1_ref, wf_ref, qn_ref, kn_ref, cos_ref, sin_ref, bd_ref, w2_ref, gb_ref,
                   q_out, kf_out, kr_out, v_out, gq_out, gk_out, gv_out, gg_out, la_out, sa_out, sb_out,
                   w_ref):
    @pl.when(pl.program_id(0) == 0)
    def _():
        _pack_w_in(wf_ref, w_ref)

    x = jnp.where(pl.program_id(0) < N_CTX // TM, xc_ref[...], xl_ref[...])
    sh1 = mod_ref[:, 0:D_MODEL]
    sc1 = mod_ref[:, D_MODEL:2 * D_MODEL]
    h = (_rms(x) * n1_ref[...] * (1.0 + sc1) + sh1).astype(BF16)

    def proj(lo, width):
        return lax.dot_general(h, w_ref[lo:lo + width, :], _NT, preferred_element_type=F32)

    cos = cos_ref[...]
    sin = sin_ref[...]
    bd = bd_ref[...]
    off = 0
    q = _head_rmsnorm(proj(off, ATTN_WIDTH), bd) * qn_ref[...]
    q_out[...] = (_rope(q, cos, sin) * HEAD_DIM ** -0.5).astype(BF16)
    off += ATTN_WIDTH
    k = _head_rmsnorm(proj(off, KV_WIDTH), bd[:KV_WIDTH, :KV_WIDTH]) * kn_ref[...]
    kf_out[...] = k
    kr_out[...] = _rope(k, cos[:, :KV_WIDTH], sin[:, :KV_WIDTH]).astype(BF16)
    off += KV_WIDTH
    v_out[...] = proj(off, KV_WIDTH)
    off += KV_WIDTH
    gq_out[...] = (proj(off, GLA_KW) * GLA_DK ** -0.5).astype(BF16)
    off += GLA_KW
    gk_out[...] = proj(off, GLA_KW).astype(BF16)
    off += GLA_KW
    gv_out[...] = proj(off, GLA_VW).astype(BF16)
    off += GLA_VW
    gg = proj(off, GLA_VW)
    gg_out[...] = (gg * jax.nn.sigmoid(gg)).astype(BF16)
    sa_out[...] = jax.nn.sigmoid(proj(OFF_GA, D_MODEL)).astype(BF16)
    sb_out[...] = jax.nn.sigmoid(proj(OFF_GB, D_MODEL)).astype(BF16)
    lr = proj(OFF_LR, LR_PAD)
    z = _dot(lr.astype(BF16), w2_ref[...]) + gb_ref[...]
    log_sig = jnp.minimum(z, 0.0) - jnp.log1p(jnp.exp(-jnp.abs(z)))
    la_out[...] = log_sig * (1.0 / GLA_GATE_NORM)


def _x_specs(x_pair):
    n_ctx_t = N_CTX // TM
    lat0 = n_ctx_t if x_pair[0] is x_pair[1] else 0
    return [pl.BlockSpec((TM, D_MODEL), lambda i: (jnp.minimum(i, n_ctx_t - 1), 0)),
            pl.BlockSpec((TM, D_MODEL), lambda i: (lat0 + jnp.maximum(i - n_ctx_t, 0), 0))]


def _inproj(layer, x_pair, mod_l, norm1, w_in, qn, kn, cos, sin, bd, w2p, gb):
    nt = N_TOK // TM
    n_ctx_t = N_CTX // TM
    t_per_seq = DEC_SEQ // TM
    row = lambda i: (i, 0)
    const = lambda i: (0, 0)
    rope_blk = lambda i: (jnp.where(i < n_ctx_t, 0, 1 + (i - n_ctx_t) % t_per_seq), 0)

    def out(width, dtype):
        return pl.BlockSpec((TM, width), row), jax.ShapeDtypeStruct((N_TOK, width), dtype)

    outs = [out(ATTN_WIDTH, BF16), out(KV_WIDTH, F32), out(KV_WIDTH, BF16), out(KV_WIDTH, F32),
            out(GLA_KW, BF16), out(GLA_KW, BF16), out(GLA_VW, BF16), out(GLA_VW, BF16),
            out(2 * GLA_KW, F32), out(D_MODEL, BF16), out(D_MODEL, BF16)]
    return pl.pallas_call(
        _inproj_kernel,
        grid=(nt,),
        in_specs=_x_specs(x_pair) + [
            pl.BlockSpec((None, 1, MOD_W), lambda i: (_mod_row(i, TM), 0, 0)),
            pl.BlockSpec((1, D_MODEL), const),
            pl.BlockSpec((None, IN_WIDTH, D_MODEL), lambda i: (layer, 0, 0), pipeline_mode=pl.Buffered(1)),
            pl.BlockSpec((1, ATTN_WIDTH), const),
            pl.BlockSpec((1, KV_WIDTH), const),
            pl.BlockSpec((TM, ATTN_WIDTH), rope_blk),
            pl.BlockSpec((TM, ATTN_WIDTH), rope_blk),
            pl.BlockSpec((ATTN_WIDTH, ATTN_WIDTH), const),
            pl.BlockSpec((LR_PAD, 2 * GLA_KW), const),
            pl.BlockSpec((1, 2 * GLA_KW), const),
        ],
        out_specs=[o[0] for o in outs],
        out_shape=[o[1] for o in outs],
        scratch_shapes=[pltpu.VMEM((IN_PACKED, D_MODEL), BF16)],
        compiler_params=pltpu.CompilerParams(dimension_semantics=("arbitrary",), vmem_limit_bytes=VMEM_LIMIT),
        name="inproj",
    )(*x_pair, mod_l, norm1, w_in, qn, kn, cos, sin, bd, w2p, gb)


def _attn_kernel(*refs, has_cache):
    if has_cache:
        q_ref, k_ref, v_ref, ck_ref, cv_ref, o_ref = refs
    else:
        q_ref, k_ref, v_ref, o_ref = refs
    tq = q_ref.shape[0]
    for g in range(KV_HEADS):
        ks = slice(g * HEAD_DIM, (g + 1) * HEAD_DIM)
        qs = jnp.concatenate(
            [q_ref[:, (g * GROUP + j) * HEAD_DIM:(g * GROUP + j + 1) * HEAD_DIM] for j in range(GROUP)], axis=0)
        k = k_ref[:, ks]
        v = v_ref[:, ks].astype(BF16)
        s = lax.dot_general(qs, k, _NT, preferred_element_type=F32)
        m = jnp.max(s, axis=-1, keepdims=True)
        if has_cache:
            ck = ck_ref[:, ks].astype(BF16)
            cv = cv_ref[:, ks].astype(BF16)
            s2 = lax.dot_general(qs, ck, _NT, preferred_element_type=F32)
            m = jnp.maximum(m, jnp.max(s2, axis=-1, keepdims=True))
        p = jnp.exp(s - m)
        den = jnp.sum(p, axis=-1, keepdims=True)
        o = _dot(p.astype(BF16), v)
        if has_cache:
            p2 = jnp.exp(s2 - m)
            den = den + jnp.sum(p2, axis=-1, keepdims=True)
            o = o + _dot(p2.astype(BF16), cv)
        o = o / den
        for j in range(GROUP):
            hd = g * GROUP + j
            o_ref[:, hd * HEAD_DIM:(hd + 1) * HEAD_DIM] = o[j * tq:(j + 1) * tq, :].astype(BF16)


def _attn_ctx(q, kr, v):
    return pl.pallas_call(
        partial(_attn_kernel, has_cache=False),
        grid=(BATCH,),
        in_specs=[
            pl.BlockSpec((SEQ, ATTN_WIDTH), lambda i: (i, 0)),
            pl.BlockSpec((SEQ, KV_WIDTH), lambda i: (i, 0)),
            pl.BlockSpec((SEQ, KV_WIDTH), lambda i: (i, 0)),
        ],
        out_specs=pl.BlockSpec((SEQ, ATTN_WIDTH), lambda i: (i, 0)),
        out_shape=jax.ShapeDtypeStruct((N_CTX, ATTN_WIDTH), BF16),
        compiler_params=pltpu.CompilerParams(dimension_semantics=("parallel",), vmem_limit_bytes=VMEM_LIMIT),
        name="attn_ctx",
    )(q, kr, v)


def _attn_lat(layer, q, kr, v, cache_k, cache_v):
    tq = 256
    nq = DEC_SEQ // tq
    q0 = N_CTX // tq
    s0 = N_CTX // DEC_SEQ
    return pl.pallas_call(
        partial(_attn_kernel, has_cache=True),
        grid=(DEC_BATCH, nq),
        in_specs=[
            pl.BlockSpec((tq, ATTN_WIDTH), lambda b, i: (q0 + b * nq + i, 0)),
            pl.BlockSpec((DEC_SEQ, KV_WIDTH), lambda b, i: (s0 + b, 0)),
            pl.BlockSpec((DEC_SEQ, KV_WIDTH), lambda b, i: (s0 + b, 0)),
            pl.BlockSpec((None, None, PAST_LEN, KV_WIDTH), lambda b, i: (b, layer, 0, 0)),
            pl.BlockSpec((None, None, PAST_LEN, KV_WIDTH), lambda b, i: (b, layer, 0, 0)),
        ],
        out_specs=pl.BlockSpec((tq, ATTN_WIDTH), lambda b, i: (b * nq + i, 0)),
        out_shape=jax.ShapeDtypeStruct((N_LAT, ATTN_WIDTH), BF16),
        compiler_params=pltpu.CompilerParams(dimension_semantics=("parallel", "parallel"),
                                             vmem_limit_bytes=VMEM_LIMIT),
        name="attn_lat",
    )(q, kr, v, cache_k, cache_v)


def _gla_masks():
    c = GLA_CHUNK
    hrow = np.arange(GLA_HEADS * c) // c
    head_k = (hrow[:, None] == (np.arange(GLA_KW) // GLA_DK)[None, :])
    head_v = (hrow[:, None] == (np.arange(GLA_VW) // GLA_DV)[None, :])
    t = np.arange(c)[:, None]
    s = (np.arange(GLA_HEADS * c) % c)[None, :]
    pair = []
    for reverse in (False, True):
        for m in GLA_LEVELS:
            pair.append(t // (2 * m) == s // (2 * m))
        pair.append((t // GLA_DIAG == s // GLA_DIAG) & ((s >= t) if reverse else (s <= t)))
    return (jnp.asarray(head_k, BF16), jnp.asarray(head_v, BF16), jnp.asarray(np.stack(pair), F32))


def _block_diag(x, head_mask):
    return jnp.concatenate([x] * GLA_HEADS, axis=0) * head_mask


def _ref_rows(b, rows, rep):
    return jnp.concatenate([jnp.broadcast_to(b[r:r + 1, :], (rep, b.shape[1])) for r in rows], axis=0)


def _score_operands(q, k, b, reverse, head_k):
    c = GLA_CHUNK
    t_idx = lax.broadcasted_iota(jnp.int32, (c, GLA_KW), 0)
    terms = []
    for m in GLA_LEVELS:
        nblk = c // (2 * m)
        ref = _ref_rows(b, [2 * m * j + (m if reverse else m - 1) for j in range(nblk)], 2 * m)
        q_side = ((t_idx & m) == 0) if reverse else ((t_idx & m) != 0)
        terms.append((jnp.where(q_side, q * jnp.exp(b - ref), 0.0), jnp.where(q_side, 0.0, k * jnp.exp(ref - b))))
    d = GLA_DIAG
    x = b - _ref_rows(b, [d * j + d // 2 for j in range(c // d)], d)
    terms.append((q * jnp.exp(x), k * jnp.exp(-x)))
    return [(qt.astype(BF16), _block_diag(kt.astype(BF16), head_k)) for qt, kt in terms]


def _score_sum(parts, reverse, pair_ref):
    term0 = (len(GLA_LEVELS) + 1) if reverse else 0
    total = jnp.where(pair_ref[term0 + len(GLA_LEVELS)] > 0.5, parts[-1], 0.0)
    for i, part in enumerate(parts[:-1]):
        total = total + part * pair_ref[term0 + i]
    return total


def _cumsum_rows(tri, la):
    hi, lo = _split2(la)
    return _dot(tri, hi) + _dot(tri, lo)


def _gla_kernel(*refs, seq_len, has_state):
    gq_ref, gk_ref, gv_ref, la_ref, gg_ref, gn_ref, hk_ref, hv_ref, pair_ref = refs[:9]
    if has_state:
        s0_ref, o_ref, acc_ref, accb_ref, st_ref = refs[9:]
    else:
        o_ref, sfin_ref, acc_ref, accb_ref, st_ref = refs[9:]
    c = GLA_CHUNK
    nc = seq_len // c
    r_i = lax.broadcasted_iota(jnp.int32, (c, c), 0)
    c_i = lax.broadcasted_iota(jnp.int32, (c, c), 1)
    tril = jnp.where(c_i <= r_i, 1.0, 0.0).astype(BF16)
    triu = jnp.where(c_i >= r_i, 1.0, 0.0).astype(BF16)
    st_head_r = lax.broadcasted_iota(jnp.int32, (GLA_VW, GLA_KW), 0) // GLA_DV
    st_head_c = lax.broadcasted_iota(jnp.int32, (GLA_VW, GLA_KW), 1) // GLA_DK
    st_mask = st_head_r == st_head_c

    for d in range(2):
        st_ref[d] = jnp.zeros((GLA_VW, GLA_KW), F32)
        if has_state:
            for hh in range(GLA_HEADS):
                st_ref[d, hh * GLA_DV:(hh + 1) * GLA_DV, hh * GLA_DK:(hh + 1) * GLA_DK] = s0_ref[d, hh].T

    def load(n):
        rows = pl.ds(pl.multiple_of(n * c, c), c)
        return (rows, gq_ref[rows, :].astype(F32), gk_ref[rows, :].astype(F32), gv_ref[rows, :])

    nt = lambda a, b: lax.dot_general(a, b, _NT, preferred_element_type=F32)
    tn = lambda a, b: lax.dot_general(a, b, _TN, preferred_element_type=F32)

    def body(n, carry):
        rows, q, k, v = load(n)
        rows_b, q_b, k_b, v_b = load(nc - 1 - n)
        b_f = _cumsum_rows(tril, la_ref[rows, 0:GLA_KW])
        b_r = _cumsum_rows(triu, la_ref[rows, GLA_KW:2 * GLA_KW])
        b_b = _cumsum_rows(triu, la_ref[rows_b, GLA_KW:2 * GLA_KW])
        head_k = hk_ref[...]
        ops_f = _score_operands(q, k, b_f, False, head_k)
        ops_r = _score_operands(q, k, b_r, True, head_k)
        end_f, end_b = b_f[c - 1:c, :], b_b[0:1, :]
        qin_f = (q * jnp.exp(b_f)).astype(BF16)
        qin_b = (q_b * jnp.exp(b_b)).astype(BF16)
        kst_f = (k * jnp.exp(end_f - b_f)).astype(BF16)
        kst_b = (k_b * jnp.exp(end_b - b_b)).astype(BF16)
        st_f, st_b = st_ref[0], st_ref[1]
        parts_f = [nt(qt, kt) for qt, kt in ops_f]
        parts_r = [nt(qt, kt) for qt, kt in ops_r]
        o_f = nt(qin_f, st_f.astype(BF16))
        o_b = nt(qin_b, st_b.astype(BF16))
        upd_f = tn(v, kst_f)
        upd_b = tn(v_b, kst_b)
        a = _score_sum(parts_f, False, pair_ref) + _score_sum(parts_r, True, pair_ref)
        acc_ref[rows, :] = _dot(a.astype(BF16), _block_diag(v, hv_ref[...])) + o_f
        accb_ref[rows_b, :] = o_b
        st_ref[0] = st_f * jnp.exp(end_f) + jnp.where(st_mask, upd_f, 0.0)
        st_ref[1] = st_b * jnp.exp(end_b) + jnp.where(st_mask, upd_b, 0.0)
        return carry

    lax.fori_loop(0, nc, body, 0)

    o = acc_ref[...] + accb_ref[...]
    gn = gn_ref[...]
    for hh in range(GLA_HEADS):
        sl = slice(hh * GLA_DV, (hh + 1) * GLA_DV)
        o_ref[:, sl] = (_rms(o[:, sl]) * gn[:, sl] * gg_ref[:, sl].astype(F32)).astype(BF16)
    if not has_state:
        for d in range(2):
            for hh in range(GLA_HEADS):
                sfin_ref[d, hh] = st_ref[d, hh * GLA_DV:(hh + 1) * GLA_DV, hh * GLA_DK:(hh + 1) * GLA_DK].T


def _gla_scratch(seq_len):
    return [pltpu.VMEM((seq_len, GLA_VW), F32), pltpu.VMEM((seq_len, GLA_VW), F32),
            pltpu.VMEM((2, GLA_VW, GLA_KW), F32)]


def _gla_mask_specs():
    n_terms = 2 * (len(GLA_LEVELS) + 1)
    return [pl.BlockSpec((GLA_HEADS * GLA_CHUNK, GLA_KW), lambda i: (0, 0)),
            pl.BlockSpec((GLA_HEADS * GLA_CHUNK, GLA_VW), lambda i: (0, 0)),
            pl.BlockSpec((n_terms, GLA_CHUNK, GLA_HEADS * GLA_CHUNK), lambda i: (0, 0, 0))]


def _gla_ctx(gq, gk, gv, la, gg, gn, masks):
    seq = lambda w: pl.BlockSpec((SEQ, w), lambda i: (i, 0))
    return pl.pallas_call(
        partial(_gla_kernel, seq_len=SEQ, has_state=False),
        grid=(BATCH,),
        in_specs=[seq(GLA_KW), seq(GLA_KW), seq(GLA_VW), seq(2 * GLA_KW), seq(GLA_VW),
                  pl.BlockSpec((1, GLA_VW), lambda i: (0, 0))] + _gla_mask_specs(),
        out_specs=[seq(GLA_VW),
                   pl.BlockSpec((None, 2, GLA_HEADS, GLA_DK, GLA_DV), lambda i: (i, 0, 0, 0, 0))],
        out_shape=[jax.ShapeDtypeStruct((N_CTX, GLA_VW), BF16),
                   jax.ShapeDtypeStruct((BATCH, 2, GLA_HEADS, GLA_DK, GLA_DV), F32)],
        scratch_shapes=_gla_scratch(SEQ),
        compiler_params=pltpu.CompilerParams(dimension_semantics=("parallel",), vmem_limit_bytes=VMEM_LIMIT),
        name="gla_ctx",
    )(gq, gk, gv, la, gg, gn, *masks)


def _gla_lat(layer, gq, gk, gv, la, gg, gn, masks, s0):
    rows = DEC_SEQ
    blk0 = N_CTX // rows
    seq = lambda w: pl.BlockSpec((rows, w), lambda b: (blk0 + b, 0))
    return pl.pallas_call(
        partial(_gla_kernel, seq_len=DEC_SEQ, has_state=True),
        grid=(DEC_BATCH,),
        in_specs=[seq(GLA_KW), seq(GLA_KW), seq(GLA_VW), seq(2 * GLA_KW), seq(GLA_VW),
                  pl.BlockSpec((1, GLA_VW), lambda b: (0, 0))] + _gla_mask_specs() + [
                  pl.BlockSpec((None, None, 2, GLA_HEADS, GLA_DK, GLA_DV), lambda b: (b, layer, 0, 0, 0, 0))],
        out_specs=pl.BlockSpec((rows, GLA_VW), lambda b: (b, 0)),
        out_shape=jax.ShapeDtypeStruct((N_LAT, GLA_VW), BF16),
        scratch_shapes=_gla_scratch(DEC_SEQ),
        compiler_params=pltpu.CompilerParams(dimension_semantics=("parallel",), vmem_limit_bytes=VMEM_LIMIT),
        name="gla_lat",
    )(gq, gk, gv, la, gg, gn, *masks, s0)


def _route(scores, bias):
    biased = scores + bias
    v = [biased[e:e + 1, :] for e in range(N_EXPERTS)]
    s = [scores[e:e + 1, :] for e in range(N_EXPERTS)]
    gscore = []
    for g in range(N_GROUPS):
        a, b, c, d = v[4 * g:4 * g + 4]
        hi1, lo1, hi2, lo2 = jnp.maximum(a, b), jnp.minimum(a, b), jnp.maximum(c, d), jnp.minimum(c, d)
        gscore.append(jnp.maximum(hi1, hi2) + jnp.maximum(jnp.minimum(hi1, hi2), jnp.maximum(lo1, lo2)))
    one = lambda cond: jnp.where(cond, 1.0, 0.0)
    picked, chosen = [], []
    for g in range(N_GROUPS):
        best = None
        for g2 in range(N_GROUPS):
            if g2 == g:
                continue
            ok = one((gscore[g] > gscore[g2]) if g2 < g else (gscore[g] >= gscore[g2]))
            best = ok if best is None else best * ok
        for i in range(EXPERTS_PER_GROUP):
            e = 4 * g + i
            beaten = None
            for j in range(EXPERTS_PER_GROUP):
                if j == i:
                    continue
                e2 = 4 * g + j
                lost = one((v[e2] > v[e]) if j > i else (v[e2] >= v[e]))
                beaten = lost if beaten is None else beaten + lost
            chosen.append(best * one(beaten < 1.5))
            picked.append(chosen[-1] * s[e])
    den = picked[0]
    for e in range(1, N_EXPERTS):
        den = den + picked[e]
    return jnp.concatenate([p / den for p in picked] + chosen, axis=0)


def _mix_kernel(xc_ref, xl_ref, oac_ref, oal_ref, ogc_ref, ogl_ref, sa_ref, sb_ref, mod_ref, wpa_f, wpb_f, wo_f, n2_ref,
                rw_ref, rb_ref, x1_out, h2_out, gt_out, wpa_ref, wpb_ref, wo_ref):
    @pl.when(pl.program_id(0) == 0)
    def _():
        wpa_ref[...] = wpa_f[...].astype(BF16)
        wpb_ref[...] = wpb_f[...].astype(BF16)
        wo_ref[...] = wo_f[...].astype(BF16)

    is_ctx = pl.program_id(0) < N_CTX // TM
    oa = jnp.where(is_ctx, oac_ref[...], oal_ref[...])
    og = jnp.where(is_ctx, ogc_ref[...], ogl_ref[...])
    a = _dot(oa, wpa_ref[...])
    b = _dot(og, wpb_ref[...])
    merged = sa_ref[...].astype(F32) * a + sb_ref[...].astype(F32) * b
    g1 = mod_ref[:, 2 * D_MODEL:3 * D_MODEL]
    sh2 = mod_ref[:, 3 * D_MODEL:4 * D_MODEL]
    sc2 = mod_ref[:, 4 * D_MODEL:5 * D_MODEL]
    x = jnp.where(is_ctx, xc_ref[...], xl_ref[...])
    x1 = x + g1 * _dot(merged.astype(BF16), wo_ref[...])
    x1_out[...] = x1
    h2 = _rms(x1) * n2_ref[...] * (1.0 + sc2) + sh2
    _to_row_tiles(h2_out, h2)
    hh, hl = _split2(h2)
    rh, rl = rw_ref[0], rw_ref[1]
    nt = lambda r, t: lax.dot_general(r, t, _NT, preferred_element_type=F32)
    logits = nt(rl, hh) + nt(rh, hl) + nt(rh, hh)
    gt_out[...] = _route(jax.nn.sigmoid(logits), rb_ref[...])


def _mix(layer, x_pair, oa_c, oa_l, og_c, og_l, sa, sb, mod_l, wpa, wpb, wo, norm2, rw3, rb):
    nt = N_TOK // TM
    n_ctx_t = N_CTX // TM
    row = lambda i: (i, 0)
    const = lambda i: (0, 0)
    ctx_row = lambda i: (jnp.minimum(i, n_ctx_t - 1), 0)
    lat_row = lambda i: (jnp.maximum(i - n_ctx_t, 0), 0)
    return pl.pallas_call(
        _mix_kernel,
        grid=(nt,),
        in_specs=_x_specs(x_pair) + [
            pl.BlockSpec((TM, ATTN_WIDTH), ctx_row),
            pl.BlockSpec((TM, ATTN_WIDTH), lat_row),
            pl.BlockSpec((TM, GLA_VW), ctx_row),
            pl.BlockSpec((TM, GLA_VW), lat_row),
            pl.BlockSpec((TM, D_MODEL), row),
            pl.BlockSpec((TM, D_MODEL), row),
            pl.BlockSpec((None, 1, MOD_W), lambda i: (_mod_row(i, TM), 0, 0)),
            pl.BlockSpec((None, ATTN_WIDTH, D_MODEL), lambda i: (layer, 0, 0), pipeline_mode=pl.Buffered(1)),
            pl.BlockSpec((None, GLA_VW, D_MODEL), lambda i: (layer, 0, 0), pipeline_mode=pl.Buffered(1)),
            pl.BlockSpec((None, D_MODEL, D_MODEL), lambda i: (layer, 0, 0), pipeline_mode=pl.Buffered(1)),
            pl.BlockSpec((1, D_MODEL), const),
            pl.BlockSpec((2, N_EXPERTS, D_MODEL), lambda i: (0, 0, 0)),
            pl.BlockSpec((N_EXPERTS, 1), const),
        ],
        out_specs=[pl.BlockSpec((TM, D_MODEL), row), pl.BlockSpec((TM * ROW_TILE, LANES), row),
                   pl.BlockSpec((2 * N_EXPERTS, TM), lambda i: (0, i))],
        out_shape=[jax.ShapeDtypeStruct((N_TOK, D_MODEL), F32),
                   jax.ShapeDtypeStruct((N_TOK * ROW_TILE, LANES), F32),
                   jax.ShapeDtypeStruct((2 * N_EXPERTS, N_TOK), F32)],
        scratch_shapes=[pltpu.VMEM((ATTN_WIDTH, D_MODEL), BF16), pltpu.VMEM((GLA_VW, D_MODEL), BF16),
                        pltpu.VMEM((D_MODEL, D_MODEL), BF16)],
        compiler_params=pltpu.CompilerParams(dimension_semantics=("arbitrary",), vmem_limit_bytes=VMEM_LIMIT),
        name="mix",
    )(*x_pair, oa_c, oa_l, og_c, og_l, sa, sb, mod_l, wpa, wpb, wo, norm2, rw3, rb)


def _route_plan(gsel):
    gates = gsel[:N_EXPERTS]
    sel = gsel[N_EXPERTS:] > 0.5
    seli = sel.astype(jnp.int32)
    incl = jnp.cumsum(seli, axis=1)
    count = incl[:, -1]
    ntile = (count + TS - 1) // TS
    tile_end = jnp.cumsum(ntile)
    tile_start = tile_end - ntile
    total = tile_end[-1]
    slot = tile_start[:, None] * TS + incl - seli
    pos_a = jnp.min(jnp.where(sel, slot, N_SLOT), axis=0)
    pos_b = jnp.max(jnp.where(sel, slot, -1), axis=0)
    w_a = jnp.sum(jnp.where(sel & (slot == pos_a[None, :]), gates, 0.0), axis=0)
    w_b = jnp.sum(jnp.where(sel & (slot == pos_b[None, :]), gates, 0.0), axis=0)
    nt = N_TOK // TM
    pos = jnp.concatenate([pos_a.reshape(nt, 1, TM), pos_b.reshape(nt, 1, TM)], axis=2).astype(jnp.int32)
    plan = jnp.concatenate([tile_start, ntile, total[None]]).astype(jnp.int32)
    return pos, jnp.stack([w_a, w_b], axis=1), plan


def _to_row_tiles(ref, x):
    for k in range(ROW_TILE):
        ref[pl.ds(k, x.shape[0], stride=ROW_TILE), :] = x[:, k * LANES:(k + 1) * LANES]


def _from_row_tiles(ref, rows):
    return jnp.concatenate([ref[pl.ds(k, rows, stride=ROW_TILE), :] for k in range(ROW_TILE)], axis=1)


def _tile_rows(row, n=1):
    return pl.ds(pl.multiple_of(row * ROW_TILE, ROW_TILE), n * ROW_TILE)


def _row_copy(src, src_row, dst, dst_row, sem):
    return pltpu.make_async_copy(src.at[_tile_rows(src_row), :], dst.at[_tile_rows(dst_row), :], sem)


def _dispatch_kernel(plan_ref, pos_ref, h_ref, xs_ref, zero_ref, sem_z, sem):
    i = pl.program_id(0)
    total = plan_ref[PLAN_TOTAL]

    def zero_copy(tile):
        return pltpu.make_async_copy(zero_ref, xs_ref.at[_tile_rows(tile * TS, TS), :], sem_z)

    def last_tile(e):
        return plan_ref[e] + plan_ref[N_EXPERTS + e] - 1

    @pl.when(i == 0)
    def _():
        zero_ref[...] = jnp.zeros(zero_ref.shape, F32)

        def each(fn):
            def expert(e, c):
                @pl.when(plan_ref[N_EXPERTS + e] > 0)
                def _():
                    fn(zero_copy(last_tile(e)))
                return c

            def unused(t, c):
                @pl.when(t >= total)
                def _():
                    fn(zero_copy(t))
                return c

            lax.fori_loop(0, N_EXPERTS, expert, 0)
            lax.fori_loop(0, N_TILE, unused, 0)

        each(lambda cp: cp.start())
        each(lambda cp: cp.wait())

    def start(r, c):
        _row_copy(h_ref, r, xs_ref, pos_ref[0, r], sem).start()
        _row_copy(h_ref, r, xs_ref, pos_ref[0, TM + r], sem).start()
        return c

    def wait(r, c):
        _row_copy(h_ref, 0, xs_ref, 0, sem).wait()
        return c

    lax.fori_loop(0, TM, start, 0, unroll=8)
    lax.fori_loop(0, 2 * TM, wait, 0, unroll=8)


def _dispatch(plan, pos, h2):
    return pl.pallas_call(
        _dispatch_kernel,
        grid_spec=pltpu.PrefetchScalarGridSpec(
            num_scalar_prefetch=1,
            grid=(N_TOK // TM,),
            in_specs=[
                pl.BlockSpec((None, 1, 2 * TM), lambda i, plan: (i, 0, 0), memory_space=pltpu.SMEM),
                pl.BlockSpec((TM * ROW_TILE, LANES), lambda i, plan: (i, 0)),
            ],
            out_specs=pl.BlockSpec(memory_space=pl.ANY),
            scratch_shapes=[pltpu.VMEM((TS * ROW_TILE, LANES), F32), pltpu.SemaphoreType.DMA,
                            pltpu.SemaphoreType.DMA],
        ),
        out_shape=jax.ShapeDtypeStruct((N_SLOT * ROW_TILE, LANES), F32),
        compiler_params=pltpu.CompilerParams(dimension_semantics=("arbitrary",), vmem_limit_bytes=VMEM_LIMIT),
        name="dispatch",
    )(plan, pos, h2)


def _moe_kernel(plan_ref, xs_ref, wg_ref, wu_ref, wd_ref, ys_ref, x_buf, y_buf, zero_ref, sem_in, sem_out, sem_z,
                wg_s, wu_s, wd_s):
    e = pl.program_id(0)
    total = plan_ref[PLAN_TOTAL]
    first = plan_ref[e]

    def in_copy(g):
        return pltpu.make_async_copy(xs_ref.at[_tile_rows(g * TS, TS), :], x_buf.at[g % 2], sem_in.at[g % 2])

    def out_copy(g):
        return pltpu.make_async_copy(y_buf.at[g % 2], ys_ref.at[_tile_rows(g * TS, TS), :], sem_out.at[g % 2])

    def zero_copy(g):
        return pltpu.make_async_copy(zero_ref, ys_ref.at[_tile_rows(g * TS, TS), :], sem_z)

    @pl.when(e == 0)
    def _():
        in_copy(0).start()

    wg_s[...] = wg_ref[...].astype(BF16)
    wu_s[...] = wu_ref[...].astype(BF16)
    wd_s[...] = wd_ref[...].astype(BF16)

    def tile(t, c):
        g = first + t
        slot = g % 2

        @pl.when(g + 1 < total)
        def _():
            in_copy(g + 1).start()

        in_copy(g).wait()

        @pl.when(g >= 2)
        def _():
            out_copy(g - 2).wait()

        x = _from_row_tiles(x_buf.at[slot], TS).astype(BF16)
        up = _dot(x, wu_s[...])
        gt = _dot(x, wg_s[...])
        act = (gt * jax.nn.sigmoid(gt) * up).astype(BF16)
        _to_row_tiles(y_buf.at[slot], _dot(act, wd_s[...]))
        out_copy(g).start()
        return c

    lax.fori_loop(0, plan_ref[N_EXPERTS + e], tile, 0)

    @pl.when(e == N_EXPERTS - 1)
    def _():
        out_copy(total - 1).wait()

        @pl.when(total >= 2)
        def _():
            out_copy(total - 2).wait()

        zero_ref[...] = jnp.zeros(zero_ref.shape, F32)

        def each(fn):
            def unused(g, c):
                @pl.when(g >= total)
                def _():
                    fn(zero_copy(g))
                return c
            lax.fori_loop(0, N_TILE, unused, 0)

        each(lambda cp: cp.start())
        each(lambda cp: cp.wait())


def _moe(layer, plan, xs, w_gate, w_up, w_down):
    w_spec = lambda a, b: pl.BlockSpec((None, None, a, b), lambda e, plan: (layer, e, 0, 0))
    tile_buf = lambda n: pltpu.VMEM((n, TS * ROW_TILE, LANES), F32)
    return pl.pallas_call(
        _moe_kernel,
        grid_spec=pltpu.PrefetchScalarGridSpec(
            num_scalar_prefetch=1,
            grid=(N_EXPERTS,),
            in_specs=[
                pl.BlockSpec(memory_space=pl.ANY),
                w_spec(D_MODEL, D_EXPERT), w_spec(D_MODEL, D_EXPERT), w_spec(D_EXPERT, D_MODEL),
            ],
            out_specs=pl.BlockSpec(memory_space=pl.ANY),
            scratch_shapes=[tile_buf(2), tile_buf(2), pltpu.VMEM((TS * ROW_TILE, LANES), F32),
                            pltpu.SemaphoreType.DMA((2,)), pltpu.SemaphoreType.DMA((2,)), pltpu.SemaphoreType.DMA,
                            pltpu.VMEM((D_MODEL, D_EXPERT), BF16), pltpu.VMEM((D_MODEL, D_EXPERT), BF16),
                            pltpu.VMEM((D_EXPERT, D_MODEL), BF16)],
        ),
        out_shape=jax.ShapeDtypeStruct((N_SLOT * ROW_TILE, LANES), F32),
        compiler_params=pltpu.CompilerParams(dimension_semantics=("arbitrary",), vmem_limit_bytes=VMEM_LIMIT),
        name="moe",
    )(plan, xs, w_gate, w_up, w_down)


def _combine_kernel(pos_ref, nxt_ref, w_ref, x1_ref, mod_ref, ys_ref, out_ref, ya_ref, yb_ref, sem, *, n):
    i = pl.program_id(0)

    def gather(p_ref, slot):
        def start(r, c):
            _row_copy(ys_ref, p_ref[0, r], ya_ref.at[slot], r, sem.at[slot]).start()
            _row_copy(ys_ref, p_ref[0, TM + r], yb_ref.at[slot], r, sem.at[slot]).start()
            return c
        lax.fori_loop(0, TM, start, 0, unroll=8)

    @pl.when(i == 0)
    def _():
        gather(pos_ref, 0)

    @pl.when(i + 1 < n)
    def _():
        gather(nxt_ref, (i + 1) % 2)

    slot = i % 2

    def wait(r, c):
        _row_copy(ys_ref, 0, ya_ref.at[slot], 0, sem.at[slot]).wait()
        return c

    lax.fori_loop(0, 2 * TM, wait, 0, unroll=8)
    g2 = mod_ref[:, 5 * D_MODEL:6 * D_MODEL]
    w = w_ref[...]
    ya = _from_row_tiles(ya_ref.at[slot], TM)
    yb = _from_row_tiles(yb_ref.at[slot], TM)
    out_ref[...] = x1_ref[...] + g2 * (w[:, 0:1] * ya + w[:, 1:2] * yb)


def _combine(pos, w_ab, x1, mod_l, ys, tile0, nt):
    pos_spec = lambda fn: pl.BlockSpec((None, 1, 2 * TM), fn, memory_space=pltpu.SMEM)
    return pl.pallas_call(
        partial(_combine_kernel, n=nt),
        grid=(nt,),
        in_specs=[
            pos_spec(lambda i: (tile0 + i, 0, 0)),
            pos_spec(lambda i: (tile0 + jnp.minimum(i + 1, nt - 1), 0, 0)),
            pl.BlockSpec((TM, 2), lambda i: (tile0 + i, 0)),
            pl.BlockSpec((TM, D_MODEL), lambda i: (tile0 + i, 0)),
            pl.BlockSpec((None, 1, MOD_W), lambda i: (_mod_row(tile0 + i, TM), 0, 0)),
            pl.BlockSpec(memory_space=pl.ANY),
        ],
        out_specs=pl.BlockSpec((TM, D_MODEL), lambda i: (i, 0)),
        out_shape=jax.ShapeDtypeStruct((nt * TM, D_MODEL), F32),
        scratch_shapes=[pltpu.VMEM((2, TM * ROW_TILE, LANES), F32), pltpu.VMEM((2, TM * ROW_TILE, LANES), F32),
                        pltpu.SemaphoreType.DMA((2,))],
        compiler_params=pltpu.CompilerParams(dimension_semantics=("arbitrary",), vmem_limit_bytes=VMEM_LIMIT),
        name="combine",
    )(pos, pos, w_ab, x1, mod_l, ys)


def _rope_tables():
    pos = np.arange(DEC_SEQ)
    rows = (pos // GRID_W).astype(np.float64)
    cols = (pos % GRID_W).astype(np.float64)
    half = HEAD_DIM // 2
    inv_freq = ROPE_THETA ** (-np.arange(0, half, 2, dtype=np.float64) / half)
    ang_r = rows[:, None] * inv_freq[None, :]
    ang_c = cols[:, None] * inv_freq[None, :]
    cos_h = np.concatenate([np.cos(ang_r), np.cos(ang_r), np.cos(ang_c), np.cos(ang_c)], axis=-1)
    sin_h = np.concatenate([-np.sin(ang_r), np.sin(ang_r), -np.sin(ang_c), np.sin(ang_c)], axis=-1)
    cos = np.concatenate([np.ones((TM, HEAD_DIM)), cos_h], axis=0)
    sin = np.concatenate([np.zeros((TM, HEAD_DIM)), sin_h], axis=0)
    return (jnp.asarray(np.tile(cos, (1, N_HEADS)), F32), jnp.asarray(np.tile(sin, (1, N_HEADS)), F32))


def _pack_w2(w2):
    z = jnp.zeros((GLA_RANK, GLA_KW), w2.dtype)
    top = jnp.concatenate([w2[0], z], axis=1)
    bot = jnp.concatenate([z, w2[1]], axis=1)
    pad = jnp.zeros((LR_PAD - 2 * GLA_RANK, 2 * GLA_KW), w2.dtype)
    return jnp.concatenate([top, bot, pad], axis=0).astype(BF16)


def kernel(x_prompt, x_sample, cache_k, cache_v, state_gla, c, c_ctx, norm1, norm2, w_ada, b_ada, w_in,
           q_norm, k_norm, gla_w2, gla_b, gla_norm, w_pa, w_pb, w_out, router_w, router_bias,
           w_gate, w_up, w_down):
    x_pair = (x_prompt.reshape(N_CTX, D_MODEL), x_sample.reshape(N_LAT, D_MODEL))
    cond =jnp.concatenate([c_ctx[None, :], c, jnp.zeros((N_COND - 1 - DEC_BATCH, D_MODEL), F32)], axis=0)
    mod = _ada(cond, w_ada, b_ada).reshape(DEPTH, N_COND, 1, MOD_W)
    cos, sin = _rope_tables()
    gla_masks = _gla_masks()
    w_in_t = jnp.swapaxes(w_in, 1, 2)
    lane_head = np.arange(ATTN_WIDTH) // HEAD_DIM
    bd = jnp.asarray(lane_head[:, None] == lane_head[None, :], BF16)
    rw3 = jnp.stack(_split2(router_w.T), axis=0)
    rb = router_bias.reshape(N_EXPERTS, 1)
    ck = cache_k.reshape(DEC_BATCH, DEPTH, PAST_LEN, KV_WIDTH)
    cv = cache_v.reshape(DEC_BATCH, DEPTH, PAST_LEN, KV_WIDTH)

    keys, vals, states = [], [], []
    for l in range(DEPTH):
        q, kf, kr, v, gq, gk, gv, gg, la, sa, sb = _inproj(
            l, x_pair, mod[l], norm1[l][None, :], w_in_t, jnp.tile(q_norm[l], N_HEADS)[None, :],
            jnp.tile(k_norm[l], KV_HEADS)[None, :], cos, sin, bd, _pack_w2(gla_w2[l]),
            gla_b[l].reshape(1, 2 * GLA_KW))
        oa_c = _attn_ctx(q, kr, v)
        oa_l = _attn_lat(l, q, kr, v, ck, cv)
        gn = jnp.tile(gla_norm[l], GLA_HEADS)[None, :]
        og_c, s_fin = _gla_ctx(gq, gk, gv, la, gg, gn, gla_masks)
        og_l = _gla_lat(l, gq, gk, gv, la, gg, gn, gla_masks, state_gla)
        x1, h2, gsel = _mix(l, x_pair, oa_c, oa_l, og_c, og_l, sa, sb, mod[l], w_pa, w_pb, w_out,
                            norm2[l][None, :], rw3, rb)
        pos, w_ab, plan = _route_plan(gsel)
        xs = _dispatch(plan, pos, h2)
        ys = _moe(l, plan, xs, w_gate, w_up, w_down)
        n_ctx_t = N_CTX // TM
        if l + 1 < DEPTH:
            x = _combine(pos, w_ab, x1, mod[l], ys, 0, N_TOK // TM)
            x_pair = (x, x)
        else:
            x_pair = (_combine(pos, w_ab, x1, mod[l], ys, 0, n_ctx_t),
                      _combine(pos, w_ab, x1, mod[l], ys, n_ctx_t, N_LAT // TM))
        keys.append(kf[:N_CTX].reshape(BATCH, SEQ, KV_HEADS, HEAD_DIM))
        vals.append(v[:N_CTX].reshape(BATCH, SEQ, KV_HEADS, HEAD_DIM))
        states.append(s_fin)
    y_prompt = x_pair[0].reshape(BATCH, SEQ, D_MODEL)
    y_sample = x_pair[1].reshape(DEC_BATCH, DEC_SEQ, D_MODEL)
    return (y_prompt, y_sample, jnp.stack(keys, axis=1), jnp.stack(vals, axis=1), jnp.stack(states, axis=1))
```

```python
from functools import partial

import jax
import jax.numpy as jnp
import numpy as np
from jax import lax
from jax.experimental import pallas as pl
from jax.experimental.pallas import tpu as pltpu

F32 = jnp.float32
BF16 = jnp.bfloat16

D_MODEL = 1024
BATCH = 16
SEQ = 256
DEPTH = 2
DEC_BATCH = 4
DEC_SEQ = 1024
PAST_LEN = 512
GRID_W = 64
N_HEADS = 8
KV_HEADS = 2
GROUP = N_HEADS // KV_HEADS
HEAD_DIM = 64
ATTN_WIDTH = N_HEADS * HEAD_DIM
KV_WIDTH = KV_HEADS * HEAD_DIM
ROPE_THETA = 10000.0
GLA_HEADS = 4
GLA_DK = 64
GLA_DV = 128
GLA_KW = GLA_HEADS * GLA_DK
GLA_VW = GLA_HEADS * GLA_DV
GLA_RANK = 16
GLA_GATE_NORM = 16.0
N_EXPERTS = 16
N_GROUPS = 4
EXPERTS_PER_GROUP = N_EXPERTS // N_GROUPS
D_EXPERT = 512
NORM_EPS = 1e-6

N_CTX = BATCH * SEQ
N_LAT = DEC_BATCH * DEC_SEQ
N_TOK = N_CTX + N_LAT
N_COND = 8
MOD_W = 6 * D_MODEL

TM = 512
TS = 256
N_TILE = 2 * N_TOK // TS + N_EXPERTS
N_SLOT = N_TILE * TS
PLAN_TOTAL = 2 * N_EXPERTS
ROW_TILE = 8
LANES = D_MODEL // ROW_TILE
GLA_CHUNK = 128
GLA_LEVELS = (64, 32, 16)
GLA_DIAG = 16
LR_PAD = 128
W_ROWS = 128
SEG_MIX = ATTN_WIDTH + 2 * KV_WIDTH + 2 * GLA_KW + 2 * GLA_VW
OFF_GA = SEG_MIX
OFF_GB = OFF_GA + D_MODEL
OFF_LR = OFF_GB + D_MODEL
IN_PACKED = OFF_LR + LR_PAD
IN_WIDTH = SEG_MIX + 2 * GLA_RANK + 2 * D_MODEL
VMEM_LIMIT = 56 * 1024 * 1024

_NT = (((1,), (1,)), ((), ()))
_TN = (((0,), (0,)), ((), ()))


def _mod_row(tile, tm):
    start = tile * tm
    return jnp.where(start < N_CTX, 0, 1 + (start - N_CTX) // DEC_SEQ)


def _split3(x):
    hi = x.astype(BF16)
    r1 = x - hi.astype(F32)
    mid = r1.astype(BF16)
    lo = (r1 - mid.astype(F32)).astype(BF16)
    return hi, mid, lo


def _split2(x):
    hi = x.astype(BF16)
    lo = (x - hi.astype(F32)).astype(BF16)
    return hi, lo


def _dot(a, b):
    return jnp.dot(a, b, preferred_element_type=F32)


def _rms(x):
    return x * lax.rsqrt(jnp.mean(x * x, axis=-1, keepdims=True) + NORM_EPS)


def _ada_kernel(cond_ref, w_ref, b_ref, out_ref):
    c = cond_ref[...]
    s = c * jax.nn.sigmoid(c)
    out_ref[...] = _dot(s.astype(BF16), w_ref[...].astype(BF16)) + b_ref[...]


def _ada(cond, w_ada, b_ada):
    nb = MOD_W // D_MODEL
    return pl.pallas_call(
        _ada_kernel,
        grid=(DEPTH, nb),
        in_specs=[
            pl.BlockSpec((N_COND, D_MODEL), lambda l, j: (0, 0)),
            pl.BlockSpec((None, D_MODEL, D_MODEL), lambda l, j: (l, 0, j)),
            pl.BlockSpec((None, 1, D_MODEL), lambda l, j: (l, 0, j)),
        ],
        out_specs=pl.BlockSpec((None, N_COND, D_MODEL), lambda l, j: (l, 0, j)),
        out_shape=jax.ShapeDtypeStruct((DEPTH, N_COND, MOD_W), F32),
        compiler_params=pltpu.CompilerParams(vmem_limit_bytes=VMEM_LIMIT),
        name="ada",
    )(cond, w_ada, b_ada.reshape(DEPTH, 1, MOD_W))


def _head_rmsnorm(x, bd):
    ssq = _dot((x * x).astype(BF16), bd)
    return x * lax.rsqrt(ssq * (1.0 / HEAD_DIM) + NORM_EPS)


def _rope(x, cos, sin):
    n = x.shape[-1]
    lane = lax.broadcasted_iota(jnp.int32, x.shape, 1)
    first = (lane & (HEAD_DIM // 4)) == 0
    partner = jnp.where(first, pltpu.roll(x, n - HEAD_DIM // 4, 1), pltpu.roll(x, HEAD_DIM // 4, 1))
    return x * cos + partner * sin


def _pack_w_in(w_ref, wp_ref):
    lr0 = SEG_MIX
    ga0 = lr0 + 2 * GLA_RANK

    def cast(dst, src, n):
        def body(i, carry):
            rows_out = pl.ds(pl.multiple_of(dst + i * W_ROWS, W_ROWS), W_ROWS)
            rows_in = pl.ds(pl.multiple_of(src + i * W_ROWS, 2 * GLA_RANK), W_ROWS)
            wp_ref[rows_out, :] = w_ref[rows_in, :].astype(BF16)
            return carry
        lax.fori_loop(0, n // W_ROWS, body, 0)

    cast(0, 0, SEG_MIX)
    cast(OFF_GA, ga0, 2 * D_MODEL)
    wp_ref[OFF_LR:IN_PACKED, :] = jnp.concatenate(
        [w_ref[lr0:ga0, :], jnp.zeros((LR_PAD - 2 * GLA_RANK, D_MODEL), F32)], axis=0).astype(BF16)


def _inproj_kernel(xc_ref, xl_ref, mod_ref, n1_ref, wf_ref, qn_ref, kn_ref, cos_ref, sin_ref, bd_ref, w2_ref, gb_ref,
                   q_out, kf_out, kr_out, v_out, gq_out, gk_out, gv_out, gg_out, la_out, sa_out, sb_out,
                   w_ref):
    @pl.when(pl.program_id(0) == 0)
    def _():
        _pack_w_in(wf_ref, w_ref)

    x = jnp.where(pl.program_id(0) < N_CTX // TM, xc_ref[...], xl_ref[...])
    sh1 = mod_ref[:, 0:D_MODEL]
    sc1 = mod_ref[:, D_MODEL:2 * D_MODEL]
    h = (_rms(x) * n1_ref[...] * (1.0 + sc1) + sh1).astype(BF16)

    def proj(lo, width):
        return lax.dot_general(h, w_ref[lo:lo + width, :], _NT, preferred_element_type=F32)

    cos = cos_ref[...]
    sin = sin_ref[...]
    bd = bd_ref[...]
    off = 0
    q = _head_rmsnorm(proj(off, ATTN_WIDTH), bd) * qn_ref[...]
    q_out[...] = (_rope(q, cos, sin) * HEAD_DIM ** -0.5).astype(BF16)
    off += ATTN_WIDTH
    k = _head_rmsnorm(proj(off, KV_WIDTH), bd[:KV_WIDTH, :KV_WIDTH]) * kn_ref[...]
    kf_out[...] = k
    kr_out[...] = _rope(k, cos[:, :KV_WIDTH], sin[:, :KV_WIDTH]).astype(BF16)
    off += KV_WIDTH
    v_out[...] = proj(off, KV_WIDTH)
    off += KV_WIDTH
    gq_out[...] = (proj(off, GLA_KW) * GLA_DK ** -0.5).astype(BF16)
    off += GLA_KW
    gk_out[...] = proj(off, GLA_KW).astype(BF16)
    off += GLA_KW
    gv_out[...] = proj(off, GLA_VW).astype(BF16)
    off += GLA_VW
    gg = proj(off, GLA_VW)
    gg_out[...] = (gg * jax.nn.sigmoid(gg)).astype(BF16)
    sa_out[...] = jax.nn.sigmoid(proj(OFF_GA, D_MODEL)).astype(BF16)
    sb_out[...] = jax.nn.sigmoid(proj(OFF_GB, D_MODEL)).astype(BF16)
    lr = proj(OFF_LR, LR_PAD)
    z = _dot(lr.astype(BF16), w2_ref[...]) + gb_ref[...]
    log_sig = jnp.minimum(z, 0.0) - jnp.log1p(jnp.exp(-jnp.abs(z)))
    la_out[...] = log_sig * (1.0 / GLA_GATE_NORM)


def _x_specs(x_pair):
    n_ctx_t = N_CTX // TM
    lat0 = n_ctx_t if x_pair[0] is x_pair[1] else 0
    return [pl.BlockSpec((TM, D_MODEL), lambda i: (jnp.minimum(i, n_ctx_t - 1), 0)),
            pl.BlockSpec((TM, D_MODEL), lambda i: (lat0 + jnp.maximum(i - n_ctx_t, 0), 0))]


def _inproj(layer, x_pair, mod_l, norm1, w_in, qn, kn, cos, sin, bd, w2p, gb):
    nt = N_TOK // TM
    n_ctx_t = N_CTX // TM
    t_per_seq = DEC_SEQ // TM
    row = lambda i: (i, 0)
    const = lambda i: (0, 0)
    rope_blk = lambda i: (jnp.where(i < n_ctx_t, 0, 1 + (i - n_ctx_t) % t_per_seq), 0)

    def out(width, dtype):
        return pl.BlockSpec((TM, width), row), jax.ShapeDtypeStruct((N_TOK, width), dtype)

    outs = [out(ATTN_WIDTH, BF16), out(KV_WIDTH, F32), out(KV_WIDTH, BF16), out(KV_WIDTH, F32),
            out(GLA_KW, BF16), out(GLA_KW, BF16), out(GLA_VW, BF16), out(GLA_VW, BF16),
            out(2 * GLA_KW, F32), out(D_MODEL, BF16), out(D_MODEL, BF16)]
    return pl.pallas_call(
        _inproj_kernel,
        grid=(nt,),
        in_specs=_x_specs(x_pair) + [
            pl.BlockSpec((None, 1, MOD_W), lambda i: (_mod_row(i, TM), 0, 0)),
            pl.BlockSpec((1, D_MODEL), const),
            pl.BlockSpec((None, IN_WIDTH, D_MODEL), lambda i: (layer, 0, 0), pipeline_mode=pl.Buffered(1)),
            pl.BlockSpec((1, ATTN_WIDTH), const),
            pl.BlockSpec((1, KV_WIDTH), const),
            pl.BlockSpec((TM, ATTN_WIDTH), rope_blk),
            pl.BlockSpec((TM, ATTN_WIDTH), rope_blk),
            pl.BlockSpec((ATTN_WIDTH, ATTN_WIDTH), const),
            pl.BlockSpec((LR_PAD, 2 * GLA_KW), const),
            pl.BlockSpec((1, 2 * GLA_KW), const),
        ],
        out_specs=[o[0] for o in outs],
        out_shape=[o[1] for o in outs],
        scratch_shapes=[pltpu.VMEM((IN_PACKED, D_MODEL), BF16)],
        compiler_params=pltpu.CompilerParams(dimension_semantics=("arbitrary",), vmem_limit_bytes=VMEM_LIMIT),
        name="inproj",
    )(*x_pair, mod_l, norm1, w_in, qn, kn, cos, sin, bd, w2p, gb)


def _attn_kernel(*refs, has_cache):
    if has_cache:
        q_ref, k_ref, v_ref, ck_ref, cv_ref, o_ref = refs
    else:
        q_ref, k_ref, v_ref, o_ref = refs
    tq = q_ref.shape[0]
    for g in range(KV_HEADS):
        ks = slice(g * HEAD_DIM, (g + 1) * HEAD_DIM)
        qs = jnp.concatenate(
            [q_ref[:, (g * GROUP + j) * HEAD_DIM:(g * GROUP + j + 1) * HEAD_DIM] for j in range(GROUP)], axis=0)
        k = k_ref[:, ks]
        v = v_ref[:, ks].astype(BF16)
        s = lax.dot_general(qs, k, _NT, preferred_element_type=F32)
        m = jnp.max(s, axis=-1, keepdims=True)
        if has_cache:
            ck = ck_ref[:, ks].astype(BF16)
            cv = cv_ref[:, ks].astype(BF16)
            s2 = lax.dot_general(qs, ck, _NT, preferred_element_type=F32)
            m = jnp.maximum(m, jnp.max(s2, axis=-1, keepdims=True))
        p = jnp.exp(s - m)
        den = jnp.sum(p, axis=-1, keepdims=True)
        o = _dot(p.astype(BF16), v)
        if has_cache:
            p2 = jnp.exp(s2 - m)
            den = den + jnp.sum(p2, axis=-1, keepdims=True)
            o = o + _dot(p2.astype(BF16), cv)
        o = o / den
        for j in range(GROUP):
            hd = g * GROUP + j
            o_ref[:, hd * HEAD_DIM:(hd + 1) * HEAD_DIM] = o[j * tq:(j + 1) * tq, :].astype(BF16)


def _attn_ctx(q, kr, v):
    return pl.pallas_call(
        partial(_attn_kernel, has_cache=False),
        grid=(BATCH,),
        in_specs=[
            pl.BlockSpec((SEQ, ATTN_WIDTH), lambda i: (i, 0)),
            pl.BlockSpec((SEQ, KV_WIDTH), lambda i: (i, 0)),
            pl.BlockSpec((SEQ, KV_WIDTH), lambda i: (i, 0)),
        ],
        out_specs=pl.BlockSpec((SEQ, ATTN_WIDTH), lambda i: (i, 0)),
        out_shape=jax.ShapeDtypeStruct((N_CTX, ATTN_WIDTH), BF16),
        compiler_params=pltpu.CompilerParams(dimension_semantics=("parallel",), vmem_limit_bytes=VMEM_LIMIT),
        name="attn_ctx",
    )(q, kr, v)


def _attn_lat(layer, q, kr, v, cache_k, cache_v):
    tq = 256
    nq = DEC_SEQ // tq
    q0 = N_CTX // tq
    s0 = N_CTX // DEC_SEQ
    return pl.pallas_call(
        partial(_attn_kernel, has_cache=True),
        grid=(DEC_BATCH, nq),
        in_specs=[
            pl.BlockSpec((tq, ATTN_WIDTH), lambda b, i: (q0 + b * nq + i, 0)),
            pl.BlockSpec((DEC_SEQ, KV_WIDTH), lambda b, i: (s0 + b, 0)),
            pl.BlockSpec((DEC_SEQ, KV_WIDTH), lambda b, i: (s0 + b, 0)),
            pl.BlockSpec((None, None, PAST_LEN, KV_WIDTH), lambda b, i: (b, layer, 0, 0)),
            pl.BlockSpec((None, None, PAST_LEN, KV_WIDTH), lambda b, i: (b, layer, 0, 0)),
        ],
        out_specs=pl.BlockSpec((tq, ATTN_WIDTH), lambda b, i: (b * nq + i, 0)),
        out_shape=jax.ShapeDtypeStruct((N_LAT, ATTN_WIDTH), BF16),
        compiler_params=pltpu.CompilerParams(dimension_semantics=("parallel", "parallel"),
                                             vmem_limit_bytes=VMEM_LIMIT),
        name="attn_lat",
    )(q, kr, v, cache_k, cache_v)


def _gla_masks():
    c = GLA_CHUNK
    hrow = np.arange(GLA_HEADS * c) // c
    head_k = (hrow[:, None] == (np.arange(GLA_KW) // GLA_DK)[None, :])
    head_v = (hrow[:, None] == (np.arange(GLA_VW) // GLA_DV)[None, :])
    t = np.arange(c)[:, None]
    s = (np.arange(GLA_HEADS * c) % c)[None, :]
    pair = []
    for reverse in (False, True):
        for m in GLA_LEVELS:
            pair.append(t // (2 * m) == s // (2 * m))
        pair.append((t // GLA_DIAG == s // GLA_DIAG) & ((s >= t) if reverse else (s <= t)))
    return (jnp.asarray(head_k, BF16), jnp.asarray(head_v, BF16), jnp.asarray(np.stack(pair), F32))


def _block_diag(x, head_mask):
    return jnp.concatenate([x] * GLA_HEADS, axis=0) * head_mask


def _ref_rows(b, rows, rep):
    return jnp.concatenate([jnp.broadcast_to(b[r:r + 1, :], (rep, b.shape[1])) for r in rows], axis=0)


def _score_operands(q, k, b, reverse, head_k):
    c = GLA_CHUNK
    t_idx = lax.broadcasted_iota(jnp.int32, (c, GLA_KW), 0)
    terms = []
    for m in GLA_LEVELS:
        nblk = c // (2 * m)
        ref = _ref_rows(b, [2 * m * j + (m if reverse else m - 1) for j in range(nblk)], 2 * m)
        q_side = ((t_idx & m) == 0) if reverse else ((t_idx & m) != 0)
        terms.append((jnp.where(q_side, q * jnp.exp(b - ref), 0.0), jnp.where(q_side, 0.0, k * jnp.exp(ref - b))))
    d = GLA_DIAG
    x = b - _ref_rows(b, [d * j + d // 2 for j in range(c // d)], d)
    terms.append((q * jnp.exp(x), k * jnp.exp(-x)))
    return [(qt.astype(BF16), _block_diag(kt.astype(BF16), head_k)) for qt, kt in terms]


def _score_sum(parts, reverse, pair_ref):
    term0 = (len(GLA_LEVELS) + 1) if reverse else 0
    total = jnp.where(pair_ref[term0 + len(GLA_LEVELS)] > 0.5, parts[-1], 0.0)
    for i, part in enumerate(parts[:-1]):
        total = total + part * pair_ref[term0 + i]
    return total


def _cumsum_rows(tri, la):
    hi, lo = _split2(la)
    return _dot(tri, hi) + _dot(tri, lo)


def _gla_kernel(*refs, seq_len, has_state):
    gq_ref, gk_ref, gv_ref, la_ref, gg_ref, gn_ref, hk_ref, hv_ref, pair_ref = refs[:9]
    if has_state:
        s0_ref, o_ref, acc_ref, accb_ref, st_ref = refs[9:]
    else:
        o_ref, sfin_ref, acc_ref, accb_ref, st_ref = refs[9:]
    c = GLA_CHUNK
    nc = seq_len // c
    r_i = lax.broadcasted_iota(jnp.int32, (c, c), 0)
    c_i = lax.broadcasted_iota(jnp.int32, (c, c), 1)
    tril = jnp.where(c_i <= r_i, 1.0, 0.0).astype(BF16)
    triu = jnp.where(c_i >= r_i, 1.0, 0.0).astype(BF16)
    st_head_r = lax.broadcasted_iota(jnp.int32, (GLA_VW, GLA_KW), 0) // GLA_DV
    st_head_c = lax.broadcasted_iota(jnp.int32, (GLA_VW, GLA_KW), 1) // GLA_DK
    st_mask = st_head_r == st_head_c

    for d in range(2):
        st_ref[d] = jnp.zeros((GLA_VW, GLA_KW), F32)
        if has_state:
            for hh in range(GLA_HEADS):
                st_ref[d, hh * GLA_DV:(hh + 1) * GLA_DV, hh * GLA_DK:(hh + 1) * GLA_DK] = s0_ref[d, hh].T

    def load(n):
        rows = pl.ds(pl.multiple_of(n * c, c), c)
        return (rows, gq_ref[rows, :].astype(F32), gk_ref[rows, :].astype(F32), gv_ref[rows, :])

    nt = lambda a, b: lax.dot_general(a, b, _NT, preferred_element_type=F32)
    tn = lambda a, b: lax.dot_general(a, b, _TN, preferred_element_type=F32)

    def body(n, carry):
        rows, q, k, v = load(n)
        rows_b, q_b, k_b, v_b = load(nc - 1 - n)
        b_f = _cumsum_rows(tril, la_ref[rows, 0:GLA_KW])
        b_r = _cumsum_rows(triu, la_ref[rows, GLA_KW:2 * GLA_KW])
        b_b = _cumsum_rows(triu, la_ref[rows_b, GLA_KW:2 * GLA_KW])
        head_k = hk_ref[...]
        ops_f = _score_operands(q, k, b_f, False, head_k)
        ops_r = _score_operands(q, k, b_r, True, head_k)
        end_f, end_b = b_f[c - 1:c, :], b_b[0:1, :]
        qin_f = (q * jnp.exp(b_f)).astype(BF16)
        qin_b = (q_b * jnp.exp(b_b)).astype(BF16)
        kst_f = (k * jnp.exp(end_f - b_f)).astype(BF16)
        kst_b = (k_b * jnp.exp(end_b - b_b)).astype(BF16)
        st_f, st_b = st_ref[0], st_ref[1]
        parts_f = [nt(qt, kt) for qt, kt in ops_f]
        parts_r = [nt(qt, kt) for qt, kt in ops_r]
        o_f = nt(qin_f, st_f.astype(BF16))
        o_b = nt(qin_b, st_b.astype(BF16))
        upd_f = tn(v, kst_f)
        upd_b = tn(v_b, kst_b)
        a = _score_sum(parts_f, False, pair_ref) + _score_sum(parts_r, True, pair_ref)
        acc_ref[rows, :] = _dot(a.astype(BF16), _block_diag(v, hv_ref[...])) + o_f
        accb_ref[rows_b, :] = o_b
        st_ref[0] = st_f * jnp.exp(end_f) + jnp.where(st_mask, upd_f, 0.0)
        st_ref[1] = st_b * jnp.exp(end_b) + jnp.where(st_mask, upd_b, 0.0)
        return carry

    lax.fori_loop(0, nc, body, 0)

    o = acc_ref[...] + accb_ref[...]
    gn = gn_ref[...]
    for hh in range(GLA_HEADS):
        sl = slice(hh * GLA_DV, (hh + 1) * GLA_DV)
        o_ref[:, sl] = (_rms(o[:, sl]) * gn[:, sl] * gg_ref[:, sl].astype(F32)).astype(BF16)
    if not has_state:
        for d in range(2):
            for hh in range(GLA_HEADS):
                sfin_ref[d, hh] = st_ref[d, hh * GLA_DV:(hh + 1) * GLA_DV, hh * GLA_DK:(hh + 1) * GLA_DK].T


def _gla_scratch(seq_len):
    return [pltpu.VMEM((seq_len, GLA_VW), F32), pltpu.VMEM((seq_len, GLA_VW), F32),
            pltpu.VMEM((2, GLA_VW, GLA_KW), F32)]


def _gla_mask_specs():
    n_terms = 2 * (len(GLA_LEVELS) + 1)
    return [pl.BlockSpec((GLA_HEADS * GLA_CHUNK, GLA_KW), lambda i: (0, 0)),
            pl.BlockSpec((GLA_HEADS * GLA_CHUNK, GLA_VW), lambda i: (0, 0)),
            pl.BlockSpec((n_terms, GLA_CHUNK, GLA_HEADS * GLA_CHUNK), lambda i: (0, 0, 0))]


def _gla_ctx(gq, gk, gv, la, gg, gn, masks):
    seq = lambda w: pl.BlockSpec((SEQ, w), lambda i: (i, 0))
    return pl.pallas_call(
        partial(_gla_kernel, seq_len=SEQ, has_state=False),
        grid=(BATCH,),
        in_specs=[seq(GLA_KW), seq(GLA_KW), seq(GLA_VW), seq(2 * GLA_KW), seq(GLA_VW),
                  pl.BlockSpec((1, GLA_VW), lambda i: (0, 0))] + _gla_mask_specs(),
        out_specs=[seq(GLA_VW),
                   pl.BlockSpec((None, 2, GLA_HEADS, GLA_DK, GLA_DV), lambda i: (i, 0, 0, 0, 0))],
        out_shape=[jax.ShapeDtypeStruct((N_CTX, GLA_VW), BF16),
                   jax.ShapeDtypeStruct((BATCH, 2, GLA_HEADS, GLA_DK, GLA_DV), F32)],
        scratch_shapes=_gla_scratch(SEQ),
        compiler_params=pltpu.CompilerParams(dimension_semantics=("parallel",), vmem_limit_bytes=VMEM_LIMIT),
        name="gla_ctx",
    )(gq, gk, gv, la, gg, gn, *masks)


def _gla_lat(layer, gq, gk, gv, la, gg, gn, masks, s0):
    rows = DEC_SEQ
    blk0 = N_CTX // rows
    seq = lambda w: pl.BlockSpec((rows, w), lambda b: (blk0 + b, 0))
    return pl.pallas_call(
        partial(_gla_kernel, seq_len=DEC_SEQ, has_state=True),
        grid=(DEC_BATCH,),
        in_specs=[seq(GLA_KW), seq(GLA_KW), seq(GLA_VW), seq(2 * GLA_KW), seq(GLA_VW),
                  pl.BlockSpec((1, GLA_VW), lambda b: (0, 0))] + _gla_mask_specs() + [
                  pl.BlockSpec((None, None, 2, GLA_HEADS, GLA_DK, GLA_DV), lambda b: (b, layer, 0, 0, 0, 0))],
        out_specs=pl.BlockSpec((rows, GLA_VW), lambda b: (b, 0)),
        out_shape=jax.ShapeDtypeStruct((N_LAT, GLA_VW), BF16),
        scratch_shapes=_gla_scratch(DEC_SEQ),
        compiler_params=pltpu.CompilerParams(dimension_semantics=("parallel",), vmem_limit_bytes=VMEM_LIMIT),
        name="gla_lat",
    )(gq, gk, gv, la, gg, gn, *masks, s0)


def _route(scores, bias):
    biased = scores + bias
    v = [biased[e:e + 1, :] for e in range(N_EXPERTS)]
    s = [scores[e:e + 1, :] for e in range(N_EXPERTS)]
    gscore = []
    for g in range(N_GROUPS):
        a, b, c, d = v[4 * g:4 * g + 4]
        hi1, lo1, hi2, lo2 = jnp.maximum(a, b), jnp.minimum(a, b), jnp.maximum(c, d), jnp.minimum(c, d)
        gscore.append(jnp.maximum(hi1, hi2) + jnp.maximum(jnp.minimum(hi1, hi2), jnp.maximum(lo1, lo2)))
    one = lambda cond: jnp.where(cond, 1.0, 0.0)
    picked, chosen = [], []
    for g in range(N_GROUPS):
        best = None
        for g2 in range(N_GROUPS):
            if g2 == g:
                continue
            ok = one((gscore[g] > gscore[g2]) if g2 < g else (gscore[g] >= gscore[g2]))
            best = ok if best is None else best * ok
        for i in range(EXPERTS_PER_GROUP):
            e = 4 * g + i
            beaten = None
            for j in range(EXPERTS_PER_GROUP):
                if j == i:
                    continue
                e2 = 4 * g + j
                lost = one((v[e2] > v[e]) if j > i else (v[e2] >= v[e]))
                beaten = lost if beaten is None else beaten + lost
            chosen.append(best * one(beaten < 1.5))
            picked.append(chosen[-1] * s[e])
    den = picked[0]
    for e in range(1, N_EXPERTS):
        den = den + picked[e]
    return jnp.concatenate([p / den for p in picked] + chosen, axis=0)


def _mix_kernel(xc_ref, xl_ref, oac_ref, oal_ref, ogc_ref, ogl_ref, sa_ref, sb_ref, mod_ref, wpa_f, wpb_f, wo_f, n2_ref,
                rw_ref, rb_ref, x1_out, h2_out, gt_out, wpa_ref, wpb_ref, wo_ref):
    @pl.when(pl.program_id(0) == 0)
    def _():
        wpa_ref[...] = wpa_f[...].astype(BF16)
        wpb_ref[...] = wpb_f[...].astype(BF16)
        wo_ref[...] = wo_f[...].astype(BF16)

    is_ctx = pl.program_id(0) < N_CTX // TM
    oa = jnp.where(is_ctx, oac_ref[...], oal_ref[...])
    og = jnp.where(is_ctx, ogc_ref[...], ogl_ref[...])
    a = _dot(oa, wpa_ref[...])
    b = _dot(og, wpb_ref[...])
    merged = sa_ref[...].astype(F32) * a + sb_ref[...].astype(F32) * b
    g1 = mod_ref[:, 2 * D_MODEL:3 * D_MODEL]
    sh2 = mod_ref[:, 3 * D_MODEL:4 * D_MODEL]
    sc2 = mod_ref[:, 4 * D_MODEL:5 * D_MODEL]
    x = jnp.where(is_ctx, xc_ref[...], xl_ref[...])
    x1 = x + g1 * _dot(merged.astype(BF16), wo_ref[...])
    x1_out[...] = x1
    h2 = _rms(x1) * n2_ref[...] * (1.0 + sc2) + sh2
    _to_row_tiles(h2_out, h2)
    hh, hl = _split2(h2)
    rh, rl = rw_ref[0], rw_ref[1]
    nt = lambda r, t: lax.dot_general(r, t, _NT, preferred_element_type=F32)
    logits = nt(rl, hh) + nt(rh, hl) + nt(rh, hh)
    gt_out[...] = _route(jax.nn.sigmoid(logits), rb_ref[...])


def _mix(layer, x_pair, oa_c, oa_l, og_c, og_l, sa, sb, mod_l, wpa, wpb, wo, norm2, rw3, rb):
    nt = N_TOK // TM
    n_ctx_t = N_CTX // TM
    row = lambda i: (i, 0)
    const = lambda i: (0, 0)
    ctx_row = lambda i: (jnp.minimum(i, n_ctx_t - 1), 0)
    lat_row = lambda i: (jnp.maximum(i - n_ctx_t, 0), 0)
    return pl.pallas_call(
        _mix_kernel,
        grid=(nt,),
        in_specs=_x_specs(x_pair) + [
            pl.BlockSpec((TM, ATTN_WIDTH), ctx_row),
            pl.BlockSpec((TM, ATTN_WIDTH), lat_row),
            pl.BlockSpec((TM, GLA_VW), ctx_row),
            pl.BlockSpec((TM, GLA_VW), lat_row),
            pl.BlockSpec((TM, D_MODEL), row),
            pl.BlockSpec((TM, D_MODEL), row),
            pl.BlockSpec((None, 1, MOD_W), lambda i: (_mod_row(i, TM), 0, 0)),
            pl.BlockSpec((None, ATTN_WIDTH, D_MODEL), lambda i: (layer, 0, 0), pipeline_mode=pl.Buffered(1)),
            pl.BlockSpec((None, GLA_VW, D_MODEL), lambda i: (layer, 0, 0), pipeline_mode=pl.Buffered(1)),
            pl.BlockSpec((None, D_MODEL, D_MODEL), lambda i: (layer, 0, 0), pipeline_mode=pl.Buffered(1)),
            pl.BlockSpec((1, D_MODEL), const),
            pl.BlockSpec((2, N_EXPERTS, D_MODEL), lambda i: (0, 0, 0)),
            pl.BlockSpec((N_EXPERTS, 1), const),
        ],
        out_specs=[pl.BlockSpec((TM, D_MODEL), row), pl.BlockSpec((TM * ROW_TILE, LANES), row),
                   pl.BlockSpec((2 * N_EXPERTS, TM), lambda i: (0, i))],
        out_shape=[jax.ShapeDtypeStruct((N_TOK, D_MODEL), F32),
                   jax.ShapeDtypeStruct((N_TOK * ROW_TILE, LANES), F32),
                   jax.ShapeDtypeStruct((2 * N_EXPERTS, N_TOK), F32)],
        scratch_shapes=[pltpu.VMEM((ATTN_WIDTH, D_MODEL), BF16), pltpu.VMEM((GLA_VW, D_MODEL), BF16),
                        pltpu.VMEM((D_MODEL, D_MODEL), BF16)],
        compiler_params=pltpu.CompilerParams(dimension_semantics=("arbitrary",), vmem_limit_bytes=VMEM_LIMIT),
        name="mix",
    )(*x_pair, oa_c, oa_l, og_c, og_l, sa, sb, mod_l, wpa, wpb, wo, norm2, rw3, rb)


def _route_plan(gsel):
    gates = gsel[:N_EXPERTS]
    sel = gsel[N_EXPERTS:] > 0.5
    seli = sel.astype(jnp.int32)
    incl = jnp.cumsum(seli, axis=1)
    count = incl[:, -1]
    ntile = (count + TS - 1) // TS
    tile_end = jnp.cumsum(ntile)
    tile_start = tile_end - ntile
    total = tile_end[-1]
    slot = tile_start[:, None] * TS + incl - seli
    pos_a = jnp.min(jnp.where(sel, slot, N_SLOT), axis=0)
    pos_b = jnp.max(jnp.where(sel, slot, -1), axis=0)
    w_a = jnp.sum(jnp.where(sel & (slot == pos_a[None, :]), gates, 0.0), axis=0)
    w_b = jnp.sum(jnp.where(sel & (slot == pos_b[None, :]), gates, 0.0), axis=0)
    nt = N_TOK // TM
    pos = jnp.concatenate([pos_a.reshape(nt, 1, TM), pos_b.reshape(nt, 1, TM)], axis=2).astype(jnp.int32)
    plan = jnp.concatenate([tile_start, ntile, total[None]]).astype(jnp.int32)
    return pos, jnp.stack([w_a, w_b], axis=1), plan


def _to_row_tiles(ref, x):
    for k in range(ROW_TILE):
        ref[pl.ds(k, x.shape[0], stride=ROW_TILE), :] = x[:, k * LANES:(k + 1) * LANES]


def _from_row_tiles(ref, rows):
    return jnp.concatenate([ref[pl.ds(k, rows, stride=ROW_TILE), :] for k in range(ROW_TILE)], axis=1)


def _tile_rows(row, n=1):
    return pl.ds(pl.multiple_of(row * ROW_TILE, ROW_TILE), n * ROW_TILE)


def _row_copy(src, src_row, dst, dst_row, sem):
    return pltpu.make_async_copy(src.at[_tile_rows(src_row), :], dst.at[_tile_rows(dst_row), :], sem)


def _dispatch_kernel(plan_ref, pos_ref, h_ref, xs_ref, zero_ref, sem_z, sem):
    i = pl.program_id(0)
    total = plan_ref[PLAN_TOTAL]

    def zero_copy(tile):
        return pltpu.make_async_copy(zero_ref, xs_ref.at[_tile_rows(tile * TS, TS), :], sem_z)

    def last_tile(e):
        return plan_ref[e] + plan_ref[N_EXPERTS + e] - 1

    @pl.when(i == 0)
    def _():
        zero_ref[...] = jnp.zeros(zero_ref.shape, F32)

        def each(fn):
            def expert(e, c):
                @pl.when(plan_ref[N_EXPERTS + e] > 0)
                def _():
                    fn(zero_copy(last_tile(e)))
                return c

            def unused(t, c):
                @pl.when(t >= total)
                def _():
                    fn(zero_copy(t))
                return c

            lax.fori_loop(0, N_EXPERTS, expert, 0)
            lax.fori_loop(0, N_TILE, unused, 0)

        each(lambda cp: cp.start())
        each(lambda cp: cp.wait())

    def start(r, c):
        _row_copy(h_ref, r, xs_ref, pos_ref[0, r], sem).start()
        _row_copy(h_ref, r, xs_ref, pos_ref[0, TM + r], sem).start()
        return c

    def wait(r, c):
        _row_copy(h_ref, 0, xs_ref, 0, sem).wait()
        return c

    lax.fori_loop(0, TM, start, 0, unroll=8)
    lax.fori_loop(0, 2 * TM, wait, 0, unroll=8)


def _dispatch(plan, pos, h2):
    return pl.pallas_call(
        _dispatch_kernel,
        grid_spec=pltpu.PrefetchScalarGridSpec(
            num_scalar_prefetch=1,
            grid=(N_TOK // TM,),
            in_specs=[
                pl.BlockSpec((None, 1, 2 * TM), lambda i, plan: (i, 0, 0), memory_space=pltpu.SMEM),
                pl.BlockSpec((TM * ROW_TILE, LANES), lambda i, plan: (i, 0)),
            ],
            out_specs=pl.BlockSpec(memory_space=pl.ANY),
            scratch_shapes=[pltpu.VMEM((TS * ROW_TILE, LANES), F32), pltpu.SemaphoreType.DMA,
                            pltpu.SemaphoreType.DMA],
        ),
        out_shape=jax.ShapeDtypeStruct((N_SLOT * ROW_TILE, LANES), F32),
        compiler_params=pltpu.CompilerParams(dimension_semantics=("arbitrary",), vmem_limit_bytes=VMEM_LIMIT),
        name="dispatch",
    )(plan, pos, h2)


def _moe_kernel(plan_ref, xs_ref, wg_ref, wu_ref, wd_ref, ys_ref, x_buf, y_buf, zero_ref, sem_in, sem_out, sem_z,
                wg_s, wu_s, wd_s):
    e = pl.program_id(0)
    total = plan_ref[PLAN_TOTAL]
    first = plan_ref[e]

    def in_copy(g):
        return pltpu.make_async_copy(xs_ref.at[_tile_rows(g * TS, TS), :], x_buf.at[g % 2], sem_in.at[g % 2])

    def out_copy(g):
        return pltpu.make_async_copy(y_buf.at[g % 2], ys_ref.at[_tile_rows(g * TS, TS), :], sem_out.at[g % 2])

    def zero_copy(g):
        return pltpu.make_async_copy(zero_ref, ys_ref.at[_tile_rows(g * TS, TS), :], sem_z)

    @pl.when(e == 0)
    def _():
        in_copy(0).start()

    wg_s[...] = wg_ref[...].astype(BF16)
    wu_s[...] = wu_ref[...].astype(BF16)
    wd_s[...] = wd_ref[...].astype(BF16)

    def tile(t, c):
        g = first + t
        slot = g % 2

        @pl.when(g + 1 < total)
        def _():
            in_copy(g + 1).start()

        in_copy(g).wait()

        @pl.when(g >= 2)
        def _():
            out_copy(g - 2).wait()

        x = _from_row_tiles(x_buf.at[slot], TS).astype(BF16)
        up = _dot(x, wu_s[...])
        gt = _dot(x, wg_s[...])
        act = (gt * jax.nn.sigmoid(gt) * up).astype(BF16)
        _to_row_tiles(y_buf.at[slot], _dot(act, wd_s[...]))
        out_copy(g).start()
        return c

    lax.fori_loop(0, plan_ref[N_EXPERTS + e], tile, 0)

    @pl.when(e == N_EXPERTS - 1)
    def _():
        out_copy(total - 1).wait()

        @pl.when(total >= 2)
        def _():
            out_copy(total - 2).wait()

        zero_ref[...] = jnp.zeros(zero_ref.shape, F32)

        def each(fn):
            def unused(g, c):
                @pl.when(g >= total)
                def _():
                    fn(zero_copy(g))
                return c
            lax.fori_loop(0, N_TILE, unused, 0)

        each(lambda cp: cp.start())
        each(lambda cp: cp.wait())


def _moe(layer, plan, xs, w_gate, w_up, w_down):
    w_spec = lambda a, b: pl.BlockSpec((None, None, a, b), lambda e, plan: (layer, e, 0, 0))
    tile_buf = lambda n: pltpu.VMEM((n, TS * ROW_TILE, LANES), F32)
    return pl.pallas_call(
        _moe_kernel,
        grid_spec=pltpu.PrefetchScalarGridSpec(
            num_scalar_prefetch=1,
            grid=(N_EXPERTS,),
            in_specs=[
                pl.BlockSpec(memory_space=pl.ANY),
                w_spec(D_MODEL, D_EXPERT), w_spec(D_MODEL, D_EXPERT), w_spec(D_EXPERT, D_MODEL),
            ],
            out_specs=pl.BlockSpec(memory_space=pl.ANY),
            scratch_shapes=[tile_buf(2), tile_buf(2), pltpu.VMEM((TS * ROW_TILE, LANES), F32),
                            pltpu.SemaphoreType.DMA((2,)), pltpu.SemaphoreType.DMA((2,)), pltpu.SemaphoreType.DMA,
                            pltpu.VMEM((D_MODEL, D_EXPERT), BF16), pltpu.VMEM((D_MODEL, D_EXPERT), BF16),
                            pltpu.VMEM((D_EXPERT, D_MODEL), BF16)],
        ),
        out_shape=jax.ShapeDtypeStruct((N_SLOT * ROW_TILE, LANES), F32),
        compiler_params=pltpu.CompilerParams(dimension_semantics=("arbitrary",), vmem_limit_bytes=VMEM_LIMIT),
        name="moe",
    )(plan, xs, w_gate, w_up, w_down)


def _combine_kernel(pos_ref, nxt_ref, w_ref, x1_ref, mod_ref, ys_ref, out_ref, ya_ref, yb_ref, sem, *, n):
    i = pl.program_id(0)

    def gather(p_ref, slot):
        def start(r, c):
            _row_copy(ys_ref, p_ref[0, r], ya_ref.at[slot], r, sem.at[slot]).start()
            _row_copy(ys_ref, p_ref[0, TM + r], yb_ref.at[slot], r, sem.at[slot]).start()
            return c
        lax.fori_loop(0, TM, start, 0, unroll=8)

    @pl.when(i == 0)
    def _():
        gather(pos_ref, 0)

    @pl.when(i + 1 < n)
    def _():
        gather(nxt_ref, (i + 1) % 2)

    slot = i % 2

    def wait(r, c):
        _row_copy(ys_ref, 0, ya_ref.at[slot], 0, sem.at[slot]).wait()
        return c

    lax.fori_loop(0, 2 * TM, wait, 0, unroll=8)
    g2 = mod_ref[:, 5 * D_MODEL:6 * D_MODEL]
    w = w_ref[...]
    ya = _from_row_tiles(ya_ref.at[slot], TM)
    yb = _from_row_tiles(yb_ref.at[slot], TM)
    out_ref[...] = x1_ref[...] + g2 * (w[:, 0:1] * ya + w[:, 1:2] * yb)


def _combine(pos, w_ab, x1, mod_l, ys, tile0, nt):
    pos_spec = lambda fn: pl.BlockSpec((None, 1, 2 * TM), fn, memory_space=pltpu.SMEM)
    return pl.pallas_call(
        partial(_combine_kernel, n=nt),
        grid=(nt,),
        in_specs=[
            pos_spec(lambda i: (tile0 + i, 0, 0)),
            pos_spec(lambda i: (tile0 + jnp.minimum(i + 1, nt - 1), 0, 0)),
            pl.BlockSpec((TM, 2), lambda i: (tile0 + i, 0)),
            pl.BlockSpec((TM, D_MODEL), lambda i: (tile0 + i, 0)),
            pl.BlockSpec((None, 1, MOD_W), lambda i: (_mod_row(tile0 + i, TM), 0, 0)),
            pl.BlockSpec(memory_space=pl.ANY),
        ],
        out_specs=pl.BlockSpec((TM, D_MODEL), lambda i: (i, 0)),
        out_shape=jax.ShapeDtypeStruct((nt * TM, D_MODEL), F32),
        scratch_shapes=[pltpu.VMEM((2, TM * ROW_TILE, LANES), F32), pltpu.VMEM((2, TM * ROW_TILE, LANES), F32),
                        pltpu.SemaphoreType.DMA((2,))],
        compiler_params=pltpu.CompilerParams(dimension_semantics=("arbitrary",), vmem_limit_bytes=VMEM_LIMIT),
        name="combine",
    )(pos, pos, w_ab, x1, mod_l, ys)


def _rope_tables():
    pos = np.arange(DEC_SEQ)
    rows = (pos // GRID_W).astype(np.float64)
    cols = (pos % GRID_W).astype(np.float64)
    half = HEAD_DIM // 2
    inv_freq = ROPE_THETA ** (-np.arange(0, half, 2, dtype=np.float64) / half)
    ang_r = rows[:, None] * inv_freq[None, :]
    ang_c = cols[:, None] * inv_freq[None, :]
    cos_h = np.concatenate([np.cos(ang_r), np.cos(ang_r), np.cos(ang_c), np.cos(ang_c)], axis=-1)
    sin_h = np.concatenate([-np.sin(ang_r), np.sin(ang_r), -np.sin(ang_c), np.sin(ang_c)], axis=-1)
    cos = np.concatenate([np.ones((TM, HEAD_DIM)), cos_h], axis=0)
    sin = np.concatenate([np.zeros((TM, HEAD_DIM)), sin_h], axis=0)
    return (jnp.asarray(np.tile(cos, (1, N_HEADS)), F32), jnp.asarray(np.tile(sin, (1, N_HEADS)), F32))


def _pack_w2(w2):
    z = jnp.zeros((GLA_RANK, GLA_KW), w2.dtype)
    top = jnp.concatenate([w2[0], z], axis=1)
    bot = jnp.concatenate([z, w2[1]], axis=1)
    pad = jnp.zeros((LR_PAD - 2 * GLA_RANK, 2 * GLA_KW), w2.dtype)
    return jnp.concatenate([top, bot, pad], axis=0).astype(BF16)


def kernel(x_prompt, x_sample, cache_k, cache_v, state_gla, c, c_ctx, norm1, norm2, w_ada, b_ada, w_in,
           q_norm, k_norm, gla_w2, gla_b, gla_norm, w_pa, w_pb, w_out, router_w, router_bias,
           w_gate, w_up, w_down):
    x_pair = (x_prompt.reshape(N_CTX, D_MODEL), x_sample.reshape(N_LAT, D_MODEL))
    cond =jnp.concatenate([c_ctx[None, :], c, jnp.zeros((N_COND - 1 - DEC_BATCH, D_MODEL), F32)], axis=0)
    mod = _ada(cond, w_ada, b_ada).reshape(DEPTH, N_COND, 1, MOD_W)
    cos, sin = _rope_tables()
    gla_masks = _gla_masks()
    w_in_t = jnp.swapaxes(w_in, 1, 2)
    lane_head = np.arange(ATTN_WIDTH) // HEAD_DIM
    bd = jnp.asarray(lane_head[:, None] == lane_head[None, :], BF16)
    rw3 = jnp.stack(_split2(router_w.T), axis=0)
    rb = router_bias.reshape(N_EXPERTS, 1)
    ck = cache_k.reshape(DEC_BATCH, DEPTH, PAST_LEN, KV_WIDTH)
    cv = cache_v.reshape(DEC_BATCH, DEPTH, PAST_LEN, KV_WIDTH)

    keys, vals, states = [], [], []
    for l in range(DEPTH):
        q, kf, kr, v, gq, gk, gv, gg, la, sa, sb = _inproj(
            l, x_pair, mod[l], norm1[l][None, :], w_in_t, jnp.tile(q_norm[l], N_HEADS)[None, :],
            jnp.tile(k_norm[l], KV_HEADS)[None, :], cos, sin, bd, _pack_w2(gla_w2[l]),
            gla_b[l].reshape(1, 2 * GLA_KW))
        oa_c = _attn_ctx(q, kr, v)
        oa_l = _attn_lat(l, q, kr, v, ck, cv)
        gn = jnp.tile(gla_norm[l], GLA_HEADS)[None, :]
        og_c, s_fin = _gla_ctx(gq, gk, gv, la, gg, gn, gla_masks)
        og_l = _gla_lat(l, gq, gk, gv, la, gg, gn, gla_masks, state_gla)
        x1, h2, gsel = _mix(l, x_pair, oa_c, oa_l, og_c, og_l, sa, sb, mod[l], w_pa, w_pb, w_out,
                            norm2[l][None, :], rw3, rb)
        pos, w_ab, plan = _route_plan(gsel)
        xs = _dispatch(plan, pos, h2)
        ys = _moe(l, plan, xs, w_gate, w_up, w_down)
        n_ctx_t = N_CTX // TM
        if l + 1 < DEPTH:
            x = _combine(pos, w_ab, x1, mod[l], ys, 0, N_TOK // TM)
            x_pair = (x, x)
        else:
            x_pair = (_combine(pos, w_ab, x1, mod[l], ys, 0, n_ctx_t),
                      _combine(pos, w_ab, x1, mod[l], ys, n_ctx_t, N_LAT // TM))
        keys.append(kf[:N_CTX].reshape(BATCH, SEQ, KV_HEADS, HEAD_DIM))
        vals.append(v[:N_CTX].reshape(BATCH, SEQ, KV_HEADS, HEAD_DIM))
        states.append(s_fin)
    y_prompt = x_pair[0].reshape(BATCH, SEQ, D_MODEL)
    y_sample = x_pair[1].reshape(DEC_BATCH, DEC_SEQ, D_MODEL)
    return (y_prompt, y_sample, jnp.stack(keys, axis=1), jnp.stack(vals, axis=1), jnp.stack(states, axis=1))
```

```python
from functools import partial

import jax
import jax.numpy as jnp
import numpy as np
from jax import lax
from jax.experimental import pallas as pl
from jax.experimental.pallas import tpu as pltpu

F32 = jnp.float32
BF16 = jnp.bfloat16

D_MODEL = 1024
BATCH = 16
SEQ = 256
DEPTH = 2
DEC_BATCH = 4
DEC_SEQ = 1024
PAST_LEN = 512
GRID_W = 64
N_HEADS = 8
KV_HEADS = 2
GROUP = N_HEADS // KV_HEADS
HEAD_DIM = 64
ATTN_WIDTH = N_HEADS * HEAD_DIM
KV_WIDTH = KV_HEADS * HEAD_DIM
ROPE_THETA = 10000.0
GLA_HEADS = 4
GLA_DK = 64
GLA_DV = 128
GLA_KW = GLA_HEADS * GLA_DK
GLA_VW = GLA_HEADS * GLA_DV
GLA_RANK = 16
GLA_GATE_NORM = 16.0
N_EXPERTS = 16
N_GROUPS = 4
EXPERTS_PER_GROUP = N_EXPERTS // N_GROUPS
D_EXPERT = 512
NORM_EPS = 1e-6

N_CTX = BATCH * SEQ
N_LAT = DEC_BATCH * DEC_SEQ
N_TOK = N_CTX + N_LAT
N_COND = 8
MOD_W = 6 * D_MODEL

TM = 512
TS = 512
N_TILE = 2 * N_TOK // TS + N_EXPERTS
N_SLOT = N_TILE * TS
PLAN_TOTAL = 2 * N_EXPERTS
ROW_TILE = 8
LANES = D_MODEL // ROW_TILE
GLA_CHUNK = 128
GLA_LEVELS = (64, 32, 16)
GLA_DIAG = 16
LR_PAD = 128
W_ROWS = 128
SEG_MIX = ATTN_WIDTH + 2 * KV_WIDTH + 2 * GLA_KW + 2 * GLA_VW
OFF_GA = SEG_MIX
OFF_GB = OFF_GA + D_MODEL
OFF_LR = OFF_GB + D_MODEL
IN_PACKED = OFF_LR + LR_PAD
IN_WIDTH = SEG_MIX + 2 * GLA_RANK + 2 * D_MODEL
VMEM_LIMIT = 56 * 1024 * 1024

_NT = (((1,), (1,)), ((), ()))
_TN = (((0,), (0,)), ((), ()))


def _mod_row(tile, tm):
    start = tile * tm
    return jnp.where(start < N_CTX, 0, 1 + (start - N_CTX) // DEC_SEQ)


def _split3(x):
    hi = x.astype(BF16)
    r1 = x - hi.astype(F32)
    mid = r1.astype(BF16)
    lo = (r1 - mid.astype(F32)).astype(BF16)
    return hi, mid, lo


def _split2(x):
    hi = x.astype(BF16)
    lo = (x - hi.astype(F32)).astype(BF16)
    return hi, lo


def _dot(a, b):
    return jnp.dot(a, b, preferred_element_type=F32)


def _rms(x):
    return x * lax.rsqrt(jnp.mean(x * x, axis=-1, keepdims=True) + NORM_EPS)


def _ada_kernel(cond_ref, w_ref, b_ref, out_ref):
    c = cond_ref[...]
    s = c * jax.nn.sigmoid(c)
    out_ref[...] = _dot(s.astype(BF16), w_ref[...].astype(BF16)) + b_ref[...]


def _ada(cond, w_ada, b_ada):
    nb = MOD_W // D_MODEL
    return pl.pallas_call(
        _ada_kernel,
        grid=(DEPTH, nb),
        in_specs=[
            pl.BlockSpec((N_COND, D_MODEL), lambda l, j: (0, 0)),
            pl.BlockSpec((None, D_MODEL, D_MODEL), lambda l, j: (l, 0, j)),
            pl.BlockSpec((None, 1, D_MODEL), lambda l, j: (l, 0, j)),
        ],
        out_specs=pl.BlockSpec((None, N_COND, D_MODEL), lambda l, j: (l, 0, j)),
        out_shape=jax.ShapeDtypeStruct((DEPTH, N_COND, MOD_W), F32),
        compiler_params=pltpu.CompilerParams(vmem_limit_bytes=VMEM_LIMIT),
        name="ada",
    )(cond, w_ada, b_ada.reshape(DEPTH, 1, MOD_W))


def _head_rmsnorm(x, bd):
    ssq = _dot((x * x).astype(BF16), bd)
    return x * lax.rsqrt(ssq * (1.0 / HEAD_DIM) + NORM_EPS)


def _rope(x, cos, sin):
    n = x.shape[-1]
    lane = lax.broadcasted_iota(jnp.int32, x.shape, 1)
    first = (lane & (HEAD_DIM // 4)) == 0
    partner = jnp.where(first, pltpu.roll(x, n - HEAD_DIM // 4, 1), pltpu.roll(x, HEAD_DIM // 4, 1))
    return x * cos + partner * sin


def _pack_w_in(w_ref, wp_ref):
    lr0 = SEG_MIX
    ga0 = lr0 + 2 * GLA_RANK

    def cast(dst, src, n):
        def body(i, carry):
            rows_out = pl.ds(pl.multiple_of(dst + i * W_ROWS, W_ROWS), W_ROWS)
            rows_in = pl.ds(pl.multiple_of(src + i * W_ROWS, 2 * GLA_RANK), W_ROWS)
            wp_ref[rows_out, :] = w_ref[rows_in, :].astype(BF16)
            return carry
        lax.fori_loop(0, n // W_ROWS, body, 0)

    cast(0, 0, SEG_MIX)
    cast(OFF_GA, ga0, 2 * D_MODEL)
    wp_ref[OFF_LR:IN_PACKED, :] = jnp.concatenate(
        [w_ref[lr0:ga0, :], jnp.zeros((LR_PAD - 2 * GLA_RANK, D_MODEL), F32)], axis=0).astype(BF16)


def _inproj_kernel(xc_ref, xl_ref, mod_ref, n1_ref, wf_ref, qn_ref, kn_ref, cos_ref, sin_ref, bd_ref, w2_ref, gb_ref,
                   q_out, kf_out, kr_out, v_out, gq_out, gk_out, gv_out, gg_out, la_out, sa_out, sb_out,
                   w_ref):
    @pl.when(pl.program_id(0) == 0)
    def _():
        _pack_w_in(wf_ref, w_ref)

    x = jnp.where(pl.program_id(0) < N_CTX // TM, xc_ref[...], xl_ref[...])
    sh1 = mod_ref[:, 0:D_MODEL]
    sc1 = mod_ref[:, D_MODEL:2 * D_MODEL]
    h = (_rms(x) * n1_ref[...] * (1.0 + sc1) + sh1).astype(BF16)

    def proj(lo, width):
        return lax.dot_general(h, w_ref[lo:lo + width, :], _NT, preferred_element_type=F32)

    cos = cos_ref[...]
    sin = sin_ref[...]
    bd = bd_ref[...]
    off = 0
    q = _head_rmsnorm(proj(off, ATTN_WIDTH), bd) * qn_ref[...]
    q_out[...] = (_rope(q, cos, sin) * HEAD_DIM ** -0.5).astype(BF16)
    off += ATTN_WIDTH
    k = _head_rmsnorm(proj(off, KV_WIDTH), bd[:KV_WIDTH, :KV_WIDTH]) * kn_ref[...]
    kf_out[...] = k
    kr_out[...] = _rope(k, cos[:, :KV_WIDTH], sin[:, :KV_WIDTH]).astype(BF16)
    off += KV_WIDTH
    v_out[...] = proj(off, KV_WIDTH)
    off += KV_WIDTH
    gq_out[...] = (proj(off, GLA_KW) * GLA_DK ** -0.5).astype(BF16)
    off += GLA_KW
    gk_out[...] = proj(off, GLA_KW).astype(BF16)
    off += GLA_KW
    gv_out[...] = proj(off, GLA_VW).astype(BF16)
    off += GLA_VW
    gg = proj(off, GLA_VW)
    gg_out[...] = (gg * jax.nn.sigmoid(gg)).astype(BF16)
    sa_out[...] = jax.nn.sigmoid(proj(OFF_GA, D_MODEL)).astype(BF16)
    sb_out[...] = jax.nn.sigmoid(proj(OFF_GB, D_MODEL)).astype(BF16)
    lr = proj(OFF_LR, LR_PAD)
    z = _dot(lr.astype(BF16), w2_ref[...]) + gb_ref[...]
    log_sig = jnp.minimum(z, 0.0) - jnp.log1p(jnp.exp(-jnp.abs(z)))
    la_out[...] = log_sig * (1.0 / GLA_GATE_NORM)


def _layer_row(layer, width):
    return pl.BlockSpec((None, 1, width), lambda *_: (layer, 0, 0))


def _mod_spec(layer, tile0=0):
    return pl.BlockSpec((None, None, 1, MOD_W), lambda i, *_: (layer, _mod_row(tile0 + i, TM), 0, 0))


def _x_specs(x_pair):
    n_ctx_t = N_CTX // TM
    lat0 = n_ctx_t if x_pair[0] is x_pair[1] else 0
    return [pl.BlockSpec((TM, D_MODEL), lambda i: (jnp.minimum(i, n_ctx_t - 1), 0)),
            pl.BlockSpec((TM, D_MODEL), lambda i: (lat0 + jnp.maximum(i - n_ctx_t, 0), 0))]


def _inproj(layer, x_pair, mod, norm1, w_in, qn, kn, cos, sin, bd, w2p, gb):
    nt = N_TOK // TM
    n_ctx_t = N_CTX // TM
    t_per_seq = DEC_SEQ // TM
    row = lambda i: (i, 0)
    const = lambda i: (0, 0)
    rope_blk = lambda i: (jnp.where(i < n_ctx_t, 0, 1 + (i - n_ctx_t) % t_per_seq), 0)

    def out(width, dtype):
        return pl.BlockSpec((TM, width), row), jax.ShapeDtypeStruct((N_TOK, width), dtype)

    outs = [out(ATTN_WIDTH, BF16), out(KV_WIDTH, F32), out(KV_WIDTH, BF16), out(KV_WIDTH, F32),
            out(GLA_KW, BF16), out(GLA_KW, BF16), out(GLA_VW, BF16), out(GLA_VW, BF16),
            out(2 * GLA_KW, F32), out(D_MODEL, BF16), out(D_MODEL, BF16)]
    return pl.pallas_call(
        _inproj_kernel,
        grid=(nt,),
        in_specs=_x_specs(x_pair) + [
            _mod_spec(layer),
            _layer_row(layer, D_MODEL),
            pl.BlockSpec((None, IN_WIDTH, D_MODEL), lambda i: (layer, 0, 0), pipeline_mode=pl.Buffered(1)),
            _layer_row(layer, ATTN_WIDTH),
            _layer_row(layer, KV_WIDTH),
            pl.BlockSpec((TM, ATTN_WIDTH), rope_blk),
            pl.BlockSpec((TM, ATTN_WIDTH), rope_blk),
            pl.BlockSpec((ATTN_WIDTH, ATTN_WIDTH), const),
            pl.BlockSpec((None, LR_PAD, 2 * GLA_KW), lambda i: (layer, 0, 0)),
            _layer_row(layer, 2 * GLA_KW),
        ],
        out_specs=[o[0] for o in outs],
        out_shape=[o[1] for o in outs],
        scratch_shapes=[pltpu.VMEM((IN_PACKED, D_MODEL), BF16)],
        compiler_params=pltpu.CompilerParams(dimension_semantics=("arbitrary",), vmem_limit_bytes=VMEM_LIMIT),
        name="inproj",
    )(*x_pair, mod, norm1, w_in, qn, kn, cos, sin, bd, w2p, gb)


def _attn_kernel(*refs, has_cache):
    if has_cache:
        q_ref, k_ref, v_ref, ck_ref, cv_ref, o_ref = refs
    else:
        q_ref, k_ref, v_ref, o_ref = refs
    tq = q_ref.shape[0]
    for g in range(KV_HEADS):
        ks = slice(g * HEAD_DIM, (g + 1) * HEAD_DIM)
        qs = jnp.concatenate(
            [q_ref[:, (g * GROUP + j) * HEAD_DIM:(g * GROUP + j + 1) * HEAD_DIM] for j in range(GROUP)], axis=0)
        k = k_ref[:, ks]
        v = v_ref[:, ks].astype(BF16)
        s = lax.dot_general(qs, k, _NT, preferred_element_type=F32)
        m = jnp.max(s, axis=-1, keepdims=True)
        if has_cache:
            ck = ck_ref[:, ks].astype(BF16)
            cv = cv_ref[:, ks].astype(BF16)
            s2 = lax.dot_general(qs, ck, _NT, preferred_element_type=F32)
            m = jnp.maximum(m, jnp.max(s2, axis=-1, keepdims=True))
        p = jnp.exp(s - m)
        den = jnp.sum(p, axis=-1, keepdims=True)
        o = _dot(p.astype(BF16), v)
        if has_cache:
            p2 = jnp.exp(s2 - m)
            den = den + jnp.sum(p2, axis=-1, keepdims=True)
            o = o + _dot(p2.astype(BF16), cv)
        o = o / den
        for j in range(GROUP):
            hd = g * GROUP + j
            o_ref[:, hd * HEAD_DIM:(hd + 1) * HEAD_DIM] = o[j * tq:(j + 1) * tq, :].astype(BF16)


def _attn_ctx(q, kr, v):
    return pl.pallas_call(
        partial(_attn_kernel, has_cache=False),
        grid=(BATCH,),
        in_specs=[
            pl.BlockSpec((SEQ, ATTN_WIDTH), lambda i: (i, 0)),
            pl.BlockSpec((SEQ, KV_WIDTH), lambda i: (i, 0)),
            pl.BlockSpec((SEQ, KV_WIDTH), lambda i: (i, 0)),
        ],
        out_specs=pl.BlockSpec((SEQ, ATTN_WIDTH), lambda i: (i, 0)),
        out_shape=jax.ShapeDtypeStruct((N_CTX, ATTN_WIDTH), BF16),
        compiler_params=pltpu.CompilerParams(dimension_semantics=("parallel",), vmem_limit_bytes=VMEM_LIMIT),
        name="attn_ctx",
    )(q, kr, v)


def _attn_lat(layer, q, kr, v, cache_k, cache_v):
    tq = 256
    nq = DEC_SEQ // tq
    q0 = N_CTX // tq
    s0 = N_CTX // DEC_SEQ
    return pl.pallas_call(
        partial(_attn_kernel, has_cache=True),
        grid=(DEC_BATCH, nq),
        in_specs=[
            pl.BlockSpec((tq, ATTN_WIDTH), lambda b, i: (q0 + b * nq + i, 0)),
            pl.BlockSpec((DEC_SEQ, KV_WIDTH), lambda b, i: (s0 + b, 0)),
            pl.BlockSpec((DEC_SEQ, KV_WIDTH), lambda b, i: (s0 + b, 0)),
            pl.BlockSpec((None, None, PAST_LEN, KV_WIDTH), lambda b, i: (b, layer, 0, 0)),
            pl.BlockSpec((None, None, PAST_LEN, KV_WIDTH), lambda b, i: (b, layer, 0, 0)),
        ],
        out_specs=pl.BlockSpec((tq, ATTN_WIDTH), lambda b, i: (b * nq + i, 0)),
        out_shape=jax.ShapeDtypeStruct((N_LAT, ATTN_WIDTH), BF16),
        compiler_params=pltpu.CompilerParams(dimension_semantics=("parallel", "parallel"),
                                             vmem_limit_bytes=VMEM_LIMIT),
        name="attn_lat",
    )(q, kr, v, cache_k, cache_v)


def _gla_masks():
    c = GLA_CHUNK
    hrow = np.arange(GLA_HEADS * c) // c
    head_k = (hrow[:, None] == (np.arange(GLA_KW) // GLA_DK)[None, :])
    head_v = (hrow[:, None] == (np.arange(GLA_VW) // GLA_DV)[None, :])
    t = np.arange(c)[:, None]
    s = (np.arange(GLA_HEADS * c) % c)[None, :]
    pair = []
    for reverse in (False, True):
        for m in GLA_LEVELS:
            pair.append(t // (2 * m) == s // (2 * m))
        pair.append((t // GLA_DIAG == s // GLA_DIAG) & ((s >= t) if reverse else (s <= t)))
    return (jnp.asarray(head_k, BF16), jnp.asarray(head_v, BF16), jnp.asarray(np.stack(pair), F32))


def _block_diag(x, head_mask):
    return jnp.concatenate([x] * GLA_HEADS, axis=0) * head_mask


def _ref_rows(b, rows, rep):
    return jnp.concatenate([jnp.broadcast_to(b[r:r + 1, :], (rep, b.shape[1])) for r in rows], axis=0)


def _score_operands(q, k, b, reverse, head_k):
    c = GLA_CHUNK
    t_idx = lax.broadcasted_iota(jnp.int32, (c, GLA_KW), 0)
    terms = []
    for m in GLA_LEVELS:
        nblk = c // (2 * m)
        ref = _ref_rows(b, [2 * m * j + (m if reverse else m - 1) for j in range(nblk)], 2 * m)
        q_side = ((t_idx & m) == 0) if reverse else ((t_idx & m) != 0)
        terms.append((jnp.where(q_side, q * jnp.exp(b - ref), 0.0), jnp.where(q_side, 0.0, k * jnp.exp(ref - b))))
    d = GLA_DIAG
    x = b - _ref_rows(b, [d * j + d // 2 for j in range(c // d)], d)
    terms.append((q * jnp.exp(x), k * jnp.exp(-x)))
    return [(qt.astype(BF16), _block_diag(kt.astype(BF16), head_k)) for qt, kt in terms]


def _score_sum(parts, reverse, pair_ref):
    term0 = (len(GLA_LEVELS) + 1) if reverse else 0
    total = jnp.where(pair_ref[term0 + len(GLA_LEVELS)] > 0.5, parts[-1], 0.0)
    for i, part in enumerate(parts[:-1]):
        total = total + part * pair_ref[term0 + i]
    return total


def _cumsum_rows(tri, la):
    hi, lo = _split2(la)
    return _dot(tri, hi) + _dot(tri, lo)


def _gla_kernel(*refs, seq_len, has_state):
    gq_ref, gk_ref, gv_ref, la_ref, gg_ref, gn_ref, hk_ref, hv_ref, pair_ref = refs[:9]
    if has_state:
        s0_ref, o_ref, acc_ref, accb_ref, st_ref = refs[9:]
    else:
        o_ref, sfin_ref, acc_ref, accb_ref, st_ref = refs[9:]
    c = GLA_CHUNK
    nc = seq_len // c
    r_i = lax.broadcasted_iota(jnp.int32, (c, c), 0)
    c_i = lax.broadcasted_iota(jnp.int32, (c, c), 1)
    tril = jnp.where(c_i <= r_i, 1.0, 0.0).astype(BF16)
    triu = jnp.where(c_i >= r_i, 1.0, 0.0).astype(BF16)
    st_head_r = lax.broadcasted_iota(jnp.int32, (GLA_VW, GLA_KW), 0) // GLA_DV
    st_head_c = lax.broadcasted_iota(jnp.int32, (GLA_VW, GLA_KW), 1) // GLA_DK
    st_mask = st_head_r == st_head_c

    for d in range(2):
        st_ref[d] = jnp.zeros((GLA_VW, GLA_KW), F32)
        if has_state:
            for hh in range(GLA_HEADS):
                st_ref[d, hh * GLA_DV:(hh + 1) * GLA_DV, hh * GLA_DK:(hh + 1) * GLA_DK] = s0_ref[d, hh].T

    def load(n):
        rows = pl.ds(pl.multiple_of(n * c, c), c)
        return (rows, gq_ref[rows, :].astype(F32), gk_ref[rows, :].astype(F32), gv_ref[rows, :])

    nt = lambda a, b: lax.dot_general(a, b, _NT, preferred_element_type=F32)
    tn = lambda a, b: lax.dot_general(a, b, _TN, preferred_element_type=F32)

    def body(n, carry):
        rows, q, k, v = load(n)
        rows_b, q_b, k_b, v_b = load(nc - 1 - n)
        b_f = _cumsum_rows(tril, la_ref[rows, 0:GLA_KW])
        b_r = _cumsum_rows(triu, la_ref[rows, GLA_KW:2 * GLA_KW])
        b_b = _cumsum_rows(triu, la_ref[rows_b, GLA_KW:2 * GLA_KW])
        head_k = hk_ref[...]
        ops_f = _score_operands(q, k, b_f, False, head_k)
        ops_r = _score_operands(q, k, b_r, True, head_k)
        end_f, end_b = b_f[c - 1:c, :], b_b[0:1, :]
        qin_f = (q * jnp.exp(b_f)).astype(BF16)
        qin_b = (q_b * jnp.exp(b_b)).astype(BF16)
        kst_f = (k * jnp.exp(end_f - b_f)).astype(BF16)
        kst_b = (k_b * jnp.exp(end_b - b_b)).astype(BF16)
        st_f, st_b = st_ref[0], st_ref[1]
        parts_f = [nt(qt, kt) for qt, kt in ops_f]
        parts_r = [nt(qt, kt) for qt, kt in ops_r]
        o_f = nt(qin_f, st_f.astype(BF16))
        o_b = nt(qin_b, st_b.astype(BF16))
        upd_f = tn(v, kst_f)
        upd_b = tn(v_b, kst_b)
        a = _score_sum(parts_f, False, pair_ref) + _score_sum(parts_r, True, pair_ref)
        acc_ref[rows, :] = _dot(a.astype(BF16), _block_diag(v, hv_ref[...])) + o_f
        accb_ref[rows_b, :] = o_b
        st_ref[0] = st_f * jnp.exp(end_f) + jnp.where(st_mask, upd_f, 0.0)
        st_ref[1] = st_b * jnp.exp(end_b) + jnp.where(st_mask, upd_b, 0.0)
        return carry

    lax.fori_loop(0, nc, body, 0)

    o = acc_ref[...] + accb_ref[...]
    gn = gn_ref[...]
    for hh in range(GLA_HEADS):
        sl = slice(hh * GLA_DV, (hh + 1) * GLA_DV)
        o_ref[:, sl] = (_rms(o[:, sl]) * gn[:, sl] * gg_ref[:, sl].astype(F32)).astype(BF16)
    if not has_state:
        for d in range(2):
            for hh in range(GLA_HEADS):
                sfin_ref[d, hh] = st_ref[d, hh * GLA_DV:(hh + 1) * GLA_DV, hh * GLA_DK:(hh + 1) * GLA_DK].T


def _gla_scratch(seq_len):
    return [pltpu.VMEM((seq_len, GLA_VW), F32), pltpu.VMEM((seq_len, GLA_VW), F32),
            pltpu.VMEM((2, GLA_VW, GLA_KW), F32)]


def _gla_mask_specs():
    n_terms = 2 * (len(GLA_LEVELS) + 1)
    return [pl.BlockSpec((GLA_HEADS * GLA_CHUNK, GLA_KW), lambda i: (0, 0)),
            pl.BlockSpec((GLA_HEADS * GLA_CHUNK, GLA_VW), lambda i: (0, 0)),
            pl.BlockSpec((n_terms, GLA_CHUNK, GLA_HEADS * GLA_CHUNK), lambda i: (0, 0, 0))]


def _gla_ctx(layer, gq, gk, gv, la, gg, gn, masks):
    seq = lambda w: pl.BlockSpec((SEQ, w), lambda i: (i, 0))
    return pl.pallas_call(
        partial(_gla_kernel, seq_len=SEQ, has_state=False),
        grid=(BATCH,),
        in_specs=[seq(GLA_KW), seq(GLA_KW), seq(GLA_VW), seq(2 * GLA_KW), seq(GLA_VW),
                  _layer_row(layer, GLA_VW)] + _gla_mask_specs(),
        out_specs=[seq(GLA_VW),
                   pl.BlockSpec((None, 2, GLA_HEADS, GLA_DK, GLA_DV), lambda i: (i, 0, 0, 0, 0))],
        out_shape=[jax.ShapeDtypeStruct((N_CTX, GLA_VW), BF16),
                   jax.ShapeDtypeStruct((BATCH, 2, GLA_HEADS, GLA_DK, GLA_DV), F32)],
        scratch_shapes=_gla_scratch(SEQ),
        compiler_params=pltpu.CompilerParams(dimension_semantics=("parallel",), vmem_limit_bytes=VMEM_LIMIT),
        name="gla_ctx",
    )(gq, gk, gv, la, gg, gn, *masks)


def _gla_lat(layer, gq, gk, gv, la, gg, gn, masks, s0):
    rows = DEC_SEQ
    blk0 = N_CTX // rows
    seq = lambda w: pl.BlockSpec((rows, w), lambda b: (blk0 + b, 0))
    return pl.pallas_call(
        partial(_gla_kernel, seq_len=DEC_SEQ, has_state=True),
        grid=(DEC_BATCH,),
        in_specs=[seq(GLA_KW), seq(GLA_KW), seq(GLA_VW), seq(2 * GLA_KW), seq(GLA_VW),
                  _layer_row(layer, GLA_VW)] + _gla_mask_specs() + [
                  pl.BlockSpec((None, None, 2, GLA_HEADS, GLA_DK, GLA_DV), lambda b: (b, layer, 0, 0, 0, 0))],
        out_specs=pl.BlockSpec((rows, GLA_VW), lambda b: (b, 0)),
        out_shape=jax.ShapeDtypeStruct((N_LAT, GLA_VW), BF16),
        scratch_shapes=_gla_scratch(DEC_SEQ),
        compiler_params=pltpu.CompilerParams(dimension_semantics=("parallel",), vmem_limit_bytes=VMEM_LIMIT),
        name="gla_lat",
    )(gq, gk, gv, la, gg, gn, *masks, s0)


def _route(scores, bias):
    biased = scores + bias
    v = [biased[e:e + 1, :] for e in range(N_EXPERTS)]
    s = [scores[e:e + 1, :] for e in range(N_EXPERTS)]
    gscore = []
    for g in range(N_GROUPS):
        a, b, c, d = v[4 * g:4 * g + 4]
        hi1, lo1, hi2, lo2 = jnp.maximum(a, b), jnp.minimum(a, b), jnp.maximum(c, d), jnp.minimum(c, d)
        gscore.append(jnp.maximum(hi1, hi2) + jnp.maximum(jnp.minimum(hi1, hi2), jnp.maximum(lo1, lo2)))
    one = lambda cond: jnp.where(cond, 1.0, 0.0)
    picked, chosen = [], []
    for g in range(N_GROUPS):
        best = None
        for g2 in range(N_GROUPS):
            if g2 == g:
                continue
            ok = one((gscore[g] > gscore[g2]) if g2 < g else (gscore[g] >= gscore[g2]))
            best = ok if best is None else best * ok
        for i in range(EXPERTS_PER_GROUP):
            e = 4 * g + i
            beaten = None
            for j in range(EXPERTS_PER_GROUP):
                if j == i:
                    continue
                e2 = 4 * g + j
                lost = one((v[e2] > v[e]) if j > i else (v[e2] >= v[e]))
                beaten = lost if beaten is None else beaten + lost
            chosen.append(best * one(beaten < 1.5))
            picked.append(chosen[-1] * s[e])
    den = picked[0]
    for e in range(1, N_EXPERTS):
        den = den + picked[e]
    return jnp.concatenate([p / den for p in picked] + chosen, axis=0)


def _mix_kernel(xc_ref, xl_ref, oac_ref, oal_ref, ogc_ref, ogl_ref, sa_ref, sb_ref, mod_ref, wpa_f, wpb_f, wo_f, n2_ref,
                rw_ref, rb_ref, x1_out, h2_out, gt_out, wpa_ref, wpb_ref, wo_ref):
    @pl.when(pl.program_id(0) == 0)
    def _():
        wpa_ref[...] = wpa_f[...].astype(BF16)
        wpb_ref[...] = wpb_f[...].astype(BF16)
        wo_ref[...] = wo_f[...].astype(BF16)

    is_ctx = pl.program_id(0) < N_CTX // TM
    oa = jnp.where(is_ctx, oac_ref[...], oal_ref[...])
    og = jnp.where(is_ctx, ogc_ref[...], ogl_ref[...])
    a = _dot(oa, wpa_ref[...])
    b = _dot(og, wpb_ref[...])
    merged = sa_ref[...].astype(F32) * a + sb_ref[...].astype(F32) * b
    g1 = mod_ref[:, 2 * D_MODEL:3 * D_MODEL]
    sh2 = mod_ref[:, 3 * D_MODEL:4 * D_MODEL]
    sc2 = mod_ref[:, 4 * D_MODEL:5 * D_MODEL]
    x = jnp.where(is_ctx, xc_ref[...], xl_ref[...])
    x1 = x + g1 * _dot(merged.astype(BF16), wo_ref[...])
    x1_out[...] = x1
    h2 = _rms(x1) * n2_ref[...] * (1.0 + sc2) + sh2
    _to_row_tiles(h2_out, h2)
    hh, hl = _split2(h2)
    rh, rl = rw_ref[0], rw_ref[1]
    nt = lambda r, t: lax.dot_general(r, t, _NT, preferred_element_type=F32)
    logits = nt(rl, hh) + nt(rh, hl) + nt(rh, hh)
    gt_out[...] = _route(jax.nn.sigmoid(logits), rb_ref[...])


def _mix(layer, x_pair, oa_c, oa_l, og_c, og_l, sa, sb, mod, wpa, wpb, wo, norm2, rw3, rb):
    nt = N_TOK // TM
    n_ctx_t = N_CTX // TM
    row = lambda i: (i, 0)
    const = lambda i: (0, 0)
    ctx_row = lambda i: (jnp.minimum(i, n_ctx_t - 1), 0)
    lat_row = lambda i: (jnp.maximum(i - n_ctx_t, 0), 0)
    return pl.pallas_call(
        _mix_kernel,
        grid=(nt,),
        in_specs=_x_specs(x_pair) + [
            pl.BlockSpec((TM, ATTN_WIDTH), ctx_row),
            pl.BlockSpec((TM, ATTN_WIDTH), lat_row),
            pl.BlockSpec((TM, GLA_VW), ctx_row),
            pl.BlockSpec((TM, GLA_VW), lat_row),
            pl.BlockSpec((TM, D_MODEL), row),
            pl.BlockSpec((TM, D_MODEL), row),
            _mod_spec(layer),
            pl.BlockSpec((None, ATTN_WIDTH, D_MODEL), lambda i: (layer, 0, 0), pipeline_mode=pl.Buffered(1)),
            pl.BlockSpec((None, GLA_VW, D_MODEL), lambda i: (layer, 0, 0), pipeline_mode=pl.Buffered(1)),
            pl.BlockSpec((None, D_MODEL, D_MODEL), lambda i: (layer, 0, 0), pipeline_mode=pl.Buffered(1)),
            _layer_row(layer, D_MODEL),
            pl.BlockSpec((2, N_EXPERTS, D_MODEL), lambda i: (0, 0, 0)),
            pl.BlockSpec((N_EXPERTS, 1), const),
        ],
        out_specs=[pl.BlockSpec((TM, D_MODEL), row), pl.BlockSpec((TM * ROW_TILE, LANES), row),
                   pl.BlockSpec((2 * N_EXPERTS, TM), lambda i: (0, i))],
        out_shape=[jax.ShapeDtypeStruct((N_TOK, D_MODEL), F32),
                   jax.ShapeDtypeStruct((N_TOK * ROW_TILE, LANES), F32),
                   jax.ShapeDtypeStruct((2 * N_EXPERTS, N_TOK), F32)],
        scratch_shapes=[pltpu.VMEM((ATTN_WIDTH, D_MODEL), BF16), pltpu.VMEM((GLA_VW, D_MODEL), BF16),
                        pltpu.VMEM((D_MODEL, D_MODEL), BF16)],
        compiler_params=pltpu.CompilerParams(dimension_semantics=("arbitrary",), vmem_limit_bytes=VMEM_LIMIT),
        name="mix",
    )(*x_pair, oa_c, oa_l, og_c, og_l, sa, sb, mod, wpa, wpb, wo, norm2, rw3, rb)


def _route_plan(gsel):
    gates = gsel[:N_EXPERTS]
    sel = gsel[N_EXPERTS:] > 0.5
    seli = sel.astype(jnp.int32)
    incl = jnp.cumsum(seli, axis=1)
    count = incl[:, -1]
    ntile = (count + TS - 1) // TS
    tile_end = jnp.cumsum(ntile)
    tile_start = tile_end - ntile
    total = tile_end[-1]
    slot = tile_start[:, None] * TS + incl - seli
    pos_a = jnp.min(jnp.where(sel, slot, N_SLOT), axis=0)
    pos_b = jnp.max(jnp.where(sel, slot, -1), axis=0)
    w_a = jnp.sum(jnp.where(sel & (slot == pos_a[None, :]), gates, 0.0), axis=0)
    w_b = jnp.sum(jnp.where(sel & (slot == pos_b[None, :]), gates, 0.0), axis=0)
    nt = N_TOK // TM
    pos = jnp.concatenate([pos_a.reshape(nt, 1, TM), pos_b.reshape(nt, 1, TM)], axis=2).astype(jnp.int32)
    plan = jnp.concatenate([tile_start, ntile, total[None]]).astype(jnp.int32)
    return pos, jnp.stack([w_a, w_b], axis=1), plan


def _to_row_tiles(ref, x):
    for k in range(ROW_TILE):
        ref[pl.ds(k, x.shape[0], stride=ROW_TILE), :] = x[:, k * LANES:(k + 1) * LANES]


def _from_row_tiles(ref, rows):
    return jnp.concatenate([ref[pl.ds(k, rows, stride=ROW_TILE), :] for k in range(ROW_TILE)], axis=1)


def _tile_rows(row, n=1):
    return pl.ds(pl.multiple_of(row * ROW_TILE, ROW_TILE), n * ROW_TILE)


def _row_copy(src, src_row, dst, dst_row, sem):
    return pltpu.make_async_copy(src.at[_tile_rows(src_row), :], dst.at[_tile_rows(dst_row), :], sem)


def _dispatch_kernel(plan_ref, pos_ref, h_ref, xs_ref, zero_ref, sem_z, sem):
    i = pl.program_id(0)
    total = plan_ref[PLAN_TOTAL]

    def zero_copy(tile):
        return pltpu.make_async_copy(zero_ref, xs_ref.at[_tile_rows(tile * TS, TS), :], sem_z)

    def last_tile(e):
        return plan_ref[e] + plan_ref[N_EXPERTS + e] - 1

    @pl.when(i == 0)
    def _():
        zero_ref[...] = jnp.zeros(zero_ref.shape, F32)

        def each(fn):
            def expert(e, c):
                @pl.when(plan_ref[N_EXPERTS + e] > 0)
                def _():
                    fn(zero_copy(last_tile(e)))
                return c

            def unused(t, c):
                @pl.when(t >= total)
                def _():
                    fn(zero_copy(t))
                return c

            lax.fori_loop(0, N_EXPERTS, expert, 0)
            lax.fori_loop(0, N_TILE, unused, 0)

        each(lambda cp: cp.start())
        each(lambda cp: cp.wait())

    def start(r, c):
        _row_copy(h_ref, r, xs_ref, pos_ref[0, r], sem).start()
        _row_copy(h_ref, r, xs_ref, pos_ref[0, TM + r], sem).start()
        return c

    def wait(r, c):
        _row_copy(h_ref, 0, xs_ref, 0, sem).wait()
        return c

    lax.fori_loop(0, TM, start, 0, unroll=8)
    lax.fori_loop(0, 2 * TM, wait, 0, unroll=8)


def _dispatch(plan, pos, h2):
    return pl.pallas_call(
        _dispatch_kernel,
        grid_spec=pltpu.PrefetchScalarGridSpec(
            num_scalar_prefetch=1,
            grid=(N_TOK // TM,),
            in_specs=[
                pl.BlockSpec((None, 1, 2 * TM), lambda i, plan: (i, 0, 0), memory_space=pltpu.SMEM),
                pl.BlockSpec((TM * ROW_TILE, LANES), lambda i, plan: (i, 0)),
            ],
            out_specs=pl.BlockSpec(memory_space=pl.ANY),
            scratch_shapes=[pltpu.VMEM((TS * ROW_TILE, LANES), F32), pltpu.SemaphoreType.DMA,
                            pltpu.SemaphoreType.DMA],
        ),
        out_shape=jax.ShapeDtypeStruct((N_SLOT * ROW_TILE, LANES), F32),
        compiler_params=pltpu.CompilerParams(dimension_semantics=("arbitrary",), vmem_limit_bytes=VMEM_LIMIT),
        name="dispatch",
    )(plan, pos, h2)


def _moe_kernel(plan_ref, xs_ref, wg_ref, wu_ref, wd_ref, ys_ref, x_buf, y_buf, zero_ref, sem_in, sem_out, sem_z,
                wg_s, wu_s, wd_s):
    e = pl.program_id(0)
    total = plan_ref[PLAN_TOTAL]
    first = plan_ref[e]

    def in_copy(g):
        return pltpu.make_async_copy(xs_ref.at[_tile_rows(g * TS, TS), :], x_buf.at[g % 2], sem_in.at[g % 2])

    def out_copy(g):
        return pltpu.make_async_copy(y_buf.at[g % 2], ys_ref.at[_tile_rows(g * TS, TS), :], sem_out.at[g % 2])

    def zero_copy(g):
        return pltpu.make_async_copy(zero_ref, ys_ref.at[_tile_rows(g * TS, TS), :], sem_z)

    @pl.when(e == 0)
    def _():
        in_copy(0).start()

    wg_s[...] = wg_ref[...].astype(BF16)
    wu_s[...] = wu_ref[...].astype(BF16)
    wd_s[...] = wd_ref[...].astype(BF16)

    def tile(t, c):
        g = first + t
        slot = g % 2

        @pl.when(g + 1 < total)
        def _():
            in_copy(g + 1).start()

        in_copy(g).wait()

        @pl.when(g >= 2)
        def _():
            out_copy(g - 2).wait()

        x = _from_row_tiles(x_buf.at[slot], TS).astype(BF16)
        up = _dot(x, wu_s[...])
        gt = _dot(x, wg_s[...])
        act = (gt * jax.nn.sigmoid(gt) * up).astype(BF16)
        _to_row_tiles(y_buf.at[slot], _dot(act, wd_s[...]))
        out_copy(g).start()
        return c

    lax.fori_loop(0, plan_ref[N_EXPERTS + e], tile, 0)

    @pl.when(e == N_EXPERTS - 1)
    def _():
        out_copy(total - 1).wait()

        @pl.when(total >= 2)
        def _():
            out_copy(total - 2).wait()

        zero_ref[...] = jnp.zeros(zero_ref.shape, F32)

        def each(fn):
            def unused(g, c):
                @pl.when(g >= total)
                def _():
                    fn(zero_copy(g))
                return c
            lax.fori_loop(0, N_TILE, unused, 0)

        each(lambda cp: cp.start())
        each(lambda cp: cp.wait())


def _moe(layer, plan, xs, w_gate, w_up, w_down):
    w_spec = lambda a, b: pl.BlockSpec((None, None, a, b), lambda e, plan: (layer, e, 0, 0))
    tile_buf = lambda n: pltpu.VMEM((n, TS * ROW_TILE, LANES), F32)
    return pl.pallas_call(
        _moe_kernel,
        grid_spec=pltpu.PrefetchScalarGridSpec(
            num_scalar_prefetch=1,
            grid=(N_EXPERTS,),
            in_specs=[
                pl.BlockSpec(memory_space=pl.ANY),
                w_spec(D_MODEL, D_EXPERT), w_spec(D_MODEL, D_EXPERT), w_spec(D_EXPERT, D_MODEL),
            ],
            out_specs=pl.BlockSpec(memory_space=pl.ANY),
            scratch_shapes=[tile_buf(2), tile_buf(2), pltpu.VMEM((TS * ROW_TILE, LANES), F32),
                            pltpu.SemaphoreType.DMA((2,)), pltpu.SemaphoreType.DMA((2,)), pltpu.SemaphoreType.DMA,
                            pltpu.VMEM((D_MODEL, D_EXPERT), BF16), pltpu.VMEM((D_MODEL, D_EXPERT), BF16),
                            pltpu.VMEM((D_EXPERT, D_MODEL), BF16)],
        ),
        out_shape=jax.ShapeDtypeStruct((N_SLOT * ROW_TILE, LANES), F32),
        compiler_params=pltpu.CompilerParams(dimension_semantics=("arbitrary",), vmem_limit_bytes=VMEM_LIMIT),
        name="moe",
    )(plan, xs, w_gate, w_up, w_down)


def _combine_kernel(pos_ref, nxt_ref, w_ref, x1_ref, mod_ref, ys_ref, out_ref, ya_ref, yb_ref, sem, *, n):
    i = pl.program_id(0)

    def gather(p_ref, slot):
        def start(r, c):
            _row_copy(ys_ref, p_ref[0, r], ya_ref.at[slot], r, sem.at[slot]).start()
            _row_copy(ys_ref, p_ref[0, TM + r], yb_ref.at[slot], r, sem.at[slot]).start()
            return c
        lax.fori_loop(0, TM, start, 0, unroll=8)

    @pl.when(i == 0)
    def _():
        gather(pos_ref, 0)

    @pl.when(i + 1 < n)
    def _():
        gather(nxt_ref, (i + 1) % 2)

    slot = i % 2

    def wait(r, c):
        _row_copy(ys_ref, 0, ya_ref.at[slot], 0, sem.at[slot]).wait()
        return c

    lax.fori_loop(0, 2 * TM, wait, 0, unroll=8)
    g2 = mod_ref[:, 5 * D_MODEL:6 * D_MODEL]
    w = w_ref[...]
    ya = _from_row_tiles(ya_ref.at[slot], TM)
    yb = _from_row_tiles(yb_ref.at[slot], TM)
    out_ref[...] = x1_ref[...] + g2 * (w[:, 0:1] * ya + w[:, 1:2] * yb)


def _combine(layer, pos, w_ab, x1, mod, ys, tile0, nt):
    pos_spec = lambda fn: pl.BlockSpec((None, 1, 2 * TM), fn, memory_space=pltpu.SMEM)
    return pl.pallas_call(
        partial(_combine_kernel, n=nt),
        grid=(nt,),
        in_specs=[
            pos_spec(lambda i: (tile0 + i, 0, 0)),
            pos_spec(lambda i: (tile0 + jnp.minimum(i + 1, nt - 1), 0, 0)),
            pl.BlockSpec((TM, 2), lambda i: (tile0 + i, 0)),
            pl.BlockSpec((TM, D_MODEL), lambda i: (tile0 + i, 0)),
            _mod_spec(layer, tile0),
            pl.BlockSpec(memory_space=pl.ANY),
        ],
        out_specs=pl.BlockSpec((TM, D_MODEL), lambda i: (i, 0)),
        out_shape=jax.ShapeDtypeStruct((nt * TM, D_MODEL), F32),
        scratch_shapes=[pltpu.VMEM((2, TM * ROW_TILE, LANES), F32), pltpu.VMEM((2, TM * ROW_TILE, LANES), F32),
                        pltpu.SemaphoreType.DMA((2,))],
        compiler_params=pltpu.CompilerParams(dimension_semantics=("arbitrary",), vmem_limit_bytes=VMEM_LIMIT),
        name="combine",
    )(pos, pos, w_ab, x1, mod, ys)


def _rope_tables():
    pos = np.arange(DEC_SEQ)
    rows = (pos // GRID_W).astype(np.float64)
    cols = (pos % GRID_W).astype(np.float64)
    half = HEAD_DIM // 2
    inv_freq = ROPE_THETA ** (-np.arange(0, half, 2, dtype=np.float64) / half)
    ang_r = rows[:, None] * inv_freq[None, :]
    ang_c = cols[:, None] * inv_freq[None, :]
    cos_h = np.concatenate([np.cos(ang_r), np.cos(ang_r), np.cos(ang_c), np.cos(ang_c)], axis=-1)
    sin_h = np.concatenate([-np.sin(ang_r), np.sin(ang_r), -np.sin(ang_c), np.sin(ang_c)], axis=-1)
    cos = np.concatenate([np.ones((TM, HEAD_DIM)), cos_h], axis=0)
    sin = np.concatenate([np.zeros((TM, HEAD_DIM)), sin_h], axis=0)
    return (jnp.asarray(np.tile(cos, (1, N_HEADS)), F32), jnp.asarray(np.tile(sin, (1, N_HEADS)), F32))


def _pack_w2(w2):
    z = jnp.zeros((DEPTH, GLA_RANK, GLA_KW), w2.dtype)
    top = jnp.concatenate([w2[:, 0], z], axis=2)
    bot = jnp.concatenate([z, w2[:, 1]], axis=2)
    pad = jnp.zeros((DEPTH, LR_PAD - 2 * GLA_RANK, 2 * GLA_KW), w2.dtype)
    return jnp.concatenate([top, bot, pad], axis=1).astype(BF16)


def kernel(x_prompt, x_sample, cache_k, cache_v, state_gla, c, c_ctx, norm1, norm2, w_ada, b_ada, w_in,
           q_norm, k_norm, gla_w2, gla_b, gla_norm, w_pa, w_pb, w_out, router_w, router_bias,
           w_gate, w_up, w_down):
    x_pair = (x_prompt.reshape(N_CTX, D_MODEL), x_sample.reshape(N_LAT, D_MODEL))
    cond =jnp.concatenate([c_ctx[None, :], c, jnp.zeros((N_COND - 1 - DEC_BATCH, D_MODEL), F32)], axis=0)
    mod = _ada(cond, w_ada, b_ada).reshape(DEPTH, N_COND, 1, MOD_W)
    cos, sin = _rope_tables()
    gla_masks = _gla_masks()
    w_in_t = jnp.swapaxes(w_in, 1, 2)
    lane_head = np.arange(ATTN_WIDTH) // HEAD_DIM
    bd = jnp.asarray(lane_head[:, None] == lane_head[None, :], BF16)
    rw3 = jnp.stack(_split2(router_w.T), axis=0)
    rb = router_bias.reshape(N_EXPERTS, 1)
    ck = cache_k.reshape(DEC_BATCH, DEPTH, PAST_LEN, KV_WIDTH)
    cv = cache_v.reshape(DEC_BATCH, DEPTH, PAST_LEN, KV_WIDTH)
    rows = lambda p, reps: jnp.tile(p, (1, reps)).reshape(DEPTH, 1, -1)
    n1, n2 = rows(norm1, 1), rows(norm2, 1)
    qn, kn, gn = rows(q_norm, N_HEADS), rows(k_norm, KV_HEADS), rows(gla_norm, GLA_HEADS)
    gb = gla_b.reshape(DEPTH, 1, 2 * GLA_KW)
    w2p = _pack_w2(gla_w2)

    keys, vals, states = [], [], []
    for l in range(DEPTH):
        q, kf, kr, v, gq, gk, gv, gg, la, sa, sb = _inproj(l, x_pair, mod, n1, w_in_t, qn, kn, cos, sin, bd, w2p, gb)
        oa_c = _attn_ctx(q, kr, v)
        oa_l = _attn_lat(l, q, kr, v, ck, cv)
        og_c, s_fin = _gla_ctx(l, gq, gk, gv, la, gg, gn, gla_masks)
        og_l = _gla_lat(l, gq, gk, gv, la, gg, gn, gla_masks, state_gla)
        x1, h2, gsel = _mix(l, x_pair, oa_c, oa_l, og_c, og_l, sa, sb, mod, w_pa, w_pb, w_out, n2, rw3, rb)
        pos, w_ab, plan = _route_plan(gsel)
        xs = _dispatch(plan, pos, h2)
        ys = _moe(l, plan, xs, w_gate, w_up, w_down)
        n_ctx_t = N_CTX // TM
        if l + 1 < DEPTH:
            x = _combine(l, pos, w_ab, x1, mod, ys, 0, N_TOK // TM)
            x_pair = (x, x)
        else:
            x_pair = (_combine(l, pos, w_ab, x1, mod, ys, 0, n_ctx_t),
                      _combine(l, pos, w_ab, x1, mod, ys, n_ctx_t, N_LAT // TM))
        keys.append(kf[:N_CTX].reshape(BATCH, SEQ, KV_HEADS, HEAD_DIM))
        vals.append(v[:N_CTX].reshape(BATCH, SEQ, KV_HEADS, HEAD_DIM))
        states.append(s_fin)
    y_prompt = x_pair[0].reshape(BATCH, SEQ, D_MODEL)
    y_sample = x_pair[1].reshape(DEC_BATCH, DEC_SEQ, D_MODEL)
    return (y_prompt, y_sample, jnp.stack(keys, axis=1), jnp.stack(vals, axis=1), jnp.stack(states, axis=1))
```

```python
from functools import partial

import jax
import jax.numpy as jnp
import numpy as np
from jax import lax
from jax.experimental import pallas as pl
from jax.experimental.pallas import tpu as pltpu

F32 = jnp.float32
BF16 = jnp.bfloat16

D_MODEL = 1024
BATCH = 16
SEQ = 256
DEPTH = 2
DEC_BATCH = 4
DEC_SEQ = 1024
PAST_LEN = 512
GRID_W = 64
N_HEADS = 8
KV_HEADS = 2
GROUP = N_HEADS // KV_HEADS
HEAD_DIM = 64
ATTN_WIDTH = N_HEADS * HEAD_DIM
KV_WIDTH = KV_HEADS * HEAD_DIM
ROPE_THETA = 10000.0
GLA_HEADS = 4
GLA_DK = 64
GLA_DV = 128
GLA_KW = GLA_HEADS * GLA_DK
GLA_VW = GLA_HEADS * GLA_DV
GLA_RANK = 16
GLA_GATE_NORM = 16.0
N_EXPERTS = 16
N_GROUPS = 4
EXPERTS_PER_GROUP = N_EXPERTS // N_GROUPS
D_EXPERT = 512
NORM_EPS = 1e-6

N_CTX = BATCH * SEQ
N_LAT = DEC_BATCH * DEC_SEQ
N_TOK = N_CTX + N_LAT
N_COND = 8
MOD_W = 6 * D_MODEL

TM = 512
TS = 512
N_TILE = 2 * N_TOK // TS + N_EXPERTS
N_SLOT = N_TILE * TS
PLAN_TOTAL = 2 * N_EXPERTS
ROW_TILE = 8
LANES = D_MODEL // ROW_TILE
GLA_CHUNK = 128
GLA_LEVELS = (64, 32, 16)
GLA_DIAG = 16
LR_PAD = 128
W_ROWS = 128
SEG_MIX = ATTN_WIDTH + 2 * KV_WIDTH + 2 * GLA_KW + 2 * GLA_VW
OFF_GA = SEG_MIX
OFF_GB = OFF_GA + D_MODEL
OFF_LR = OFF_GB + D_MODEL
IN_PACKED = OFF_LR + LR_PAD
IN_WIDTH = SEG_MIX + 2 * GLA_RANK + 2 * D_MODEL
VMEM_LIMIT = 56 * 1024 * 1024

_NT = (((1,), (1,)), ((), ()))
_TN = (((0,), (0,)), ((), ()))


def _mod_row(tile, tm):
    start = tile * tm
    return jnp.where(start < N_CTX, 0, 1 + (start - N_CTX) // DEC_SEQ)


def _split2(x):
    hi = x.astype(BF16)
    lo = (x - hi.astype(F32)).astype(BF16)
    return hi, lo


def _dot(a, b):
    return jnp.dot(a, b, preferred_element_type=F32)


def _rms(x):
    return x * lax.rsqrt(jnp.mean(x * x, axis=-1, keepdims=True) + NORM_EPS)


def _ada_kernel(cond_ref, w_ref, b_ref, out_ref):
    c = cond_ref[...]
    s = c * jax.nn.sigmoid(c)
    out_ref[...] = _dot(s.astype(BF16), w_ref[...].astype(BF16)) + b_ref[...]


def _ada(cond, w_ada, b_ada):
    nb = MOD_W // D_MODEL
    return pl.pallas_call(
        _ada_kernel,
        grid=(DEPTH, nb),
        in_specs=[
            pl.BlockSpec((N_COND, D_MODEL), lambda l, j: (0, 0)),
            pl.BlockSpec((None, D_MODEL, D_MODEL), lambda l, j: (l, 0, j)),
            pl.BlockSpec((None, 1, D_MODEL), lambda l, j: (l, 0, j)),
        ],
        out_specs=pl.BlockSpec((None, N_COND, D_MODEL), lambda l, j: (l, 0, j)),
        out_shape=jax.ShapeDtypeStruct((DEPTH, N_COND, MOD_W), F32),
        compiler_params=pltpu.CompilerParams(vmem_limit_bytes=VMEM_LIMIT),
        name="ada",
    )(cond, w_ada, b_ada.reshape(DEPTH, 1, MOD_W))


def _head_rmsnorm(x, bd):
    ssq = _dot((x * x).astype(BF16), bd)
    return x * lax.rsqrt(ssq * (1.0 / HEAD_DIM) + NORM_EPS)


def _rope(x, cos, sin):
    n = x.shape[-1]
    lane = lax.broadcasted_iota(jnp.int32, x.shape, 1)
    first = (lane & (HEAD_DIM // 4)) == 0
    partner = jnp.where(first, pltpu.roll(x, n - HEAD_DIM // 4, 1), pltpu.roll(x, HEAD_DIM // 4, 1))
    return x * cos + partner * sin


def _pack_w_in(w_ref, wp_ref):
    lr0 = SEG_MIX
    ga0 = lr0 + 2 * GLA_RANK

    def cast(dst, src, n):
        def body(i, carry):
            rows_out = pl.ds(pl.multiple_of(dst + i * W_ROWS, W_ROWS), W_ROWS)
            rows_in = pl.ds(pl.multiple_of(src + i * W_ROWS, 2 * GLA_RANK), W_ROWS)
            wp_ref[rows_out, :] = w_ref[rows_in, :].astype(BF16)
            return carry
        lax.fori_loop(0, n // W_ROWS, body, 0)

    cast(0, 0, SEG_MIX)
    cast(OFF_GA, ga0, 2 * D_MODEL)
    wp_ref[OFF_LR:IN_PACKED, :] = jnp.concatenate(
        [w_ref[lr0:ga0, :], jnp.zeros((LR_PAD - 2 * GLA_RANK, D_MODEL), F32)], axis=0).astype(BF16)


def _inproj_kernel(xc_ref, xl_ref, mod_ref, n1_ref, wf_ref, qn_ref, kn_ref, cos_ref, sin_ref, bd_ref, w2_ref, gb_ref,
                   q_out, kf_out, kr_out, v_out, gq_out, gk_out, gv_out, gg_out, la_out, sa_out, sb_out,
                   w_ref):
    @pl.when(pl.program_id(0) == 0)
    def _():
        _pack_w_in(wf_ref, w_ref)

    x = jnp.where(pl.program_id(0) < N_CTX // TM, xc_ref[...], xl_ref[...])
    sh1 = mod_ref[:, 0:D_MODEL]
    sc1 = mod_ref[:, D_MODEL:2 * D_MODEL]
    h = (_rms(x) * (n1_ref[...] * (1.0 + sc1)) + sh1).astype(BF16)

    def proj(lo, width):
        return lax.dot_general(h, w_ref[lo:lo + width, :], _NT, preferred_element_type=F32)

    cos = cos_ref[...]
    sin = sin_ref[...]
    bd = bd_ref[...]
    off = 0
    q = _head_rmsnorm(proj(off, ATTN_WIDTH), bd) * qn_ref[...]
    q_out[...] = (_rope(q, cos, sin) * HEAD_DIM ** -0.5).astype(BF16)
    off += ATTN_WIDTH
    k = _head_rmsnorm(proj(off, KV_WIDTH), bd[:KV_WIDTH, :KV_WIDTH]) * kn_ref[...]
    kf_out[...] = k
    kr_out[...] = _rope(k, cos[:, :KV_WIDTH], sin[:, :KV_WIDTH]).astype(BF16)
    off += KV_WIDTH
    v_out[...] = proj(off, KV_WIDTH)
    off += KV_WIDTH
    gq_out[...] = (proj(off, GLA_KW) * GLA_DK ** -0.5).astype(BF16)
    off += GLA_KW
    gk_out[...] = proj(off, GLA_KW).astype(BF16)
    off += GLA_KW
    gv_out[...] = proj(off, GLA_VW).astype(BF16)
    off += GLA_VW
    gg = proj(off, GLA_VW)
    gg_out[...] = (gg * jax.nn.sigmoid(gg)).astype(BF16)
    sa_out[...] = jax.nn.sigmoid(proj(OFF_GA, D_MODEL)).astype(BF16)
    sb_out[...] = jax.nn.sigmoid(proj(OFF_GB, D_MODEL)).astype(BF16)
    lr = proj(OFF_LR, LR_PAD)
    z = _dot(lr.astype(BF16), w2_ref[...]) + gb_ref[...]
    log_sig = jnp.minimum(z, 0.0) - jnp.log1p(jnp.exp(-jnp.abs(z)))
    la_out[...] = log_sig * (1.0 / GLA_GATE_NORM)


def _layer_row(layer, width):
    return pl.BlockSpec((None, 1, width), lambda *_: (layer, 0, 0))


def _mod_spec(layer, tile0=0):
    return pl.BlockSpec((None, None, 1, MOD_W), lambda i, *_: (layer, _mod_row(tile0 + i, TM), 0, 0))


def _x_specs(x_pair):
    n_ctx_t = N_CTX // TM
    lat0 = n_ctx_t if x_pair[0] is x_pair[1] else 0
    return [pl.BlockSpec((TM, D_MODEL), lambda i: (jnp.minimum(i, n_ctx_t - 1), 0)),
            pl.BlockSpec((TM, D_MODEL), lambda i: (lat0 + jnp.maximum(i - n_ctx_t, 0), 0))]


def _inproj(layer, x_pair, mod, norm1, w_in, qn, kn, cos, sin, bd, w2p, gb):
    nt = N_TOK // TM
    n_ctx_t = N_CTX // TM
    t_per_seq = DEC_SEQ // TM
    row = lambda i: (i, 0)
    const = lambda i: (0, 0)
    rope_blk = lambda i: (jnp.where(i < n_ctx_t, 0, 1 + (i - n_ctx_t) % t_per_seq), 0)

    def out(width, dtype):
        return pl.BlockSpec((TM, width), row), jax.ShapeDtypeStruct((N_TOK, width), dtype)

    outs = [out(ATTN_WIDTH, BF16), out(KV_WIDTH, F32), out(KV_WIDTH, BF16), out(KV_WIDTH, F32),
            out(GLA_KW, BF16), out(GLA_KW, BF16), out(GLA_VW, BF16), out(GLA_VW, BF16),
            out(2 * GLA_KW, F32), out(D_MODEL, BF16), out(D_MODEL, BF16)]
    return pl.pallas_call(
        _inproj_kernel,
        grid=(nt,),
        in_specs=_x_specs(x_pair) + [
            _mod_spec(layer),
            _layer_row(layer, D_MODEL),
            pl.BlockSpec((None, IN_WIDTH, D_MODEL), lambda i: (layer, 0, 0), pipeline_mode=pl.Buffered(1)),
            _layer_row(layer, ATTN_WIDTH),
            _layer_row(layer, KV_WIDTH),
            pl.BlockSpec((TM, ATTN_WIDTH), rope_blk),
            pl.BlockSpec((TM, ATTN_WIDTH), rope_blk),
            pl.BlockSpec((ATTN_WIDTH, ATTN_WIDTH), const),
            pl.BlockSpec((None, LR_PAD, 2 * GLA_KW), lambda i: (layer, 0, 0)),
            _layer_row(layer, 2 * GLA_KW),
        ],
        out_specs=[o[0] for o in outs],
        out_shape=[o[1] for o in outs],
        scratch_shapes=[pltpu.VMEM((IN_PACKED, D_MODEL), BF16)],
        compiler_params=pltpu.CompilerParams(dimension_semantics=("arbitrary",), vmem_limit_bytes=VMEM_LIMIT),
        name="inproj",
    )(*x_pair, mod, norm1, w_in, qn, kn, cos, sin, bd, w2p, gb)


def _attn_kernel(*refs, has_cache):
    if has_cache:
        q_ref, k_ref, v_ref, ck_ref, cv_ref, o_ref = refs
    else:
        q_ref, k_ref, v_ref, o_ref = refs
    tq = q_ref.shape[0]
    for g in range(KV_HEADS):
        ks = slice(g * HEAD_DIM, (g + 1) * HEAD_DIM)
        qs = jnp.concatenate(
            [q_ref[:, (g * GROUP + j) * HEAD_DIM:(g * GROUP + j + 1) * HEAD_DIM] for j in range(GROUP)], axis=0)
        k = k_ref[:, ks]
        v = v_ref[:, ks].astype(BF16)
        s = lax.dot_general(qs, k, _NT, preferred_element_type=F32)
        m = jnp.max(s, axis=-1, keepdims=True)
        if has_cache:
            ck = ck_ref[:, ks].astype(BF16)
            cv = cv_ref[:, ks].astype(BF16)
            s2 = lax.dot_general(qs, ck, _NT, preferred_element_type=F32)
            m = jnp.maximum(m, jnp.max(s2, axis=-1, keepdims=True))
        p = jnp.exp(s - m)
        den = jnp.sum(p, axis=-1, keepdims=True)
        o = _dot(p.astype(BF16), v)
        if has_cache:
            p2 = jnp.exp(s2 - m)
            den = den + jnp.sum(p2, axis=-1, keepdims=True)
            o = o + _dot(p2.astype(BF16), cv)
        o = o / den
        for j in range(GROUP):
            hd = g * GROUP + j
            o_ref[:, hd * HEAD_DIM:(hd + 1) * HEAD_DIM] = o[j * tq:(j + 1) * tq, :].astype(BF16)


def _attn_ctx(q, kr, v):
    return pl.pallas_call(
        partial(_attn_kernel, has_cache=False),
        grid=(BATCH,),
        in_specs=[
            pl.BlockSpec((SEQ, ATTN_WIDTH), lambda i: (i, 0)),
            pl.BlockSpec((SEQ, KV_WIDTH), lambda i: (i, 0)),
            pl.BlockSpec((SEQ, KV_WIDTH), lambda i: (i, 0)),
        ],
        out_specs=pl.BlockSpec((SEQ, ATTN_WIDTH), lambda i: (i, 0)),
        out_shape=jax.ShapeDtypeStruct((N_CTX, ATTN_WIDTH), BF16),
        compiler_params=pltpu.CompilerParams(dimension_semantics=("parallel",), vmem_limit_bytes=VMEM_LIMIT),
        name="attn_ctx",
    )(q, kr, v)


def _attn_lat(layer, q, kr, v, cache_k, cache_v):
    tq = 256
    nq = DEC_SEQ // tq
    q0 = N_CTX // tq
    s0 = N_CTX // DEC_SEQ
    return pl.pallas_call(
        partial(_attn_kernel, has_cache=True),
        grid=(DEC_BATCH, nq),
        in_specs=[
            pl.BlockSpec((tq, ATTN_WIDTH), lambda b, i: (q0 + b * nq + i, 0)),
            pl.BlockSpec((DEC_SEQ, KV_WIDTH), lambda b, i: (s0 + b, 0)),
            pl.BlockSpec((DEC_SEQ, KV_WIDTH), lambda b, i: (s0 + b, 0)),
            pl.BlockSpec((None, None, PAST_LEN, KV_WIDTH), lambda b, i: (b, layer, 0, 0)),
            pl.BlockSpec((None, None, PAST_LEN, KV_WIDTH), lambda b, i: (b, layer, 0, 0)),
        ],
        out_specs=pl.BlockSpec((tq, ATTN_WIDTH), lambda b, i: (b * nq + i, 0)),
        out_shape=jax.ShapeDtypeStruct((N_LAT, ATTN_WIDTH), BF16),
        compiler_params=pltpu.CompilerParams(dimension_semantics=("parallel", "parallel"),
                                             vmem_limit_bytes=VMEM_LIMIT),
        name="attn_lat",
    )(q, kr, v, cache_k, cache_v)


def _gla_masks():
    c = GLA_CHUNK
    hrow = np.arange(GLA_HEADS * c) // c
    head_k = (hrow[:, None] == (np.arange(GLA_KW) // GLA_DK)[None, :])
    head_v = (hrow[:, None] == (np.arange(GLA_VW) // GLA_DV)[None, :])
    t = np.arange(c)[:, None]
    s = (np.arange(GLA_HEADS * c) % c)[None, :]
    pair = []
    for reverse in (False, True):
        for m in GLA_LEVELS:
            pair.append(t // (2 * m) == s // (2 * m))
        pair.append((t // GLA_DIAG == s // GLA_DIAG) & ((s >= t) if reverse else (s <= t)))
    return (jnp.asarray(head_k, BF16), jnp.asarray(head_v, BF16), jnp.asarray(np.stack(pair), F32))


def _block_diag(x, head_mask):
    return jnp.concatenate([x] * GLA_HEADS, axis=0) * head_mask


def _ref_rows(b, rows, rep):
    return jnp.concatenate([jnp.broadcast_to(b[r:r + 1, :], (rep, b.shape[1])) for r in rows], axis=0)


def _score_operands(q, k, b, reverse, head_k):
    c = GLA_CHUNK
    t_idx = lax.broadcasted_iota(jnp.int32, (c, GLA_KW), 0)
    terms = []
    for m in GLA_LEVELS:
        nblk = c // (2 * m)
        ref = _ref_rows(b, [2 * m * j + (m if reverse else m - 1) for j in range(nblk)], 2 * m)
        q_side = ((t_idx & m) == 0) if reverse else ((t_idx & m) != 0)
        terms.append((jnp.where(q_side, q * jnp.exp(b - ref), 0.0), jnp.where(q_side, 0.0, k * jnp.exp(ref - b))))
    d = GLA_DIAG
    x = b - _ref_rows(b, [d * j + d // 2 for j in range(c // d)], d)
    terms.append((q * jnp.exp(x), k * jnp.exp(-x)))
    return [(qt.astype(BF16), _block_diag(kt.astype(BF16), head_k)) for qt, kt in terms]


def _score_sum(parts, reverse, pair_ref):
    term0 = (len(GLA_LEVELS) + 1) if reverse else 0
    total = jnp.where(pair_ref[term0 + len(GLA_LEVELS)] > 0.5, parts[-1], 0.0)
    for i, part in enumerate(parts[:-1]):
        total = total + part * pair_ref[term0 + i]
    return total


def _cumsum_rows(tri, la):
    hi, lo = _split2(la)
    return _dot(tri, hi) + _dot(tri, lo)


def _gla_kernel(*refs, seq_len, has_state):
    gq_ref, gk_ref, gv_ref, la_ref, gg_ref, gn_ref, hk_ref, hv_ref, pair_ref = refs[:9]
    if has_state:
        s0_ref, o_ref, acc_ref, accb_ref, st_ref = refs[9:]
    else:
        o_ref, sfin_ref, acc_ref, accb_ref, st_ref = refs[9:]
    c = GLA_CHUNK
    nc = seq_len // c
    r_i = lax.broadcasted_iota(jnp.int32, (c, c), 0)
    c_i = lax.broadcasted_iota(jnp.int32, (c, c), 1)
    tril = jnp.where(c_i <= r_i, 1.0, 0.0).astype(BF16)
    triu = jnp.where(c_i >= r_i, 1.0, 0.0).astype(BF16)
    st_head_r = lax.broadcasted_iota(jnp.int32, (GLA_VW, GLA_KW), 0) // GLA_DV
    st_head_c = lax.broadcasted_iota(jnp.int32, (GLA_VW, GLA_KW), 1) // GLA_DK
    st_mask = st_head_r == st_head_c

    for d in range(2):
        st_ref[d] = jnp.zeros((GLA_VW, GLA_KW), F32)
        if has_state:
            for hh in range(GLA_HEADS):
                st_ref[d, hh * GLA_DV:(hh + 1) * GLA_DV, hh * GLA_DK:(hh + 1) * GLA_DK] = s0_ref[d, hh].T

    def load(n):
        rows = pl.ds(pl.multiple_of(n * c, c), c)
        return (rows, gq_ref[rows, :].astype(F32), gk_ref[rows, :].astype(F32), gv_ref[rows, :])

    nt = lambda a, b: lax.dot_general(a, b, _NT, preferred_element_type=F32)
    tn = lambda a, b: lax.dot_general(a, b, _TN, preferred_element_type=F32)

    def body(n, carry):
        rows, q, k, v = load(n)
        rows_b, q_b, k_b, v_b = load(nc - 1 - n)
        b_f = _cumsum_rows(tril, la_ref[rows, 0:GLA_KW])
        b_r = _cumsum_rows(triu, la_ref[rows, GLA_KW:2 * GLA_KW])
        b_b = _cumsum_rows(triu, la_ref[rows_b, GLA_KW:2 * GLA_KW])
        head_k = hk_ref[...]
        ops_f = _score_operands(q, k, b_f, False, head_k)
        ops_r = _score_operands(q, k, b_r, True, head_k)
        end_f, end_b = b_f[c - 1:c, :], b_b[0:1, :]
        qin_f = (q * jnp.exp(b_f)).astype(BF16)
        qin_b = (q_b * jnp.exp(b_b)).astype(BF16)
        kst_f = (k * jnp.exp(end_f - b_f)).astype(BF16)
        kst_b = (k_b * jnp.exp(end_b - b_b)).astype(BF16)
        st_f, st_b = st_ref[0], st_ref[1]
        parts_f = [nt(qt, kt) for qt, kt in ops_f]
        parts_r = [nt(qt, kt) for qt, kt in ops_r]
        o_f = nt(qin_f, st_f.astype(BF16))
        o_b = nt(qin_b, st_b.astype(BF16))
        upd_f = tn(v, kst_f)
        upd_b = tn(v_b, kst_b)
        a = _score_sum(parts_f, False, pair_ref) + _score_sum(parts_r, True, pair_ref)
        acc_ref[rows, :] = _dot(a.astype(BF16), _block_diag(v, hv_ref[...])) + o_f
        accb_ref[rows_b, :] = o_b
        st_ref[0] = st_f * jnp.exp(end_f) + jnp.where(st_mask, upd_f, 0.0)
        st_ref[1] = st_b * jnp.exp(end_b) + jnp.where(st_mask, upd_b, 0.0)
        return carry

    lax.fori_loop(0, nc, body, 0)

    o = acc_ref[...] + accb_ref[...]
    gn = gn_ref[...]
    for hh in range(GLA_HEADS):
        sl = slice(hh * GLA_DV, (hh + 1) * GLA_DV)
        o_ref[:, sl] = (_rms(o[:, sl]) * gn[:, sl] * gg_ref[:, sl].astype(F32)).astype(BF16)
    if not has_state:
        for d in range(2):
            for hh in range(GLA_HEADS):
                sfin_ref[d, hh] = st_ref[d, hh * GLA_DV:(hh + 1) * GLA_DV, hh * GLA_DK:(hh + 1) * GLA_DK].T


def _gla_scratch(seq_len):
    return [pltpu.VMEM((seq_len, GLA_VW), F32), pltpu.VMEM((seq_len, GLA_VW), F32),
            pltpu.VMEM((2, GLA_VW, GLA_KW), F32)]


def _gla_mask_specs():
    n_terms = 2 * (len(GLA_LEVELS) + 1)
    return [pl.BlockSpec((GLA_HEADS * GLA_CHUNK, GLA_KW), lambda i: (0, 0)),
            pl.BlockSpec((GLA_HEADS * GLA_CHUNK, GLA_VW), lambda i: (0, 0)),
            pl.BlockSpec((n_terms, GLA_CHUNK, GLA_HEADS * GLA_CHUNK), lambda i: (0, 0, 0))]


def _gla_ctx(layer, gq, gk, gv, la, gg, gn, masks):
    seq = lambda w: pl.BlockSpec((SEQ, w), lambda i: (i, 0))
    return pl.pallas_call(
        partial(_gla_kernel, seq_len=SEQ, has_state=False),
        grid=(BATCH,),
        in_specs=[seq(GLA_KW), seq(GLA_KW), seq(GLA_VW), seq(2 * GLA_KW), seq(GLA_VW),
                  _layer_row(layer, GLA_VW)] + _gla_mask_specs(),
        out_specs=[seq(GLA_VW),
                   pl.BlockSpec((None, 2, GLA_HEADS, GLA_DK, GLA_DV), lambda i: (i, 0, 0, 0, 0))],
        out_shape=[jax.ShapeDtypeStruct((N_CTX, GLA_VW), BF16),
                   jax.ShapeDtypeStruct((BATCH, 2, GLA_HEADS, GLA_DK, GLA_DV), F32)],
        scratch_shapes=_gla_scratch(SEQ),
        compiler_params=pltpu.CompilerParams(dimension_semantics=("parallel",), vmem_limit_bytes=VMEM_LIMIT),
        name="gla_ctx",
    )(gq, gk, gv, la, gg, gn, *masks)


def _gla_lat(layer, gq, gk, gv, la, gg, gn, masks, s0):
    rows = DEC_SEQ
    blk0 = N_CTX // rows
    seq = lambda w: pl.BlockSpec((rows, w), lambda b: (blk0 + b, 0))
    return pl.pallas_call(
        partial(_gla_kernel, seq_len=DEC_SEQ, has_state=True),
        grid=(DEC_BATCH,),
        in_specs=[seq(GLA_KW), seq(GLA_KW), seq(GLA_VW), seq(2 * GLA_KW), seq(GLA_VW),
                  _layer_row(layer, GLA_VW)] + _gla_mask_specs() + [
                  pl.BlockSpec((None, None, 2, GLA_HEADS, GLA_DK, GLA_DV), lambda b: (b, layer, 0, 0, 0, 0))],
        out_specs=pl.BlockSpec((rows, GLA_VW), lambda b: (b, 0)),
        out_shape=jax.ShapeDtypeStruct((N_LAT, GLA_VW), BF16),
        scratch_shapes=_gla_scratch(DEC_SEQ),
        compiler_params=pltpu.CompilerParams(dimension_semantics=("parallel",), vmem_limit_bytes=VMEM_LIMIT),
        name="gla_lat",
    )(gq, gk, gv, la, gg, gn, *masks, s0)


def _route(scores, bias):
    biased = scores + bias
    v = [biased[e:e + 1, :] for e in range(N_EXPERTS)]
    s = [scores[e:e + 1, :] for e in range(N_EXPERTS)]
    gscore = []
    for g in range(N_GROUPS):
        a, b, c, d = v[4 * g:4 * g + 4]
        hi1, lo1, hi2, lo2 = jnp.maximum(a, b), jnp.minimum(a, b), jnp.maximum(c, d), jnp.minimum(c, d)
        gscore.append(jnp.maximum(hi1, hi2) + jnp.maximum(jnp.minimum(hi1, hi2), jnp.maximum(lo1, lo2)))
    one = lambda cond: jnp.where(cond, 1.0, 0.0)
    picked, chosen = [], []
    for g in range(N_GROUPS):
        best = None
        for g2 in range(N_GROUPS):
            if g2 == g:
                continue
            ok = one((gscore[g] > gscore[g2]) if g2 < g else (gscore[g] >= gscore[g2]))
            best = ok if best is None else best * ok
        for i in range(EXPERTS_PER_GROUP):
            e = 4 * g + i
            beaten = None
            for j in range(EXPERTS_PER_GROUP):
                if j == i:
                    continue
                e2 = 4 * g + j
                lost = one((v[e2] > v[e]) if j > i else (v[e2] >= v[e]))
                beaten = lost if beaten is None else beaten + lost
            chosen.append(best * one(beaten < 1.5))
            picked.append(chosen[-1] * s[e])
    den = picked[0]
    for e in range(1, N_EXPERTS):
        den = den + picked[e]
    return jnp.concatenate([p / den for p in picked] + chosen, axis=0)


def _mix_kernel(xc_ref, xl_ref, oac_ref, oal_ref, ogc_ref, ogl_ref, sa_ref, sb_ref, mod_ref, wpa_f, wpb_f, wo_f, n2_ref,
                rw_ref, rb_ref, x1_out, h2_out, gt_out, wpa_ref, wpb_ref, wo_ref):
    @pl.when(pl.program_id(0) == 0)
    def _():
        wpa_ref[...] = wpa_f[...].astype(BF16)
        wpb_ref[...] = wpb_f[...].astype(BF16)
        wo_ref[...] = wo_f[...].astype(BF16)

    is_ctx = pl.program_id(0) < N_CTX // TM
    oa = jnp.where(is_ctx, oac_ref[...], oal_ref[...])
    og = jnp.where(is_ctx, ogc_ref[...], ogl_ref[...])
    a = _dot(oa, wpa_ref[...])
    b = _dot(og, wpb_ref[...])
    merged = sa_ref[...].astype(F32) * a + sb_ref[...].astype(F32) * b
    g1 = mod_ref[:, 2 * D_MODEL:3 * D_MODEL]
    sh2 = mod_ref[:, 3 * D_MODEL:4 * D_MODEL]
    sc2 = mod_ref[:, 4 * D_MODEL:5 * D_MODEL]
    x = jnp.where(is_ctx, xc_ref[...], xl_ref[...])
    x1 = x + g1 * _dot(merged.astype(BF16), wo_ref[...])
    x1_out[...] = x1
    h2 = _rms(x1) * (n2_ref[...] * (1.0 + sc2)) + sh2
    _to_row_tiles(h2_out, h2)
    hh, hl = _split2(h2)
    rh, rl = rw_ref[0], rw_ref[1]
    nt = lambda r, t: lax.dot_general(r, t, _NT, preferred_element_type=F32)
    logits = nt(rl, hh) + nt(rh, hl) + nt(rh, hh)
    gt_out[...] = _route(jax.nn.sigmoid(logits), rb_ref[...])


def _mix(layer, x_pair, oa_c, oa_l, og_c, og_l, sa, sb, mod, wpa, wpb, wo, norm2, rw3, rb):
    nt = N_TOK // TM
    n_ctx_t = N_CTX // TM
    row = lambda i: (i, 0)
    const = lambda i: (0, 0)
    ctx_row = lambda i: (jnp.minimum(i, n_ctx_t - 1), 0)
    lat_row = lambda i: (jnp.maximum(i - n_ctx_t, 0), 0)
    return pl.pallas_call(
        _mix_kernel,
        grid=(nt,),
        in_specs=_x_specs(x_pair) + [
            pl.BlockSpec((TM, ATTN_WIDTH), ctx_row),
            pl.BlockSpec((TM, ATTN_WIDTH), lat_row),
            pl.BlockSpec((TM, GLA_VW), ctx_row),
            pl.BlockSpec((TM, GLA_VW), lat_row),
            pl.BlockSpec((TM, D_MODEL), row),
            pl.BlockSpec((TM, D_MODEL), row),
            _mod_spec(layer),
            pl.BlockSpec((None, ATTN_WIDTH, D_MODEL), lambda i: (layer, 0, 0), pipeline_mode=pl.Buffered(1)),
            pl.BlockSpec((None, GLA_VW, D_MODEL), lambda i: (layer, 0, 0), pipeline_mode=pl.Buffered(1)),
            pl.BlockSpec((None, D_MODEL, D_MODEL), lambda i: (layer, 0, 0), pipeline_mode=pl.Buffered(1)),
            _layer_row(layer, D_MODEL),
            pl.BlockSpec((2, N_EXPERTS, D_MODEL), lambda i: (0, 0, 0)),
            pl.BlockSpec((N_EXPERTS, 1), const),
        ],
        out_specs=[pl.BlockSpec((TM, D_MODEL), row), pl.BlockSpec((TM * ROW_TILE, LANES), row),
                   pl.BlockSpec((2 * N_EXPERTS, TM), lambda i: (0, i))],
        out_shape=[jax.ShapeDtypeStruct((N_TOK, D_MODEL), F32),
                   jax.ShapeDtypeStruct((N_TOK * ROW_TILE, LANES), F32),
                   jax.ShapeDtypeStruct((2 * N_EXPERTS, N_TOK), F32)],
        scratch_shapes=[pltpu.VMEM((ATTN_WIDTH, D_MODEL), BF16), pltpu.VMEM((GLA_VW, D_MODEL), BF16),
                        pltpu.VMEM((D_MODEL, D_MODEL), BF16)],
        compiler_params=pltpu.CompilerParams(dimension_semantics=("arbitrary",), vmem_limit_bytes=VMEM_LIMIT),
        name="mix",
    )(*x_pair, oa_c, oa_l, og_c, og_l, sa, sb, mod, wpa, wpb, wo, norm2, rw3, rb)


def _route_plan(gsel):
    gates = gsel[:N_EXPERTS]
    sel = gsel[N_EXPERTS:] > 0.5
    seli = sel.astype(jnp.int32)
    incl = jnp.cumsum(seli, axis=1)
    count = incl[:, -1]
    ntile = (count + TS - 1) // TS
    tile_end = jnp.cumsum(ntile)
    tile_start = tile_end - ntile
    total = tile_end[-1]
    slot = tile_start[:, None] * TS + incl - seli
    pos_a = jnp.min(jnp.where(sel, slot, N_SLOT), axis=0)
    pos_b = jnp.max(jnp.where(sel, slot, -1), axis=0)
    w_a = jnp.sum(jnp.where(sel & (slot == pos_a[None, :]), gates, 0.0), axis=0)
    w_b = jnp.sum(jnp.where(sel & (slot == pos_b[None, :]), gates, 0.0), axis=0)
    nt = N_TOK // TM
    pos = jnp.concatenate([pos_a.reshape(nt, 1, TM), pos_b.reshape(nt, 1, TM)], axis=2).astype(jnp.int32)
    plan = jnp.concatenate([tile_start, ntile, total[None]]).astype(jnp.int32)
    return pos, jnp.stack([w_a, w_b], axis=1), plan


def _to_row_tiles(ref, x):
    for k in range(ROW_TILE):
        ref[pl.ds(k, x.shape[0], stride=ROW_TILE), :] = x[:, k * LANES:(k + 1) * LANES]


def _from_row_tiles(ref, rows):
    return jnp.concatenate([ref[pl.ds(k, rows, stride=ROW_TILE), :] for k in range(ROW_TILE)], axis=1)


def _tile_rows(row, n=1):
    return pl.ds(pl.multiple_of(row * ROW_TILE, ROW_TILE), n * ROW_TILE)


def _row_copy(src, src_row, dst, dst_row, sem):
    return pltpu.make_async_copy(src.at[_tile_rows(src_row), :], dst.at[_tile_rows(dst_row), :], sem)


def _dispatch_kernel(plan_ref, pos_ref, h_ref, xs_ref, zero_ref, sem_z, sem):
    i = pl.program_id(0)
    total = plan_ref[PLAN_TOTAL]

    def zero_copy(tile):
        return pltpu.make_async_copy(zero_ref, xs_ref.at[_tile_rows(tile * TS, TS), :], sem_z)

    def last_tile(e):
        return plan_ref[e] + plan_ref[N_EXPERTS + e] - 1

    @pl.when(i == 0)
    def _():
        zero_ref[...] = jnp.zeros(zero_ref.shape, F32)

        def each(fn):
            def expert(e, c):
                @pl.when(plan_ref[N_EXPERTS + e] > 0)
                def _():
                    fn(zero_copy(last_tile(e)))
                return c

            def unused(t, c):
                @pl.when(t >= total)
                def _():
                    fn(zero_copy(t))
                return c

            lax.fori_loop(0, N_EXPERTS, expert, 0)
            lax.fori_loop(0, N_TILE, unused, 0)

        each(lambda cp: cp.start())
        each(lambda cp: cp.wait())

    def start(r, c):
        _row_copy(h_ref, r, xs_ref, pos_ref[0, r], sem).start()
        _row_copy(h_ref, r, xs_ref, pos_ref[0, TM + r], sem).start()
        return c

    def wait(r, c):
        _row_copy(h_ref, 0, xs_ref, 0, sem).wait()
        return c

    lax.fori_loop(0, TM, start, 0, unroll=8)
    lax.fori_loop(0, 2 * TM, wait, 0, unroll=8)


def _dispatch(plan, pos, h2):
    return pl.pallas_call(
        _dispatch_kernel,
        grid_spec=pltpu.PrefetchScalarGridSpec(
            num_scalar_prefetch=1,
            grid=(N_TOK // TM,),
            in_specs=[
                pl.BlockSpec((None, 1, 2 * TM), lambda i, plan: (i, 0, 0), memory_space=pltpu.SMEM),
                pl.BlockSpec((TM * ROW_TILE, LANES), lambda i, plan: (i, 0)),
            ],
            out_specs=pl.BlockSpec(memory_space=pl.ANY),
            scratch_shapes=[pltpu.VMEM((TS * ROW_TILE, LANES), F32), pltpu.SemaphoreType.DMA,
                            pltpu.SemaphoreType.DMA],
        ),
        out_shape=jax.ShapeDtypeStruct((N_SLOT * ROW_TILE, LANES), F32),
        compiler_params=pltpu.CompilerParams(dimension_semantics=("arbitrary",), vmem_limit_bytes=VMEM_LIMIT),
        name="dispatch",
    )(plan, pos, h2)


def _moe_kernel(plan_ref, xs_ref, wg_ref, wu_ref, wd_ref, ys_ref, x_buf, y_buf, zero_ref, sem_in, sem_out, sem_z,
                wg_s, wu_s, wd_s):
    e = pl.program_id(0)
    total = plan_ref[PLAN_TOTAL]
    first = plan_ref[e]

    def in_copy(g):
        return pltpu.make_async_copy(xs_ref.at[_tile_rows(g * TS, TS), :], x_buf.at[g % 2], sem_in.at[g % 2])

    def out_copy(g):
        return pltpu.make_async_copy(y_buf.at[g % 2], ys_ref.at[_tile_rows(g * TS, TS), :], sem_out.at[g % 2])

    def zero_copy(g):
        return pltpu.make_async_copy(zero_ref, ys_ref.at[_tile_rows(g * TS, TS), :], sem_z)

    @pl.when(e == 0)
    def _():
        in_copy(0).start()

    wg_s[...] = wg_ref[...].astype(BF16)
    wu_s[...] = wu_ref[...].astype(BF16)
    wd_s[...] = wd_ref[...].astype(BF16)

    def tile(t, c):
        g = first + t
        slot = g % 2

        @pl.when(g + 1 < total)
        def _():
            in_copy(g + 1).start()

        in_copy(g).wait()

        @pl.when(g >= 2)
        def _():
            out_copy(g - 2).wait()

        x = _from_row_tiles(x_buf.at[slot], TS).astype(BF16)
        up = _dot(x, wu_s[...])
        gt = _dot(x, wg_s[...])
        act = (gt * jax.nn.sigmoid(gt) * up).astype(BF16)
        _to_row_tiles(y_buf.at[slot], _dot(act, wd_s[...]))
        out_copy(g).start()
        return c

    lax.fori_loop(0, plan_ref[N_EXPERTS + e], tile, 0)

    @pl.when(e == N_EXPERTS - 1)
    def _():
        out_copy(total - 1).wait()

        @pl.when(total >= 2)
        def _():
            out_copy(total - 2).wait()

        zero_ref[...] = jnp.zeros(zero_ref.shape, F32)

        def each(fn):
            def unused(g, c):
                @pl.when(g >= total)
                def _():
                    fn(zero_copy(g))
                return c
            lax.fori_loop(0, N_TILE, unused, 0)

        each(lambda cp: cp.start())
        each(lambda cp: cp.wait())


def _moe(layer, plan, xs, w_gate, w_up, w_down):
    w_spec = lambda a, b: pl.BlockSpec((None, None, a, b), lambda e, plan: (layer, e, 0, 0))
    tile_buf = lambda n: pltpu.VMEM((n, TS * ROW_TILE, LANES), F32)
    return pl.pallas_call(
        _moe_kernel,
        grid_spec=pltpu.PrefetchScalarGridSpec(
            num_scalar_prefetch=1,
            grid=(N_EXPERTS,),
            in_specs=[
                pl.BlockSpec(memory_space=pl.ANY),
                w_spec(D_MODEL, D_EXPERT), w_spec(D_MODEL, D_EXPERT), w_spec(D_EXPERT, D_MODEL),
            ],
            out_specs=pl.BlockSpec(memory_space=pl.ANY),
            scratch_shapes=[tile_buf(2), tile_buf(2), pltpu.VMEM((TS * ROW_TILE, LANES), F32),
                            pltpu.SemaphoreType.DMA((2,)), pltpu.SemaphoreType.DMA((2,)), pltpu.SemaphoreType.DMA,
                            pltpu.VMEM((D_MODEL, D_EXPERT), BF16), pltpu.VMEM((D_MODEL, D_EXPERT), BF16),
                            pltpu.VMEM((D_EXPERT, D_MODEL), BF16)],
        ),
        out_shape=jax.ShapeDtypeStruct((N_SLOT * ROW_TILE, LANES), F32),
        compiler_params=pltpu.CompilerParams(dimension_semantics=("arbitrary",), vmem_limit_bytes=VMEM_LIMIT),
        name="moe",
    )(plan, xs, w_gate, w_up, w_down)


def _combine_kernel(pos_ref, nxt_ref, w_ref, x1_ref, mod_ref, ys_ref, out_ref, ya_ref, yb_ref, sem, *, n):
    i = pl.program_id(0)

    def gather(p_ref, slot):
        def start(r, c):
            _row_copy(ys_ref, p_ref[0, r], ya_ref.at[slot], r, sem.at[slot]).start()
            _row_copy(ys_ref, p_ref[0, TM + r], yb_ref.at[slot], r, sem.at[slot]).start()
            return c
        lax.fori_loop(0, TM, start, 0, unroll=8)

    @pl.when(i == 0)
    def _():
        gather(pos_ref, 0)

    @pl.when(i + 1 < n)
    def _():
        gather(nxt_ref, (i + 1) % 2)

    slot = i % 2

    def wait(r, c):
        _row_copy(ys_ref, 0, ya_ref.at[slot], 0, sem.at[slot]).wait()
        return c

    lax.fori_loop(0, 2 * TM, wait, 0, unroll=8)
    g2 = mod_ref[:, 5 * D_MODEL:6 * D_MODEL]
    w = w_ref[...]
    ya = _from_row_tiles(ya_ref.at[slot], TM)
    yb = _from_row_tiles(yb_ref.at[slot], TM)
    out_ref[...] = x1_ref[...] + g2 * (w[:, 0:1] * ya + w[:, 1:2] * yb)


def _combine(layer, pos, w_ab, x1, mod, ys, tile0, nt):
    pos_spec = lambda fn: pl.BlockSpec((None, 1, 2 * TM), fn, memory_space=pltpu.SMEM)
    return pl.pallas_call(
        partial(_combine_kernel, n=nt),
        grid=(nt,),
        in_specs=[
            pos_spec(lambda i: (tile0 + i, 0, 0)),
            pos_spec(lambda i: (tile0 + jnp.minimum(i + 1, nt - 1), 0, 0)),
            pl.BlockSpec((TM, 2), lambda i: (tile0 + i, 0)),
            pl.BlockSpec((TM, D_MODEL), lambda i: (tile0 + i, 0)),
            _mod_spec(layer, tile0),
            pl.BlockSpec(memory_space=pl.ANY),
        ],
        out_specs=pl.BlockSpec((TM, D_MODEL), lambda i: (i, 0)),
        out_shape=jax.ShapeDtypeStruct((nt * TM, D_MODEL), F32),
        scratch_shapes=[pltpu.VMEM((2, TM * ROW_TILE, LANES), F32), pltpu.VMEM((2, TM * ROW_TILE, LANES), F32),
                        pltpu.SemaphoreType.DMA((2,))],
        compiler_params=pltpu.CompilerParams(dimension_semantics=("arbitrary",), vmem_limit_bytes=VMEM_LIMIT),
        name="combine",
    )(pos, pos, w_ab, x1, mod, ys)


def _rope_tables():
    pos = np.arange(DEC_SEQ)
    rows = (pos // GRID_W).astype(np.float64)
    cols = (pos % GRID_W).astype(np.float64)
    half = HEAD_DIM // 2
    inv_freq = ROPE_THETA ** (-np.arange(0, half, 2, dtype=np.float64) / half)
    ang_r = rows[:, None] * inv_freq[None, :]
    ang_c = cols[:, None] * inv_freq[None, :]
    cos_h = np.concatenate([np.cos(ang_r), np.cos(ang_r), np.cos(ang_c), np.cos(ang_c)], axis=-1)
    sin_h = np.concatenate([-np.sin(ang_r), np.sin(ang_r), -np.sin(ang_c), np.sin(ang_c)], axis=-1)
    cos = np.concatenate([np.ones((TM, HEAD_DIM)), cos_h], axis=0)
    sin = np.concatenate([np.zeros((TM, HEAD_DIM)), sin_h], axis=0)
    return (jnp.asarray(np.tile(cos, (1, N_HEADS)), F32), jnp.asarray(np.tile(sin, (1, N_HEADS)), F32))


def _pack_w2(w2):
    z = jnp.zeros((DEPTH, GLA_RANK, GLA_KW), w2.dtype)
    top = jnp.concatenate([w2[:, 0], z], axis=2)
    bot = jnp.concatenate([z, w2[:, 1]], axis=2)
    pad = jnp.zeros((DEPTH, LR_PAD - 2 * GLA_RANK, 2 * GLA_KW), w2.dtype)
    return jnp.concatenate([top, bot, pad], axis=1).astype(BF16)


def kernel(x_prompt, x_sample, cache_k, cache_v, state_gla, c, c_ctx, norm1, norm2, w_ada, b_ada, w_in,
           q_norm, k_norm, gla_w2, gla_b, gla_norm, w_pa, w_pb, w_out, router_w, router_bias,
           w_gate, w_up, w_down):
    x_pair = (x_prompt.reshape(N_CTX, D_MODEL), x_sample.reshape(N_LAT, D_MODEL))
    cond = jnp.concatenate([c_ctx[None, :], c, jnp.zeros((N_COND - 1 - DEC_BATCH, D_MODEL), F32)], axis=0)
    mod = _ada(cond, w_ada, b_ada).reshape(DEPTH, N_COND, 1, MOD_W)
    cos, sin = _rope_tables()
    gla_masks = _gla_masks()
    w_in_t = jnp.swapaxes(w_in, 1, 2)
    lane_head = np.arange(ATTN_WIDTH) // HEAD_DIM
    bd = jnp.asarray(lane_head[:, None] == lane_head[None, :], BF16)
    rw3 = jnp.stack(_split2(router_w.T), axis=0)
    rb = router_bias.reshape(N_EXPERTS, 1)
    ck = cache_k.reshape(DEC_BATCH, DEPTH, PAST_LEN, KV_WIDTH)
    cv = cache_v.reshape(DEC_BATCH, DEPTH, PAST_LEN, KV_WIDTH)
    rows = lambda p, reps: jnp.tile(p, (1, reps)).reshape(DEPTH, 1, -1)
    n1, n2 = rows(norm1, 1), rows(norm2, 1)
    qn, kn, gn = rows(q_norm, N_HEADS), rows(k_norm, KV_HEADS), rows(gla_norm, GLA_HEADS)
    gb = gla_b.reshape(DEPTH, 1, 2 * GLA_KW)
    w2p = _pack_w2(gla_w2)

    keys, vals, states = [], [], []
    for l in range(DEPTH):
        q, kf, kr, v, gq, gk, gv, gg, la, sa, sb = _inproj(l, x_pair, mod, n1, w_in_t, qn, kn, cos, sin, bd, w2p, gb)
        oa_c = _attn_ctx(q, kr, v)
        oa_l = _attn_lat(l, q, kr, v, ck, cv)
        og_c, s_fin = _gla_ctx(l, gq, gk, gv, la, gg, gn, gla_masks)
        og_l = _gla_lat(l, gq, gk, gv, la, gg, gn, gla_masks, state_gla)
        x1, h2, gsel = _mix(l, x_pair, oa_c, oa_l, og_c, og_l, sa, sb, mod, w_pa, w_pb, w_out, n2, rw3, rb)
        pos, w_ab, plan = _route_plan(gsel)
        xs = _dispatch(plan, pos, h2)
        ys = _moe(l, plan, xs, w_gate, w_up, w_down)
        n_ctx_t = N_CTX // TM
        if l + 1 < DEPTH:
            x = _combine(l, pos, w_ab, x1, mod, ys, 0, N_TOK // TM)
            x_pair = (x, x)
        else:
            x_pair = (_combine(l, pos, w_ab, x1, mod, ys, 0, n_ctx_t),
                      _combine(l, pos, w_ab, x1, mod, ys, n_ctx_t, N_LAT // TM))
        keys.append(kf[:N_CTX].reshape(BATCH, SEQ, KV_HEADS, HEAD_DIM))
        vals.append(v[:N_CTX].reshape(BATCH, SEQ, KV_HEADS, HEAD_DIM))
        states.append(s_fin)
    y_prompt = x_pair[0].reshape(BATCH, SEQ, D_MODEL)
    y_sample = x_pair[1].reshape(DEC_BATCH, DEC_SEQ, D_MODEL)
    return (y_prompt, y_sample, jnp.stack(keys, axis=1), jnp.stack(vals, axis=1), jnp.stack(states, axis=1))
```

```python
from functools import partial

import jax
import jax.numpy as jnp
import numpy as np
from jax import lax
from jax.experimental import pallas as pl
from jax.experimental.pallas import tpu as pltpu

F32 = jnp.float32
BF16 = jnp.bfloat16

D_MODEL = 1024
BATCH = 16
SEQ = 256
DEPTH = 2
DEC_BATCH = 4
DEC_SEQ = 1024
PAST_LEN = 512
GRID_W = 64
N_HEADS = 8
KV_HEADS = 2
GROUP = N_HEADS // KV_HEADS
HEAD_DIM = 64
ATTN_WIDTH = N_HEADS * HEAD_DIM
KV_WIDTH = KV_HEADS * HEAD_DIM
ROPE_THETA = 10000.0
GLA_HEADS = 4
GLA_DK = 64
GLA_DV = 128
GLA_KW = GLA_HEADS * GLA_DK
GLA_VW = GLA_HEADS * GLA_DV
GLA_RANK = 16
GLA_GATE_NORM = 16.0
N_EXPERTS = 16
N_GROUPS = 4
EXPERTS_PER_GROUP = N_EXPERTS // N_GROUPS
D_EXPERT = 512
NORM_EPS = 1e-6

N_CTX = BATCH * SEQ
N_LAT = DEC_BATCH * DEC_SEQ
N_TOK = N_CTX + N_LAT
N_COND = 8
MOD_W = 6 * D_MODEL

TM = 512
TS = 512
N_TILE = 2 * N_TOK // TS + N_EXPERTS
N_SLOT = N_TILE * TS
PLAN_TOTAL = 3 * N_EXPERTS
Z_ROWS = 64
ROW_TILE = 8
LANES = D_MODEL // ROW_TILE
GLA_CHUNK = 128
GLA_LEVELS = (64, 32, 16)
GLA_DIAG = 16
LR_PAD = 128
W_ROWS = 128
SEG_MIX = ATTN_WIDTH + 2 * KV_WIDTH + 2 * GLA_KW + 2 * GLA_VW
OFF_GA = SEG_MIX
OFF_GB = OFF_GA + D_MODEL
OFF_LR = OFF_GB + D_MODEL
IN_PACKED = OFF_LR + LR_PAD
IN_WIDTH = SEG_MIX + 2 * GLA_RANK + 2 * D_MODEL
VMEM_LIMIT = 56 * 1024 * 1024

_NT = (((1,), (1,)), ((), ()))
_TN = (((0,), (0,)), ((), ()))


def _mod_row(tile, tm):
    start = tile * tm
    return jnp.where(start < N_CTX, 0, 1 + (start - N_CTX) // DEC_SEQ)


def _split2(x):
    hi = x.astype(BF16)
    lo = (x - hi.astype(F32)).astype(BF16)
    return hi, lo


def _dot(a, b):
    return jnp.dot(a, b, preferred_element_type=F32)


def _rms(x):
    return x * lax.rsqrt(jnp.mean(x * x, axis=-1, keepdims=True) + NORM_EPS)


def _ada_kernel(cond_ref, w_ref, b_ref, out_ref):
    c = cond_ref[...]
    s = c * jax.nn.sigmoid(c)
    out_ref[...] = _dot(s.astype(BF16), w_ref[...].astype(BF16)) + b_ref[...]


def _ada(cond, w_ada, b_ada):
    nb = MOD_W // D_MODEL
    return pl.pallas_call(
        _ada_kernel,
        grid=(DEPTH, nb),
        in_specs=[
            pl.BlockSpec((N_COND, D_MODEL), lambda l, j: (0, 0)),
            pl.BlockSpec((None, D_MODEL, D_MODEL), lambda l, j: (l, 0, j)),
            pl.BlockSpec((None, 1, D_MODEL), lambda l, j: (l, 0, j)),
        ],
        out_specs=pl.BlockSpec((None, N_COND, D_MODEL), lambda l, j: (l, 0, j)),
        out_shape=jax.ShapeDtypeStruct((DEPTH, N_COND, MOD_W), F32),
        compiler_params=pltpu.CompilerParams(vmem_limit_bytes=VMEM_LIMIT),
        name="ada",
    )(cond, w_ada, b_ada.reshape(DEPTH, 1, MOD_W))


def _head_rmsnorm(x, bd):
    ssq = _dot((x * x).astype(BF16), bd)
    return x * lax.rsqrt(ssq * (1.0 / HEAD_DIM) + NORM_EPS)


def _rope(x, cos, sin):
    n = x.shape[-1]
    lane = lax.broadcasted_iota(jnp.int32, x.shape, 1)
    first = (lane & (HEAD_DIM // 4)) == 0
    partner = jnp.where(first, pltpu.roll(x, n - HEAD_DIM // 4, 1), pltpu.roll(x, HEAD_DIM // 4, 1))
    return x * cos + partner * sin


def _pack_w_in(w_ref, wp_ref):
    lr0 = SEG_MIX
    ga0 = lr0 + 2 * GLA_RANK

    def cast(dst, src, n):
        def body(i, carry):
            rows_out = pl.ds(pl.multiple_of(dst + i * W_ROWS, W_ROWS), W_ROWS)
            rows_in = pl.ds(pl.multiple_of(src + i * W_ROWS, 2 * GLA_RANK), W_ROWS)
            wp_ref[rows_out, :] = w_ref[rows_in, :].astype(BF16)
            return carry
        lax.fori_loop(0, n // W_ROWS, body, 0)

    cast(0, 0, SEG_MIX)
    cast(OFF_GA, ga0, 2 * D_MODEL)
    wp_ref[OFF_LR:IN_PACKED, :] = jnp.concatenate(
        [w_ref[lr0:ga0, :], jnp.zeros((LR_PAD - 2 * GLA_RANK, D_MODEL), F32)], axis=0).astype(BF16)


def _inproj_kernel(xc_ref, xl_ref, mod_ref, n1_ref, wf_ref, qn_ref, kn_ref, cos_ref, sin_ref, bd_ref, w2_ref, gb_ref,
                   q_out, kf_out, kr_out, v_out, gq_out, gk_out, gv_out, gg_out, la_out, sa_out, sb_out,
                   w_ref):
    @pl.when(pl.program_id(0) == 0)
    def _():
        _pack_w_in(wf_ref, w_ref)

    x = jnp.where(pl.program_id(0) < N_CTX // TM, xc_ref[...], xl_ref[...])
    sh1 = mod_ref[:, 0:D_MODEL]
    sc1 = mod_ref[:, D_MODEL:2 * D_MODEL]
    h = (_rms(x) * (n1_ref[...] * (1.0 + sc1)) + sh1).astype(BF16)

    def proj(lo, width):
        return lax.dot_general(h, w_ref[lo:lo + width, :], _NT, preferred_element_type=F32)

    cos = cos_ref[...]
    sin = sin_ref[...]
    bd = bd_ref[...]
    off = 0
    q = _head_rmsnorm(proj(off, ATTN_WIDTH), bd) * qn_ref[...]
    q_out[...] = (_rope(q, cos, sin) * HEAD_DIM ** -0.5).astype(BF16)
    off += ATTN_WIDTH
    k = _head_rmsnorm(proj(off, KV_WIDTH), bd[:KV_WIDTH, :KV_WIDTH]) * kn_ref[...]
    kf_out[...] = k
    kr_out[...] = _rope(k, cos[:, :KV_WIDTH], sin[:, :KV_WIDTH]).astype(BF16)
    off += KV_WIDTH
    v_out[...] = proj(off, KV_WIDTH)
    off += KV_WIDTH
    gq_out[...] = (proj(off, GLA_KW) * GLA_DK ** -0.5).astype(BF16)
    off += GLA_KW
    gk_out[...] = proj(off, GLA_KW).astype(BF16)
    off += GLA_KW
    gv_out[...] = proj(off, GLA_VW).astype(BF16)
    off += GLA_VW
    gg = proj(off, GLA_VW)
    gg_out[...] = (gg * jax.nn.sigmoid(gg)).astype(BF16)
    sa_out[...] = jax.nn.sigmoid(proj(OFF_GA, D_MODEL)).astype(BF16)
    sb_out[...] = jax.nn.sigmoid(proj(OFF_GB, D_MODEL)).astype(BF16)
    lr = proj(OFF_LR, LR_PAD)
    z = _dot(lr.astype(BF16), w2_ref[...]) + gb_ref[...]
    log_sig = jnp.minimum(z, 0.0) - jnp.log1p(jnp.exp(-jnp.abs(z)))
    la_out[...] = log_sig * (1.0 / GLA_GATE_NORM)


def _layer_row(layer, width):
    return pl.BlockSpec((None, 1, width), lambda *_: (layer, 0, 0))


def _mod_spec(layer, tile0=0):
    return pl.BlockSpec((None, None, 1, MOD_W), lambda i, *_: (layer, _mod_row(tile0 + i, TM), 0, 0))


def _x_specs(x_pair):
    n_ctx_t = N_CTX // TM
    lat0 = n_ctx_t if x_pair[0] is x_pair[1] else 0
    return [pl.BlockSpec((TM, D_MODEL), lambda i: (jnp.minimum(i, n_ctx_t - 1), 0)),
            pl.BlockSpec((TM, D_MODEL), lambda i: (lat0 + jnp.maximum(i - n_ctx_t, 0), 0))]


def _inproj(layer, x_pair, mod, norm1, w_in, qn, kn, cos, sin, bd, w2p, gb):
    nt = N_TOK // TM
    n_ctx_t = N_CTX // TM
    t_per_seq = DEC_SEQ // TM
    row = lambda i: (i, 0)
    const = lambda i: (0, 0)
    rope_blk = lambda i: (jnp.where(i < n_ctx_t, 0, 1 + (i - n_ctx_t) % t_per_seq), 0)

    def out(width, dtype):
        return pl.BlockSpec((TM, width), row), jax.ShapeDtypeStruct((N_TOK, width), dtype)

    outs = [out(ATTN_WIDTH, BF16), out(KV_WIDTH, F32), out(KV_WIDTH, BF16), out(KV_WIDTH, F32),
            out(GLA_KW, BF16), out(GLA_KW, BF16), out(GLA_VW, BF16), out(GLA_VW, BF16),
            out(2 * GLA_KW, F32), out(D_MODEL, BF16), out(D_MODEL, BF16)]
    return pl.pallas_call(
        _inproj_kernel,
        grid=(nt,),
        in_specs=_x_specs(x_pair) + [
            _mod_spec(layer),
            _layer_row(layer, D_MODEL),
            pl.BlockSpec((None, IN_WIDTH, D_MODEL), lambda i: (layer, 0, 0), pipeline_mode=pl.Buffered(1)),
            _layer_row(layer, ATTN_WIDTH),
            _layer_row(layer, KV_WIDTH),
            pl.BlockSpec((TM, ATTN_WIDTH), rope_blk),
            pl.BlockSpec((TM, ATTN_WIDTH), rope_blk),
            pl.BlockSpec((ATTN_WIDTH, ATTN_WIDTH), const),
            pl.BlockSpec((None, LR_PAD, 2 * GLA_KW), lambda i: (layer, 0, 0)),
            _layer_row(layer, 2 * GLA_KW),
        ],
        out_specs=[o[0] for o in outs],
        out_shape=[o[1] for o in outs],
        scratch_shapes=[pltpu.VMEM((IN_PACKED, D_MODEL), BF16)],
        compiler_params=pltpu.CompilerParams(dimension_semantics=("arbitrary",), vmem_limit_bytes=VMEM_LIMIT),
        name="inproj",
    )(*x_pair, mod, norm1, w_in, qn, kn, cos, sin, bd, w2p, gb)


def _attn_kernel(*refs, has_cache):
    if has_cache:
        q_ref, k_ref, v_ref, ck_ref, cv_ref, o_ref = refs
    else:
        q_ref, k_ref, v_ref, o_ref = refs
    tq = q_ref.shape[0]
    for g in range(KV_HEADS):
        ks = slice(g * HEAD_DIM, (g + 1) * HEAD_DIM)
        qs = jnp.concatenate(
            [q_ref[:, (g * GROUP + j) * HEAD_DIM:(g * GROUP + j + 1) * HEAD_DIM] for j in range(GROUP)], axis=0)
        k = k_ref[:, ks]
        v = v_ref[:, ks].astype(BF16)
        s = lax.dot_general(qs, k, _NT, preferred_element_type=F32)
        m = jnp.max(s, axis=-1, keepdims=True)
        if has_cache:
            ck = ck_ref[:, ks].astype(BF16)
            cv = cv_ref[:, ks].astype(BF16)
            s2 = lax.dot_general(qs, ck, _NT, preferred_element_type=F32)
            m = jnp.maximum(m, jnp.max(s2, axis=-1, keepdims=True))
        p = jnp.exp(s - m)
        den = jnp.sum(p, axis=-1, keepdims=True)
        o = _dot(p.astype(BF16), v)
        if has_cache:
            p2 = jnp.exp(s2 - m)
            den = den + jnp.sum(p2, axis=-1, keepdims=True)
            o = o + _dot(p2.astype(BF16), cv)
        o = o / den
        for j in range(GROUP):
            hd = g * GROUP + j
            o_ref[:, hd * HEAD_DIM:(hd + 1) * HEAD_DIM] = o[j * tq:(j + 1) * tq, :].astype(BF16)


def _attn_ctx(q, kr, v):
    return pl.pallas_call(
        partial(_attn_kernel, has_cache=False),
        grid=(BATCH,),
        in_specs=[
            pl.BlockSpec((SEQ, ATTN_WIDTH), lambda i: (i, 0)),
            pl.BlockSpec((SEQ, KV_WIDTH), lambda i: (i, 0)),
            pl.BlockSpec((SEQ, KV_WIDTH), lambda i: (i, 0)),
        ],
        out_specs=pl.BlockSpec((SEQ, ATTN_WIDTH), lambda i: (i, 0)),
        out_shape=jax.ShapeDtypeStruct((N_CTX, ATTN_WIDTH), BF16),
        compiler_params=pltpu.CompilerParams(dimension_semantics=("parallel",), vmem_limit_bytes=VMEM_LIMIT),
        name="attn_ctx",
    )(q, kr, v)


def _attn_lat(layer, q, kr, v, cache_k, cache_v):
    tq = 256
    nq = DEC_SEQ // tq
    q0 = N_CTX // tq
    s0 = N_CTX // DEC_SEQ
    return pl.pallas_call(
        partial(_attn_kernel, has_cache=True),
        grid=(DEC_BATCH, nq),
        in_specs=[
            pl.BlockSpec((tq, ATTN_WIDTH), lambda b, i: (q0 + b * nq + i, 0)),
            pl.BlockSpec((DEC_SEQ, KV_WIDTH), lambda b, i: (s0 + b, 0)),
            pl.BlockSpec((DEC_SEQ, KV_WIDTH), lambda b, i: (s0 + b, 0)),
            pl.BlockSpec((None, None, PAST_LEN, KV_WIDTH), lambda b, i: (b, layer, 0, 0)),
            pl.BlockSpec((None, None, PAST_LEN, KV_WIDTH), lambda b, i: (b, layer, 0, 0)),
        ],
        out_specs=pl.BlockSpec((tq, ATTN_WIDTH), lambda b, i: (b * nq + i, 0)),
        out_shape=jax.ShapeDtypeStruct((N_LAT, ATTN_WIDTH), BF16),
        compiler_params=pltpu.CompilerParams(dimension_semantics=("parallel", "parallel"),
                                             vmem_limit_bytes=VMEM_LIMIT),
        name="attn_lat",
    )(q, kr, v, cache_k, cache_v)


def _gla_masks():
    c = GLA_CHUNK
    hrow = np.arange(GLA_HEADS * c) // c
    head_k = (hrow[:, None] == (np.arange(GLA_KW) // GLA_DK)[None, :])
    head_v = (hrow[:, None] == (np.arange(GLA_VW) // GLA_DV)[None, :])
    t = np.arange(c)[:, None]
    s = (np.arange(GLA_HEADS * c) % c)[None, :]
    pair = []
    for reverse in (False, True):
        for m in GLA_LEVELS:
            pair.append(t // (2 * m) == s // (2 * m))
        pair.append((t // GLA_DIAG == s // GLA_DIAG) & ((s >= t) if reverse else (s <= t)))
    return (jnp.asarray(head_k, BF16), jnp.asarray(head_v, BF16), jnp.asarray(np.stack(pair), F32))


def _block_diag(x, head_mask):
    return jnp.concatenate([x] * GLA_HEADS, axis=0) * head_mask


def _ref_rows(b, rows, rep):
    return jnp.concatenate([jnp.broadcast_to(b[r:r + 1, :], (rep, b.shape[1])) for r in rows], axis=0)


def _score_operands(q, k, b, reverse, head_k):
    c = GLA_CHUNK
    t_idx = lax.broadcasted_iota(jnp.int32, (c, GLA_KW), 0)
    terms = []
    for m in GLA_LEVELS:
        nblk = c // (2 * m)
        ref = _ref_rows(b, [2 * m * j + (m if reverse else m - 1) for j in range(nblk)], 2 * m)
        q_side = ((t_idx & m) == 0) if reverse else ((t_idx & m) != 0)
        terms.append((jnp.where(q_side, q * jnp.exp(b - ref), 0.0), jnp.where(q_side, 0.0, k * jnp.exp(ref - b))))
    d = GLA_DIAG
    x = b - _ref_rows(b, [d * j + d // 2 for j in range(c // d)], d)
    terms.append((q * jnp.exp(x), k * jnp.exp(-x)))
    return [(qt.astype(BF16), _block_diag(kt.astype(BF16), head_k)) for qt, kt in terms]


def _score_sum(parts, reverse, pair_ref):
    term0 = (len(GLA_LEVELS) + 1) if reverse else 0
    total = jnp.where(pair_ref[term0 + len(GLA_LEVELS)] > 0.5, parts[-1], 0.0)
    for i, part in enumerate(parts[:-1]):
        total = total + part * pair_ref[term0 + i]
    return total


def _cumsum_rows(tri, la):
    hi, lo = _split2(la)
    return _dot(tri, hi) + _dot(tri, lo)


def _gla_kernel(*refs, seq_len, has_state):
    gq_ref, gk_ref, gv_ref, la_ref, gg_ref, gn_ref, hk_ref, hv_ref, pair_ref = refs[:9]
    if has_state:
        s0_ref, o_ref, acc_ref, accb_ref, st_ref = refs[9:]
    else:
        o_ref, sfin_ref, acc_ref, accb_ref, st_ref = refs[9:]
    c = GLA_CHUNK
    nc = seq_len // c
    r_i = lax.broadcasted_iota(jnp.int32, (c, c), 0)
    c_i = lax.broadcasted_iota(jnp.int32, (c, c), 1)
    tril = jnp.where(c_i <= r_i, 1.0, 0.0).astype(BF16)
    triu = jnp.where(c_i >= r_i, 1.0, 0.0).astype(BF16)
    st_head_r = lax.broadcasted_iota(jnp.int32, (GLA_VW, GLA_KW), 0) // GLA_DV
    st_head_c = lax.broadcasted_iota(jnp.int32, (GLA_VW, GLA_KW), 1) // GLA_DK
    st_mask = st_head_r == st_head_c

    for d in range(2):
        st_ref[d] = jnp.zeros((GLA_VW, GLA_KW), F32)
        if has_state:
            for hh in range(GLA_HEADS):
                st_ref[d, hh * GLA_DV:(hh + 1) * GLA_DV, hh * GLA_DK:(hh + 1) * GLA_DK] = s0_ref[d, hh].T

    def load(n):
        rows = pl.ds(pl.multiple_of(n * c, c), c)
        return (rows, gq_ref[rows, :].astype(F32), gk_ref[rows, :].astype(F32), gv_ref[rows, :])

    nt = lambda a, b: lax.dot_general(a, b, _NT, preferred_element_type=F32)
    tn = lambda a, b: lax.dot_general(a, b, _TN, preferred_element_type=F32)

    def body(n, carry):
        rows, q, k, v = load(n)
        rows_b, q_b, k_b, v_b = load(nc - 1 - n)
        b_f = _cumsum_rows(tril, la_ref[rows, 0:GLA_KW])
        b_r = _cumsum_rows(triu, la_ref[rows, GLA_KW:2 * GLA_KW])
        b_b = _cumsum_rows(triu, la_ref[rows_b, GLA_KW:2 * GLA_KW])
        head_k = hk_ref[...]
        ops_f = _score_operands(q, k, b_f, False, head_k)
        ops_r = _score_operands(q, k, b_r, True, head_k)
        end_f, end_b = b_f[c - 1:c, :], b_b[0:1, :]
        qin_f = (q * jnp.exp(b_f)).astype(BF16)
        qin_b = (q_b * jnp.exp(b_b)).astype(BF16)
        kst_f = (k * jnp.exp(end_f - b_f)).astype(BF16)
        kst_b = (k_b * jnp.exp(end_b - b_b)).astype(BF16)
        st_f, st_b = st_ref[0], st_ref[1]
        parts_f = [nt(qt, kt) for qt, kt in ops_f]
        parts_r = [nt(qt, kt) for qt, kt in ops_r]
        o_f = nt(qin_f, st_f.astype(BF16))
        o_b = nt(qin_b, st_b.astype(BF16))
        upd_f = tn(v, kst_f)
        upd_b = tn(v_b, kst_b)
        a = _score_sum(parts_f, False, pair_ref) + _score_sum(parts_r, True, pair_ref)
        acc_ref[rows, :] = _dot(a.astype(BF16), _block_diag(v, hv_ref[...])) + o_f
        accb_ref[rows_b, :] = o_b
        st_ref[0] = st_f * jnp.exp(end_f) + jnp.where(st_mask, upd_f, 0.0)
        st_ref[1] = st_b * jnp.exp(end_b) + jnp.where(st_mask, upd_b, 0.0)
        return carry

    lax.fori_loop(0, nc, body, 0)

    o = acc_ref[...] + accb_ref[...]
    gn = gn_ref[...]
    for hh in range(GLA_HEADS):
        sl = slice(hh * GLA_DV, (hh + 1) * GLA_DV)
        o_ref[:, sl] = (_rms(o[:, sl]) * gn[:, sl] * gg_ref[:, sl].astype(F32)).astype(BF16)
    if not has_state:
        for d in range(2):
            for hh in range(GLA_HEADS):
                sfin_ref[d, hh] = st_ref[d, hh * GLA_DV:(hh + 1) * GLA_DV, hh * GLA_DK:(hh + 1) * GLA_DK].T


def _gla_scratch(seq_len):
    return [pltpu.VMEM((seq_len, GLA_VW), F32), pltpu.VMEM((seq_len, GLA_VW), F32),
            pltpu.VMEM((2, GLA_VW, GLA_KW), F32)]


def _gla_mask_specs():
    n_terms = 2 * (len(GLA_LEVELS) + 1)
    return [pl.BlockSpec((GLA_HEADS * GLA_CHUNK, GLA_KW), lambda i: (0, 0)),
            pl.BlockSpec((GLA_HEADS * GLA_CHUNK, GLA_VW), lambda i: (0, 0)),
            pl.BlockSpec((n_terms, GLA_CHUNK, GLA_HEADS * GLA_CHUNK), lambda i: (0, 0, 0))]


def _gla_ctx(layer, gq, gk, gv, la, gg, gn, masks):
    seq = lambda w: pl.BlockSpec((SEQ, w), lambda i: (i, 0))
    return pl.pallas_call(
        partial(_gla_kernel, seq_len=SEQ, has_state=False),
        grid=(BATCH,),
        in_specs=[seq(GLA_KW), seq(GLA_KW), seq(GLA_VW), seq(2 * GLA_KW), seq(GLA_VW),
                  _layer_row(layer, GLA_VW)] + _gla_mask_specs(),
        out_specs=[seq(GLA_VW),
                   pl.BlockSpec((None, 2, GLA_HEADS, GLA_DK, GLA_DV), lambda i: (i, 0, 0, 0, 0))],
        out_shape=[jax.ShapeDtypeStruct((N_CTX, GLA_VW), BF16),
                   jax.ShapeDtypeStruct((BATCH, 2, GLA_HEADS, GLA_DK, GLA_DV), F32)],
        scratch_shapes=_gla_scratch(SEQ),
        compiler_params=pltpu.CompilerParams(dimension_semantics=("parallel",), vmem_limit_bytes=VMEM_LIMIT),
        name="gla_ctx",
    )(gq, gk, gv, la, gg, gn, *masks)


def _gla_lat(layer, gq, gk, gv, la, gg, gn, masks, s0):
    rows = DEC_SEQ
    blk0 = N_CTX // rows
    seq = lambda w: pl.BlockSpec((rows, w), lambda b: (blk0 + b, 0))
    return pl.pallas_call(
        partial(_gla_kernel, seq_len=DEC_SEQ, has_state=True),
        grid=(DEC_BATCH,),
        in_specs=[seq(GLA_KW), seq(GLA_KW), seq(GLA_VW), seq(2 * GLA_KW), seq(GLA_VW),
                  _layer_row(layer, GLA_VW)] + _gla_mask_specs() + [
                  pl.BlockSpec((None, None, 2, GLA_HEADS, GLA_DK, GLA_DV), lambda b: (b, layer, 0, 0, 0, 0))],
        out_specs=pl.BlockSpec((rows, GLA_VW), lambda b: (b, 0)),
        out_shape=jax.ShapeDtypeStruct((N_LAT, GLA_VW), BF16),
        scratch_shapes=_gla_scratch(DEC_SEQ),
        compiler_params=pltpu.CompilerParams(dimension_semantics=("parallel",), vmem_limit_bytes=VMEM_LIMIT),
        name="gla_lat",
    )(gq, gk, gv, la, gg, gn, *masks, s0)


def _route(scores, bias):
    biased = scores + bias
    v = [biased[e:e + 1, :] for e in range(N_EXPERTS)]
    s = [scores[e:e + 1, :] for e in range(N_EXPERTS)]
    gscore = []
    for g in range(N_GROUPS):
        a, b, c, d = v[4 * g:4 * g + 4]
        hi1, lo1, hi2, lo2 = jnp.maximum(a, b), jnp.minimum(a, b), jnp.maximum(c, d), jnp.minimum(c, d)
        gscore.append(jnp.maximum(hi1, hi2) + jnp.maximum(jnp.minimum(hi1, hi2), jnp.maximum(lo1, lo2)))
    one = lambda cond: jnp.where(cond, 1.0, 0.0)
    picked, chosen = [], []
    for g in range(N_GROUPS):
        best = None
        for g2 in range(N_GROUPS):
            if g2 == g:
                continue
            ok = one((gscore[g] > gscore[g2]) if g2 < g else (gscore[g] >= gscore[g2]))
            best = ok if best is None else best * ok
        for i in range(EXPERTS_PER_GROUP):
            e = 4 * g + i
            beaten = None
            for j in range(EXPERTS_PER_GROUP):
                if j == i:
                    continue
                e2 = 4 * g + j
                lost = one((v[e2] > v[e]) if j > i else (v[e2] >= v[e]))
                beaten = lost if beaten is None else beaten + lost
            chosen.append(best * one(beaten < 1.5))
            picked.append(chosen[-1] * s[e])
    den = picked[0]
    for e in range(1, N_EXPERTS):
        den = den + picked[e]
    return jnp.concatenate([p / den for p in picked] + chosen, axis=0)


def _mix_kernel(xc_ref, xl_ref, oac_ref, oal_ref, ogc_ref, ogl_ref, sa_ref, sb_ref, mod_ref, wpa_f, wpb_f, wo_f, n2_ref,
                rw_ref, rb_ref, x1_out, h2_out, gt_out, wpa_ref, wpb_ref, wo_ref):
    @pl.when(pl.program_id(0) == 0)
    def _():
        wpa_ref[...] = wpa_f[...].astype(BF16)
        wpb_ref[...] = wpb_f[...].astype(BF16)
        wo_ref[...] = wo_f[...].astype(BF16)

    is_ctx = pl.program_id(0) < N_CTX // TM
    oa = jnp.where(is_ctx, oac_ref[...], oal_ref[...])
    og = jnp.where(is_ctx, ogc_ref[...], ogl_ref[...])
    a = _dot(oa, wpa_ref[...])
    b = _dot(og, wpb_ref[...])
    merged = sa_ref[...].astype(F32) * a + sb_ref[...].astype(F32) * b
    g1 = mod_ref[:, 2 * D_MODEL:3 * D_MODEL]
    sh2 = mod_ref[:, 3 * D_MODEL:4 * D_MODEL]
    sc2 = mod_ref[:, 4 * D_MODEL:5 * D_MODEL]
    x = jnp.where(is_ctx, xc_ref[...], xl_ref[...])
    x1 = x + g1 * _dot(merged.astype(BF16), wo_ref[...])
    x1_out[...] = x1
    h2 = _rms(x1) * (n2_ref[...] * (1.0 + sc2)) + sh2
    _to_row_tiles(h2_out, h2)
    hh, hl = _split2(h2)
    rh, rl = rw_ref[0], rw_ref[1]
    nt = lambda r, t: lax.dot_general(r, t, _NT, preferred_element_type=F32)
    logits = nt(rl, hh) + nt(rh, hl) + nt(rh, hh)
    gt_out[...] = _route(jax.nn.sigmoid(logits), rb_ref[...])


def _mix(layer, x_pair, oa_c, oa_l, og_c, og_l, sa, sb, mod, wpa, wpb, wo, norm2, rw3, rb):
    nt = N_TOK // TM
    n_ctx_t = N_CTX // TM
    row = lambda i: (i, 0)
    const = lambda i: (0, 0)
    ctx_row = lambda i: (jnp.minimum(i, n_ctx_t - 1), 0)
    lat_row = lambda i: (jnp.maximum(i - n_ctx_t, 0), 0)
    return pl.pallas_call(
        _mix_kernel,
        grid=(nt,),
        in_specs=_x_specs(x_pair) + [
            pl.BlockSpec((TM, ATTN_WIDTH), ctx_row),
            pl.BlockSpec((TM, ATTN_WIDTH), lat_row),
            pl.BlockSpec((TM, GLA_VW), ctx_row),
            pl.BlockSpec((TM, GLA_VW), lat_row),
            pl.BlockSpec((TM, D_MODEL), row),
            pl.BlockSpec((TM, D_MODEL), row),
            _mod_spec(layer),
            pl.BlockSpec((None, ATTN_WIDTH, D_MODEL), lambda i: (layer, 0, 0), pipeline_mode=pl.Buffered(1)),
            pl.BlockSpec((None, GLA_VW, D_MODEL), lambda i: (layer, 0, 0), pipeline_mode=pl.Buffered(1)),
            pl.BlockSpec((None, D_MODEL, D_MODEL), lambda i: (layer, 0, 0), pipeline_mode=pl.Buffered(1)),
            _layer_row(layer, D_MODEL),
            pl.BlockSpec((2, N_EXPERTS, D_MODEL), lambda i: (0, 0, 0)),
            pl.BlockSpec((N_EXPERTS, 1), const),
        ],
        out_specs=[pl.BlockSpec((TM, D_MODEL), row), pl.BlockSpec((TM * ROW_TILE, LANES), row),
                   pl.BlockSpec((2 * N_EXPERTS, TM), lambda i: (0, i))],
        out_shape=[jax.ShapeDtypeStruct((N_TOK, D_MODEL), F32),
                   jax.ShapeDtypeStruct((N_TOK * ROW_TILE, LANES), F32),
                   jax.ShapeDtypeStruct((2 * N_EXPERTS, N_TOK), F32)],
        scratch_shapes=[pltpu.VMEM((ATTN_WIDTH, D_MODEL), BF16), pltpu.VMEM((GLA_VW, D_MODEL), BF16),
                        pltpu.VMEM((D_MODEL, D_MODEL), BF16)],
        compiler_params=pltpu.CompilerParams(dimension_semantics=("arbitrary",), vmem_limit_bytes=VMEM_LIMIT),
        name="mix",
    )(*x_pair, oa_c, oa_l, og_c, og_l, sa, sb, mod, wpa, wpb, wo, norm2, rw3, rb)


def _route_plan(gsel):
    gates = gsel[:N_EXPERTS]
    sel = gsel[N_EXPERTS:] > 0.5
    seli = sel.astype(jnp.int32)
    incl = jnp.cumsum(seli, axis=1)
    count = incl[:, -1]
    ntile = (count + TS - 1) // TS
    tile_end = jnp.cumsum(ntile)
    tile_start = tile_end - ntile
    total = tile_end[-1]
    slot = tile_start[:, None] * TS + incl - seli
    pos_a = jnp.min(jnp.where(sel, slot, N_SLOT), axis=0)
    pos_b = jnp.max(jnp.where(sel, slot, -1), axis=0)
    w_a = jnp.sum(jnp.where(sel & (slot == pos_a[None, :]), gates, 0.0), axis=0)
    w_b = jnp.sum(jnp.where(sel & (slot == pos_b[None, :]), gates, 0.0), axis=0)
    nt = N_TOK // TM
    pos = jnp.concatenate([pos_a.reshape(nt, 1, TM), pos_b.reshape(nt, 1, TM)], axis=2).astype(jnp.int32)
    plan = jnp.concatenate([tile_start, ntile, count, total[None]]).astype(jnp.int32)
    return pos, jnp.stack([w_a, w_b], axis=1), plan


def _to_row_tiles(ref, x):
    for k in range(ROW_TILE):
        ref[pl.ds(k, x.shape[0], stride=ROW_TILE), :] = x[:, k * LANES:(k + 1) * LANES]


def _from_row_tiles(ref, rows):
    return jnp.concatenate([ref[pl.ds(k, rows, stride=ROW_TILE), :] for k in range(ROW_TILE)], axis=1)


def _tile_rows(row, n=1):
    return pl.ds(pl.multiple_of(row * ROW_TILE, ROW_TILE), n * ROW_TILE)


def _row_copy(src, src_row, dst, dst_row, sem):
    return pltpu.make_async_copy(src.at[_tile_rows(src_row), :], dst.at[_tile_rows(dst_row), :], sem)


def _dispatch_kernel(plan_ref, pos_ref, h_ref, xs_ref, zero_ref, sem_u, sem_z, sem):
    i = pl.program_id(0)
    total = plan_ref[PLAN_TOTAL]

    def each_unused(fn):
        def unused(t, c):
            @pl.when(t >= total)
            def _():
                fn(pltpu.make_async_copy(zero_ref, xs_ref.at[_tile_rows(t * TS, TS), :], sem_u))
            return c
        lax.fori_loop(0, N_TILE, unused, 0)

    def each_tail(fn):
        def expert(e, c):
            n_tiles = plan_ref[N_EXPERTS + e]

            @pl.when(n_tiles > 0)
            def _():
                row0 = (plan_ref[e] + n_tiles - 1) * TS
                filled = plan_ref[2 * N_EXPERTS + e] - (n_tiles - 1) * TS

                def chunk(j, c2):
                    fn(pltpu.make_async_copy(zero_ref.at[pl.ds(0, Z_ROWS * ROW_TILE), :],
                                             xs_ref.at[_tile_rows(row0 + j * Z_ROWS, Z_ROWS), :], sem_z))
                    return c2

                lax.fori_loop(filled // Z_ROWS, TS // Z_ROWS, chunk, 0)
            return c
        lax.fori_loop(0, N_EXPERTS, expert, 0)

    @pl.when(i == 0)
    def _():
        zero_ref[...] = jnp.zeros(zero_ref.shape, F32)
        each_unused(lambda cp: cp.start())
        each_tail(lambda cp: cp.start())
        each_tail(lambda cp: cp.wait())

    @pl.when(i == N_TOK // TM - 1)
    def _():
        each_unused(lambda cp: cp.wait())

    def start(r, c):
        _row_copy(h_ref, r, xs_ref, pos_ref[0, r], sem).start()
        _row_copy(h_ref, r, xs_ref, pos_ref[0, TM + r], sem).start()
        return c

    def wait(r, c):
        _row_copy(h_ref, 0, xs_ref, 0, sem).wait()
        return c

    lax.fori_loop(0, TM, start, 0, unroll=8)
    lax.fori_loop(0, 2 * TM, wait, 0, unroll=8)


def _dispatch(plan, pos, h2):
    return pl.pallas_call(
        _dispatch_kernel,
        grid_spec=pltpu.PrefetchScalarGridSpec(
            num_scalar_prefetch=1,
            grid=(N_TOK // TM,),
            in_specs=[
                pl.BlockSpec((None, 1, 2 * TM), lambda i, plan: (i, 0, 0), memory_space=pltpu.SMEM),
                pl.BlockSpec((TM * ROW_TILE, LANES), lambda i, plan: (i, 0)),
            ],
            out_specs=pl.BlockSpec(memory_space=pl.ANY),
            scratch_shapes=[pltpu.VMEM((TS * ROW_TILE, LANES), F32), pltpu.SemaphoreType.DMA,
                            pltpu.SemaphoreType.DMA, pltpu.SemaphoreType.DMA],
        ),
        out_shape=jax.ShapeDtypeStruct((N_SLOT * ROW_TILE, LANES), F32),
        compiler_params=pltpu.CompilerParams(dimension_semantics=("arbitrary",), vmem_limit_bytes=VMEM_LIMIT),
        name="dispatch",
    )(plan, pos, h2)


def _moe_kernel(plan_ref, xs_ref, wg_ref, wu_ref, wd_ref, ys_ref, x_buf, y_buf, zero_ref, sem_in, sem_out, sem_z,
                wg_s, wu_s, wd_s):
    e = pl.program_id(0)
    total = plan_ref[PLAN_TOTAL]
    first = plan_ref[e]

    def in_copy(g):
        return pltpu.make_async_copy(xs_ref.at[_tile_rows(g * TS, TS), :], x_buf.at[g % 2], sem_in.at[g % 2])

    def out_copy(g):
        return pltpu.make_async_copy(y_buf.at[g % 2], ys_ref.at[_tile_rows(g * TS, TS), :], sem_out.at[g % 2])

    def zero_copy(g):
        return pltpu.make_async_copy(zero_ref, ys_ref.at[_tile_rows(g * TS, TS), :], sem_z)

    @pl.when(e == 0)
    def _():
        in_copy(0).start()

    wg_s[...] = wg_ref[...].astype(BF16)
    wu_s[...] = wu_ref[...].astype(BF16)
    wd_s[...] = wd_ref[...].astype(BF16)

    def tile(t, c):
        g = first + t
        slot = g % 2

        @pl.when(g + 1 < total)
        def _():
            in_copy(g + 1).start()

        in_copy(g).wait()

        @pl.when(g >= 2)
        def _():
            out_copy(g - 2).wait()

        x = _from_row_tiles(x_buf.at[slot], TS).astype(BF16)
        up = _dot(x, wu_s[...])
        gt = _dot(x, wg_s[...])
        act = (gt * jax.nn.sigmoid(gt) * up).astype(BF16)
        _to_row_tiles(y_buf.at[slot], _dot(act, wd_s[...]))
        out_copy(g).start()
        return c

    lax.fori_loop(0, plan_ref[N_EXPERTS + e], tile, 0)

    @pl.when(e == N_EXPERTS - 1)
    def _():
        out_copy(total - 1).wait()

        @pl.when(total >= 2)
        def _():
            out_copy(total - 2).wait()

        zero_ref[...] = jnp.zeros(zero_ref.shape, F32)

        def each(fn):
            def unused(g, c):
                @pl.when(g >= total)
                def _():
                    fn(zero_copy(g))
                return c
            lax.fori_loop(0, N_TILE, unused, 0)

        each(lambda cp: cp.start())
        each(lambda cp: cp.wait())


def _moe(layer, plan, xs, w_gate, w_up, w_down):
    w_spec = lambda a, b: pl.BlockSpec((None, None, a, b), lambda e, plan: (layer, e, 0, 0))
    tile_buf = lambda n: pltpu.VMEM((n, TS * ROW_TILE, LANES), F32)
    return pl.pallas_call(
        _moe_kernel,
        grid_spec=pltpu.PrefetchScalarGridSpec(
            num_scalar_prefetch=1,
            grid=(N_EXPERTS,),
            in_specs=[
                pl.BlockSpec(memory_space=pl.ANY),
                w_spec(D_MODEL, D_EXPERT), w_spec(D_MODEL, D_EXPERT), w_spec(D_EXPERT, D_MODEL),
            ],
            out_specs=pl.BlockSpec(memory_space=pl.ANY),
            scratch_shapes=[tile_buf(2), tile_buf(2), pltpu.VMEM((TS * ROW_TILE, LANES), F32),
                            pltpu.SemaphoreType.DMA((2,)), pltpu.SemaphoreType.DMA((2,)), pltpu.SemaphoreType.DMA,
                            pltpu.VMEM((D_MODEL, D_EXPERT), BF16), pltpu.VMEM((D_MODEL, D_EXPERT), BF16),
                            pltpu.VMEM((D_EXPERT, D_MODEL), BF16)],
        ),
        out_shape=jax.ShapeDtypeStruct((N_SLOT * ROW_TILE, LANES), F32),
        compiler_params=pltpu.CompilerParams(dimension_semantics=("arbitrary",), vmem_limit_bytes=VMEM_LIMIT),
        name="moe",
    )(plan, xs, w_gate, w_up, w_down)


def _combine_kernel(pos_ref, nxt_ref, w_ref, x1_ref, mod_ref, ys_ref, out_ref, ya_ref, yb_ref, sem, *, n):
    i = pl.program_id(0)

    def gather(p_ref, slot):
        def start(r, c):
            _row_copy(ys_ref, p_ref[0, r], ya_ref.at[slot], r, sem.at[slot]).start()
            _row_copy(ys_ref, p_ref[0, TM + r], yb_ref.at[slot], r, sem.at[slot]).start()
            return c
        lax.fori_loop(0, TM, start, 0, unroll=8)

    @pl.when(i == 0)
    def _():
        gather(pos_ref, 0)

    @pl.when(i + 1 < n)
    def _():
        gather(nxt_ref, (i + 1) % 2)

    slot = i % 2

    def wait(r, c):
        _row_copy(ys_ref, 0, ya_ref.at[slot], 0, sem.at[slot]).wait()
        return c

    lax.fori_loop(0, 2 * TM, wait, 0, unroll=8)
    g2 = mod_ref[:, 5 * D_MODEL:6 * D_MODEL]
    w = w_ref[...]
    ya = _from_row_tiles(ya_ref.at[slot], TM)
    yb = _from_row_tiles(yb_ref.at[slot], TM)
    out_ref[...] = x1_ref[...] + g2 * (w[:, 0:1] * ya + w[:, 1:2] * yb)


def _combine(layer, pos, w_ab, x1, mod, ys, tile0, nt):
    pos_spec = lambda fn: pl.BlockSpec((None, 1, 2 * TM), fn, memory_space=pltpu.SMEM)
    return pl.pallas_call(
        partial(_combine_kernel, n=nt),
        grid=(nt,),
        in_specs=[
            pos_spec(lambda i: (tile0 + i, 0, 0)),
            pos_spec(lambda i: (tile0 + jnp.minimum(i + 1, nt - 1), 0, 0)),
            pl.BlockSpec((TM, 2), lambda i: (tile0 + i, 0)),
            pl.BlockSpec((TM, D_MODEL), lambda i: (tile0 + i, 0)),
            _mod_spec(layer, tile0),
            pl.BlockSpec(memory_space=pl.ANY),
        ],
        out_specs=pl.BlockSpec((TM, D_MODEL), lambda i: (i, 0)),
        out_shape=jax.ShapeDtypeStruct((nt * TM, D_MODEL), F32),
        scratch_shapes=[pltpu.VMEM((2, TM * ROW_TILE, LANES), F32), pltpu.VMEM((2, TM * ROW_TILE, LANES), F32),
                        pltpu.SemaphoreType.DMA((2,))],
        compiler_params=pltpu.CompilerParams(dimension_semantics=("arbitrary",), vmem_limit_bytes=VMEM_LIMIT),
        name="combine",
    )(pos, pos, w_ab, x1, mod, ys)


def _rope_tables():
    pos = np.arange(DEC_SEQ)
    rows = (pos // GRID_W).astype(np.float64)
    cols = (pos % GRID_W).astype(np.float64)
    half = HEAD_DIM // 2
    inv_freq = ROPE_THETA ** (-np.arange(0, half, 2, dtype=np.float64) / half)
    ang_r = rows[:, None] * inv_freq[None, :]
    ang_c = cols[:, None] * inv_freq[None, :]
    cos_h = np.concatenate([np.cos(ang_r), np.cos(ang_r), np.cos(ang_c), np.cos(ang_c)], axis=-1)
    sin_h = np.concatenate([-np.sin(ang_r), np.sin(ang_r), -np.sin(ang_c), np.sin(ang_c)], axis=-1)
    cos = np.concatenate([np.ones((TM, HEAD_DIM)), cos_h], axis=0)
    sin = np.concatenate([np.zeros((TM, HEAD_DIM)), sin_h], axis=0)
    return (jnp.asarray(np.tile(cos, (1, N_HEADS)), F32), jnp.asarray(np.tile(sin, (1, N_HEADS)), F32))


def _pack_w2(w2):
    z = jnp.zeros((DEPTH, GLA_RANK, GLA_KW), w2.dtype)
    top = jnp.concatenate([w2[:, 0], z], axis=2)
    bot = jnp.concatenate([z, w2[:, 1]], axis=2)
    pad = jnp.zeros((DEPTH, LR_PAD - 2 * GLA_RANK, 2 * GLA_KW), w2.dtype)
    return jnp.concatenate([top, bot, pad], axis=1).astype(BF16)


def kernel(x_prompt, x_sample, cache_k, cache_v, state_gla, c, c_ctx, norm1, norm2, w_ada, b_ada, w_in,
           q_norm, k_norm, gla_w2, gla_b, gla_norm, w_pa, w_pb, w_out, router_w, router_bias,
           w_gate, w_up, w_down):
    x_pair = (x_prompt.reshape(N_CTX, D_MODEL), x_sample.reshape(N_LAT, D_MODEL))
    cond = jnp.concatenate([c_ctx[None, :], c, jnp.zeros((N_COND - 1 - DEC_BATCH, D_MODEL), F32)], axis=0)
    mod = _ada(cond, w_ada, b_ada).reshape(DEPTH, N_COND, 1, MOD_W)
    cos, sin = _rope_tables()
    gla_masks = _gla_masks()
    w_in_t = jnp.swapaxes(w_in, 1, 2)
    lane_head = np.arange(ATTN_WIDTH) // HEAD_DIM
    bd = jnp.asarray(lane_head[:, None] == lane_head[None, :], BF16)
    rw3 = jnp.stack(_split2(router_w.T), axis=0)
    rb = router_bias.reshape(N_EXPERTS, 1)
    ck = cache_k.reshape(DEC_BATCH, DEPTH, PAST_LEN, KV_WIDTH)
    cv = cache_v.reshape(DEC_BATCH, DEPTH, PAST_LEN, KV_WIDTH)
    rows = lambda p, reps: jnp.tile(p, (1, reps)).reshape(DEPTH, 1, -1)
    n1, n2 = rows(norm1, 1), rows(norm2, 1)
    qn, kn, gn = rows(q_norm, N_HEADS), rows(k_norm, KV_HEADS), rows(gla_norm, GLA_HEADS)
    gb = gla_b.reshape(DEPTH, 1, 2 * GLA_KW)
    w2p = _pack_w2(gla_w2)

    keys, vals, states = [], [], []
    for l in range(DEPTH):
        q, kf, kr, v, gq, gk, gv, gg, la, sa, sb = _inproj(l, x_pair, mod, n1, w_in_t, qn, kn, cos, sin, bd, w2p, gb)
        oa_c = _attn_ctx(q, kr, v)
        oa_l = _attn_lat(l, q, kr, v, ck, cv)
        og_c, s_fin = _gla_ctx(l, gq, gk, gv, la, gg, gn, gla_masks)
        og_l = _gla_lat(l, gq, gk, gv, la, gg, gn, gla_masks, state_gla)
        x1, h2, gsel = _mix(l, x_pair, oa_c, oa_l, og_c, og_l, sa, sb, mod, w_pa, w_pb, w_out, n2, rw3, rb)
        pos, w_ab, plan = _route_plan(gsel)
        xs = _dispatch(plan, pos, h2)
        ys = _moe(l, plan, xs, w_gate, w_up, w_down)
        n_ctx_t = N_CTX // TM
        if l + 1 < DEPTH:
            x = _combine(l, pos, w_ab, x1, mod, ys, 0, N_TOK // TM)
            x_pair = (x, x)
        else:
            x_pair = (_combine(l, pos, w_ab, x1, mod, ys, 0, n_ctx_t),
                      _combine(l, pos, w_ab, x1, mod, ys, n_ctx_t, N_LAT // TM))
        keys.append(kf[:N_CTX].reshape(BATCH, SEQ, KV_HEADS, HEAD_DIM))
        vals.append(v[:N_CTX].reshape(BATCH, SEQ, KV_HEADS, HEAD_DIM))
        states.append(s_fin)
    y_prompt = x_pair[0].reshape(BATCH, SEQ, D_MODEL)
    y_sample = x_pair[1].reshape(DEC_BATCH, DEC_SEQ, D_MODEL)
    return (y_prompt, y_sample, jnp.stack(keys, axis=1), jnp.stack(vals, axis=1), jnp.stack(states, axis=1))
```

```python
from functools import partial

import jax
import jax.numpy as jnp
import numpy as np
from jax import lax
from jax.experimental import pallas as pl
from jax.experimental.pallas import tpu as pltpu

F32 = jnp.float32
BF16 = jnp.bfloat16

D_MODEL = 1024
BATCH = 16
SEQ = 256
DEPTH = 2
DEC_BATCH = 4
DEC_SEQ = 1024
PAST_LEN = 512
GRID_W = 64
N_HEADS = 8
KV_HEADS = 2
GROUP = N_HEADS // KV_HEADS
HEAD_DIM = 64
ATTN_WIDTH = N_HEADS * HEAD_DIM
KV_WIDTH = KV_HEADS * HEAD_DIM
ROPE_THETA = 10000.0
GLA_HEADS = 4
GLA_DK = 64
GLA_DV = 128
GLA_KW = GLA_HEADS * GLA_DK
GLA_VW = GLA_HEADS * GLA_DV
GLA_RANK = 16
GLA_GATE_NORM = 16.0
N_EXPERTS = 16
N_GROUPS = 4
EXPERTS_PER_GROUP = N_EXPERTS // N_GROUPS
D_EXPERT = 512
NORM_EPS = 1e-6

N_CTX = BATCH * SEQ
N_LAT = DEC_BATCH * DEC_SEQ
N_TOK = N_CTX + N_LAT
N_COND = 8
MOD_W = 6 * D_MODEL

TM = 512
TS = 512
N_TILE = 2 * N_TOK // TS + N_EXPERTS
N_SLOT = N_TILE * TS
PLAN_TOTAL = 3 * N_EXPERTS
Z_ROWS = 64
ROW_TILE = 8
LANES = D_MODEL // ROW_TILE
GLA_CHUNK = 128
GLA_LEVELS = (64, 32, 16)
GLA_DIAG = 16
LR_PAD = 128
W_ROWS = 128
SEG_MIX = ATTN_WIDTH + 2 * KV_WIDTH + 2 * GLA_KW + 2 * GLA_VW
OFF_GA = SEG_MIX
OFF_GB = OFF_GA + D_MODEL
OFF_LR = OFF_GB + D_MODEL
IN_PACKED = OFF_LR + LR_PAD
IN_WIDTH = SEG_MIX + 2 * GLA_RANK + 2 * D_MODEL
VMEM_LIMIT = 56 * 1024 * 1024

_NT = (((1,), (1,)), ((), ()))
_TN = (((0,), (0,)), ((), ()))


def _mod_row(tile, tm):
    start = tile * tm
    return jnp.where(start < N_CTX, 0, 1 + (start - N_CTX) // DEC_SEQ)


def _split2(x):
    hi = x.astype(BF16)
    lo = (x - hi.astype(F32)).astype(BF16)
    return hi, lo


def _dot(a, b):
    return jnp.dot(a, b, preferred_element_type=F32)


def _rms(x):
    return x * lax.rsqrt(jnp.mean(x * x, axis=-1, keepdims=True) + NORM_EPS)


def _ada_kernel(cond_ref, w_ref, b_ref, out_ref):
    c = cond_ref[...]
    s = c * jax.nn.sigmoid(c)
    out_ref[...] = _dot(s.astype(BF16), w_ref[...].astype(BF16)) + b_ref[...]


def _ada(cond, w_ada, b_ada):
    nb = MOD_W // D_MODEL
    return pl.pallas_call(
        _ada_kernel,
        grid=(DEPTH, nb),
        in_specs=[
            pl.BlockSpec((N_COND, D_MODEL), lambda l, j: (0, 0)),
            pl.BlockSpec((None, D_MODEL, D_MODEL), lambda l, j: (l, 0, j)),
            pl.BlockSpec((None, 1, D_MODEL), lambda l, j: (l, 0, j)),
        ],
        out_specs=pl.BlockSpec((None, N_COND, D_MODEL), lambda l, j: (l, 0, j)),
        out_shape=jax.ShapeDtypeStruct((DEPTH, N_COND, MOD_W), F32),
        compiler_params=pltpu.CompilerParams(vmem_limit_bytes=VMEM_LIMIT),
        name="ada",
    )(cond, w_ada, b_ada.reshape(DEPTH, 1, MOD_W))


def _head_rmsnorm(x, bd):
    ssq = _dot((x * x).astype(BF16), bd)
    return x * lax.rsqrt(ssq * (1.0 / HEAD_DIM) + NORM_EPS)


def _rope(x, cos, sin):
    n = x.shape[-1]
    lane = lax.broadcasted_iota(jnp.int32, x.shape, 1)
    first = (lane & (HEAD_DIM // 4)) == 0
    partner = jnp.where(first, pltpu.roll(x, n - HEAD_DIM // 4, 1), pltpu.roll(x, HEAD_DIM // 4, 1))
    return x * cos + partner * sin


def _pack_w_in(w_ref, wp_ref):
    lr0 = SEG_MIX
    ga0 = lr0 + 2 * GLA_RANK

    def cast(dst, src, n):
        def body(i, carry):
            rows_out = pl.ds(pl.multiple_of(dst + i * W_ROWS, W_ROWS), W_ROWS)
            rows_in = pl.ds(pl.multiple_of(src + i * W_ROWS, 2 * GLA_RANK), W_ROWS)
            wp_ref[rows_out, :] = w_ref[rows_in, :].astype(BF16)
            return carry
        lax.fori_loop(0, n // W_ROWS, body, 0)

    cast(0, 0, SEG_MIX)
    cast(OFF_GA, ga0, 2 * D_MODEL)
    wp_ref[OFF_LR:IN_PACKED, :] = jnp.concatenate(
        [w_ref[lr0:ga0, :], jnp.zeros((LR_PAD - 2 * GLA_RANK, D_MODEL), F32)], axis=0).astype(BF16)


def _inproj_kernel(xc_ref, xl_ref, mod_ref, n1_ref, wf_ref, qn_ref, kn_ref, cos_ref, sin_ref, bd_ref, w2_ref, gb_ref,
                   q_out, kf_out, kr_out, v_out, gq_out, gk_out, gv_out, gg_out, la_out, sa_out, sb_out,
                   w_ref):
    @pl.when(pl.program_id(0) == 0)
    def _():
        _pack_w_in(wf_ref, w_ref)

    x = jnp.where(pl.program_id(0) < N_CTX // TM, xc_ref[...], xl_ref[...])
    sh1 = mod_ref[:, 0:D_MODEL]
    sc1 = mod_ref[:, D_MODEL:2 * D_MODEL]
    h = (_rms(x) * (n1_ref[...] * (1.0 + sc1)) + sh1).astype(BF16)

    def proj(lo, width):
        return lax.dot_general(h, w_ref[lo:lo + width, :], _NT, preferred_element_type=F32)

    cos = cos_ref[...]
    sin = sin_ref[...]
    bd = bd_ref[...]
    off = 0
    q = _head_rmsnorm(proj(off, ATTN_WIDTH), bd) * qn_ref[...]
    q_out[...] = (_rope(q, cos, sin) * HEAD_DIM ** -0.5).astype(BF16)
    off += ATTN_WIDTH
    k = _head_rmsnorm(proj(off, KV_WIDTH), bd[:KV_WIDTH, :KV_WIDTH]) * kn_ref[...]
    kf_out[...] = k
    kr_out[...] = _rope(k, cos[:, :KV_WIDTH], sin[:, :KV_WIDTH]).astype(BF16)
    off += KV_WIDTH
    v_out[...] = proj(off, KV_WIDTH)
    off += KV_WIDTH
    gq_out[...] = (proj(off, GLA_KW) * GLA_DK ** -0.5).astype(BF16)
    off += GLA_KW
    gk_out[...] = proj(off, GLA_KW).astype(BF16)
    off += GLA_KW
    gv_out[...] = proj(off, GLA_VW).astype(BF16)
    off += GLA_VW
    gg = proj(off, GLA_VW)
    gg_out[...] = (gg * jax.nn.sigmoid(gg)).astype(BF16)
    sa_out[...] = jax.nn.sigmoid(proj(OFF_GA, D_MODEL)).astype(BF16)
    sb_out[...] = jax.nn.sigmoid(proj(OFF_GB, D_MODEL)).astype(BF16)
    lr = proj(OFF_LR, LR_PAD)
    z = _dot(lr.astype(BF16), w2_ref[...]) + gb_ref[...]
    log_sig = jnp.minimum(z, 0.0) - jnp.log1p(jnp.exp(-jnp.abs(z)))
    la_out[...] = log_sig * (1.0 / GLA_GATE_NORM)


def _layer_row(layer, width):
    return pl.BlockSpec((None, 1, width), lambda *_: (layer, 0, 0))


def _mod_spec(layer, tile0=0):
    return pl.BlockSpec((None, None, 1, MOD_W), lambda i, *_: (layer, _mod_row(tile0 + i, TM), 0, 0))


def _x_specs(x_pair):
    n_ctx_t = N_CTX // TM
    lat0 = n_ctx_t if x_pair[0] is x_pair[1] else 0
    return [pl.BlockSpec((TM, D_MODEL), lambda i: (jnp.minimum(i, n_ctx_t - 1), 0)),
            pl.BlockSpec((TM, D_MODEL), lambda i: (lat0 + jnp.maximum(i - n_ctx_t, 0), 0))]


def _inproj(layer, x_pair, mod, norm1, w_in, qn, kn, cos, sin, bd, w2p, gb):
    nt = N_TOK // TM
    n_ctx_t = N_CTX // TM
    t_per_seq = DEC_SEQ // TM
    row = lambda i: (i, 0)
    const = lambda i: (0, 0)
    rope_blk = lambda i: (jnp.where(i < n_ctx_t, 0, 1 + (i - n_ctx_t) % t_per_seq), 0)

    def out(width, dtype):
        return pl.BlockSpec((TM, width), row), jax.ShapeDtypeStruct((N_TOK, width), dtype)

    outs = [out(ATTN_WIDTH, BF16), out(KV_WIDTH, F32), out(KV_WIDTH, BF16), out(KV_WIDTH, F32),
            out(GLA_KW, BF16), out(GLA_KW, BF16), out(GLA_VW, BF16), out(GLA_VW, BF16),
            out(2 * GLA_KW, F32), out(D_MODEL, BF16), out(D_MODEL, BF16)]
    return pl.pallas_call(
        _inproj_kernel,
        grid=(nt,),
        in_specs=_x_specs(x_pair) + [
            _mod_spec(layer),
            _layer_row(layer, D_MODEL),
            pl.BlockSpec((None, IN_WIDTH, D_MODEL), lambda i: (layer, 0, 0), pipeline_mode=pl.Buffered(1)),
            _layer_row(layer, ATTN_WIDTH),
            _layer_row(layer, KV_WIDTH),
            pl.BlockSpec((TM, ATTN_WIDTH), rope_blk),
            pl.BlockSpec((TM, ATTN_WIDTH), rope_blk),
            pl.BlockSpec((ATTN_WIDTH, ATTN_WIDTH), const),
            pl.BlockSpec((None, LR_PAD, 2 * GLA_KW), lambda i: (layer, 0, 0)),
            _layer_row(layer, 2 * GLA_KW),
        ],
        out_specs=[o[0] for o in outs],
        out_shape=[o[1] for o in outs],
        scratch_shapes=[pltpu.VMEM((IN_PACKED, D_MODEL), BF16)],
        compiler_params=pltpu.CompilerParams(dimension_semantics=("arbitrary",), vmem_limit_bytes=VMEM_LIMIT),
        name="inproj",
    )(*x_pair, mod, norm1, w_in, qn, kn, cos, sin, bd, w2p, gb)


def _attn_kernel(*refs, has_cache):
    if has_cache:
        q_ref, k_ref, v_ref, ck_ref, cv_ref, o_ref = refs
    else:
        q_ref, k_ref, v_ref, o_ref = refs
    tq = q_ref.shape[0]
    for g in range(KV_HEADS):
        ks = slice(g * HEAD_DIM, (g + 1) * HEAD_DIM)
        qs = jnp.concatenate(
            [q_ref[:, (g * GROUP + j) * HEAD_DIM:(g * GROUP + j + 1) * HEAD_DIM] for j in range(GROUP)], axis=0)
        k = k_ref[:, ks]
        v = v_ref[:, ks].astype(BF16)
        s = lax.dot_general(qs, k, _NT, preferred_element_type=F32)
        m = jnp.max(s, axis=-1, keepdims=True)
        if has_cache:
            ck = ck_ref[:, ks].astype(BF16)
            cv = cv_ref[:, ks].astype(BF16)
            s2 = lax.dot_general(qs, ck, _NT, preferred_element_type=F32)
            m = jnp.maximum(m, jnp.max(s2, axis=-1, keepdims=True))
        p = jnp.exp(s - m)
        den = jnp.sum(p, axis=-1, keepdims=True)
        o = _dot(p.astype(BF16), v)
        if has_cache:
            p2 = jnp.exp(s2 - m)
            den = den + jnp.sum(p2, axis=-1, keepdims=True)
            o = o + _dot(p2.astype(BF16), cv)
        o = o / den
        for j in range(GROUP):
            hd = g * GROUP + j
            o_ref[:, hd * HEAD_DIM:(hd + 1) * HEAD_DIM] = o[j * tq:(j + 1) * tq, :].astype(BF16)


def _attn_ctx(q, kr, v):
    return pl.pallas_call(
        partial(_attn_kernel, has_cache=False),
        grid=(BATCH,),
        in_specs=[
            pl.BlockSpec((SEQ, ATTN_WIDTH), lambda i: (i, 0)),
            pl.BlockSpec((SEQ, KV_WIDTH), lambda i: (i, 0)),
            pl.BlockSpec((SEQ, KV_WIDTH), lambda i: (i, 0)),
        ],
        out_specs=pl.BlockSpec((SEQ, ATTN_WIDTH), lambda i: (i, 0)),
        out_shape=jax.ShapeDtypeStruct((N_CTX, ATTN_WIDTH), BF16),
        compiler_params=pltpu.CompilerParams(dimension_semantics=("parallel",), vmem_limit_bytes=VMEM_LIMIT),
        name="attn_ctx",
    )(q, kr, v)


def _attn_lat(layer, q, kr, v, cache_k, cache_v):
    tq = 256
    nq = DEC_SEQ // tq
    q0 = N_CTX // tq
    s0 = N_CTX // DEC_SEQ
    return pl.pallas_call(
        partial(_attn_kernel, has_cache=True),
        grid=(DEC_BATCH, nq),
        in_specs=[
            pl.BlockSpec((tq, ATTN_WIDTH), lambda b, i: (q0 + b * nq + i, 0)),
            pl.BlockSpec((DEC_SEQ, KV_WIDTH), lambda b, i: (s0 + b, 0)),
            pl.BlockSpec((DEC_SEQ, KV_WIDTH), lambda b, i: (s0 + b, 0)),
            pl.BlockSpec((None, None, PAST_LEN, KV_WIDTH), lambda b, i: (b, layer, 0, 0)),
            pl.BlockSpec((None, None, PAST_LEN, KV_WIDTH), lambda b, i: (b, layer, 0, 0)),
        ],
        out_specs=pl.BlockSpec((tq, ATTN_WIDTH), lambda b, i: (b * nq + i, 0)),
        out_shape=jax.ShapeDtypeStruct((N_LAT, ATTN_WIDTH), BF16),
        compiler_params=pltpu.CompilerParams(dimension_semantics=("parallel", "parallel"),
                                             vmem_limit_bytes=VMEM_LIMIT),
        name="attn_lat",
    )(q, kr, v, cache_k, cache_v)


def _gla_masks():
    c = GLA_CHUNK
    hrow = np.arange(GLA_HEADS * c) // c
    head_k = (hrow[:, None] == (np.arange(GLA_KW) // GLA_DK)[None, :])
    head_v = (hrow[:, None] == (np.arange(GLA_VW) // GLA_DV)[None, :])
    t = np.arange(c)[:, None]
    s = (np.arange(GLA_HEADS * c) % c)[None, :]
    pair = []
    for reverse in (False, True):
        for m in GLA_LEVELS:
            pair.append(t // (2 * m) == s // (2 * m))
        pair.append((t // GLA_DIAG == s // GLA_DIAG) & ((s >= t) if reverse else (s <= t)))
    return (jnp.asarray(head_k, BF16), jnp.asarray(head_v, BF16), jnp.asarray(np.stack(pair), F32))


def _block_diag(x, head_mask):
    return jnp.concatenate([x] * GLA_HEADS, axis=0) * head_mask


def _ref_rows(b, rows, rep):
    return jnp.concatenate([jnp.broadcast_to(b[r:r + 1, :], (rep, b.shape[1])) for r in rows], axis=0)


def _score_operands(q, k, b, reverse, head_k):
    c = GLA_CHUNK
    t_idx = lax.broadcasted_iota(jnp.int32, (c, GLA_KW), 0)
    terms = []
    for m in GLA_LEVELS:
        nblk = c // (2 * m)
        ref = _ref_rows(b, [2 * m * j + (m if reverse else m - 1) for j in range(nblk)], 2 * m)
        q_side = ((t_idx & m) == 0) if reverse else ((t_idx & m) != 0)
        terms.append((jnp.where(q_side, q * jnp.exp(b - ref), 0.0), jnp.where(q_side, 0.0, k * jnp.exp(ref - b))))
    d = GLA_DIAG
    x = b - _ref_rows(b, [d * j + d // 2 for j in range(c // d)], d)
    terms.append((q * jnp.exp(x), k * jnp.exp(-x)))
    return [(qt.astype(BF16), _block_diag(kt.astype(BF16), head_k)) for qt, kt in terms]


def _score_sum(parts, reverse, pair_ref):
    term0 = (len(GLA_LEVELS) + 1) if reverse else 0
    total = jnp.where(pair_ref[term0 + len(GLA_LEVELS)] > 0.5, parts[-1], 0.0)
    for i, part in enumerate(parts[:-1]):
        total = total + part * pair_ref[term0 + i]
    return total


def _cumsum_rows(tri, la):
    hi, lo = _split2(la)
    return _dot(tri, hi) + _dot(tri, lo)


def _gla_kernel(*refs, seq_len, has_state):
    gq_ref, gk_ref, gv_ref, la_ref, gg_ref, gn_ref, hk_ref, hv_ref, pair_ref = refs[:9]
    if has_state:
        s0_ref, o_ref, acc_ref, accb_ref, st_ref = refs[9:]
    else:
        o_ref, sfin_ref, acc_ref, accb_ref, st_ref = refs[9:]
    c = GLA_CHUNK
    nc = seq_len // c
    r_i = lax.broadcasted_iota(jnp.int32, (c, c), 0)
    c_i = lax.broadcasted_iota(jnp.int32, (c, c), 1)
    tril = jnp.where(c_i <= r_i, 1.0, 0.0).astype(BF16)
    triu = jnp.where(c_i >= r_i, 1.0, 0.0).astype(BF16)
    st_head_r = lax.broadcasted_iota(jnp.int32, (GLA_VW, GLA_KW), 0) // GLA_DV
    st_head_c = lax.broadcasted_iota(jnp.int32, (GLA_VW, GLA_KW), 1) // GLA_DK
    st_mask = st_head_r == st_head_c

    for d in range(2):
        st_ref[d] = jnp.zeros((GLA_VW, GLA_KW), F32)
        if has_state:
            for hh in range(GLA_HEADS):
                st_ref[d, hh * GLA_DV:(hh + 1) * GLA_DV, hh * GLA_DK:(hh + 1) * GLA_DK] = s0_ref[d, hh].T

    def load(n):
        rows = pl.ds(pl.multiple_of(n * c, c), c)
        return (rows, gq_ref[rows, :].astype(F32), gk_ref[rows, :].astype(F32), gv_ref[rows, :])

    nt = lambda a, b: lax.dot_general(a, b, _NT, preferred_element_type=F32)
    tn = lambda a, b: lax.dot_general(a, b, _TN, preferred_element_type=F32)

    def body(n, carry):
        rows, q, k, v = load(n)
        rows_b, q_b, k_b, v_b = load(nc - 1 - n)
        b_f = _cumsum_rows(tril, la_ref[rows, 0:GLA_KW])
        b_r = _cumsum_rows(triu, la_ref[rows, GLA_KW:2 * GLA_KW])
        b_b = _cumsum_rows(triu, la_ref[rows_b, GLA_KW:2 * GLA_KW])
        head_k = hk_ref[...]
        ops_f = _score_operands(q, k, b_f, False, head_k)
        ops_r = _score_operands(q, k, b_r, True, head_k)
        end_f, end_b = b_f[c - 1:c, :], b_b[0:1, :]
        qin_f = (q * jnp.exp(b_f)).astype(BF16)
        qin_b = (q_b * jnp.exp(b_b)).astype(BF16)
        kst_f = (k * jnp.exp(end_f - b_f)).astype(BF16)
        kst_b = (k_b * jnp.exp(end_b - b_b)).astype(BF16)
        st_f, st_b = st_ref[0], st_ref[1]
        parts_f = [nt(qt, kt) for qt, kt in ops_f]
        parts_r = [nt(qt, kt) for qt, kt in ops_r]
        o_f = nt(qin_f, st_f.astype(BF16))
        o_b = nt(qin_b, st_b.astype(BF16))
        upd_f = tn(v, kst_f)
        upd_b = tn(v_b, kst_b)
        a = _score_sum(parts_f, False, pair_ref) + _score_sum(parts_r, True, pair_ref)
        acc_ref[rows, :] = _dot(a.astype(BF16), _block_diag(v, hv_ref[...])) + o_f
        accb_ref[rows_b, :] = o_b
        st_ref[0] = st_f * jnp.exp(end_f) + jnp.where(st_mask, upd_f, 0.0)
        st_ref[1] = st_b * jnp.exp(end_b) + jnp.where(st_mask, upd_b, 0.0)
        return carry

    lax.fori_loop(0, nc, body, 0)

    o = acc_ref[...] + accb_ref[...]
    gn = gn_ref[...]
    for hh in range(GLA_HEADS):
        sl = slice(hh * GLA_DV, (hh + 1) * GLA_DV)
        o_ref[:, sl] = (_rms(o[:, sl]) * gn[:, sl] * gg_ref[:, sl].astype(F32)).astype(BF16)
    if not has_state:
        for d in range(2):
            for hh in range(GLA_HEADS):
                sfin_ref[d, hh] = st_ref[d, hh * GLA_DV:(hh + 1) * GLA_DV, hh * GLA_DK:(hh + 1) * GLA_DK].T


def _gla_scratch(seq_len):
    return [pltpu.VMEM((seq_len, GLA_VW), F32), pltpu.VMEM((seq_len, GLA_VW), F32),
            pltpu.VMEM((2, GLA_VW, GLA_KW), F32)]


def _gla_mask_specs():
    n_terms = 2 * (len(GLA_LEVELS) + 1)
    return [pl.BlockSpec((GLA_HEADS * GLA_CHUNK, GLA_KW), lambda i: (0, 0)),
            pl.BlockSpec((GLA_HEADS * GLA_CHUNK, GLA_VW), lambda i: (0, 0)),
            pl.BlockSpec((n_terms, GLA_CHUNK, GLA_HEADS * GLA_CHUNK), lambda i: (0, 0, 0))]


def _gla_ctx(layer, gq, gk, gv, la, gg, gn, masks):
    seq = lambda w: pl.BlockSpec((SEQ, w), lambda i: (i, 0))
    return pl.pallas_call(
        partial(_gla_kernel, seq_len=SEQ, has_state=False),
        grid=(BATCH,),
        in_specs=[seq(GLA_KW), seq(GLA_KW), seq(GLA_VW), seq(2 * GLA_KW), seq(GLA_VW),
                  _layer_row(layer, GLA_VW)] + _gla_mask_specs(),
        out_specs=[seq(GLA_VW),
                   pl.BlockSpec((None, 2, GLA_HEADS, GLA_DK, GLA_DV), lambda i: (i, 0, 0, 0, 0))],
        out_shape=[jax.ShapeDtypeStruct((N_CTX, GLA_VW), BF16),
                   jax.ShapeDtypeStruct((BATCH, 2, GLA_HEADS, GLA_DK, GLA_DV), F32)],
        scratch_shapes=_gla_scratch(SEQ),
        compiler_params=pltpu.CompilerParams(dimension_semantics=("parallel",), vmem_limit_bytes=VMEM_LIMIT),
        name="gla_ctx",
    )(gq, gk, gv, la, gg, gn, *masks)


def _gla_lat(layer, gq, gk, gv, la, gg, gn, masks, s0):
    rows = DEC_SEQ
    blk0 = N_CTX // rows
    seq = lambda w: pl.BlockSpec((rows, w), lambda b: (blk0 + b, 0))
    return pl.pallas_call(
        partial(_gla_kernel, seq_len=DEC_SEQ, has_state=True),
        grid=(DEC_BATCH,),
        in_specs=[seq(GLA_KW), seq(GLA_KW), seq(GLA_VW), seq(2 * GLA_KW), seq(GLA_VW),
                  _layer_row(layer, GLA_VW)] + _gla_mask_specs() + [
                  pl.BlockSpec((None, None, 2, GLA_HEADS, GLA_DK, GLA_DV), lambda b: (b, layer, 0, 0, 0, 0))],
        out_specs=pl.BlockSpec((rows, GLA_VW), lambda b: (b, 0)),
        out_shape=jax.ShapeDtypeStruct((N_LAT, GLA_VW), BF16),
        scratch_shapes=_gla_scratch(DEC_SEQ),
        compiler_params=pltpu.CompilerParams(dimension_semantics=("parallel",), vmem_limit_bytes=VMEM_LIMIT),
        name="gla_lat",
    )(gq, gk, gv, la, gg, gn, *masks, s0)


def _route(scores, bias):
    biased = scores + bias
    v = [biased[e:e + 1, :] for e in range(N_EXPERTS)]
    s = [scores[e:e + 1, :] for e in range(N_EXPERTS)]
    gscore = []
    for g in range(N_GROUPS):
        a, b, c, d = v[4 * g:4 * g + 4]
        hi1, lo1, hi2, lo2 = jnp.maximum(a, b), jnp.minimum(a, b), jnp.maximum(c, d), jnp.minimum(c, d)
        gscore.append(jnp.maximum(hi1, hi2) + jnp.maximum(jnp.minimum(hi1, hi2), jnp.maximum(lo1, lo2)))
    one = lambda cond: jnp.where(cond, 1.0, 0.0)
    picked, chosen = [], []
    for g in range(N_GROUPS):
        best = None
        for g2 in range(N_GROUPS):
            if g2 == g:
                continue
            ok = one((gscore[g] > gscore[g2]) if g2 < g else (gscore[g] >= gscore[g2]))
            best = ok if best is None else best * ok
        for i in range(EXPERTS_PER_GROUP):
            e = 4 * g + i
            beaten = None
            for j in range(EXPERTS_PER_GROUP):
                if j == i:
                    continue
                e2 = 4 * g + j
                lost = one((v[e2] > v[e]) if j > i else (v[e2] >= v[e]))
                beaten = lost if beaten is None else beaten + lost
            chosen.append(best * one(beaten < 1.5))
            picked.append(chosen[-1] * s[e])
    den = picked[0]
    for e in range(1, N_EXPERTS):
        den = den + picked[e]
    return jnp.concatenate([p / den for p in picked] + chosen, axis=0)


def _mix_kernel(xc_ref, xl_ref, oac_ref, oal_ref, ogc_ref, ogl_ref, sa_ref, sb_ref, mod_ref, wpa_f, wpb_f, wo_f, n2_ref,
                rw_ref, rb_ref, x1_out, h2_out, gt_out, wpa_ref, wpb_ref, wo_ref):
    @pl.when(pl.program_id(0) == 0)
    def _():
        wpa_ref[...] = wpa_f[...].astype(BF16)
        wpb_ref[...] = wpb_f[...].astype(BF16)
        wo_ref[...] = wo_f[...].astype(BF16)

    is_ctx = pl.program_id(0) < N_CTX // TM
    oa = jnp.where(is_ctx, oac_ref[...], oal_ref[...])
    og = jnp.where(is_ctx, ogc_ref[...], ogl_ref[...])
    a = _dot(oa, wpa_ref[...])
    b = _dot(og, wpb_ref[...])
    merged = sa_ref[...].astype(F32) * a + sb_ref[...].astype(F32) * b
    g1 = mod_ref[:, 2 * D_MODEL:3 * D_MODEL]
    sh2 = mod_ref[:, 3 * D_MODEL:4 * D_MODEL]
    sc2 = mod_ref[:, 4 * D_MODEL:5 * D_MODEL]
    x = jnp.where(is_ctx, xc_ref[...], xl_ref[...])
    x1 = x + g1 * _dot(merged.astype(BF16), wo_ref[...])
    x1_out[...] = x1
    h2 = _rms(x1) * (n2_ref[...] * (1.0 + sc2)) + sh2
    _to_row_tiles(h2_out, h2)
    hh, hl = _split2(h2)
    rh, rl = rw_ref[0], rw_ref[1]
    nt = lambda r, t: lax.dot_general(r, t, _NT, preferred_element_type=F32)
    logits = nt(rl, hh) + nt(rh, hl) + nt(rh, hh)
    gt_out[...] = _route(jax.nn.sigmoid(logits), rb_ref[...])


def _mix(layer, x_pair, oa_c, oa_l, og_c, og_l, sa, sb, mod, wpa, wpb, wo, norm2, rw3, rb):
    nt = N_TOK // TM
    n_ctx_t = N_CTX // TM
    row = lambda i: (i, 0)
    const = lambda i: (0, 0)
    ctx_row = lambda i: (jnp.minimum(i, n_ctx_t - 1), 0)
    lat_row = lambda i: (jnp.maximum(i - n_ctx_t, 0), 0)
    return pl.pallas_call(
        _mix_kernel,
        grid=(nt,),
        in_specs=_x_specs(x_pair) + [
            pl.BlockSpec((TM, ATTN_WIDTH), ctx_row),
            pl.BlockSpec((TM, ATTN_WIDTH), lat_row),
            pl.BlockSpec((TM, GLA_VW), ctx_row),
            pl.BlockSpec((TM, GLA_VW), lat_row),
            pl.BlockSpec((TM, D_MODEL), row),
            pl.BlockSpec((TM, D_MODEL), row),
            _mod_spec(layer),
            pl.BlockSpec((None, ATTN_WIDTH, D_MODEL), lambda i: (layer, 0, 0), pipeline_mode=pl.Buffered(1)),
            pl.BlockSpec((None, GLA_VW, D_MODEL), lambda i: (layer, 0, 0), pipeline_mode=pl.Buffered(1)),
            pl.BlockSpec((None, D_MODEL, D_MODEL), lambda i: (layer, 0, 0), pipeline_mode=pl.Buffered(1)),
            _layer_row(layer, D_MODEL),
            pl.BlockSpec((2, N_EXPERTS, D_MODEL), lambda i: (0, 0, 0)),
            pl.BlockSpec((N_EXPERTS, 1), const),
        ],
        out_specs=[pl.BlockSpec((TM, D_MODEL), row), pl.BlockSpec((TM * ROW_TILE, LANES), row),
                   pl.BlockSpec((2 * N_EXPERTS, TM), lambda i: (0, i))],
        out_shape=[jax.ShapeDtypeStruct((N_TOK, D_MODEL), F32),
                   jax.ShapeDtypeStruct((N_TOK * ROW_TILE, LANES), F32),
                   jax.ShapeDtypeStruct((2 * N_EXPERTS, N_TOK), F32)],
        scratch_shapes=[pltpu.VMEM((ATTN_WIDTH, D_MODEL), BF16), pltpu.VMEM((GLA_VW, D_MODEL), BF16),
                        pltpu.VMEM((D_MODEL, D_MODEL), BF16)],
        compiler_params=pltpu.CompilerParams(dimension_semantics=("arbitrary",), vmem_limit_bytes=VMEM_LIMIT),
        name="mix",
    )(*x_pair, oa_c, oa_l, og_c, og_l, sa, sb, mod, wpa, wpb, wo, norm2, rw3, rb)


def _route_plan(gsel):
    gates = gsel[:N_EXPERTS]
    sel = gsel[N_EXPERTS:] > 0.5
    seli = sel.astype(jnp.int32)
    incl = jnp.cumsum(seli, axis=1)
    count = incl[:, -1]
    ntile = (count + TS - 1) // TS
    tile_end = jnp.cumsum(ntile)
    tile_start = tile_end - ntile
    total = tile_end[-1]
    slot = tile_start[:, None] * TS + incl - seli
    pos_a = jnp.min(jnp.where(sel, slot, N_SLOT), axis=0)
    pos_b = jnp.max(jnp.where(sel, slot, -1), axis=0)
    w_a = jnp.sum(jnp.where(sel & (slot == pos_a[None, :]), gates, 0.0), axis=0)
    w_b = jnp.sum(jnp.where(sel & (slot == pos_b[None, :]), gates, 0.0), axis=0)
    nt = N_TOK // TM
    pos = jnp.concatenate([pos_a.reshape(nt, 1, TM), pos_b.reshape(nt, 1, TM)], axis=2).astype(jnp.int32)
    plan = jnp.concatenate([tile_start, ntile, count, total[None]]).astype(jnp.int32)
    return pos, jnp.stack([w_a, w_b], axis=1), plan


def _to_row_tiles(ref, x):
    for k in range(ROW_TILE):
        ref[pl.ds(k, x.shape[0], stride=ROW_TILE), :] = x[:, k * LANES:(k + 1) * LANES]


def _from_row_tiles(ref, rows):
    return jnp.concatenate([ref[pl.ds(k, rows, stride=ROW_TILE), :] for k in range(ROW_TILE)], axis=1)


def _tile_rows(row, n=1):
    return pl.ds(pl.multiple_of(row * ROW_TILE, ROW_TILE), n * ROW_TILE)


def _row_copy(src, src_row, dst, dst_row, sem):
    return pltpu.make_async_copy(src.at[_tile_rows(src_row), :], dst.at[_tile_rows(dst_row), :], sem)


def _dispatch_kernel(plan_ref, pos_ref, h_ref, xs_ref, zero_ref, sem_u, sem_z, sem):
    i = pl.program_id(0)
    total = plan_ref[PLAN_TOTAL]

    def each_unused(fn):
        def unused(t, c):
            @pl.when(t >= total)
            def _():
                fn(pltpu.make_async_copy(zero_ref, xs_ref.at[_tile_rows(t * TS, TS), :], sem_u))
            return c
        lax.fori_loop(0, N_TILE, unused, 0)

    def each_tail(fn):
        def expert(e, c):
            n_tiles = plan_ref[N_EXPERTS + e]

            @pl.when(n_tiles > 0)
            def _():
                row0 = (plan_ref[e] + n_tiles - 1) * TS
                filled = plan_ref[2 * N_EXPERTS + e] - (n_tiles - 1) * TS

                def chunk(j, c2):
                    fn(pltpu.make_async_copy(zero_ref.at[pl.ds(0, Z_ROWS * ROW_TILE), :],
                                             xs_ref.at[_tile_rows(row0 + j * Z_ROWS, Z_ROWS), :], sem_z))
                    return c2

                lax.fori_loop(filled // Z_ROWS, TS // Z_ROWS, chunk, 0)
            return c
        lax.fori_loop(0, N_EXPERTS, expert, 0)

    @pl.when(i == 0)
    def _():
        zero_ref[...] = jnp.zeros(zero_ref.shape, F32)
        each_unused(lambda cp: cp.start())
        each_tail(lambda cp: cp.start())
        each_tail(lambda cp: cp.wait())

    @pl.when(i == N_TOK // TM - 1)
    def _():
        each_unused(lambda cp: cp.wait())

    def start(r, c):
        _row_copy(h_ref, r, xs_ref, pos_ref[0, r], sem).start()
        _row_copy(h_ref, r, xs_ref, pos_ref[0, TM + r], sem).start()
        return c

    def wait(r, c):
        _row_copy(h_ref, 0, xs_ref, 0, sem).wait()
        return c

    lax.fori_loop(0, TM, start, 0, unroll=8)
    lax.fori_loop(0, 2 * TM, wait, 0, unroll=8)


def _dispatch(plan, pos, h2):
    return pl.pallas_call(
        _dispatch_kernel,
        grid_spec=pltpu.PrefetchScalarGridSpec(
            num_scalar_prefetch=1,
            grid=(N_TOK // TM,),
            in_specs=[
                pl.BlockSpec((None, 1, 2 * TM), lambda i, plan: (i, 0, 0), memory_space=pltpu.SMEM),
                pl.BlockSpec((TM * ROW_TILE, LANES), lambda i, plan: (i, 0)),
            ],
            out_specs=pl.BlockSpec(memory_space=pl.ANY),
            scratch_shapes=[pltpu.VMEM((TS * ROW_TILE, LANES), F32), pltpu.SemaphoreType.DMA,
                            pltpu.SemaphoreType.DMA, pltpu.SemaphoreType.DMA],
        ),
        out_shape=jax.ShapeDtypeStruct((N_SLOT * ROW_TILE, LANES), F32),
        compiler_params=pltpu.CompilerParams(dimension_semantics=("arbitrary",), vmem_limit_bytes=VMEM_LIMIT),
        name="dispatch",
    )(plan, pos, h2)


def _moe_kernel(plan_ref, xs_ref, wg_ref, wu_ref, wd_ref, ys_ref, x_buf, y_buf, zero_ref, sem_in, sem_out, sem_z,
                wg_s, wu_s, wd_s):
    e = pl.program_id(0)
    total = plan_ref[PLAN_TOTAL]
    first = plan_ref[e]

    def in_copy(g):
        return pltpu.make_async_copy(xs_ref.at[_tile_rows(g * TS, TS), :], x_buf.at[g % 2], sem_in.at[g % 2])

    def out_copy(g):
        return pltpu.make_async_copy(y_buf.at[g % 2], ys_ref.at[_tile_rows(g * TS, TS), :], sem_out.at[g % 2])

    def zero_copy(g):
        return pltpu.make_async_copy(zero_ref, ys_ref.at[_tile_rows(g * TS, TS), :], sem_z)

    def each_unused(fn):
        def unused(g, c):
            @pl.when(g >= total)
            def _():
                fn(zero_copy(g))
            return c
        lax.fori_loop(0, N_TILE, unused, 0)

    @pl.when(e == 0)
    def _():
        in_copy(0).start()
        zero_ref[...] = jnp.zeros(zero_ref.shape, F32)
        each_unused(lambda cp: cp.start())

    wg_s[...] = wg_ref[...].astype(BF16)
    wu_s[...] = wu_ref[...].astype(BF16)
    wd_s[...] = wd_ref[...].astype(BF16)

    def tile(t, c):
        g = first + t
        slot = g % 2

        @pl.when(g + 1 < total)
        def _():
            in_copy(g + 1).start()

        in_copy(g).wait()

        @pl.when(g >= 2)
        def _():
            out_copy(g - 2).wait()

        x = _from_row_tiles(x_buf.at[slot], TS).astype(BF16)
        up = _dot(x, wu_s[...])
        gt = _dot(x, wg_s[...])
        act = (gt * jax.nn.sigmoid(gt) * up).astype(BF16)
        _to_row_tiles(y_buf.at[slot], _dot(act, wd_s[...]))
        out_copy(g).start()
        return c

    lax.fori_loop(0, plan_ref[N_EXPERTS + e], tile, 0)

    @pl.when(e == N_EXPERTS - 1)
    def _():
        out_copy(total - 1).wait()

        @pl.when(total >= 2)
        def _():
            out_copy(total - 2).wait()

        each_unused(lambda cp: cp.wait())


def _moe(layer, plan, xs, w_gate, w_up, w_down):
    w_spec = lambda a, b: pl.BlockSpec((None, None, a, b), lambda e, plan: (layer, e, 0, 0))
    tile_buf = lambda n: pltpu.VMEM((n, TS * ROW_TILE, LANES), F32)
    return pl.pallas_call(
        _moe_kernel,
        grid_spec=pltpu.PrefetchScalarGridSpec(
            num_scalar_prefetch=1,
            grid=(N_EXPERTS,),
            in_specs=[
                pl.BlockSpec(memory_space=pl.ANY),
                w_spec(D_MODEL, D_EXPERT), w_spec(D_MODEL, D_EXPERT), w_spec(D_EXPERT, D_MODEL),
            ],
            out_specs=pl.BlockSpec(memory_space=pl.ANY),
            scratch_shapes=[tile_buf(2), tile_buf(2), pltpu.VMEM((TS * ROW_TILE, LANES), F32),
                            pltpu.SemaphoreType.DMA((2,)), pltpu.SemaphoreType.DMA((2,)), pltpu.SemaphoreType.DMA,
                            pltpu.VMEM((D_MODEL, D_EXPERT), BF16), pltpu.VMEM((D_MODEL, D_EXPERT), BF16),
                            pltpu.VMEM((D_EXPERT, D_MODEL), BF16)],
        ),
        out_shape=jax.ShapeDtypeStruct((N_SLOT * ROW_TILE, LANES), F32),
        compiler_params=pltpu.CompilerParams(dimension_semantics=("arbitrary",), vmem_limit_bytes=VMEM_LIMIT),
        name="moe",
    )(plan, xs, w_gate, w_up, w_down)


def _combine_kernel(pos_ref, nxt_ref, w_ref, x1_ref, mod_ref, ys_ref, out_ref, ya_ref, yb_ref, sem, *, n):
    i = pl.program_id(0)

    def gather(p_ref, slot):
        def start(r, c):
            _row_copy(ys_ref, p_ref[0, r], ya_ref.at[slot], r, sem.at[slot]).start()
            _row_copy(ys_ref, p_ref[0, TM + r], yb_ref.at[slot], r, sem.at[slot]).start()
            return c
        lax.fori_loop(0, TM, start, 0, unroll=8)

    @pl.when(i == 0)
    def _():
        gather(pos_ref, 0)

    @pl.when(i + 1 < n)
    def _():
        gather(nxt_ref, (i + 1) % 2)

    slot = i % 2

    def wait(r, c):
        _row_copy(ys_ref, 0, ya_ref.at[slot], 0, sem.at[slot]).wait()
        return c

    lax.fori_loop(0, 2 * TM, wait, 0, unroll=8)
    g2 = mod_ref[:, 5 * D_MODEL:6 * D_MODEL]
    w = w_ref[...]
    ya = _from_row_tiles(ya_ref.at[slot], TM)
    yb = _from_row_tiles(yb_ref.at[slot], TM)
    out_ref[...] = x1_ref[...] + g2 * (w[:, 0:1] * ya + w[:, 1:2] * yb)


def _combine(layer, pos, w_ab, x1, mod, ys, tile0, nt):
    pos_spec = lambda fn: pl.BlockSpec((None, 1, 2 * TM), fn, memory_space=pltpu.SMEM)
    return pl.pallas_call(
        partial(_combine_kernel, n=nt),
        grid=(nt,),
        in_specs=[
            pos_spec(lambda i: (tile0 + i, 0, 0)),
            pos_spec(lambda i: (tile0 + jnp.minimum(i + 1, nt - 1), 0, 0)),
            pl.BlockSpec((TM, 2), lambda i: (tile0 + i, 0)),
            pl.BlockSpec((TM, D_MODEL), lambda i: (tile0 + i, 0)),
            _mod_spec(layer, tile0),
            pl.BlockSpec(memory_space=pl.ANY),
        ],
        out_specs=pl.BlockSpec((TM, D_MODEL), lambda i: (i, 0)),
        out_shape=jax.ShapeDtypeStruct((nt * TM, D_MODEL), F32),
        scratch_shapes=[pltpu.VMEM((2, TM * ROW_TILE, LANES), F32), pltpu.VMEM((2, TM * ROW_TILE, LANES), F32),
                        pltpu.SemaphoreType.DMA((2,))],
        compiler_params=pltpu.CompilerParams(dimension_semantics=("arbitrary",), vmem_limit_bytes=VMEM_LIMIT),
        name="combine",
    )(pos, pos, w_ab, x1, mod, ys)


def _rope_tables():
    pos = np.arange(DEC_SEQ)
    rows = (pos // GRID_W).astype(np.float64)
    cols = (pos % GRID_W).astype(np.float64)
    half = HEAD_DIM // 2
    inv_freq = ROPE_THETA ** (-np.arange(0, half, 2, dtype=np.float64) / half)
    ang_r = rows[:, None] * inv_freq[None, :]
    ang_c = cols[:, None] * inv_freq[None, :]
    cos_h = np.concatenate([np.cos(ang_r), np.cos(ang_r), np.cos(ang_c), np.cos(ang_c)], axis=-1)
    sin_h = np.concatenate([-np.sin(ang_r), np.sin(ang_r), -np.sin(ang_c), np.sin(ang_c)], axis=-1)
    cos = np.concatenate([np.ones((TM, HEAD_DIM)), cos_h], axis=0)
    sin = np.concatenate([np.zeros((TM, HEAD_DIM)), sin_h], axis=0)
    return (jnp.asarray(np.tile(cos, (1, N_HEADS)), F32), jnp.asarray(np.tile(sin, (1, N_HEADS)), F32))


def _pack_w2(w2):
    z = jnp.zeros((DEPTH, GLA_RANK, GLA_KW), w2.dtype)
    top = jnp.concatenate([w2[:, 0], z], axis=2)
    bot = jnp.concatenate([z, w2[:, 1]], axis=2)
    pad = jnp.zeros((DEPTH, LR_PAD - 2 * GLA_RANK, 2 * GLA_KW), w2.dtype)
    return jnp.concatenate([top, bot, pad], axis=1).astype(BF16)


def kernel(x_prompt, x_sample, cache_k, cache_v, state_gla, c, c_ctx, norm1, norm2, w_ada, b_ada, w_in,
           q_norm, k_norm, gla_w2, gla_b, gla_norm, w_pa, w_pb, w_out, router_w, router_bias,
           w_gate, w_up, w_down):
    x_pair = (x_prompt.reshape(N_CTX, D_MODEL), x_sample.reshape(N_LAT, D_MODEL))
    cond = jnp.concatenate([c_ctx[None, :], c, jnp.zeros((N_COND - 1 - DEC_BATCH, D_MODEL), F32)], axis=0)
    mod = _ada(cond, w_ada, b_ada).reshape(DEPTH, N_COND, 1, MOD_W)
    cos, sin = _rope_tables()
    gla_masks = _gla_masks()
    w_in_t = jnp.swapaxes(w_in, 1, 2)
    lane_head = np.arange(ATTN_WIDTH) // HEAD_DIM
    bd = jnp.asarray(lane_head[:, None] == lane_head[None, :], BF16)
    rw3 = jnp.stack(_split2(router_w.T), axis=0)
    rb = router_bias.reshape(N_EXPERTS, 1)
    ck = cache_k.reshape(DEC_BATCH, DEPTH, PAST_LEN, KV_WIDTH)
    cv = cache_v.reshape(DEC_BATCH, DEPTH, PAST_LEN, KV_WIDTH)
    rows = lambda p, reps: jnp.tile(p, (1, reps)).reshape(DEPTH, 1, -1)
    n1, n2 = rows(norm1, 1), rows(norm2, 1)
    qn, kn, gn = rows(q_norm, N_HEADS), rows(k_norm, KV_HEADS), rows(gla_norm, GLA_HEADS)
    gb = gla_b.reshape(DEPTH, 1, 2 * GLA_KW)
    w2p = _pack_w2(gla_w2)

    keys, vals, states = [], [], []
    for l in range(DEPTH):
        q, kf, kr, v, gq, gk, gv, gg, la, sa, sb = _inproj(l, x_pair, mod, n1, w_in_t, qn, kn, cos, sin, bd, w2p, gb)
        oa_c = _attn_ctx(q, kr, v)
        oa_l = _attn_lat(l, q, kr, v, ck, cv)
        og_c, s_fin = _gla_ctx(l, gq, gk, gv, la, gg, gn, gla_masks)
        og_l = _gla_lat(l, gq, gk, gv, la, gg, gn, gla_masks, state_gla)
        x1, h2, gsel = _mix(l, x_pair, oa_c, oa_l, og_c, og_l, sa, sb, mod, w_pa, w_pb, w_out, n2, rw3, rb)
        pos, w_ab, plan = _route_plan(gsel)
        xs = _dispatch(plan, pos, h2)
        ys = _moe(l, plan, xs, w_gate, w_up, w_down)
        n_ctx_t = N_CTX // TM
        if l + 1 < DEPTH:
            x = _combine(l, pos, w_ab, x1, mod, ys, 0, N_TOK // TM)
            x_pair = (x, x)
        else:
            x_pair = (_combine(l, pos, w_ab, x1, mod, ys, 0, n_ctx_t),
                      _combine(l, pos, w_ab, x1, mod, ys, n_ctx_t, N_LAT // TM))
        keys.append(kf[:N_CTX].reshape(BATCH, SEQ, KV_HEADS, HEAD_DIM))
        vals.append(v[:N_CTX].reshape(BATCH, SEQ, KV_HEADS, HEAD_DIM))
        states.append(s_fin)
    y_prompt = x_pair[0].reshape(BATCH, SEQ, D_MODEL)
    y_sample = x_pair[1].reshape(DEC_BATCH, DEC_SEQ, D_MODEL)
    return (y_prompt, y_sample, jnp.stack(keys, axis=1), jnp.stack(vals, axis=1), jnp.stack(states, axis=1))
```

```python
from functools import partial

import jax
import jax.numpy as jnp
import numpy as np
from jax import lax
from jax.experimental import pallas as pl
from jax.experimental.pallas import tpu as pltpu

F32 = jnp.float32
BF16 = jnp.bfloat16

D_MODEL = 1024
BATCH = 16
SEQ = 256
DEPTH = 2
DEC_BATCH = 4
DEC_SEQ = 1024
PAST_LEN = 512
GRID_W = 64
N_HEADS = 8
KV_HEADS = 2
GROUP = N_HEADS // KV_HEADS
HEAD_DIM = 64
ATTN_WIDTH = N_HEADS * HEAD_DIM
KV_WIDTH = KV_HEADS * HEAD_DIM
ROPE_THETA = 10000.0
GLA_HEADS = 4
GLA_DK = 64
GLA_DV = 128
GLA_KW = GLA_HEADS * GLA_DK
GLA_VW = GLA_HEADS * GLA_DV
GLA_RANK = 16
GLA_GATE_NORM = 16.0
N_EXPERTS = 16
N_GROUPS = 4
EXPERTS_PER_GROUP = N_EXPERTS // N_GROUPS
D_EXPERT = 512
NORM_EPS = 1e-6

N_CTX = BATCH * SEQ
N_LAT = DEC_BATCH * DEC_SEQ
N_TOK = N_CTX + N_LAT
N_COND = 8
MOD_W = 6 * D_MODEL

TM = 512
TS = 512
N_TILE = 2 * N_TOK // TS + N_EXPERTS
N_SLOT = N_TILE * TS
PLAN_TOTAL = 3 * N_EXPERTS
Z_ROWS = 64
ROW_TILE = 8
LANES = D_MODEL // ROW_TILE
GLA_CHUNK = 128
GLA_LEVELS = (64, 32, 16)
GLA_DIAG = 16
LR_PAD = 128
W_ROWS = 128
SEG_MIX = ATTN_WIDTH + 2 * KV_WIDTH + 2 * GLA_KW + 2 * GLA_VW
OFF_GA = SEG_MIX
OFF_GB = OFF_GA + D_MODEL
OFF_LR = OFF_GB + D_MODEL
IN_PACKED = OFF_LR + LR_PAD
IN_WIDTH = SEG_MIX + 2 * GLA_RANK + 2 * D_MODEL
VMEM_LIMIT = 56 * 1024 * 1024

_NT = (((1,), (1,)), ((), ()))
_TN = (((0,), (0,)), ((), ()))


def _mod_row(tile, tm):
    start = tile * tm
    return jnp.where(start < N_CTX, 0, 1 + (start - N_CTX) // DEC_SEQ)


def _split2(x):
    hi = x.astype(BF16)
    lo = (x - hi.astype(F32)).astype(BF16)
    return hi, lo


def _dot(a, b):
    return jnp.dot(a, b, preferred_element_type=F32)


def _rms(x):
    return x * lax.rsqrt(jnp.mean(x * x, axis=-1, keepdims=True) + NORM_EPS)


def _ada_kernel(cond_ref, w_ref, b_ref, out_ref):
    c = cond_ref[...]
    s = c * jax.nn.sigmoid(c)
    out_ref[...] = _dot(s.astype(BF16), w_ref[...].astype(BF16)) + b_ref[...]


def _ada(cond, w_ada, b_ada):
    nb = MOD_W // D_MODEL
    return pl.pallas_call(
        _ada_kernel,
        grid=(DEPTH, nb),
        in_specs=[
            pl.BlockSpec((N_COND, D_MODEL), lambda l, j: (0, 0)),
            pl.BlockSpec((None, D_MODEL, D_MODEL), lambda l, j: (l, 0, j)),
            pl.BlockSpec((None, 1, D_MODEL), lambda l, j: (l, 0, j)),
        ],
        out_specs=pl.BlockSpec((None, N_COND, D_MODEL), lambda l, j: (l, 0, j)),
        out_shape=jax.ShapeDtypeStruct((DEPTH, N_COND, MOD_W), F32),
        compiler_params=pltpu.CompilerParams(vmem_limit_bytes=VMEM_LIMIT),
        name="ada",
    )(cond, w_ada, b_ada.reshape(DEPTH, 1, MOD_W))


def _head_rmsnorm(x, bd):
    ssq = _dot((x * x).astype(BF16), bd)
    return x * lax.rsqrt(ssq * (1.0 / HEAD_DIM) + NORM_EPS)


def _rope(x, cos, sin):
    n = x.shape[-1]
    lane = lax.broadcasted_iota(jnp.int32, x.shape, 1)
    first = (lane & (HEAD_DIM // 4)) == 0
    partner = jnp.where(first, pltpu.roll(x, n - HEAD_DIM // 4, 1), pltpu.roll(x, HEAD_DIM // 4, 1))
    return x * cos + partner * sin


def _pack_w_in(w_ref, wp_ref):
    lr0 = SEG_MIX
    ga0 = lr0 + 2 * GLA_RANK

    def cast(dst, src, n):
        def body(i, carry):
            rows_out = pl.ds(pl.multiple_of(dst + i * W_ROWS, W_ROWS), W_ROWS)
            rows_in = pl.ds(pl.multiple_of(src + i * W_ROWS, 2 * GLA_RANK), W_ROWS)
            wp_ref[rows_out, :] = w_ref[rows_in, :].astype(BF16)
            return carry
        lax.fori_loop(0, n // W_ROWS, body, 0)

    cast(0, 0, SEG_MIX)
    cast(OFF_GA, ga0, 2 * D_MODEL)
    wp_ref[OFF_LR:IN_PACKED, :] = jnp.concatenate(
        [w_ref[lr0:ga0, :], jnp.zeros((LR_PAD - 2 * GLA_RANK, D_MODEL), F32)], axis=0).astype(BF16)


def _inproj_kernel(xc_ref, xl_ref, mod_ref, n1_ref, wf_ref, qn_ref, kn_ref, cos_ref, sin_ref, bd_ref, w2_ref, gb_ref,
                   q_out, kf_out, kr_out, v_out, gq_out, gk_out, gv_out, gg_out, la_out, sa_out, sb_out,
                   w_ref):
    @pl.when(pl.program_id(0) == 0)
    def _():
        _pack_w_in(wf_ref, w_ref)

    x = jnp.where(pl.program_id(0) < N_CTX // TM, xc_ref[...], xl_ref[...])
    sh1 = mod_ref[:, 0:D_MODEL]
    sc1 = mod_ref[:, D_MODEL:2 * D_MODEL]
    h = (_rms(x) * (n1_ref[...] * (1.0 + sc1)) + sh1).astype(BF16)

    def proj(lo, width):
        return lax.dot_general(h, w_ref[lo:lo + width, :], _NT, preferred_element_type=F32)

    cos = cos_ref[...]
    sin = sin_ref[...]
    bd = bd_ref[...]
    off = 0
    q = _head_rmsnorm(proj(off, ATTN_WIDTH), bd) * qn_ref[...]
    q_out[...] = (_rope(q, cos, sin) * HEAD_DIM ** -0.5).astype(BF16)
    off += ATTN_WIDTH
    k = _head_rmsnorm(proj(off, KV_WIDTH), bd[:KV_WIDTH, :KV_WIDTH]) * kn_ref[...]
    kf_out[...] = k
    kr_out[...] = _rope(k, cos[:, :KV_WIDTH], sin[:, :KV_WIDTH]).astype(BF16)
    off += KV_WIDTH
    v_out[...] = proj(off, KV_WIDTH)
    off += KV_WIDTH
    gq_out[...] = (proj(off, GLA_KW) * GLA_DK ** -0.5).astype(BF16)
    off += GLA_KW
    gk_out[...] = proj(off, GLA_KW).astype(BF16)
    off += GLA_KW
    gv_out[...] = proj(off, GLA_VW).astype(BF16)
    off += GLA_VW
    gg = proj(off, GLA_VW)
    gg_out[...] = (gg * jax.nn.sigmoid(gg)).astype(BF16)
    sa_out[...] = jax.nn.sigmoid(proj(OFF_GA, D_MODEL)).astype(BF16)
    sb_out[...] = jax.nn.sigmoid(proj(OFF_GB, D_MODEL)).astype(BF16)
    lr = proj(OFF_LR, LR_PAD)
    z = _dot(lr.astype(BF16), w2_ref[...]) + gb_ref[...]
    log_sig = jnp.minimum(z, 0.0) - jnp.log1p(jnp.exp(-jnp.abs(z)))
    la_out[...] = log_sig * (1.0 / GLA_GATE_NORM)


def _layer_row(layer, width):
    return pl.BlockSpec((None, 1, width), lambda *_: (layer, 0, 0))


def _mod_spec(layer, tile0=0):
    return pl.BlockSpec((None, None, 1, MOD_W), lambda i, *_: (layer, _mod_row(tile0 + i, TM), 0, 0))


def _x_specs(x_pair):
    n_ctx_t = N_CTX // TM
    lat0 = n_ctx_t if x_pair[0] is x_pair[1] else 0
    return [pl.BlockSpec((TM, D_MODEL), lambda i: (jnp.minimum(i, n_ctx_t - 1), 0)),
            pl.BlockSpec((TM, D_MODEL), lambda i: (lat0 + jnp.maximum(i - n_ctx_t, 0), 0))]


def _inproj(layer, x_pair, mod, norm1, w_in, qn, kn, cos, sin, bd, w2p, gb):
    nt = N_TOK // TM
    n_ctx_t = N_CTX // TM
    t_per_seq = DEC_SEQ // TM
    row = lambda i: (i, 0)
    const = lambda i: (0, 0)
    rope_blk = lambda i: (jnp.where(i < n_ctx_t, 0, 1 + (i - n_ctx_t) % t_per_seq), 0)

    def out(width, dtype):
        return pl.BlockSpec((TM, width), row), jax.ShapeDtypeStruct((N_TOK, width), dtype)

    outs = [out(ATTN_WIDTH, BF16), out(KV_WIDTH, F32), out(KV_WIDTH, BF16), out(KV_WIDTH, F32),
            out(GLA_KW, BF16), out(GLA_KW, BF16), out(GLA_VW, BF16), out(GLA_VW, BF16),
            out(2 * GLA_KW, F32), out(D_MODEL, BF16), out(D_MODEL, BF16)]
    return pl.pallas_call(
        _inproj_kernel,
        grid=(nt,),
        in_specs=_x_specs(x_pair) + [
            _mod_spec(layer),
            _layer_row(layer, D_MODEL),
            pl.BlockSpec((None, IN_WIDTH, D_MODEL), lambda i: (layer, 0, 0), pipeline_mode=pl.Buffered(1)),
            _layer_row(layer, ATTN_WIDTH),
            _layer_row(layer, KV_WIDTH),
            pl.BlockSpec((TM, ATTN_WIDTH), rope_blk),
            pl.BlockSpec((TM, ATTN_WIDTH), rope_blk),
            pl.BlockSpec((ATTN_WIDTH, ATTN_WIDTH), const),
            pl.BlockSpec((None, LR_PAD, 2 * GLA_KW), lambda i: (layer, 0, 0)),
            _layer_row(layer, 2 * GLA_KW),
        ],
        out_specs=[o[0] for o in outs],
        out_shape=[o[1] for o in outs],
        scratch_shapes=[pltpu.VMEM((IN_PACKED, D_MODEL), BF16)],
        compiler_params=pltpu.CompilerParams(dimension_semantics=("arbitrary",), vmem_limit_bytes=VMEM_LIMIT),
        name="inproj",
    )(*x_pair, mod, norm1, w_in, qn, kn, cos, sin, bd, w2p, gb)


def _attn_kernel(*refs, has_cache):
    if has_cache:
        q_ref, k_ref, v_ref, ck_ref, cv_ref, o_ref = refs
    else:
        q_ref, k_ref, v_ref, o_ref = refs
    tq = q_ref.shape[0]
    for g in range(KV_HEADS):
        ks = slice(g * HEAD_DIM, (g + 1) * HEAD_DIM)
        qs = jnp.concatenate(
            [q_ref[:, (g * GROUP + j) * HEAD_DIM:(g * GROUP + j + 1) * HEAD_DIM] for j in range(GROUP)], axis=0)
        k = k_ref[:, ks]
        v = v_ref[:, ks].astype(BF16)
        s = lax.dot_general(qs, k, _NT, preferred_element_type=F32)
        m = jnp.max(s, axis=-1, keepdims=True)
        if has_cache:
            ck = ck_ref[:, ks].astype(BF16)
            cv = cv_ref[:, ks].astype(BF16)
            s2 = lax.dot_general(qs, ck, _NT, preferred_element_type=F32)
            m = jnp.maximum(m, jnp.max(s2, axis=-1, keepdims=True))
        p = jnp.exp(s - m)
        den = jnp.sum(p, axis=-1, keepdims=True)
        o = _dot(p.astype(BF16), v)
        if has_cache:
            p2 = jnp.exp(s2 - m)
            den = den + jnp.sum(p2, axis=-1, keepdims=True)
            o = o + _dot(p2.astype(BF16), cv)
        o = o / den
        for j in range(GROUP):
            hd = g * GROUP + j
            o_ref[:, hd * HEAD_DIM:(hd + 1) * HEAD_DIM] = o[j * tq:(j + 1) * tq, :].astype(BF16)


def _attn_ctx(q, kr, v):
    return pl.pallas_call(
        partial(_attn_kernel, has_cache=False),
        grid=(BATCH,),
        in_specs=[
            pl.BlockSpec((SEQ, ATTN_WIDTH), lambda i: (i, 0)),
            pl.BlockSpec((SEQ, KV_WIDTH), lambda i: (i, 0)),
            pl.BlockSpec((SEQ, KV_WIDTH), lambda i: (i, 0)),
        ],
        out_specs=pl.BlockSpec((SEQ, ATTN_WIDTH), lambda i: (i, 0)),
        out_shape=jax.ShapeDtypeStruct((N_CTX, ATTN_WIDTH), BF16),
        compiler_params=pltpu.CompilerParams(dimension_semantics=("parallel",), vmem_limit_bytes=VMEM_LIMIT),
        name="attn_ctx",
    )(q, kr, v)


def _attn_lat(layer, q, kr, v, cache_k, cache_v):
    tq = 256
    nq = DEC_SEQ // tq
    q0 = N_CTX // tq
    s0 = N_CTX // DEC_SEQ
    return pl.pallas_call(
        partial(_attn_kernel, has_cache=True),
        grid=(DEC_BATCH, nq),
        in_specs=[
            pl.BlockSpec((tq, ATTN_WIDTH), lambda b, i: (q0 + b * nq + i, 0)),
            pl.BlockSpec((DEC_SEQ, KV_WIDTH), lambda b, i: (s0 + b, 0)),
            pl.BlockSpec((DEC_SEQ, KV_WIDTH), lambda b, i: (s0 + b, 0)),
            pl.BlockSpec((None, None, PAST_LEN, KV_WIDTH), lambda b, i: (b, layer, 0, 0)),
            pl.BlockSpec((None, None, PAST_LEN, KV_WIDTH), lambda b, i: (b, layer, 0, 0)),
        ],
        out_specs=pl.BlockSpec((tq, ATTN_WIDTH), lambda b, i: (b * nq + i, 0)),
        out_shape=jax.ShapeDtypeStruct((N_LAT, ATTN_WIDTH), BF16),
        compiler_params=pltpu.CompilerParams(dimension_semantics=("parallel", "parallel"),
                                             vmem_limit_bytes=VMEM_LIMIT),
        name="attn_lat",
    )(q, kr, v, cache_k, cache_v)


def _gla_masks():
    c = GLA_CHUNK
    hrow = np.arange(GLA_HEADS * c) // c
    head_k = (hrow[:, None] == (np.arange(GLA_KW) // GLA_DK)[None, :])
    head_v = (hrow[:, None] == (np.arange(GLA_VW) // GLA_DV)[None, :])
    t = np.arange(c)[:, None]
    s = (np.arange(GLA_HEADS * c) % c)[None, :]
    pair = []
    for reverse in (False, True):
        for m in GLA_LEVELS:
            pair.append(t // (2 * m) == s // (2 * m))
        pair.append((t // GLA_DIAG == s // GLA_DIAG) & ((s >= t) if reverse else (s <= t)))
    return (jnp.asarray(head_k, BF16), jnp.asarray(head_v, BF16), jnp.asarray(np.stack(pair), F32))


def _block_diag(x, head_mask):
    return jnp.concatenate([x] * GLA_HEADS, axis=0) * head_mask


def _ref_rows(b, rows, rep):
    return jnp.concatenate([jnp.broadcast_to(b[r:r + 1, :], (rep, b.shape[1])) for r in rows], axis=0)


def _score_operands(q, k, b, reverse, head_k):
    c = GLA_CHUNK
    t_idx = lax.broadcasted_iota(jnp.int32, (c, GLA_KW), 0)
    terms = []
    for m in GLA_LEVELS:
        nblk = c // (2 * m)
        ref = _ref_rows(b, [2 * m * j + (m if reverse else m - 1) for j in range(nblk)], 2 * m)
        q_side = ((t_idx & m) == 0) if reverse else ((t_idx & m) != 0)
        terms.append((jnp.where(q_side, q * jnp.exp(b - ref), 0.0), jnp.where(q_side, 0.0, k * jnp.exp(ref - b))))
    d = GLA_DIAG
    x = b - _ref_rows(b, [d * j + d // 2 for j in range(c // d)], d)
    terms.append((q * jnp.exp(x), k * jnp.exp(-x)))
    return [(qt.astype(BF16), _block_diag(kt.astype(BF16), head_k)) for qt, kt in terms]


def _score_sum(parts, reverse, pair_ref):
    term0 = (len(GLA_LEVELS) + 1) if reverse else 0
    total = jnp.where(pair_ref[term0 + len(GLA_LEVELS)] > 0.5, parts[-1], 0.0)
    for i, part in enumerate(parts[:-1]):
        total = total + part * pair_ref[term0 + i]
    return total


def _cumsum_rows(tri, la):
    hi, lo = _split2(la)
    return _dot(tri, hi) + _dot(tri, lo)


def _gla_kernel(*refs, seq_len, has_state):
    gq_ref, gk_ref, gv_ref, la_ref, gg_ref, gn_ref, hk_ref, hv_ref, pair_ref = refs[:9]
    if has_state:
        s0_ref, o_ref, acc_ref, accb_ref, st_ref = refs[9:]
    else:
        o_ref, sfin_ref, acc_ref, accb_ref, st_ref = refs[9:]
    c = GLA_CHUNK
    nc = seq_len // c
    r_i = lax.broadcasted_iota(jnp.int32, (c, c), 0)
    c_i = lax.broadcasted_iota(jnp.int32, (c, c), 1)
    tril = jnp.where(c_i <= r_i, 1.0, 0.0).astype(BF16)
    triu = jnp.where(c_i >= r_i, 1.0, 0.0).astype(BF16)
    st_head_r = lax.broadcasted_iota(jnp.int32, (GLA_VW, GLA_KW), 0) // GLA_DV
    st_head_c = lax.broadcasted_iota(jnp.int32, (GLA_VW, GLA_KW), 1) // GLA_DK
    st_mask = st_head_r == st_head_c

    for d in range(2):
        st_ref[d] = jnp.zeros((GLA_VW, GLA_KW), F32)
        if has_state:
            for hh in range(GLA_HEADS):
                st_ref[d, hh * GLA_DV:(hh + 1) * GLA_DV, hh * GLA_DK:(hh + 1) * GLA_DK] = s0_ref[d, hh].T

    def load(n):
        rows = pl.ds(pl.multiple_of(n * c, c), c)
        return (rows, gq_ref[rows, :].astype(F32), gk_ref[rows, :].astype(F32), gv_ref[rows, :])

    nt = lambda a, b: lax.dot_general(a, b, _NT, preferred_element_type=F32)
    tn = lambda a, b: lax.dot_general(a, b, _TN, preferred_element_type=F32)

    def body(n, carry):
        rows, q, k, v = load(n)
        rows_b, q_b, k_b, v_b = load(nc - 1 - n)
        b_f = _cumsum_rows(tril, la_ref[rows, 0:GLA_KW])
        b_r = _cumsum_rows(triu, la_ref[rows, GLA_KW:2 * GLA_KW])
        b_b = _cumsum_rows(triu, la_ref[rows_b, GLA_KW:2 * GLA_KW])
        head_k = hk_ref[...]
        ops_f = _score_operands(q, k, b_f, False, head_k)
        ops_r = _score_operands(q, k, b_r, True, head_k)
        end_f, end_b = b_f[c - 1:c, :], b_b[0:1, :]
        qin_f = (q * jnp.exp(b_f)).astype(BF16)
        qin_b = (q_b * jnp.exp(b_b)).astype(BF16)
        kst_f = (k * jnp.exp(end_f - b_f)).astype(BF16)
        kst_b = (k_b * jnp.exp(end_b - b_b)).astype(BF16)
        st_f, st_b = st_ref[0], st_ref[1]
        parts_f = [nt(qt, kt) for qt, kt in ops_f]
        parts_r = [nt(qt, kt) for qt, kt in ops_r]
        o_f = nt(qin_f, st_f.astype(BF16))
        o_b = nt(qin_b, st_b.astype(BF16))
        upd_f = tn(v, kst_f)
        upd_b = tn(v_b, kst_b)
        a = _score_sum(parts_f, False, pair_ref) + _score_sum(parts_r, True, pair_ref)
        acc_ref[rows, :] = _dot(a.astype(BF16), _block_diag(v, hv_ref[...])) + o_f
        accb_ref[rows_b, :] = o_b
        st_ref[0] = st_f * jnp.exp(end_f) + jnp.where(st_mask, upd_f, 0.0)
        st_ref[1] = st_b * jnp.exp(end_b) + jnp.where(st_mask, upd_b, 0.0)
        return carry

    lax.fori_loop(0, nc, body, 0)

    o = acc_ref[...] + accb_ref[...]
    gn = gn_ref[...]
    for hh in range(GLA_HEADS):
        sl = slice(hh * GLA_DV, (hh + 1) * GLA_DV)
        o_ref[:, sl] = (_rms(o[:, sl]) * gn[:, sl] * gg_ref[:, sl].astype(F32)).astype(BF16)
    if not has_state:
        for d in range(2):
            for hh in range(GLA_HEADS):
                sfin_ref[d, hh] = st_ref[d, hh * GLA_DV:(hh + 1) * GLA_DV, hh * GLA_DK:(hh + 1) * GLA_DK].T


def _gla_scratch(seq_len):
    return [pltpu.VMEM((seq_len, GLA_VW), F32), pltpu.VMEM((seq_len, GLA_VW), F32),
            pltpu.VMEM((2, GLA_VW, GLA_KW), F32)]


def _gla_mask_specs():
    n_terms = 2 * (len(GLA_LEVELS) + 1)
    return [pl.BlockSpec((GLA_HEADS * GLA_CHUNK, GLA_KW), lambda i: (0, 0)),
            pl.BlockSpec((GLA_HEADS * GLA_CHUNK, GLA_VW), lambda i: (0, 0)),
            pl.BlockSpec((n_terms, GLA_CHUNK, GLA_HEADS * GLA_CHUNK), lambda i: (0, 0, 0))]


def _gla_ctx(layer, gq, gk, gv, la, gg, gn, masks):
    seq = lambda w: pl.BlockSpec((SEQ, w), lambda i: (i, 0))
    return pl.pallas_call(
        partial(_gla_kernel, seq_len=SEQ, has_state=False),
        grid=(BATCH,),
        in_specs=[seq(GLA_KW), seq(GLA_KW), seq(GLA_VW), seq(2 * GLA_KW), seq(GLA_VW),
                  _layer_row(layer, GLA_VW)] + _gla_mask_specs(),
        out_specs=[seq(GLA_VW),
                   pl.BlockSpec((None, 2, GLA_HEADS, GLA_DK, GLA_DV), lambda i: (i, 0, 0, 0, 0))],
        out_shape=[jax.ShapeDtypeStruct((N_CTX, GLA_VW), BF16),
                   jax.ShapeDtypeStruct((BATCH, 2, GLA_HEADS, GLA_DK, GLA_DV), F32)],
        scratch_shapes=_gla_scratch(SEQ),
        compiler_params=pltpu.CompilerParams(dimension_semantics=("parallel",), vmem_limit_bytes=VMEM_LIMIT),
        name="gla_ctx",
    )(gq, gk, gv, la, gg, gn, *masks)


def _gla_lat(layer, gq, gk, gv, la, gg, gn, masks, s0):
    rows = DEC_SEQ
    blk0 = N_CTX // rows
    seq = lambda w: pl.BlockSpec((rows, w), lambda b: (blk0 + b, 0))
    return pl.pallas_call(
        partial(_gla_kernel, seq_len=DEC_SEQ, has_state=True),
        grid=(DEC_BATCH,),
        in_specs=[seq(GLA_KW), seq(GLA_KW), seq(GLA_VW), seq(2 * GLA_KW), seq(GLA_VW),
                  _layer_row(layer, GLA_VW)] + _gla_mask_specs() + [
                  pl.BlockSpec((None, None, 2, GLA_HEADS, GLA_DK, GLA_DV), lambda b: (b, layer, 0, 0, 0, 0))],
        out_specs=pl.BlockSpec((rows, GLA_VW), lambda b: (b, 0)),
        out_shape=jax.ShapeDtypeStruct((N_LAT, GLA_VW), BF16),
        scratch_shapes=_gla_scratch(DEC_SEQ),
        compiler_params=pltpu.CompilerParams(dimension_semantics=("parallel",), vmem_limit_bytes=VMEM_LIMIT),
        name="gla_lat",
    )(gq, gk, gv, la, gg, gn, *masks, s0)


def _route(scores, bias):
    biased = scores + bias
    v = [biased[e:e + 1, :] for e in range(N_EXPERTS)]
    s = [scores[e:e + 1, :] for e in range(N_EXPERTS)]
    gscore = []
    for g in range(N_GROUPS):
        a, b, c, d = v[4 * g:4 * g + 4]
        hi1, lo1, hi2, lo2 = jnp.maximum(a, b), jnp.minimum(a, b), jnp.maximum(c, d), jnp.minimum(c, d)
        gscore.append(jnp.maximum(hi1, hi2) + jnp.maximum(jnp.minimum(hi1, hi2), jnp.maximum(lo1, lo2)))
    one = lambda cond: jnp.where(cond, 1.0, 0.0)
    picked, chosen = [], []
    for g in range(N_GROUPS):
        best = None
        for g2 in range(N_GROUPS):
            if g2 == g:
                continue
            ok = one((gscore[g] > gscore[g2]) if g2 < g else (gscore[g] >= gscore[g2]))
            best = ok if best is None else best * ok
        for i in range(EXPERTS_PER_GROUP):
            e = 4 * g + i
            beaten = None
            for j in range(EXPERTS_PER_GROUP):
                if j == i:
                    continue
                e2 = 4 * g + j
                lost = one((v[e2] > v[e]) if j > i else (v[e2] >= v[e]))
                beaten = lost if beaten is None else beaten + lost
            chosen.append(best * one(beaten < 1.5))
            picked.append(chosen[-1] * s[e])
    den = picked[0]
    for e in range(1, N_EXPERTS):
        den = den + picked[e]
    return jnp.concatenate([p / den for p in picked] + chosen, axis=0)


def _mix_kernel(xc_ref, xl_ref, oac_ref, oal_ref, ogc_ref, ogl_ref, sa_ref, sb_ref, mod_ref, wpa_f, wpb_f, wo_f, n2_ref,
                rw_ref, rb_ref, x1_out, h2_out, gt_out, wpa_ref, wpb_ref, wo_ref):
    @pl.when(pl.program_id(0) == 0)
    def _():
        wpa_ref[...] = wpa_f[...].astype(BF16)
        wpb_ref[...] = wpb_f[...].astype(BF16)
        wo_ref[...] = wo_f[...].astype(BF16)

    is_ctx = pl.program_id(0) < N_CTX // TM
    oa = jnp.where(is_ctx, oac_ref[...], oal_ref[...])
    og = jnp.where(is_ctx, ogc_ref[...], ogl_ref[...])
    a = _dot(oa, wpa_ref[...])
    b = _dot(og, wpb_ref[...])
    merged = sa_ref[...].astype(F32) * a + sb_ref[...].astype(F32) * b
    g1 = mod_ref[:, 2 * D_MODEL:3 * D_MODEL]
    sh2 = mod_ref[:, 3 * D_MODEL:4 * D_MODEL]
    sc2 = mod_ref[:, 4 * D_MODEL:5 * D_MODEL]
    x = jnp.where(is_ctx, xc_ref[...], xl_ref[...])
    x1 = x + g1 * _dot(merged.astype(BF16), wo_ref[...])
    x1_out[...] = x1
    h2 = _rms(x1) * (n2_ref[...] * (1.0 + sc2)) + sh2
    _to_row_tiles(h2_out, h2)
    hh, hl = _split2(h2)
    rh, rl = rw_ref[0], rw_ref[1]
    nt = lambda r, t: lax.dot_general(r, t, _NT, preferred_element_type=F32)
    logits = nt(rl, hh) + nt(rh, hl) + nt(rh, hh)
    gt_out[...] = _route(jax.nn.sigmoid(logits), rb_ref[...])


def _mix(layer, x_pair, oa_c, oa_l, og_c, og_l, sa, sb, mod, wpa, wpb, wo, norm2, rw3, rb):
    nt = N_TOK // TM
    n_ctx_t = N_CTX // TM
    row = lambda i: (i, 0)
    const = lambda i: (0, 0)
    ctx_row = lambda i: (jnp.minimum(i, n_ctx_t - 1), 0)
    lat_row = lambda i: (jnp.maximum(i - n_ctx_t, 0), 0)
    return pl.pallas_call(
        _mix_kernel,
        grid=(nt,),
        in_specs=_x_specs(x_pair) + [
            pl.BlockSpec((TM, ATTN_WIDTH), ctx_row),
            pl.BlockSpec((TM, ATTN_WIDTH), lat_row),
            pl.BlockSpec((TM, GLA_VW), ctx_row),
            pl.BlockSpec((TM, GLA_VW), lat_row),
            pl.BlockSpec((TM, D_MODEL), row),
            pl.BlockSpec((TM, D_MODEL), row),
            _mod_spec(layer),
            pl.BlockSpec((None, ATTN_WIDTH, D_MODEL), lambda i: (layer, 0, 0), pipeline_mode=pl.Buffered(1)),
            pl.BlockSpec((None, GLA_VW, D_MODEL), lambda i: (layer, 0, 0), pipeline_mode=pl.Buffered(1)),
            pl.BlockSpec((None, D_MODEL, D_MODEL), lambda i: (layer, 0, 0), pipeline_mode=pl.Buffered(1)),
            _layer_row(layer, D_MODEL),
            pl.BlockSpec((2, N_EXPERTS, D_MODEL), lambda i: (0, 0, 0)),
            pl.BlockSpec((N_EXPERTS, 1), const),
        ],
        out_specs=[pl.BlockSpec((TM, D_MODEL), row), pl.BlockSpec((TM * ROW_TILE, LANES), row),
                   pl.BlockSpec((2 * N_EXPERTS, TM), lambda i: (0, i))],
        out_shape=[jax.ShapeDtypeStruct((N_TOK, D_MODEL), F32),
                   jax.ShapeDtypeStruct((N_TOK * ROW_TILE, LANES), F32),
                   jax.ShapeDtypeStruct((2 * N_EXPERTS, N_TOK), F32)],
        scratch_shapes=[pltpu.VMEM((ATTN_WIDTH, D_MODEL), BF16), pltpu.VMEM((GLA_VW, D_MODEL), BF16),
                        pltpu.VMEM((D_MODEL, D_MODEL), BF16)],
        compiler_params=pltpu.CompilerParams(dimension_semantics=("arbitrary",), vmem_limit_bytes=VMEM_LIMIT),
        name="mix",
    )(*x_pair, oa_c, oa_l, og_c, og_l, sa, sb, mod, wpa, wpb, wo, norm2, rw3, rb)


def _route_plan(gsel):
    gates = gsel[:N_EXPERTS]
    sel = gsel[N_EXPERTS:] > 0.5
    seli = sel.astype(jnp.int32)
    incl = jnp.cumsum(seli, axis=1)
    count = incl[:, -1]
    ntile = (count + TS - 1) // TS
    tile_end = jnp.cumsum(ntile)
    tile_start = tile_end - ntile
    total = tile_end[-1]
    slot = tile_start[:, None] * TS + incl - seli
    pos_a = jnp.min(jnp.where(sel, slot, N_SLOT), axis=0)
    pos_b = jnp.max(jnp.where(sel, slot, -1), axis=0)
    w_a = jnp.sum(jnp.where(sel & (slot == pos_a[None, :]), gates, 0.0), axis=0)
    w_b = jnp.sum(jnp.where(sel & (slot == pos_b[None, :]), gates, 0.0), axis=0)
    nt = N_TOK // TM
    pos = jnp.concatenate([pos_a.reshape(nt, 1, TM), pos_b.reshape(nt, 1, TM)], axis=2).astype(jnp.int32)
    plan = jnp.concatenate([tile_start, ntile, count, total[None]]).astype(jnp.int32)
    return pos, jnp.stack([w_a, w_b], axis=1), plan


def _to_row_tiles(ref, x):
    for k in range(ROW_TILE):
        ref[pl.ds(k, x.shape[0], stride=ROW_TILE), :] = x[:, k * LANES:(k + 1) * LANES]


def _from_row_tiles(ref, rows):
    return jnp.concatenate([ref[pl.ds(k, rows, stride=ROW_TILE), :] for k in range(ROW_TILE)], axis=1)


def _tile_rows(row, n=1):
    return pl.ds(pl.multiple_of(row * ROW_TILE, ROW_TILE), n * ROW_TILE)


def _row_copy(src, src_row, dst, dst_row, sem):
    return pltpu.make_async_copy(src.at[_tile_rows(src_row), :], dst.at[_tile_rows(dst_row), :], sem)


def _dispatch_kernel(plan_ref, pos_ref, h_ref, xs_ref, zero_ref, sem_u, sem_z, sem):
    i = pl.program_id(0)
    total = plan_ref[PLAN_TOTAL]

    def each_unused(fn):
        def unused(t, c):
            @pl.when(t >= total)
            def _():
                fn(pltpu.make_async_copy(zero_ref, xs_ref.at[_tile_rows(t * TS, TS), :], sem_u))
            return c
        lax.fori_loop(0, N_TILE, unused, 0)

    def each_tail(fn):
        def expert(e, c):
            n_tiles = plan_ref[N_EXPERTS + e]

            @pl.when(n_tiles > 0)
            def _():
                row0 = (plan_ref[e] + n_tiles - 1) * TS
                filled = plan_ref[2 * N_EXPERTS + e] - (n_tiles - 1) * TS

                def chunk(j, c2):
                    fn(pltpu.make_async_copy(zero_ref.at[pl.ds(0, Z_ROWS * ROW_TILE), :],
                                             xs_ref.at[_tile_rows(row0 + j * Z_ROWS, Z_ROWS), :], sem_z))
                    return c2

                lax.fori_loop(filled // Z_ROWS, TS // Z_ROWS, chunk, 0)
            return c
        lax.fori_loop(0, N_EXPERTS, expert, 0)

    @pl.when(i == 0)
    def _():
        zero_ref[...] = jnp.zeros(zero_ref.shape, F32)
        each_unused(lambda cp: cp.start())
        each_tail(lambda cp: cp.start())
        each_tail(lambda cp: cp.wait())

    @pl.when(i == N_TOK // TM - 1)
    def _():
        each_unused(lambda cp: cp.wait())

    def start(r, c):
        _row_copy(h_ref, r, xs_ref, pos_ref[0, r], sem).start(priority=0)
        _row_copy(h_ref, r, xs_ref, pos_ref[0, TM + r], sem).start(priority=1)
        return c

    def wait(r, c):
        _row_copy(h_ref, 0, xs_ref, 0, sem).wait()
        return c

    lax.fori_loop(0, TM, start, 0, unroll=8)
    lax.fori_loop(0, 2 * TM, wait, 0, unroll=8)


def _dispatch(plan, pos, h2):
    return pl.pallas_call(
        _dispatch_kernel,
        grid_spec=pltpu.PrefetchScalarGridSpec(
            num_scalar_prefetch=1,
            grid=(N_TOK // TM,),
            in_specs=[
                pl.BlockSpec((None, 1, 2 * TM), lambda i, plan: (i, 0, 0), memory_space=pltpu.SMEM),
                pl.BlockSpec((TM * ROW_TILE, LANES), lambda i, plan: (i, 0)),
            ],
            out_specs=pl.BlockSpec(memory_space=pl.ANY),
            scratch_shapes=[pltpu.VMEM((TS * ROW_TILE, LANES), F32), pltpu.SemaphoreType.DMA,
                            pltpu.SemaphoreType.DMA, pltpu.SemaphoreType.DMA],
        ),
        out_shape=jax.ShapeDtypeStruct((N_SLOT * ROW_TILE, LANES), F32),
        compiler_params=pltpu.CompilerParams(dimension_semantics=("arbitrary",), vmem_limit_bytes=VMEM_LIMIT),
        name="dispatch",
    )(plan, pos, h2)


def _moe_kernel(plan_ref, xs_ref, wg_ref, wu_ref, wd_ref, ys_ref, x_buf, y_buf, zero_ref, sem_in, sem_out, sem_z,
                wg_s, wu_s, wd_s):
    e = pl.program_id(0)
    total = plan_ref[PLAN_TOTAL]
    first = plan_ref[e]

    def in_copy(g):
        return pltpu.make_async_copy(xs_ref.at[_tile_rows(g * TS, TS), :], x_buf.at[g % 2], sem_in.at[g % 2])

    def out_copy(g):
        return pltpu.make_async_copy(y_buf.at[g % 2], ys_ref.at[_tile_rows(g * TS, TS), :], sem_out.at[g % 2])

    def zero_copy(g):
        return pltpu.make_async_copy(zero_ref, ys_ref.at[_tile_rows(g * TS, TS), :], sem_z)

    def each_unused(fn):
        def unused(g, c):
            @pl.when(g >= total)
            def _():
                fn(zero_copy(g))
            return c
        lax.fori_loop(0, N_TILE, unused, 0)

    @pl.when(e == 0)
    def _():
        in_copy(0).start()
        zero_ref[...] = jnp.zeros(zero_ref.shape, F32)
        each_unused(lambda cp: cp.start())

    wg_s[...] = wg_ref[...].astype(BF16)
    wu_s[...] = wu_ref[...].astype(BF16)
    wd_s[...] = wd_ref[...].astype(BF16)

    def tile(t, c):
        g = first + t
        slot = g % 2

        @pl.when(g + 1 < total)
        def _():
            in_copy(g + 1).start()

        in_copy(g).wait()

        @pl.when(g >= 2)
        def _():
            out_copy(g - 2).wait()

        x = _from_row_tiles(x_buf.at[slot], TS).astype(BF16)
        up = _dot(x, wu_s[...])
        gt = _dot(x, wg_s[...])
        act = (gt * jax.nn.sigmoid(gt) * up).astype(BF16)
        _to_row_tiles(y_buf.at[slot], _dot(act, wd_s[...]))
        out_copy(g).start()
        return c

    lax.fori_loop(0, plan_ref[N_EXPERTS + e], tile, 0)

    @pl.when(e == N_EXPERTS - 1)
    def _():
        out_copy(total - 1).wait()

        @pl.when(total >= 2)
        def _():
            out_copy(total - 2).wait()

        each_unused(lambda cp: cp.wait())


def _moe(layer, plan, xs, w_gate, w_up, w_down):
    w_spec = lambda a, b: pl.BlockSpec((None, None, a, b), lambda e, plan: (layer, e, 0, 0))
    tile_buf = lambda n: pltpu.VMEM((n, TS * ROW_TILE, LANES), F32)
    return pl.pallas_call(
        _moe_kernel,
        grid_spec=pltpu.PrefetchScalarGridSpec(
            num_scalar_prefetch=1,
            grid=(N_EXPERTS,),
            in_specs=[
                pl.BlockSpec(memory_space=pl.ANY),
                w_spec(D_MODEL, D_EXPERT), w_spec(D_MODEL, D_EXPERT), w_spec(D_EXPERT, D_MODEL),
            ],
            out_specs=pl.BlockSpec(memory_space=pl.ANY),
            scratch_shapes=[tile_buf(2), tile_buf(2), pltpu.VMEM((TS * ROW_TILE, LANES), F32),
                            pltpu.SemaphoreType.DMA((2,)), pltpu.SemaphoreType.DMA((2,)), pltpu.SemaphoreType.DMA,
                            pltpu.VMEM((D_MODEL, D_EXPERT), BF16), pltpu.VMEM((D_MODEL, D_EXPERT), BF16),
                            pltpu.VMEM((D_EXPERT, D_MODEL), BF16)],
        ),
        out_shape=jax.ShapeDtypeStruct((N_SLOT * ROW_TILE, LANES), F32),
        compiler_params=pltpu.CompilerParams(dimension_semantics=("arbitrary",), vmem_limit_bytes=VMEM_LIMIT),
        name="moe",
    )(plan, xs, w_gate, w_up, w_down)


def _combine_kernel(pos_ref, nxt_ref, w_ref, x1_ref, mod_ref, ys_ref, out_ref, ya_ref, yb_ref, sem, *, n):
    i = pl.program_id(0)

    def gather(p_ref, slot):
        def start(r, c):
            _row_copy(ys_ref, p_ref[0, r], ya_ref.at[slot], r, sem.at[slot]).start(priority=0)
            _row_copy(ys_ref, p_ref[0, TM + r], yb_ref.at[slot], r, sem.at[slot]).start(priority=1)
            return c
        lax.fori_loop(0, TM, start, 0, unroll=8)

    @pl.when(i == 0)
    def _():
        gather(pos_ref, 0)

    @pl.when(i + 1 < n)
    def _():
        gather(nxt_ref, (i + 1) % 2)

    slot = i % 2

    def wait(r, c):
        _row_copy(ys_ref, 0, ya_ref.at[slot], 0, sem.at[slot]).wait()
        return c

    lax.fori_loop(0, 2 * TM, wait, 0, unroll=8)
    g2 = mod_ref[:, 5 * D_MODEL:6 * D_MODEL]
    w = w_ref[...]
    ya = _from_row_tiles(ya_ref.at[slot], TM)
    yb = _from_row_tiles(yb_ref.at[slot], TM)
    out_ref[...] = x1_ref[...] + g2 * (w[:, 0:1] * ya + w[:, 1:2] * yb)


def _combine(layer, pos, w_ab, x1, mod, ys, tile0, nt):
    pos_spec = lambda fn: pl.BlockSpec((None, 1, 2 * TM), fn, memory_space=pltpu.SMEM)
    return pl.pallas_call(
        partial(_combine_kernel, n=nt),
        grid=(nt,),
        in_specs=[
            pos_spec(lambda i: (tile0 + i, 0, 0)),
            pos_spec(lambda i: (tile0 + jnp.minimum(i + 1, nt - 1), 0, 0)),
            pl.BlockSpec((TM, 2), lambda i: (tile0 + i, 0)),
            pl.BlockSpec((TM, D_MODEL), lambda i: (tile0 + i, 0)),
            _mod_spec(layer, tile0),
            pl.BlockSpec(memory_space=pl.ANY),
        ],
        out_specs=pl.BlockSpec((TM, D_MODEL), lambda i: (i, 0)),
        out_shape=jax.ShapeDtypeStruct((nt * TM, D_MODEL), F32),
        scratch_shapes=[pltpu.VMEM((2, TM * ROW_TILE, LANES), F32), pltpu.VMEM((2, TM * ROW_TILE, LANES), F32),
                        pltpu.SemaphoreType.DMA((2,))],
        compiler_params=pltpu.CompilerParams(dimension_semantics=("arbitrary",), vmem_limit_bytes=VMEM_LIMIT),
        name="combine",
    )(pos, pos, w_ab, x1, mod, ys)


def _rope_tables():
    pos = np.arange(DEC_SEQ)
    rows = (pos // GRID_W).astype(np.float64)
    cols = (pos % GRID_W).astype(np.float64)
    half = HEAD_DIM // 2
    inv_freq = ROPE_THETA ** (-np.arange(0, half, 2, dtype=np.float64) / half)
    ang_r = rows[:, None] * inv_freq[None, :]
    ang_c = cols[:, None] * inv_freq[None, :]
    cos_h = np.concatenate([np.cos(ang_r), np.cos(ang_r), np.cos(ang_c), np.cos(ang_c)], axis=-1)
    sin_h = np.concatenate([-np.sin(ang_r), np.sin(ang_r), -np.sin(ang_c), np.sin(ang_c)], axis=-1)
    cos = np.concatenate([np.ones((TM, HEAD_DIM)), cos_h], axis=0)
    sin = np.concatenate([np.zeros((TM, HEAD_DIM)), sin_h], axis=0)
    return (jnp.asarray(np.tile(cos, (1, N_HEADS)), F32), jnp.asarray(np.tile(sin, (1, N_HEADS)), F32))


def _pack_w2(w2):
    z = jnp.zeros((DEPTH, GLA_RANK, GLA_KW), w2.dtype)
    top = jnp.concatenate([w2[:, 0], z], axis=2)
    bot = jnp.concatenate([z, w2[:, 1]], axis=2)
    pad = jnp.zeros((DEPTH, LR_PAD - 2 * GLA_RANK, 2 * GLA_KW), w2.dtype)
    return jnp.concatenate([top, bot, pad], axis=1).astype(BF16)


def kernel(x_prompt, x_sample, cache_k, cache_v, state_gla, c, c_ctx, norm1, norm2, w_ada, b_ada, w_in,
           q_norm, k_norm, gla_w2, gla_b, gla_norm, w_pa, w_pb, w_out, router_w, router_bias,
           w_gate, w_up, w_down):
    x_pair = (x_prompt.reshape(N_CTX, D_MODEL), x_sample.reshape(N_LAT, D_MODEL))
    cond = jnp.concatenate([c_ctx[None, :], c, jnp.zeros((N_COND - 1 - DEC_BATCH, D_MODEL), F32)], axis=0)
    mod = _ada(cond, w_ada, b_ada).reshape(DEPTH, N_COND, 1, MOD_W)
    cos, sin = _rope_tables()
    gla_masks = _gla_masks()
    w_in_t = jnp.swapaxes(w_in, 1, 2)
    lane_head = np.arange(ATTN_WIDTH) // HEAD_DIM
    bd = jnp.asarray(lane_head[:, None] == lane_head[None, :], BF16)
    rw3 = jnp.stack(_split2(router_w.T), axis=0)
    rb = router_bias.reshape(N_EXPERTS, 1)
    ck = cache_k.reshape(DEC_BATCH, DEPTH, PAST_LEN, KV_WIDTH)
    cv = cache_v.reshape(DEC_BATCH, DEPTH, PAST_LEN, KV_WIDTH)
    rows = lambda p, reps: jnp.tile(p, (1, reps)).reshape(DEPTH, 1, -1)
    n1, n2 = rows(norm1, 1), rows(norm2, 1)
    qn, kn, gn = rows(q_norm, N_HEADS), rows(k_norm, KV_HEADS), rows(gla_norm, GLA_HEADS)
    gb = gla_b.reshape(DEPTH, 1, 2 * GLA_KW)
    w2p = _pack_w2(gla_w2)

    keys, vals, states = [], [], []
    for l in range(DEPTH):
        q, kf, kr, v, gq, gk, gv, gg, la, sa, sb = _inproj(l, x_pair, mod, n1, w_in_t, qn, kn, cos, sin, bd, w2p, gb)
        oa_c = _attn_ctx(q, kr, v)
        oa_l = _attn_lat(l, q, kr, v, ck, cv)
        og_c, s_fin = _gla_ctx(l, gq, gk, gv, la, gg, gn, gla_masks)
        og_l = _gla_lat(l, gq, gk, gv, la, gg, gn, gla_masks, state_gla)
        x1, h2, gsel = _mix(l, x_pair, oa_c, oa_l, og_c, og_l, sa, sb, mod, w_pa, w_pb, w_out, n2, rw3, rb)
        pos, w_ab, plan = _route_plan(gsel)
        xs = _dispatch(plan, pos, h2)
        ys = _moe(l, plan, xs, w_gate, w_up, w_down)
        n_ctx_t = N_CTX // TM
        if l + 1 < DEPTH:
            x = _combine(l, pos, w_ab, x1, mod, ys, 0, N_TOK // TM)
            x_pair = (x, x)
        else:
            x_pair = (_combine(l, pos, w_ab, x1, mod, ys, 0, n_ctx_t),
                      _combine(l, pos, w_ab, x1, mod, ys, n_ctx_t, N_LAT // TM))
        keys.append(kf[:N_CTX].reshape(BATCH, SEQ, KV_HEADS, HEAD_DIM))
        vals.append(v[:N_CTX].reshape(BATCH, SEQ, KV_HEADS, HEAD_DIM))
        states.append(s_fin)
    y_prompt = x_pair[0].reshape(BATCH, SEQ, D_MODEL)
    y_sample = x_pair[1].reshape(DEC_BATCH, DEC_SEQ, D_MODEL)
    return (y_prompt, y_sample, jnp.stack(keys, axis=1), jnp.stack(vals, axis=1), jnp.stack(states, axis=1))
```
